```python
import math
import jax, jax.numpy as jnp
from jax import lax
import numpy as np

D_MODEL = 1024
BATCH = 8
SEQ = 8192
DEPTH = 4
DEC_BATCH = 8
DEC_SEQ = 32
PAST_LEN = 4096

CHUNK = 64
N_MIXERS = 3
Q_BLOCK = 128
EPS = 1e-6
NEG = -1e30
MIX_W = D_MODEL
MEM_TOKENS = 256
MEM_HEADS = 4
MEM_HEAD_DIM = 64
MEM_W = MEM_HEADS * MEM_HEAD_DIM
BRANCH_W = MIX_W - MEM_W
GATE_W = MIX_W
A_HEADS = 12
A_HEAD_DIM = 64
A_BAND_CHUNKS = 8
A_PAST_ROWS = A_BAND_CHUNKS * CHUNK
A_REL_CLIP = 128
B_HEADS = 6
B_HEAD_DIM = 64
B_V_DIM = 2 * B_HEAD_DIM
C_HEADS = 12
C_NOPE = 64
C_ROPE = 32
C_V = 64
C_Q_LORA = 384
C_KV_LORA = 256
ROPE_BASE = 10000.0
N_A = (DEPTH + 2) // 3
N_B = (DEPTH + 1) // 3
N_C = DEPTH // 3
IN_A = 3 * A_HEADS * A_HEAD_DIM + MEM_W + GATE_W
IN_B = 2 * B_HEADS * 2 * B_HEAD_DIM + B_HEADS * B_V_DIM + MEM_W + GATE_W
IN_C = C_Q_LORA + C_KV_LORA + C_ROPE + MEM_W + GATE_W

kernel_name = "hybrid_chunk_stream_encoder_step"


def _rmsnorm(x, g):
    xf = x.astype(jnp.float32)
    y = xf * lax.rsqrt(jnp.mean(xf * xf, axis=-1, keepdims=True) + EPS)
    return (y * g.astype(jnp.float32)).astype(x.dtype)


def _past_pos(rows):
    return PAST_LEN - rows + jnp.arange(rows)


def _chunk_mask(q_pos, k_pos):
    return (k_pos[None, :] // CHUNK) <= (q_pos[:, None] // CHUNK)


def _sweep_queries(fn, q_pos, *qs):
    T = q_pos.shape[0]
    nb = T // Q_BLOCK
    to_blocks = lambda a: jnp.moveaxis(a.reshape((a.shape[0], nb, Q_BLOCK) + a.shape[2:]), 1, 0)
    out = lax.map(lambda args: fn(*args), (q_pos.reshape(nb, Q_BLOCK),) + tuple(to_blocks(a) for a in qs))
    out = jnp.moveaxis(out, 0, 1)
    return out.reshape((out.shape[0], T) + out.shape[3:])


def _rope(x, pos):
    d = x.shape[-1]
    half = d // 2
    inv = jnp.exp(-math.log(ROPE_BASE) * jnp.arange(half, dtype=jnp.float32) * 2.0 / d)
    ang = pos.astype(jnp.float32)[:, None] * inv[None, :]
    ang = ang.reshape((1, ang.shape[0]) + (1,) * (x.ndim - 3) + (half,))
    cos, sin = jnp.cos(ang), jnp.sin(ang)
    x1 = x[..., :half].astype(jnp.float32)
    x2 = x[..., half:].astype(jnp.float32)
    return jnp.concatenate([x1 * cos - x2 * sin, x2 * cos + x1 * sin], axis=-1).astype(x.dtype)


def _band_attend(q, k, v, q_pos, k_pos, rel_bias):
    s = jnp.einsum('bqhd,bkhd->bhqk', q, k, preferred_element_type=jnp.float32) * (A_HEAD_DIM ** -0.5)
    rel = jnp.clip(q_pos[:, None] - k_pos[None, :], -A_REL_CLIP, A_REL_CLIP) + A_REL_CLIP
    bias = jnp.moveaxis(rel_bias[rel], -1, 0).astype(jnp.float32)
    qc = q_pos[:, None] // CHUNK
    kc = k_pos[None, :] // CHUNK
    valid = (k_pos[None, :] >= 0) & (kc <= qc) & (kc >= qc - A_BAND_CHUNKS)
    p = jax.nn.softmax(jnp.where(valid, s + bias, NEG), axis=-1)
    return jnp.einsum('bhqk,bkhd->bqhd', p, v.astype(jnp.float32))


def _mixer_a(u, pos, rel_bias, past):
    B, T, _ = u.shape
    hd = A_HEADS * A_HEAD_DIM
    q, k, v = (u[..., n * hd:(n + 1) * hd].reshape(B, T, A_HEADS, A_HEAD_DIM) for n in range(3))
    if past is None:
        nc = T // CHUNK
        band = A_PAST_ROWS + CHUNK
        pad = ((0, 0), (A_PAST_ROWS, 0), (0, 0), (0, 0))
        kp, vp = jnp.pad(k, pad), jnp.pad(v, pad)
        qc = jnp.moveaxis(q.reshape(B, nc, CHUNK, A_HEADS, A_HEAD_DIM), 1, 0)

        def one_chunk(args):
            c, qb = args
            start = c * CHUNK
            kb = lax.dynamic_slice_in_dim(kp, start, band, axis=1)
            vb = lax.dynamic_slice_in_dim(vp, start, band, axis=1)
            q_pos = start + jnp.arange(CHUNK)
            k_pos = start - A_PAST_ROWS + jnp.arange(band)
            return _band_attend(qb, kb, vb, q_pos, k_pos, rel_bias)

        o = lax.map(one_chunk, (jnp.arange(nc), qc))
        o = jnp.moveaxis(o, 0, 1).reshape(B, T, hd)
        keep = min(A_PAST_ROWS, T)
        new = (k[:, T - keep:], v[:, T - keep:])
    else:
        ck, cv = past
        kk = jnp.concatenate([ck, k], axis=1)
        vv = jnp.concatenate([cv, v], axis=1)
        k_pos = jnp.concatenate([_past_pos(ck.shape[1]), pos])
        o = _band_attend(q, kk, vv, pos, k_pos, rel_bias).reshape(B, T, hd)
        new = (k, v)
    return o.astype(u.dtype), new


def _mixer_b(u, pos, lam_p, sub_g, lam_init, past):
    B, T, _ = u.shape
    qk_w = B_HEADS * 2 * B_HEAD_DIM
    q = u[..., :qk_w].reshape(B, T, B_HEADS, 2, B_HEAD_DIM)
    k = u[..., qk_w:2 * qk_w].reshape(B, T, B_HEADS, 2, B_HEAD_DIM)
    v = u[..., 2 * qk_w:].reshape(B, T, B_HEADS, B_V_DIM)
    lp = lam_p.astype(jnp.float32)
    lam = jnp.exp(jnp.sum(lp[0] * lp[1])) - jnp.exp(jnp.sum(lp[2] * lp[3])) + lam_init
    slopes = jnp.exp2(-8.0 * jnp.arange(1, B_HEADS + 1, dtype=jnp.float32) / B_HEADS)
    if past is None:
        kk, vv, k_pos = k, v, pos
    else:
        ck, cv = past
        kk = jnp.concatenate([ck, k], axis=1)
        vv = jnp.concatenate([cv, v], axis=1)
        k_pos = jnp.concatenate([_past_pos(ck.shape[1]), pos])

    def attend(q_pos, qb):
        s = jnp.einsum('bqhid,bkhid->bihqk', qb, kk, preferred_element_type=jnp.float32) * (B_HEAD_DIM ** -0.5)
        dist = jnp.abs(q_pos[:, None] - k_pos[None, :]).astype(jnp.float32)
        s = jnp.where(_chunk_mask(q_pos, k_pos), s - slopes[:, None, None] * dist[None], NEG)
        p = jax.nn.softmax(s, axis=-1)
        p_diff = p[:, 0] - lam * p[:, 1]
        return jnp.einsum('bhqk,bkhe->bqhe', p_diff, vv.astype(jnp.float32))

    o = _sweep_queries(attend, pos, q) if past is None else attend(pos, q)
    o = _rmsnorm(o, sub_g) * (1.0 - lam_init)
    return o.reshape(B, T, B_HEADS * B_V_DIM).astype(u.dtype), (k, v)


def _mixer_c(u, pos, q_norm_g, kv_norm_g, w_uq, w_ukv, past):
    B, T, _ = u.shape
    cq = u[..., :C_Q_LORA]
    ckv = u[..., C_Q_LORA:C_Q_LORA + C_KV_LORA]
    kr = u[..., C_Q_LORA + C_KV_LORA:C_Q_LORA + C_KV_LORA + C_ROPE]
    q = (_rmsnorm(cq, q_norm_g) @ w_uq).reshape(B, T, C_HEADS, C_NOPE + C_ROPE)
    q_nope, q_rope = q[..., :C_NOPE], _rope(q[..., C_NOPE:], pos)
    lat = _rmsnorm(ckv, kv_norm_g)
    kr = _rope(kr, pos)
    if past is None:
        lat_all, kr_all, k_pos = lat, kr, pos
    else:
        cl, cr = past
        lat_all = jnp.concatenate([cl, lat], axis=1)
        kr_all = jnp.concatenate([cr, kr], axis=1)
        k_pos = jnp.concatenate([_past_pos(cl.shape[1]), pos])
    kv = (lat_all @ w_ukv).reshape(B, lat_all.shape[1], C_HEADS, C_NOPE + C_V)
    k_nope, v = kv[..., :C_NOPE], kv[..., C_NOPE:]
    scale = (C_NOPE + C_ROPE) ** -0.5

    def attend(q_pos, qn, qr):
        s = (jnp.einsum('bqhd,bkhd->bhqk', qn, k_nope, preferred_element_type=jnp.float32)
             + jnp.einsum('bqhd,bkd->bhqk', qr, kr_all, preferred_element_type=jnp.float32)) * scale
        p = jax.nn.softmax(jnp.where(_chunk_mask(q_pos, k_pos), s, NEG), axis=-1)
        return jnp.einsum('bhqk,bkhd->bqhd', p, v.astype(jnp.float32))

    o = _sweep_queries(attend, pos, q_nope, q_rope) if past is None else attend(pos, q_nope, q_rope)
    return o.reshape(B, T, C_HEADS * C_V).astype(u.dtype), (lat, kr)


def _mem_attend(mq, mk, mv):
    s = jnp.einsum('bqhd,bmhd->bhqm', mq, mk, preferred_element_type=jnp.float32) * (MEM_HEAD_DIM ** -0.5)
    p = jax.nn.softmax(s, axis=-1)
    return jnp.einsum('bhqm,bmhd->bqhd', p, mv.astype(jnp.float32))


def _trunk(x, pos, mem_k, mem_v, past, norm_g, final_g, w_out, w_in_a, rel_bias_a, w_in_b, lambda_b,
           subln_g_b, w_in_c, q_norm_g_c, kv_norm_g_c, w_uq_c, w_ukv_c):
    new_a, new_b, new_c = [], [], []
    B, T, _ = x.shape
    for i in range(DEPTH):
        kind, j = i % N_MIXERS, i // N_MIXERS
        h = _rmsnorm(x, norm_g[i])
        u = h @ (w_in_a, w_in_b, w_in_c)[kind][j]
        n_mix = u.shape[-1] - MEM_W - GATE_W
        mix_u, mq, gate = u[..., :n_mix], u[..., n_mix:n_mix + MEM_W], u[..., n_mix + MEM_W:]
        if kind == 0:
            lp = None if past is None else (past[0][j], past[1][j])
            br, st = _mixer_a(mix_u, pos, rel_bias_a[j], lp)
            new_a.append(st)
        elif kind == 1:
            lp = None if past is None else (past[2][j], past[3][j])
            lam_init = 0.8 - 0.6 * math.exp(-0.3 * i)
            br, st = _mixer_b(mix_u, pos, lambda_b[j], subln_g_b[j], lam_init, lp)
            new_b.append(st)
        else:
            lp = None if past is None else (past[4][j], past[5][j])
            br, st = _mixer_c(mix_u, pos, q_norm_g_c[j], kv_norm_g_c[j], w_uq_c[j], w_ukv_c[j], lp)
            new_c.append(st)
        mo = _mem_attend(mq.reshape(B, T, MEM_HEADS, MEM_HEAD_DIM), mem_k[i], mem_v[i]).reshape(B, T, MEM_W)
        y = jnp.concatenate([br.astype(x.dtype), mo.astype(x.dtype)], axis=-1) * jax.nn.silu(gate)
        x = x + y @ w_out[i]
    return _rmsnorm(x, final_g), new_a, new_b, new_c


def setup_inputs(seed: int = 0) -> dict:
    key = jax.random.key(seed)
    ks = jax.random.split(key, 32)
    nrm = lambda k, shape, s: jax.random.normal(k, shape, jnp.float32) * s
    a_rows = min(A_PAST_ROWS, PAST_LEN)
    return {
        "x_prompt": nrm(ks[0], (BATCH, SEQ, D_MODEL), 1.0),
        "x_sample": nrm(ks[1], (DEC_BATCH, DEC_SEQ, D_MODEL), 1.0),
        "cache_a_k": nrm(ks[2], (N_A, DEC_BATCH, a_rows, A_HEADS, A_HEAD_DIM), 1.0),
        "cache_a_v": nrm(ks[3], (N_A, DEC_BATCH, a_rows, A_HEADS, A_HEAD_DIM), 1.0),
        "cache_b_k": nrm(ks[4], (N_B, DEC_BATCH, PAST_LEN, B_HEADS, 2, B_HEAD_DIM), 1.0),
        "cache_b_v": nrm(ks[5], (N_B, DEC_BATCH, PAST_LEN, B_HEADS, B_V_DIM), 1.0),
        "cache_c_lat": nrm(ks[6], (N_C, DEC_BATCH, PAST_LEN, C_KV_LORA), 1.0),
        "cache_c_rope": nrm(ks[7], (N_C, DEC_BATCH, PAST_LEN, C_ROPE), 1.0),
        "cache_mem_k": nrm(ks[8], (DEPTH, DEC_BATCH, MEM_TOKENS, MEM_HEADS, MEM_HEAD_DIM), 1.0),
        "cache_mem_v": nrm(ks[9], (DEPTH, DEC_BATCH, MEM_TOKENS, MEM_HEADS, MEM_HEAD_DIM), 1.0),
        "mem_prompt": nrm(ks[10], (BATCH, MEM_TOKENS, D_MODEL), 1.0),
        "norm_g": 1.0 + nrm(ks[11], (DEPTH, D_MODEL), 0.02),
        "final_g": 1.0 + nrm(ks[12], (D_MODEL,), 0.02),
        "mem_norm_g": 1.0 + nrm(ks[13], (DEPTH, D_MODEL), 0.02),
        "w_mem_kv": nrm(ks[14], (DEPTH, D_MODEL, 2 * MEM_W), D_MODEL ** -0.5),
        "w_out": nrm(ks[15], (DEPTH, MIX_W, D_MODEL), MIX_W ** -0.5),
        "w_in_a": nrm(ks[16], (N_A, D_MODEL, IN_A), D_MODEL ** -0.5),
        "rel_bias_a": nrm(ks[17], (N_A, 2 * A_REL_CLIP + 1, A_HEADS), 0.2),
        "w_in_b": nrm(ks[18], (N_B, D_MODEL, IN_B), D_MODEL ** -0.5),
        "lambda_b": nrm(ks[19], (N_B, 4, B_HEAD_DIM), 0.1),
        "subln_g_b": 1.0 + nrm(ks[20], (N_B, B_V_DIM), 0.02),
        "w_in_c": nrm(ks[21], (N_C, D_MODEL, IN_C), D_MODEL ** -0.5),
        "q_norm_g_c": 1.0 + nrm(ks[22], (N_C, C_Q_LORA), 0.02),
        "kv_norm_g_c": 1.0 + nrm(ks[23], (N_C, C_KV_LORA), 0.02),
        "w_uq_c": nrm(ks[24], (N_C, C_Q_LORA, C_HEADS * (C_NOPE + C_ROPE)), C_Q_LORA ** -0.5),
        "w_ukv_c": nrm(ks[25], (N_C, C_KV_LORA, C_HEADS * (C_NOPE + C_V)), C_KV_LORA ** -0.5),
    }


def reference(x_prompt, x_sample, cache_a_k, cache_a_v, cache_b_k, cache_b_v, cache_c_lat, cache_c_rope,
              cache_mem_k, cache_mem_v, mem_prompt, norm_g, final_g, mem_norm_g, w_mem_kv, w_out, w_in_a,
              rel_bias_a, w_in_b, lambda_b, subln_g_b, w_in_c, q_norm_g_c, kv_norm_g_c, w_uq_c, w_ukv_c):
    Bp = mem_prompt.shape[0]
    mem_k_p, mem_v_p = [], []
    for i in range(DEPTH):
        mkv = _rmsnorm(mem_prompt, mem_norm_g[i]) @ w_mem_kv[i]
        mem_k_p.append(mkv[..., :MEM_W].reshape(Bp, MEM_TOKENS, MEM_HEADS, MEM_HEAD_DIM))
        mem_v_p.append(mkv[..., MEM_W:].reshape(Bp, MEM_TOKENS, MEM_HEADS, MEM_HEAD_DIM))
    weights = (norm_g, final_g, w_out, w_in_a, rel_bias_a, w_in_b, lambda_b, subln_g_b,
               w_in_c, q_norm_g_c, kv_norm_g_c, w_uq_c, w_ukv_c)
    pos_p = jnp.arange(x_prompt.shape[1])
    y_prompt, na_p, nb_p, nc_p = _trunk(x_prompt, pos_p, mem_k_p, mem_v_p, None, *weights)
    pos_s = PAST_LEN + jnp.arange(x_sample.shape[1])
    past = (cache_a_k, cache_a_v, cache_b_k, cache_b_v, cache_c_lat, cache_c_rope)
    y_sample, na_s, nb_s, nc_s = _trunk(x_sample, pos_s, cache_mem_k, cache_mem_v, past, *weights)
    stk = lambda lst, n: jnp.stack([s[n] for s in lst])
    return (y_prompt, y_sample,
            stk(na_p, 0), stk(na_p, 1), stk(na_s, 0), stk(na_s, 1),
            stk(nb_p, 0), stk(nb_p, 1), stk(nb_s, 0), stk(nb_s, 1),
            stk(nc_p, 0), stk(nc_p, 1), stk(nc_s, 0), stk(nc_s, 1),
            jnp.stack(mem_k_p), jnp.stack(mem_v_p))
```

```python
import functools
import math

import jax
import jax.numpy as jnp
from jax import lax
from jax.experimental import pallas as pl
from jax.experimental.pallas import tpu as pltpu

F32 = jnp.float32
BF16 = jnp.bfloat16

LANES = 128
VMEM_LIMIT = 56 * 1024 * 1024

EPS = 1e-6
NEG = -1e30
CHUNK = 64
HEAD_DIM = 64
MEM_W = 256
A_HEADS = 12
A_PAST_ROWS = 512
A_BAND_CHUNKS = 8
A_REL_CLIP = 128
B_HEADS = 6
C_HEADS = 12
C_NOPE = 64
C_ROPE = 32
C_V = 64
C_Q_LORA = 384
C_KV_LORA = 256
ROPE_BASE = 10000.0
N_MIXERS = 3


def _params(n_grid):
    return pltpu.CompilerParams(dimension_semantics=("arbitrary",) * n_grid,
                                vmem_limit_bytes=VMEM_LIMIT)


def _row_tile(m, want):
    t = min(m, want)
    assert m % t == 0, (m, t)
    return t


def _rms(x, g):
    return x * lax.rsqrt(jnp.mean(x * x, axis=-1, keepdims=True) + EPS) * g


def _norm_matmul_kernel(x_ref, g_ref, w_ref, *out_refs, widths):
    h = _rms(x_ref[...], g_ref[...]).astype(BF16)
    off = 0
    for o_ref, w in zip(out_refs, widths):
        o_ref[...] = jnp.dot(h, w_ref[:, off:off + w], preferred_element_type=F32).astype(o_ref.dtype)
        off += w


def norm_matmul(x, g, w, widths, dtypes, tm=512):
    m, d = x.shape
    tm = _row_tile(m, tm)
    n = w.shape[1]
    assert n == sum(widths)
    return pl.pallas_call(
        functools.partial(_norm_matmul_kernel, widths=tuple(widths)),
        grid=(m // tm,),
        in_specs=[pl.BlockSpec((tm, d), lambda i: (i, 0)),
                  pl.BlockSpec((1, d), lambda i: (0, 0)),
                  pl.BlockSpec((d, n), lambda i: (0, 0))],
        out_specs=[pl.BlockSpec((tm, wd), lambda i: (i, 0)) for wd in widths],
        out_shape=[jax.ShapeDtypeStruct((m, wd), dt) for wd, dt in zip(widths, dtypes)],
        compiler_params=_params(1),
    )(x, g.reshape(1, d).astype(F32), w)


def _gated_out_kernel(br_ref, mo_ref, gate_ref, x_ref, w_ref, fg_ref, o_ref, *, final):
    nb = br_ref.shape[-1]
    gate = gate_ref[...]
    sg = gate * jax.nn.sigmoid(gate)
    y1 = (br_ref[...] * sg[:, :nb]).astype(BF16)
    y2 = (mo_ref[...] * sg[:, nb:]).astype(BF16)
    acc = jnp.dot(y1, w_ref[:nb, :], preferred_element_type=F32)
    acc += jnp.dot(y2, w_ref[nb:, :], preferred_element_type=F32)
    xn = x_ref[...] + acc
    o_ref[...] = _rms(xn, fg_ref[...]) if final else xn


def gated_out(br, mo, gate, x, w, final_g, final, tm=512):
    m, d = x.shape
    tm = _row_tile(m, tm)
    nb, nm, ng = br.shape[1], mo.shape[1], gate.shape[1]
    row = lambda i: (i, 0)
    fixed = lambda i: (0, 0)
    return pl.pallas_call(
        functools.partial(_gated_out_kernel, final=final),
        grid=(m // tm,),
        in_specs=[pl.BlockSpec((tm, nb), row), pl.BlockSpec((tm, nm), row), pl.BlockSpec((tm, ng), row),
                  pl.BlockSpec((tm, d), row), pl.BlockSpec((ng, d), fixed), pl.BlockSpec((1, d), fixed)],
        out_specs=pl.BlockSpec((tm, d), row),
        out_shape=jax.ShapeDtypeStruct((m, d), F32),
        compiler_params=_params(1),
    )(br, mo, gate, x, w, final_g.reshape(1, d).astype(F32))


def _half_masks():
    lane = lax.broadcasted_iota(jnp.int32, (1, LANES), 1)
    lo = lane < HEAD_DIM
    return lo, jnp.logical_not(lo)


def _attn_window_kernel(q_ref, k_ref, v_ref, *rest, tq, win, back):
    if len(rest) == 2:
        bias_ref, o_ref = rest
    else:
        bias_ref, (o_ref,) = None, rest
    tk_all = k_ref.shape[0]
    if win == tk_all:
        k = k_ref[...]
        v = v_ref[...]
    else:
        start = jnp.clip(pl.program_id(2) * tq - back, 0, tk_all - win)
        start = pl.multiple_of(start, CHUNK)
        k = k_ref[pl.ds(start, win), :]
        v = v_ref[pl.ds(start, win), :]
    k = k.astype(BF16)
    v = v.astype(BF16)
    q = q_ref[...]
    halves = _half_masks()
    outs = []
    for i in range(2):
        qm = jnp.where(halves[i], q, jnp.zeros_like(q))
        s = lax.dot_general(qm, k, (((1,), (1,)), ((), ())), preferred_element_type=F32)
        if bias_ref is not None:
            s = s + bias_ref[i]
        m = jnp.max(s, axis=-1, keepdims=True)
        p = jnp.exp(s - m)
        l = jnp.sum(p, axis=-1, keepdims=True)
        o = jnp.dot(p.astype(BF16), v, preferred_element_type=F32)
        outs.append(o / l)
    o_ref[...] = jnp.where(halves[0], outs[0], outs[1])


def attn_window(q, k, v, bias, *, tq, win, back):
    bt, t_q, gw = q.shape
    g = gw // LANES
    t_k = k.shape[1]
    tq = _row_tile(t_q, tq)
    in_specs = [pl.BlockSpec((None, tq, LANES), lambda b, h, i: (b, i, h)),
                pl.BlockSpec((None, t_k, LANES), lambda b, h, i: (b, 0, h)),
                pl.BlockSpec((None, t_k, LANES), lambda b, h, i: (b, 0, h))]
    args = [q, k, v]
    if bias is not None:
        last = bias.shape[0] - 1
        in_specs.append(pl.BlockSpec((None, 2, tq, win), lambda b, h, i: (jnp.minimum(i, last), h, 0, 0)))
        args.append(bias)
    return pl.pallas_call(
        functools.partial(_attn_window_kernel, tq=tq, win=win, back=back),
        grid=(bt, g, t_q // tq),
        in_specs=in_specs,
        out_specs=pl.BlockSpec((None, tq, LANES), lambda b, h, i: (b, i, h)),
        out_shape=jax.ShapeDtypeStruct((bt, t_q, gw), F32),
        compiler_params=_params(3),
    )(*args)


def _flash_kernel(slope_ref, q_ref, km_ref, vm_ref, kt_ref, vt_ref, *rest,
                  mode, tq, tk, tt, n_main_static, q_pos0, lam_init):
    if mode == "diff":
        lam_ref, subg_ref, o_ref, m_scr, l_scr, acc_scr = rest
    else:
        o_ref, m_scr, l_scr, acc_scr = rest
    head = pl.program_id(1)
    qi = pl.program_id(2)
    q = q_ref[...]
    halves = _half_masks()
    if mode == "diff":
        qs = [jnp.where(halves[i], q, jnp.zeros_like(q)) for i in range(2)]
        slope = slope_ref[head]
    else:
        qs = [q[:, :LANES], q[:, LANES:]]
        slope = None

    m_scr[...] = jnp.full(m_scr.shape, NEG, F32)
    l_scr[...] = jnp.zeros(l_scr.shape, F32)
    acc_scr[...] = jnp.zeros(acc_scr.shape, F32)

    def update(i, s, shift, v):
        m_old = m_scr[i]
        m_new = jnp.maximum(m_old, jnp.max(s, axis=-1, keepdims=True) + shift)
        alpha = jnp.exp(m_old - m_new)
        p = jnp.exp(s - (m_new - shift))
        l_scr[i] = alpha * l_scr[i] + jnp.sum(p, axis=-1, keepdims=True)
        acc_scr[i] = alpha * acc_scr[i] + jnp.dot(p.astype(BF16), v, preferred_element_type=F32)
        m_scr[i] = m_new

    def k_of(kblk, i):
        return kblk if mode == "diff" else kblk[:, i * LANES:(i + 1) * LANES]

    nt = (((1,), (1,)), ((), ()))
    q_base = q_pos0 + qi * tq

    if mode == "diff":
        ri = lax.broadcasted_iota(jnp.int32, (tq, tk), 0)
        ci = lax.broadcasted_iota(jnp.int32, (tq, tk), 1)
        main_tile = slope * (ci - ri).astype(F32)

    def main_body(kb, carry):
        off = pl.multiple_of(kb * tk, tk)
        kblk = km_ref[pl.ds(off, tk), :].astype(BF16)
        vblk = vm_ref[pl.ds(off, tk), :].astype(BF16)
        for i in range(2):
            s = lax.dot_general(qs[i], k_of(kblk, i), nt, preferred_element_type=F32)
            if mode == "diff":
                shift = -slope * (q_base - kb * tk).astype(F32)
                update(i, s + main_tile, shift, vblk)
            else:
                update(i, s, 0.0, vblk)
        return carry

    n_main = qi if n_main_static is None else n_main_static
    lax.fori_loop(0, n_main, main_body, 0)

    ri = lax.broadcasted_iota(jnp.int32, (tq, tt), 0)
    ci = lax.broadcasted_iota(jnp.int32, (tq, tt), 1)
    valid = (ci // CHUNK) <= (ri // CHUNK)
    if mode == "diff":
        tail_tile = jnp.where(valid, -slope * jnp.abs(ri - ci).astype(F32), NEG)
    else:
        tail_tile = jnp.where(valid, 0.0, NEG).astype(F32)
    kblk = kt_ref[...].astype(BF16)
    vblk = vt_ref[...].astype(BF16)
    for i in range(2):
        s = lax.dot_general(qs[i], k_of(kblk, i), nt, preferred_element_type=F32)
        update(i, s + tail_tile, 0.0, vblk)

    o0 = acc_scr[0] / l_scr[0]
    o1 = acc_scr[1] / l_scr[1]
    if mode == "diff":
        lp = lam_ref[...]
        lam = (jnp.exp(jnp.sum(lp[0:1] * lp[1:2], axis=-1, keepdims=True))
               - jnp.exp(jnp.sum(lp[2:3] * lp[3:4], axis=-1, keepdims=True)) + lam_init)
        o = o0 - lam * o1
        o_ref[...] = _rms(o, subg_ref[...]) * (1.0 - lam_init)
    else:
        o_ref[...] = jnp.where(halves[0], o0, o1)


def flash_attention(q, k_main, v_main, k_tail, v_tail, *, mode, tq, tk, diagonal_tail, q_pos0,
                    lam_p=None, sub_g=None, lam_init=0.0):
    bt, t_q = q.shape[0], q.shape[1]
    tq = _row_tile(t_q, tq)
    qw = LANES if mode == "diff" else 2 * LANES
    groups = q.shape[2] // qw
    t_main = k_main.shape[1]
    tk = _row_tile(t_main, tk)
    if diagonal_tail:
        assert tk == tq
        tt, n_main_static = tq, None
        tail_idx = lambda b, h, i, *_: (b, i, h)
    else:
        tt, n_main_static = k_tail.shape[1], t_main // tk
        tail_idx = lambda b, h, i, *_: (b, 0, h)
    q_idx = lambda b, h, i, *_: (b, i, h)
    main_idx = lambda b, h, i, *_: (b, 0, h)
    in_specs = [pl.BlockSpec((None, tq, qw), q_idx),
                pl.BlockSpec((None, t_main, qw), main_idx),
                pl.BlockSpec((None, t_main, LANES), main_idx),
                pl.BlockSpec((None, tt, qw), tail_idx),
                pl.BlockSpec((None, tt, LANES), tail_idx)]
    args = [q, k_main, v_main, k_tail, v_tail]
    if mode == "diff":
        slopes = jnp.exp2(-8.0 * jnp.arange(1, groups + 1, dtype=F32) / groups)
        in_specs += [pl.BlockSpec(lam_p.shape, lambda b, h, i, *_: (0, 0)),
                     pl.BlockSpec((1, LANES), lambda b, h, i, *_: (0, 0))]
        args += [lam_p.astype(F32), sub_g.reshape(1, LANES).astype(F32)]
    else:
        slopes = jnp.zeros((groups,), F32)
    grid_spec = pltpu.PrefetchScalarGridSpec(
        num_scalar_prefetch=1,
        grid=(bt, groups, t_q // tq),
        in_specs=in_specs,
        out_specs=pl.BlockSpec((None, tq, LANES), q_idx),
        scratch_shapes=[pltpu.VMEM((2, tq, 1), F32), pltpu.VMEM((2, tq, 1), F32),
                        pltpu.VMEM((2, tq, LANES), F32)])
    return pl.pallas_call(
        functools.partial(_flash_kernel, mode=mode, tq=tq, tk=tk, tt=tt, n_main_static=n_main_static,
                          q_pos0=q_pos0, lam_init=lam_init),
        grid_spec=grid_spec,
        out_shape=jax.ShapeDtypeStruct((bt, t_q, groups * LANES), F32),
        compiler_params=_params(3),
    )(slopes, *args)


def _c_pre_kernel(cq_ref, ckv_ref, kra_ref, krb_ref, gq_ref, gkv_ref, wq_ref, wqr_ref,
                  cq_tab, sq_tab, ck_tab, sk_tab, q_out, lat_out, kr_out):
    qn = _rms(cq_ref[...], gq_ref[...]).astype(BF16)
    qc = jnp.dot(qn, wq_ref[...], preferred_element_type=F32)
    qr = jnp.dot(qn, wqr_ref[...], preferred_element_type=F32)
    cos, sin = cq_tab[...], sq_tab[...]
    for h in range(C_HEADS):
        sl = slice(h * LANES, (h + 1) * LANES)
        q_out[:, sl] = (qc[:, sl] * cos + qr[:, sl] * sin).astype(q_out.dtype)
    lat_out[...] = _rms(ckv_ref[...], gkv_ref[...])
    kr = kra_ref[...] * ck_tab[...] + krb_ref[...] * sk_tab[...]
    kr_out[...] = kr[:, :C_ROPE]


def c_pre(cq, ckv, kra, krb, gq, gkv, wq, wqr, tabs, t_len, tm=512):
    m = cq.shape[0]
    tm = _row_tile(t_len, tm)
    n_t = t_len // tm
    row = lambda i: (i, 0)
    fixed = lambda i: (0, 0)
    trow = lambda i: (i % n_t, 0)
    qw = wq.shape[1]
    return pl.pallas_call(
        _c_pre_kernel,
        grid=(m // tm,),
        in_specs=[pl.BlockSpec((tm, C_Q_LORA), row), pl.BlockSpec((tm, C_KV_LORA), row),
                  pl.BlockSpec((tm, LANES), row), pl.BlockSpec((tm, LANES), row),
                  pl.BlockSpec((1, C_Q_LORA), fixed), pl.BlockSpec((1, C_KV_LORA), fixed),
                  pl.BlockSpec(wq.shape, fixed), pl.BlockSpec(wqr.shape, fixed),
                  pl.BlockSpec((tm, LANES), trow), pl.BlockSpec((tm, LANES), trow),
                  pl.BlockSpec((tm, LANES), trow), pl.BlockSpec((tm, LANES), trow)],
        out_specs=[pl.BlockSpec((tm, qw), row), pl.BlockSpec((tm, C_KV_LORA), row),
                   pl.BlockSpec((tm, C_ROPE), row)],
        out_shape=[jax.ShapeDtypeStruct((m, qw), BF16), jax.ShapeDtypeStruct((m, C_KV_LORA), F32),
                   jax.ShapeDtypeStruct((m, C_ROPE), F32)],
        compiler_params=_params(1),
    )(cq, ckv, kra, krb, gq.reshape(1, -1).astype(F32), gkv.reshape(1, -1).astype(F32), wq, wqr, *tabs)


def _c_kv_kernel(lat_ref, kr_ref, wk_ref, wv_ref, place_ref, k_out, v_out):
    lat = lat_ref[...].astype(BF16)
    k = jnp.dot(lat, wk_ref[...], preferred_element_type=F32)
    k += jnp.dot(kr_ref[...].astype(BF16), place_ref[...], preferred_element_type=F32)
    k_out[...] = k.astype(k_out.dtype)
    v_out[...] = jnp.dot(lat, wv_ref[...], preferred_element_type=F32).astype(v_out.dtype)


def c_kv(lat, kr, wk, wv, place, tm=512):
    m = lat.shape[0]
    tm = _row_tile(m, tm)
    row = lambda i: (i, 0)
    fixed = lambda i: (0, 0)
    return pl.pallas_call(
        _c_kv_kernel,
        grid=(m // tm,),
        in_specs=[pl.BlockSpec((tm, C_KV_LORA), row), pl.BlockSpec((tm, C_ROPE), row),
                  pl.BlockSpec(wk.shape, fixed), pl.BlockSpec(wv.shape, fixed), pl.BlockSpec(place.shape, fixed)],
        out_specs=[pl.BlockSpec((tm, wk.shape[1]), row), pl.BlockSpec((tm, wv.shape[1]), row)],
        out_shape=[jax.ShapeDtypeStruct((m, wk.shape[1]), BF16), jax.ShapeDtypeStruct((m, wv.shape[1]), BF16)],
        compiler_params=_params(1),
    )(lat, kr, wk, wv, place)


def _prep_in_ab(w):
    n_mix = w.shape[1] - MEM_W - w.shape[0]
    n_q = n_mix // 3
    scale = jnp.concatenate([jnp.full((n_q,), HEAD_DIM ** -0.5, F32), jnp.ones((n_mix - n_q,), F32),
                             jnp.full((MEM_W,), HEAD_DIM ** -0.5, F32), jnp.ones((w.shape[0],), F32)])
    return (w * scale[None, :]).astype(BF16)


def _prep_in_c(w):
    d = w.shape[0]
    o = C_Q_LORA + C_KV_LORA
    half = C_ROPE // 2
    kr = w[:, o:o + C_ROPE]
    rot = jnp.concatenate([-kr[:, half:], kr[:, :half]], axis=1)
    pad = jnp.zeros((d, LANES - C_ROPE), F32)
    mq = w[:, o + C_ROPE:o + C_ROPE + MEM_W] * (HEAD_DIM ** -0.5)
    gate = w[:, o + C_ROPE + MEM_W:]
    return jnp.concatenate([w[:, :o], kr, pad, rot, pad, mq, gate], axis=1).astype(BF16)


def _prep_uq(w_uq):
    r = w_uq.shape[0]
    w = w_uq.reshape(r, C_HEADS, C_NOPE + C_ROPE)
    nope, rope = w[..., :C_NOPE], w[..., C_NOPE:]
    half = C_ROPE // 2
    rot = jnp.concatenate([-rope[..., half:], rope[..., :half]], axis=-1)
    pad = jnp.zeros((r, C_HEADS, LANES - C_NOPE - C_ROPE), F32)
    plain = jnp.concatenate([nope, rope, pad], axis=-1).reshape(r, C_HEADS * LANES)
    rotated = jnp.concatenate([jnp.zeros_like(nope), rot, pad], axis=-1).reshape(r, C_HEADS * LANES)
    return plain.astype(BF16), rotated.astype(BF16)


def _prep_ukv(w_ukv):
    r = w_ukv.shape[0]
    w = w_ukv.reshape(r, C_HEADS, C_NOPE + C_V)
    wk = jnp.concatenate([w[..., :C_NOPE], jnp.zeros((r, C_HEADS, LANES - C_NOPE), F32)], axis=-1)
    wv = w[..., C_NOPE:]
    eye = jnp.eye(C_ROPE, dtype=F32)
    place = jnp.concatenate([jnp.zeros((C_ROPE, C_NOPE), F32), eye,
                             jnp.zeros((C_ROPE, LANES - C_NOPE - C_ROPE), F32)], axis=1)
    place = jnp.tile(place, (1, C_HEADS))
    return (wk.reshape(r, C_HEADS * LANES).astype(BF16), wv.reshape(r, C_HEADS * C_V).astype(BF16),
            place.astype(BF16))


def _rope_tables(pos):
    half = C_ROPE // 2
    inv = jnp.exp(-math.log(ROPE_BASE) * jnp.arange(half, dtype=F32) * 2.0 / C_ROPE)
    ang = pos.astype(F32)[:, None] * inv[None, :]
    cos, sin = jnp.cos(ang), jnp.sin(ang)
    t = pos.shape[0]
    cos2 = jnp.concatenate([cos, cos], axis=1)
    sin2 = jnp.concatenate([sin, sin], axis=1)
    scale = (C_NOPE + C_ROPE) ** -0.5
    z = lambda n: jnp.zeros((t, n), F32)
    cq = jnp.concatenate([jnp.ones((t, C_NOPE), F32), cos2, z(LANES - C_NOPE - C_ROPE)], axis=1) * scale
    sq = jnp.concatenate([z(C_NOPE), sin2, z(LANES - C_NOPE - C_ROPE)], axis=1) * scale
    ck = jnp.concatenate([cos2, z(LANES - C_ROPE)], axis=1)
    sk = jnp.concatenate([sin2, z(LANES - C_ROPE)], axis=1)
    return cq, sq, ck, sk


def _band_bias(rel_bias, q_pos, k_pos):
    qp = q_pos[:, :, None]
    kp = k_pos[:, None, :]
    rel = jnp.clip(qp - kp, -A_REL_CLIP, A_REL_CLIP) + A_REL_CLIP
    bias = jnp.moveaxis(rel_bias.astype(F32)[rel], -1, 1)
    qc, kc = qp // CHUNK, kp // CHUNK
    valid = (kp >= 0) & (kc <= qc) & (kc >= qc - A_BAND_CHUNKS)
    return jnp.where(valid[:, None], bias, NEG)


A_TQ = 256
FLASH_T = 512
FAR = 1 << 24


def _trunk(x, pos0, mem_k, mem_v, past, wts):
    bt, t, d = x.shape
    m = bt * t
    depth = len(wts["w_out"])
    x2 = x.reshape(m, d)
    new_a, new_b, new_c = [], [], []
    pos = pos0 + jnp.arange(t, dtype=jnp.int32)
    seg_dt = (BF16, F32, F32, BF16, F32)
    for i in range(depth):
        kind, j = i % N_MIXERS, i // N_MIXERS
        if kind == 0:
            hd = A_HEADS * HEAD_DIM
            q, k, v, mq, gate = norm_matmul(x2, wts["norm_g"][i], wts["w_in_a"][j],
                                            (hd, hd, hd, MEM_W, d), seg_dt)
            q3, k3, v3 = (a.reshape(bt, t, hd) for a in (q, k, v))
            if past is None:
                tq = min(A_TQ, t)
                win = min(A_PAST_ROWS + tq, t)
                n_cls = min(A_PAST_ROWS // tq + 1, t // tq)
                starts = jnp.clip(jnp.arange(n_cls) * tq - A_PAST_ROWS, 0, t - win)
                q_pos = (jnp.arange(n_cls) * tq)[:, None] + jnp.arange(tq)[None, :]
                k_pos = starts[:, None] + jnp.arange(win)[None, :]
                bias = _band_bias(wts["rel_bias_a"][j], q_pos, k_pos)
                br = attn_window(q3, k3, v3, bias, tq=tq, win=win, back=A_PAST_ROWS)
                keep = min(A_PAST_ROWS, t)
                new_a.append((k3[:, t - keep:].reshape(bt, keep, A_HEADS, HEAD_DIM),
                              v3[:, t - keep:].reshape(bt, keep, A_HEADS, HEAD_DIM)))
            else:
                ck, cv = past[0][j], past[1][j]
                rows = ck.shape[1]
                n_keys = rows + t
                pad = (-n_keys) % LANES
                zeros = jnp.zeros((bt, pad, hd), F32)
                kk = jnp.concatenate([ck.reshape(bt, rows, hd), k3, zeros], axis=1)
                vv = jnp.concatenate([cv.reshape(bt, rows, hd), v3, zeros], axis=1)
                k_pos = jnp.concatenate([pos0 - rows + jnp.arange(rows, dtype=jnp.int32), pos,
                                         jnp.full((pad,), FAR, jnp.int32)])
                bias = _band_bias(wts["rel_bias_a"][j], pos[None, :], k_pos[None, :])
                br = attn_window(q3, kk, vv, bias, tq=t, win=n_keys + pad, back=0)
                new_a.append((k3.reshape(bt, t, A_HEADS, HEAD_DIM), v3.reshape(bt, t, A_HEADS, HEAD_DIM)))
        elif kind == 1:
            hd = B_HEADS * 2 * HEAD_DIM
            q, k, v, mq, gate = norm_matmul(x2, wts["norm_g"][i], wts["w_in_b"][j],
                                            (hd, hd, hd, MEM_W, d), seg_dt)
            q3, k3, v3 = (a.reshape(bt, t, hd) for a in (q, k, v))
            lam_init = 0.8 - 0.6 * math.exp(-0.3 * i)
            common = dict(mode="diff", lam_p=wts["lambda_b"][j], sub_g=wts["subln_g_b"][j], lam_init=lam_init)
            if past is None:
                br = flash_attention(q3, k3, v3, k3, v3, tq=FLASH_T, tk=min(FLASH_T, t), diagonal_tail=True,
                                     q_pos0=0, **common)
            else:
                ck, cv = past[2][j], past[3][j]
                rows = ck.shape[1]
                br = flash_attention(q3, ck.reshape(bt, rows, hd), cv.reshape(bt, rows, hd), k3, v3,
                                     tq=t, tk=FLASH_T, diagonal_tail=False, q_pos0=rows, **common)
            new_b.append((k3.reshape(bt, t, B_HEADS, 2, HEAD_DIM), v3.reshape(bt, t, B_HEADS, 2 * HEAD_DIM)))
        else:
            cq, ckv, kra, krb, mq, gate = norm_matmul(
                x2, wts["norm_g"][i], wts["w_in_c"][j],
                (C_Q_LORA, C_KV_LORA, LANES, LANES, MEM_W, d), (F32, F32, F32, F32, BF16, F32))
            wq, wqr = wts["w_uq_c"][j]
            wk, wv, place = wts["w_ukv_c"][j]
            qcat, lat, kr = c_pre(cq, ckv, kra, krb, wts["q_norm_g_c"][j], wts["kv_norm_g_c"][j],
                                  wq, wqr, _rope_tables(pos), t)
            kcat, vcat = c_kv(lat, kr, wk, wv, place)
            q3 = qcat.reshape(bt, t, -1)
            k3 = kcat.reshape(bt, t, -1)
            v3 = vcat.reshape(bt, t, -1)
            if past is None:
                br = flash_attention(q3, k3, v3, k3, v3, mode="mla", tq=FLASH_T, tk=min(FLASH_T, t),
                                     diagonal_tail=True, q_pos0=0)
            else:
                cl, cr = past[4][j], past[5][j]
                rows = cl.shape[1]
                kc_, vc_ = c_kv(cl.reshape(bt * rows, -1), cr.reshape(bt * rows, -1), wk, wv, place)
                br = flash_attention(q3, kc_.reshape(bt, rows, -1), vc_.reshape(bt, rows, -1), k3, v3,
                                     mode="mla", tq=t, tk=FLASH_T, diagonal_tail=False, q_pos0=rows)
            new_c.append((lat.reshape(bt, t, -1), kr.reshape(bt, t, -1)))
        mo = attn_window(mq.reshape(bt, t, MEM_W), mem_k[i], mem_v[i], None,
                         tq=min(FLASH_T, t), win=mem_k[i].shape[1], back=0)
        last = i == depth - 1
        x2 = gated_out(br.reshape(m, -1), mo.reshape(m, MEM_W), gate, x2, wts["w_out"][i],
                       wts["final_g"], final=last)
    return x2.reshape(bt, t, d), new_a, new_b, new_c


def kernel(x_prompt, x_sample, cache_a_k, cache_a_v, cache_b_k, cache_b_v, cache_c_lat, cache_c_rope,
           cache_mem_k, cache_mem_v, mem_prompt, norm_g, final_g, mem_norm_g, w_mem_kv, w_out, w_in_a,
           rel_bias_a, w_in_b, lambda_b, subln_g_b, w_in_c, q_norm_g_c, kv_norm_g_c, w_uq_c, w_ukv_c):
    depth = w_out.shape[0]
    bp, n_mem, d = mem_prompt.shape
    mem_heads = MEM_W // HEAD_DIM
    wts = dict(
        norm_g=norm_g, final_g=final_g, rel_bias_a=rel_bias_a, lambda_b=lambda_b, subln_g_b=subln_g_b,
        q_norm_g_c=q_norm_g_c, kv_norm_g_c=kv_norm_g_c,
        w_out=[w_out[i].astype(BF16) for i in range(depth)],
        w_in_a=[_prep_in_ab(w_in_a[j]) for j in range(w_in_a.shape[0])],
        w_in_b=[_prep_in_ab(w_in_b[j]) for j in range(w_in_b.shape[0])],
        w_in_c=[_prep_in_c(w_in_c[j]) for j in range(w_in_c.shape[0])],
        w_uq_c=[_prep_uq(w_uq_c[j]) for j in range(w_uq_c.shape[0])],
        w_ukv_c=[_prep_ukv(w_ukv_c[j]) for j in range(w_ukv_c.shape[0])],
    )
    mem2 = mem_prompt.reshape(bp * n_mem, d)
    mem_k_p, mem_v_p = [], []
    for i in range(depth):
        mk, mv = norm_matmul(mem2, mem_norm_g[i], w_mem_kv[i].astype(BF16), (MEM_W, MEM_W), (F32, F32))
        mem_k_p.append(mk.reshape(bp, n_mem, MEM_W))
        mem_v_p.append(mv.reshape(bp, n_mem, MEM_W))

    y_p, na_p, nb_p, nc_p = _trunk(x_prompt, 0, mem_k_p, mem_v_p, None, wts)

    bs = x_sample.shape[0]
    mem_k_s = [cache_mem_k[i].reshape(bs, n_mem, MEM_W) for i in range(depth)]
    mem_v_s = [cache_mem_v[i].reshape(bs, n_mem, MEM_W) for i in range(depth)]
    past = (cache_a_k, cache_a_v, cache_b_k, cache_b_v, cache_c_lat, cache_c_rope)
    y_s, na_s, nb_s, nc_s = _trunk(x_sample, cache_b_k.shape[2], mem_k_s, mem_v_s, past, wts)

    stk = lambda lst, n: jnp.stack([s[n] for s in lst])
    heads4 = lambda lst: jnp.stack([a.reshape(bp, n_mem, mem_heads, HEAD_DIM) for a in lst])
    return (y_p, y_s,
            stk(na_p, 0), stk(na_p, 1), stk(na_s, 0), stk(na_s, 1),
            stk(nb_p, 0), stk(nb_p, 1), stk(nb_s, 0), stk(nb_s, 1),
            stk(nc_p, 0), stk(nc_p, 1), stk(nc_s, 0), stk(nc_s, 1),
            heads4(mem_k_p), heads4(mem_v_p))
```

```python
import functools
import math

import jax
import jax.numpy as jnp
from jax import lax
from jax.experimental import pallas as pl
from jax.experimental.pallas import tpu as pltpu

F32 = jnp.float32
BF16 = jnp.bfloat16

LANES = 128
VMEM_LIMIT = 56 * 1024 * 1024

EPS = 1e-6
NEG = -1e30
CHUNK = 64
HEAD_DIM = 64
MEM_W = 256
A_HEADS = 12
A_PAST_ROWS = 512
A_BAND_CHUNKS = 8
A_REL_CLIP = 128
B_HEADS = 6
C_HEADS = 12
C_NOPE = 64
C_ROPE = 32
C_V = 64
C_Q_LORA = 384
C_KV_LORA = 256
ROPE_BASE = 10000.0
N_MIXERS = 3


def _params(n_grid):
    return pltpu.CompilerParams(dimension_semantics=("arbitrary",) * n_grid,
                                vmem_limit_bytes=VMEM_LIMIT)


def _row_tile(m, want):
    t = min(m, want)
    assert m % t == 0, (m, t)
    return t


def _rms(x, g):
    return x * lax.rsqrt(jnp.mean(x * x, axis=-1, keepdims=True) + EPS) * g


NT = (((1,), (1,)), ((), ()))


def _norm_matmul_kernel(x_ref, g_ref, w_ref, *rest, segs, has_t):
    h = _rms(x_ref[...], g_ref[...]).astype(BF16)
    out_refs = rest[1:] if has_t else rest
    off = n_out = 0
    for width, dtypes in segs:
        r = jnp.dot(h, w_ref[:, off:off + width], preferred_element_type=F32)
        for dt in dtypes:
            out_refs[n_out][...] = r.astype(dt)
            n_out += 1
        off += width
    if has_t:
        o_ref = out_refs[n_out]
        o_ref[...] = lax.dot_general(rest[0][...], h, NT, preferred_element_type=F32).astype(o_ref.dtype)


def norm_matmul(x, g, w, segs, wt=None, tm=512):
    m, d = x.shape
    tm = _row_tile(m, tm)
    n = w.shape[1]
    assert n == sum(wd for wd, _ in segs)
    row = lambda i: (i, 0)
    fixed = lambda i: (0, 0)
    in_specs = [pl.BlockSpec((tm, d), row), pl.BlockSpec((1, d), fixed), pl.BlockSpec((d, n), fixed)]
    args = [x, g.reshape(1, d).astype(F32), w]
    out_specs = [pl.BlockSpec((tm, wd), row) for wd, dts in segs for _ in dts]
    out_shape = [jax.ShapeDtypeStruct((m, wd), dt) for wd, dts in segs for dt in dts]
    if wt is not None:
        in_specs.append(pl.BlockSpec(wt.shape, fixed))
        args.append(wt)
        out_specs.append(pl.BlockSpec((None, wt.shape[0], tm), lambda i: (i, 0, 0)))
        out_shape.append(jax.ShapeDtypeStruct((m // tm, wt.shape[0], tm), BF16))
    return pl.pallas_call(
        functools.partial(_norm_matmul_kernel, segs=tuple(segs), has_t=wt is not None),
        grid=(m // tm,),
        in_specs=in_specs,
        out_specs=out_specs,
        out_shape=out_shape,
        compiler_params=_params(1),
    )(*args)


def _gated_out_kernel(br_ref, mo_ref, gate_ref, x_ref, w_ref, fg_ref, o_ref, *, final):
    nb = br_ref.shape[-1]
    gate = gate_ref[...]
    sg = gate * jax.nn.sigmoid(gate)
    y1 = (br_ref[...] * sg[:, :nb]).astype(BF16)
    y2 = (mo_ref[...] * sg[:, nb:]).astype(BF16)
    acc = jnp.dot(y1, w_ref[:nb, :], preferred_element_type=F32)
    acc += jnp.dot(y2, w_ref[nb:, :], preferred_element_type=F32)
    xn = x_ref[...] + acc
    o_ref[...] = _rms(xn, fg_ref[...]) if final else xn


def gated_out(br, mo, gate, x, w, final_g, final, tm=512):
    m, d = x.shape
    tm = _row_tile(m, tm)
    nb, nm, ng = br.shape[1], mo.shape[1], gate.shape[1]
    row = lambda i: (i, 0)
    fixed = lambda i: (0, 0)
    return pl.pallas_call(
        functools.partial(_gated_out_kernel, final=final),
        grid=(m // tm,),
        in_specs=[pl.BlockSpec((tm, nb), row), pl.BlockSpec((tm, nm), row), pl.BlockSpec((tm, ng), row),
                  pl.BlockSpec((tm, d), row), pl.BlockSpec((ng, d), fixed), pl.BlockSpec((1, d), fixed)],
        out_specs=pl.BlockSpec((tm, d), row),
        out_shape=jax.ShapeDtypeStruct((m, d), F32),
        compiler_params=_params(1),
    )(br, mo, gate, x, w, final_g.reshape(1, d).astype(F32))


def _half_masks():
    lane = lax.broadcasted_iota(jnp.int32, (1, LANES), 1)
    lo = lane < HEAD_DIM
    return lo, jnp.logical_not(lo)


def _attn_window_kernel(q_ref, k_ref, v_ref, *rest, tq, win, back):
    if len(rest) == 2:
        bias_ref, o_ref = rest
    else:
        bias_ref, (o_ref,) = None, rest
    tk_all = k_ref.shape[0]
    if win == tk_all:
        k = k_ref[...]
        v = v_ref[...]
    else:
        start = jnp.clip(pl.program_id(2) * tq - back, 0, tk_all - win)
        start = pl.multiple_of(start, CHUNK)
        k = k_ref[pl.ds(start, win), :]
        v = v_ref[pl.ds(start, win), :]
    k = k.astype(BF16)
    v = v.astype(BF16)
    q = q_ref[...]
    halves = _half_masks()
    outs = []
    for i in range(2):
        qm = jnp.where(halves[i], q, jnp.zeros_like(q))
        s = lax.dot_general(qm, k, (((1,), (1,)), ((), ())), preferred_element_type=F32)
        if bias_ref is not None:
            s = s + bias_ref[i]
        m = jnp.max(s, axis=-1, keepdims=True)
        p = jnp.exp(s - m)
        l = jnp.sum(p, axis=-1, keepdims=True)
        o = jnp.dot(p.astype(BF16), v, preferred_element_type=F32)
        outs.append(o / l)
    o_ref[...] = jnp.where(halves[0], outs[0], outs[1])


def attn_window(q, k, v, bias, *, tq, win, back):
    bt, t_q, gw = q.shape
    g = gw // LANES
    t_k = k.shape[1]
    tq = _row_tile(t_q, tq)
    in_specs = [pl.BlockSpec((None, tq, LANES), lambda b, h, i: (b, i, h)),
                pl.BlockSpec((None, t_k, LANES), lambda b, h, i: (b, 0, h)),
                pl.BlockSpec((None, t_k, LANES), lambda b, h, i: (b, 0, h))]
    args = [q, k, v]
    if bias is not None:
        last = bias.shape[0] - 1
        in_specs.append(pl.BlockSpec((None, 2, tq, win), lambda b, h, i: (jnp.minimum(i, last), h, 0, 0)))
        args.append(bias)
    return pl.pallas_call(
        functools.partial(_attn_window_kernel, tq=tq, win=win, back=back),
        grid=(bt, g, t_q // tq),
        in_specs=in_specs,
        out_specs=pl.BlockSpec((None, tq, LANES), lambda b, h, i: (b, i, h)),
        out_shape=jax.ShapeDtypeStruct((bt, t_q, gw), F32),
        compiler_params=_params(3),
    )(*args)


def _flash_kernel(slope_ref, q_ref, km_ref, vm_ref, kt_ref, vt_ref, *rest,
                  mode, tq, tk, tt, n_main_static, q_pos0, lam_init):
    if mode == "diff":
        lam_ref, subg_ref, o_ref, m_scr, l_scr, acc_scr = rest
    else:
        o_ref, m_scr, l_scr, acc_scr = rest
    head = pl.program_id(1)
    qi = pl.program_id(2)
    q = q_ref[...]
    halves = _half_masks()
    if mode == "diff":
        qs = [jnp.where(halves[i], q, jnp.zeros_like(q)) for i in range(2)]
        slope = slope_ref[head]
    else:
        qs = [q[:, :LANES], q[:, LANES:]]
        slope = None

    m_scr[...] = jnp.full(m_scr.shape, NEG, F32)
    l_scr[...] = jnp.zeros(l_scr.shape, F32)
    acc_scr[...] = jnp.zeros(acc_scr.shape, F32)

    def update(i, s, shift, v):
        m_old = m_scr[i]
        m_new = jnp.maximum(m_old, jnp.max(s, axis=-1, keepdims=True) + shift)
        alpha = jnp.exp(m_old - m_new)
        p = jnp.exp(s - (m_new - shift))
        l_scr[i] = alpha * l_scr[i] + jnp.sum(p, axis=-1, keepdims=True)
        acc_scr[i] = alpha * acc_scr[i] + jnp.dot(p.astype(BF16), v, preferred_element_type=F32)
        m_scr[i] = m_new

    def k_of(kblk, i):
        return kblk if mode == "diff" else kblk[:, i * LANES:(i + 1) * LANES]

    nt = (((1,), (1,)), ((), ()))
    q_base = q_pos0 + qi * tq

    if mode == "diff":
        ri = lax.broadcasted_iota(jnp.int32, (tq, tk), 0)
        ci = lax.broadcasted_iota(jnp.int32, (tq, tk), 1)
        main_tile = slope * (ci - ri).astype(F32)

    def main_body(kb, carry):
        off = pl.multiple_of(kb * tk, tk)
        kblk = km_ref[pl.ds(off, tk), :].astype(BF16)
        vblk = vm_ref[pl.ds(off, tk), :].astype(BF16)
        for i in range(2):
            s = lax.dot_general(qs[i], k_of(kblk, i), nt, preferred_element_type=F32)
            if mode == "diff":
                shift = -slope * (q_base - kb * tk).astype(F32)
                update(i, s + main_tile, shift, vblk)
            else:
                update(i, s, 0.0, vblk)
        return carry

    n_main = qi if n_main_static is None else n_main_static
    lax.fori_loop(0, n_main, main_body, 0)

    ri = lax.broadcasted_iota(jnp.int32, (tq, tt), 0)
    ci = lax.broadcasted_iota(jnp.int32, (tq, tt), 1)
    valid = (ci // CHUNK) <= (ri // CHUNK)
    if mode == "diff":
        tail_tile = jnp.where(valid, -slope * jnp.abs(ri - ci).astype(F32), NEG)
    else:
        tail_tile = jnp.where(valid, 0.0, NEG).astype(F32)
    kblk = kt_ref[...].astype(BF16)
    vblk = vt_ref[...].astype(BF16)
    for i in range(2):
        s = lax.dot_general(qs[i], k_of(kblk, i), nt, preferred_element_type=F32)
        update(i, s + tail_tile, 0.0, vblk)

    o0 = acc_scr[0] / l_scr[0]
    o1 = acc_scr[1] / l_scr[1]
    if mode == "diff":
        lp = lam_ref[...]
        lam = (jnp.exp(jnp.sum(lp[0:1] * lp[1:2], axis=-1, keepdims=True))
               - jnp.exp(jnp.sum(lp[2:3] * lp[3:4], axis=-1, keepdims=True)) + lam_init)
        o = o0 - lam * o1
        o_ref[...] = _rms(o, subg_ref[...]) * (1.0 - lam_init)
    else:
        o_ref[...] = jnp.where(halves[0], o0, o1)


def flash_attention(q, k_main, v_main, k_tail, v_tail, *, mode, tq, tk, diagonal_tail, q_pos0,
                    lam_p=None, sub_g=None, lam_init=0.0):
    bt, t_q = q.shape[0], q.shape[1]
    tq = _row_tile(t_q, tq)
    qw = LANES if mode == "diff" else 2 * LANES
    groups = q.shape[2] // qw
    t_main = k_main.shape[1]
    tk = _row_tile(t_main, tk)
    if diagonal_tail:
        assert tk == tq
        tt, n_main_static = tq, None
        tail_idx = lambda b, h, i, *_: (b, i, h)
    else:
        tt, n_main_static = k_tail.shape[1], t_main // tk
        tail_idx = lambda b, h, i, *_: (b, 0, h)
    q_idx = lambda b, h, i, *_: (b, i, h)
    main_idx = lambda b, h, i, *_: (b, 0, h)
    in_specs = [pl.BlockSpec((None, tq, qw), q_idx),
                pl.BlockSpec((None, t_main, qw), main_idx),
                pl.BlockSpec((None, t_main, LANES), main_idx),
                pl.BlockSpec((None, tt, qw), tail_idx),
                pl.BlockSpec((None, tt, LANES), tail_idx)]
    args = [q, k_main, v_main, k_tail, v_tail]
    if mode == "diff":
        slopes = jnp.exp2(-8.0 * jnp.arange(1, groups + 1, dtype=F32) / groups)
        in_specs += [pl.BlockSpec(lam_p.shape, lambda b, h, i, *_: (0, 0)),
                     pl.BlockSpec((1, LANES), lambda b, h, i, *_: (0, 0))]
        args += [lam_p.astype(F32), sub_g.reshape(1, LANES).astype(F32)]
    else:
        slopes = jnp.zeros((groups,), F32)
    grid_spec = pltpu.PrefetchScalarGridSpec(
        num_scalar_prefetch=1,
        grid=(bt, groups, t_q // tq),
        in_specs=in_specs,
        out_specs=pl.BlockSpec((None, tq, LANES), q_idx),
        scratch_shapes=[pltpu.VMEM((2, tq, 1), F32), pltpu.VMEM((2, tq, 1), F32),
                        pltpu.VMEM((2, tq, LANES), F32)])
    return pl.pallas_call(
        functools.partial(_flash_kernel, mode=mode, tq=tq, tk=tk, tt=tt, n_main_static=n_main_static,
                          q_pos0=q_pos0, lam_init=lam_init),
        grid_spec=grid_spec,
        out_shape=jax.ShapeDtypeStruct((bt, t_q, groups * LANES), F32),
        compiler_params=_params(3),
    )(slopes, *args)


def _flash_prompt_kernel(slope_ref, q_ref, k_ref, vt_ref, *rest, mode, t, lam_init):
    if mode == "diff":
        lam_ref, subg_ref, o_ref, m_scr, l_scr, acc_scr = rest
    else:
        o_ref, m_scr, l_scr, acc_scr = rest
    head = pl.program_id(1)
    qi = pl.program_id(2)
    q = q_ref[...]
    halves = _half_masks()
    if mode == "diff":
        qs = [jnp.where(halves[i], q, jnp.zeros_like(q)) for i in range(2)]
        slope = slope_ref[head]
    else:
        qs = [q[:, :LANES], q[:, LANES:]]
    v_rows = acc_scr.shape[1]

    m_scr[...] = jnp.full(m_scr.shape, NEG, F32)
    l_scr[...] = jnp.zeros(l_scr.shape, F32)
    acc_scr[...] = jnp.zeros(acc_scr.shape, F32)

    def update(i, st, shift, vt):
        m_old = m_scr[i]
        m_new = jnp.maximum(m_old, jnp.max(st, axis=0, keepdims=True) + shift)
        alpha = jnp.exp(m_old - m_new)
        p = jnp.exp(st - (m_new - shift))
        l_scr[i] = alpha * l_scr[i] + jnp.sum(p, axis=0, keepdims=True)
        acc_scr[i] = alpha * acc_scr[i] + jnp.dot(vt, p.astype(BF16), preferred_element_type=F32)
        m_scr[i] = m_new

    def block(kb, tile, shift):
        off = pl.multiple_of(kb * t, t)
        kblk = k_ref[pl.ds(off, t), :]
        vt = vt_ref[kb]
        for i in range(2):
            if mode == "diff":
                ki, vi = kblk, vt
            else:
                ki, vi = kblk[:, i * LANES:(i + 1) * LANES], vt[i * v_rows:(i + 1) * v_rows, :]
            st = lax.dot_general(ki, qs[i], NT, preferred_element_type=F32)
            update(i, st if tile is None else st + tile, shift, vi)

    kj = lax.broadcasted_iota(jnp.int32, (t, t), 0)
    qj = lax.broadcasted_iota(jnp.int32, (t, t), 1)

    if mode == "diff":
        main_tile = slope * (kj - qj).astype(F32)

        def main_body(kb, carry):
            block(kb, main_tile, -slope * ((qi - kb) * t).astype(F32))
            return carry
    else:
        def main_body(kb, carry):
            block(kb, None, 0.0)
            return carry

    lax.fori_loop(0, qi, main_body, 0)

    valid = (kj // CHUNK) <= (qj // CHUNK)
    if mode == "diff":
        diag_tile = jnp.where(valid, -slope * jnp.abs(qj - kj).astype(F32), NEG)
    else:
        diag_tile = jnp.where(valid, 0.0, NEG).astype(F32)
    block(qi, diag_tile, 0.0)

    o0 = acc_scr[0] / l_scr[0]
    o1 = acc_scr[1] / l_scr[1]
    if mode == "diff":
        lp = lam_ref[...]
        lam = (jnp.exp(jnp.sum(lp[0:1] * lp[1:2], axis=-1, keepdims=True))
               - jnp.exp(jnp.sum(lp[2:3] * lp[3:4], axis=-1, keepdims=True)) + lam_init)
        o = (o0 - lam * o1).T
        o_ref[...] = _rms(o, subg_ref[...]) * (1.0 - lam_init)
    else:
        o_ref[...] = jnp.concatenate([o0, o1], axis=0).T


def flash_prompt(q, k, vt, *, mode, t, lam_p=None, sub_g=None, lam_init=0.0):
    bt, t_all = q.shape[0], q.shape[1]
    n_blk = t_all // t
    qw = LANES if mode == "diff" else 2 * LANES
    groups = q.shape[2] // qw
    assert vt.shape == (bt * n_blk, groups * LANES, t), vt.shape
    q_idx = lambda b, h, i, *_: (b, i, h)
    in_specs = [pl.BlockSpec((None, t, qw), q_idx),
                pl.BlockSpec((None, t_all, qw), lambda b, h, i, *_: (b, 0, h)),
                pl.BlockSpec((n_blk, LANES, t), lambda b, h, i, *_: (b, h, 0))]
    args = [q, k, vt]
    if mode == "diff":
        slopes = jnp.exp2(-8.0 * jnp.arange(1, groups + 1, dtype=F32) / groups)
        in_specs += [pl.BlockSpec(lam_p.shape, lambda b, h, i, *_: (0, 0)),
                     pl.BlockSpec((1, LANES), lambda b, h, i, *_: (0, 0))]
        args += [lam_p.astype(F32), sub_g.reshape(1, LANES).astype(F32)]
        v_rows = LANES
    else:
        slopes = jnp.zeros((groups,), F32)
        v_rows = LANES // 2
    grid_spec = pltpu.PrefetchScalarGridSpec(
        num_scalar_prefetch=1,
        grid=(bt, groups, n_blk),
        in_specs=in_specs,
        out_specs=pl.BlockSpec((None, t, LANES), q_idx),
        scratch_shapes=[pltpu.VMEM((2, 1, t), F32), pltpu.VMEM((2, 1, t), F32),
                        pltpu.VMEM((2, v_rows, t), F32)])
    return pl.pallas_call(
        functools.partial(_flash_prompt_kernel, mode=mode, t=t, lam_init=lam_init),
        grid_spec=grid_spec,
        out_shape=jax.ShapeDtypeStruct((bt, t_all, groups * LANES), F32),
        compiler_params=_params(3),
    )(slopes, *args)


def _c_pre_kernel(cq_ref, ckv_ref, kra_ref, krb_ref, gq_ref, gkv_ref, wq_ref, wqr_ref,
                  cq_tab, sq_tab, ck_tab, sk_tab, q_out, lat_out, kr_out):
    qn = _rms(cq_ref[...], gq_ref[...]).astype(BF16)
    qc = jnp.dot(qn, wq_ref[...], preferred_element_type=F32)
    qr = jnp.dot(qn, wqr_ref[...], preferred_element_type=F32)
    cos, sin = cq_tab[...], sq_tab[...]
    for h in range(C_HEADS):
        sl = slice(h * LANES, (h + 1) * LANES)
        q_out[:, sl] = (qc[:, sl] * cos + qr[:, sl] * sin).astype(q_out.dtype)
    lat_out[...] = _rms(ckv_ref[...], gkv_ref[...])
    kr = kra_ref[...] * ck_tab[...] + krb_ref[...] * sk_tab[...]
    kr_out[...] = kr[:, :C_ROPE]


def c_pre(cq, ckv, kra, krb, gq, gkv, wq, wqr, tabs, t_len, tm=512):
    m = cq.shape[0]
    tm = _row_tile(t_len, tm)
    n_t = t_len // tm
    row = lambda i: (i, 0)
    fixed = lambda i: (0, 0)
    trow = lambda i: (i % n_t, 0)
    qw = wq.shape[1]
    return pl.pallas_call(
        _c_pre_kernel,
        grid=(m // tm,),
        in_specs=[pl.BlockSpec((tm, C_Q_LORA), row), pl.BlockSpec((tm, C_KV_LORA), row),
                  pl.BlockSpec((tm, LANES), row), pl.BlockSpec((tm, LANES), row),
                  pl.BlockSpec((1, C_Q_LORA), fixed), pl.BlockSpec((1, C_KV_LORA), fixed),
                  pl.BlockSpec(wq.shape, fixed), pl.BlockSpec(wqr.shape, fixed),
                  pl.BlockSpec((tm, LANES), trow), pl.BlockSpec((tm, LANES), trow),
                  pl.BlockSpec((tm, LANES), trow), pl.BlockSpec((tm, LANES), trow)],
        out_specs=[pl.BlockSpec((tm, qw), row), pl.BlockSpec((tm, C_KV_LORA), row),
                   pl.BlockSpec((tm, C_ROPE), row)],
        out_shape=[jax.ShapeDtypeStruct((m, qw), BF16), jax.ShapeDtypeStruct((m, C_KV_LORA), F32),
                   jax.ShapeDtypeStruct((m, C_ROPE), F32)],
        compiler_params=_params(1),
    )(cq, ckv, kra, krb, gq.reshape(1, -1).astype(F32), gkv.reshape(1, -1).astype(F32), wq, wqr, *tabs)


def _c_kv_kernel(lat_ref, kr_ref, wk_ref, wv_ref, place_ref, k_out, v_out, *, v_transposed):
    lat = lat_ref[...].astype(BF16)
    k = jnp.dot(lat, wk_ref[...], preferred_element_type=F32)
    k += jnp.dot(kr_ref[...].astype(BF16), place_ref[...], preferred_element_type=F32)
    k_out[...] = k.astype(k_out.dtype)
    if v_transposed:
        v = lax.dot_general(wv_ref[...], lat, NT, preferred_element_type=F32)
    else:
        v = jnp.dot(lat, wv_ref[...], preferred_element_type=F32)
    v_out[...] = v.astype(v_out.dtype)


def c_kv(lat, kr, wk, wv, place, v_transposed, tm=512):
    m = lat.shape[0]
    tm = _row_tile(m, tm)
    row = lambda i: (i, 0)
    fixed = lambda i: (0, 0)
    if v_transposed:
        n_v = wv.shape[0]
        v_spec = pl.BlockSpec((None, n_v, tm), lambda i: (i, 0, 0))
        v_shape = jax.ShapeDtypeStruct((m // tm, n_v, tm), BF16)
    else:
        n_v = wv.shape[1]
        v_spec = pl.BlockSpec((tm, n_v), row)
        v_shape = jax.ShapeDtypeStruct((m, n_v), BF16)
    return pl.pallas_call(
        functools.partial(_c_kv_kernel, v_transposed=v_transposed),
        grid=(m // tm,),
        in_specs=[pl.BlockSpec((tm, C_KV_LORA), row), pl.BlockSpec((tm, C_ROPE), row),
                  pl.BlockSpec(wk.shape, fixed), pl.BlockSpec(wv.shape, fixed), pl.BlockSpec(place.shape, fixed)],
        out_specs=[pl.BlockSpec((tm, wk.shape[1]), row), v_spec],
        out_shape=[jax.ShapeDtypeStruct((m, wk.shape[1]), BF16), v_shape],
        compiler_params=_params(1),
    )(lat, kr, wk, wv, place)


def _prep_in_ab(w):
    n_mix = w.shape[1] - MEM_W - w.shape[0]
    n_q = n_mix // 3
    scale = jnp.concatenate([jnp.full((n_q,), HEAD_DIM ** -0.5, F32), jnp.ones((n_mix - n_q,), F32),
                             jnp.full((MEM_W,), HEAD_DIM ** -0.5, F32), jnp.ones((w.shape[0],), F32)])
    return (w * scale[None, :]).astype(BF16)


def _prep_in_c(w):
    d = w.shape[0]
    o = C_Q_LORA + C_KV_LORA
    half = C_ROPE // 2
    kr = w[:, o:o + C_ROPE]
    rot = jnp.concatenate([-kr[:, half:], kr[:, :half]], axis=1)
    pad = jnp.zeros((d, LANES - C_ROPE), F32)
    mq = w[:, o + C_ROPE:o + C_ROPE + MEM_W] * (HEAD_DIM ** -0.5)
    gate = w[:, o + C_ROPE + MEM_W:]
    return jnp.concatenate([w[:, :o], kr, pad, rot, pad, mq, gate], axis=1).astype(BF16)


def _prep_uq(w_uq):
    r = w_uq.shape[0]
    w = w_uq.reshape(r, C_HEADS, C_NOPE + C_ROPE)
    nope, rope = w[..., :C_NOPE], w[..., C_NOPE:]
    half = C_ROPE // 2
    rot = jnp.concatenate([-rope[..., half:], rope[..., :half]], axis=-1)
    pad = jnp.zeros((r, C_HEADS, LANES - C_NOPE - C_ROPE), F32)
    plain = jnp.concatenate([nope, rope, pad], axis=-1).reshape(r, C_HEADS * LANES)
    rotated = jnp.concatenate([jnp.zeros_like(nope), rot, pad], axis=-1).reshape(r, C_HEADS * LANES)
    return plain.astype(BF16), rotated.astype(BF16)


def _prep_ukv(w_ukv):
    r = w_ukv.shape[0]
    w = w_ukv.reshape(r, C_HEADS, C_NOPE + C_V)
    wk = jnp.concatenate([w[..., :C_NOPE], jnp.zeros((r, C_HEADS, LANES - C_NOPE), F32)], axis=-1)
    wv = w[..., C_NOPE:]
    eye = jnp.eye(C_ROPE, dtype=F32)
    place = jnp.concatenate([jnp.zeros((C_ROPE, C_NOPE), F32), eye,
                             jnp.zeros((C_ROPE, LANES - C_NOPE - C_ROPE), F32)], axis=1)
    place = jnp.tile(place, (1, C_HEADS))
    return (wk.reshape(r, C_HEADS * LANES).astype(BF16), wv.reshape(r, C_HEADS * C_V).astype(BF16),
            place.astype(BF16))


def _rope_tables(pos):
    half = C_ROPE // 2
    inv = jnp.exp(-math.log(ROPE_BASE) * jnp.arange(half, dtype=F32) * 2.0 / C_ROPE)
    ang = pos.astype(F32)[:, None] * inv[None, :]
    cos, sin = jnp.cos(ang), jnp.sin(ang)
    t = pos.shape[0]
    cos2 = jnp.concatenate([cos, cos], axis=1)
    sin2 = jnp.concatenate([sin, sin], axis=1)
    scale = (C_NOPE + C_ROPE) ** -0.5
    z = lambda n: jnp.zeros((t, n), F32)
    cq = jnp.concatenate([jnp.ones((t, C_NOPE), F32), cos2, z(LANES - C_NOPE - C_ROPE)], axis=1) * scale
    sq = jnp.concatenate([z(C_NOPE), sin2, z(LANES - C_NOPE - C_ROPE)], axis=1) * scale
    ck = jnp.concatenate([cos2, z(LANES - C_ROPE)], axis=1)
    sk = jnp.concatenate([sin2, z(LANES - C_ROPE)], axis=1)
    return cq, sq, ck, sk


def _band_bias_kernel(pos_ref, row_ref, o_ref, *, tq, win, n_valid):
    c = pl.program_id(0)
    q0, k0 = pos_ref[0, c], pos_ref[1, c]
    wp = row_ref.shape[-1]
    rows = pltpu.roll(jnp.broadcast_to(row_ref[...], (tq, wp)), 0, 1, stride=1, stride_axis=0)
    qp = q0 + lax.broadcasted_iota(jnp.int32, (tq, win), 0)
    kj = lax.broadcasted_iota(jnp.int32, (tq, win), 1)
    kp = k0 + kj
    qc, kc = qp // CHUNK, kp // CHUNK
    valid = (kp >= 0) & (kc <= qc) & (kc >= qc - A_BAND_CHUNKS) & (kj < n_valid)
    o_ref[...] = jnp.where(valid, rows[:, :win], NEG)


def band_bias(rel_bias, q0, k0, *, tq, win, n_valid):
    n_cls = q0.shape[0]
    heads = rel_bias.shape[1]
    wp = -(-(tq + win) // LANES) * LANES
    mm = jnp.arange(wp, dtype=jnp.int32)
    mm = jnp.where(mm < win, mm, mm - wp)
    rel = jnp.clip((q0 - k0)[:, None] - mm[None, :], -A_REL_CLIP, A_REL_CLIP) + A_REL_CLIP
    rows = jnp.moveaxis(rel_bias.astype(F32)[rel], -1, 1).reshape(n_cls, heads, 1, wp)
    pos = jnp.stack([q0, k0]).astype(jnp.int32)
    grid_spec = pltpu.PrefetchScalarGridSpec(
        num_scalar_prefetch=1,
        grid=(n_cls, heads),
        in_specs=[pl.BlockSpec((None, None, 1, wp), lambda c, h, *_: (c, h, 0, 0))],
        out_specs=pl.BlockSpec((None, None, tq, win), lambda c, h, *_: (c, h, 0, 0)))
    return pl.pallas_call(
        functools.partial(_band_bias_kernel, tq=tq, win=win, n_valid=n_valid),
        grid_spec=grid_spec,
        out_shape=jax.ShapeDtypeStruct((n_cls, heads, tq, win), F32),
        compiler_params=_params(2),
    )(pos, rows)


A_TQ = 256
FLASH_T = 512
FAR = 1 << 24


def _trunk(x, pos0, mem_k, mem_v, past, wts):
    bt, t, d = x.shape
    m = bt * t
    depth = len(wts["w_out"])
    x2 = x.reshape(m, d)
    new_a, new_b, new_c = [], [], []
    pos = pos0 + jnp.arange(t, dtype=jnp.int32)
    one = lambda width, dt: (width, (dt,))
    for i in range(depth):
        kind, j = i % N_MIXERS, i // N_MIXERS
        if kind == 0:
            hd = A_HEADS * HEAD_DIM
            q, k, v, mq, gate = norm_matmul(
                x2, wts["norm_g"][i], wts["w_in_a"][j],
                (one(hd, BF16), one(hd, F32), one(hd, F32), one(MEM_W, BF16), one(d, F32)))
            q3, k3, v3 = (a.reshape(bt, t, hd) for a in (q, k, v))
            if past is None:
                tq = min(A_TQ, t)
                win = min(A_PAST_ROWS + tq, t)
                n_cls = min(A_PAST_ROWS // tq + 1, t // tq)
                q0 = jnp.arange(n_cls, dtype=jnp.int32) * tq
                k0 = jnp.clip(q0 - A_PAST_ROWS, 0, t - win)
                bias = band_bias(wts["rel_bias_a"][j], q0, k0, tq=tq, win=win, n_valid=win)
                br = attn_window(q3, k3, v3, bias, tq=tq, win=win, back=A_PAST_ROWS)
                keep = min(A_PAST_ROWS, t)
                new_a.append((k3[:, t - keep:].reshape(bt, keep, A_HEADS, HEAD_DIM),
                              v3[:, t - keep:].reshape(bt, keep, A_HEADS, HEAD_DIM)))
            else:
                ck, cv = past[0][j], past[1][j]
                rows = ck.shape[1]
                n_keys = rows + t
                pad = (-n_keys) % LANES
                zeros = jnp.zeros((bt, pad, hd), F32)
                kk = jnp.concatenate([ck.reshape(bt, rows, hd), k3, zeros], axis=1)
                vv = jnp.concatenate([cv.reshape(bt, rows, hd), v3, zeros], axis=1)
                start = jnp.full((1,), pos0, jnp.int32)
                bias = band_bias(wts["rel_bias_a"][j], start, start - rows, tq=t, win=n_keys + pad,
                                 n_valid=n_keys)
                br = attn_window(q3, kk, vv, bias, tq=t, win=n_keys + pad, back=0)
                new_a.append((k3.reshape(bt, t, A_HEADS, HEAD_DIM), v3.reshape(bt, t, A_HEADS, HEAD_DIM)))
        elif kind == 1:
            hd = B_HEADS * 2 * HEAD_DIM
            lam_init = 0.8 - 0.6 * math.exp(-0.3 * i)
            common = dict(mode="diff", lam_p=wts["lambda_b"][j], sub_g=wts["subln_g_b"][j], lam_init=lam_init)
            w_in = wts["w_in_b"][j]
            if past is None:
                q, k, k16, v, mq, gate, vt = norm_matmul(
                    x2, wts["norm_g"][i], w_in,
                    (one(hd, BF16), (hd, (F32, BF16)), one(hd, F32), one(MEM_W, BF16), one(d, F32)),
                    wt=w_in[:, 2 * hd:3 * hd].T, tm=FLASH_T)
                br = flash_prompt(q.reshape(bt, t, hd), k16.reshape(bt, t, hd), vt, t=FLASH_T, **common)
            else:
                q, k, v, mq, gate = norm_matmul(
                    x2, wts["norm_g"][i], w_in,
                    (one(hd, BF16), one(hd, F32), one(hd, F32), one(MEM_W, BF16), one(d, F32)))
                ck, cv = past[2][j], past[3][j]
                rows = ck.shape[1]
                br = flash_attention(q.reshape(bt, t, hd), ck.reshape(bt, rows, hd), cv.reshape(bt, rows, hd),
                                     k.reshape(bt, t, hd), v.reshape(bt, t, hd),
                                     tq=t, tk=FLASH_T, diagonal_tail=False, q_pos0=rows, **common)
            new_b.append((k.reshape(bt, t, B_HEADS, 2, HEAD_DIM), v.reshape(bt, t, B_HEADS, 2 * HEAD_DIM)))
        else:
            cq, ckv, kra, krb, mq, gate = norm_matmul(
                x2, wts["norm_g"][i], wts["w_in_c"][j],
                (one(C_Q_LORA, F32), one(C_KV_LORA, F32), one(LANES, F32), one(LANES, F32),
                 one(MEM_W, BF16), one(d, F32)))
            wq, wqr = wts["w_uq_c"][j]
            wk, wv, place = wts["w_ukv_c"][j]
            qcat, lat, kr = c_pre(cq, ckv, kra, krb, wts["q_norm_g_c"][j], wts["kv_norm_g_c"][j],
                                  wq, wqr, _rope_tables(pos), t)
            q3 = qcat.reshape(bt, t, -1)
            if past is None:
                kcat, vt = c_kv(lat, kr, wk, wv.T, place, v_transposed=True, tm=FLASH_T)
                br = flash_prompt(q3, kcat.reshape(bt, t, -1), vt, mode="mla", t=FLASH_T)
            else:
                kcat, vcat = c_kv(lat, kr, wk, wv, place, v_transposed=False)
                cl, cr = past[4][j], past[5][j]
                rows = cl.shape[1]
                kc_, vc_ = c_kv(cl.reshape(bt * rows, -1), cr.reshape(bt * rows, -1), wk, wv, place,
                                v_transposed=False)
                br = flash_attention(q3, kc_.reshape(bt, rows, -1), vc_.reshape(bt, rows, -1),
                                     kcat.reshape(bt, t, -1), vcat.reshape(bt, t, -1),
                                     mode="mla", tq=t, tk=FLASH_T, diagonal_tail=False, q_pos0=rows)
            new_c.append((lat.reshape(bt, t, -1), kr.reshape(bt, t, -1)))
        mo = attn_window(mq.reshape(bt, t, MEM_W), mem_k[i], mem_v[i], None,
                         tq=min(FLASH_T, t), win=mem_k[i].shape[1], back=0)
        last = i == depth - 1
        x2 = gated_out(br.reshape(m, -1), mo.reshape(m, MEM_W), gate, x2, wts["w_out"][i],
                       wts["final_g"], final=last)
    return x2.reshape(bt, t, d), new_a, new_b, new_c


def kernel(x_prompt, x_sample, cache_a_k, cache_a_v, cache_b_k, cache_b_v, cache_c_lat, cache_c_rope,
           cache_mem_k, cache_mem_v, mem_prompt, norm_g, final_g, mem_norm_g, w_mem_kv, w_out, w_in_a,
           rel_bias_a, w_in_b, lambda_b, subln_g_b, w_in_c, q_norm_g_c, kv_norm_g_c, w_uq_c, w_ukv_c):
    depth = w_out.shape[0]
    bp, n_mem, d = mem_prompt.shape
    mem_heads = MEM_W // HEAD_DIM
    wts = dict(
        norm_g=norm_g, final_g=final_g, rel_bias_a=rel_bias_a, lambda_b=lambda_b, subln_g_b=subln_g_b,
        q_norm_g_c=q_norm_g_c, kv_norm_g_c=kv_norm_g_c,
        w_out=[w_out[i].astype(BF16) for i in range(depth)],
        w_in_a=[_prep_in_ab(w_in_a[j]) for j in range(w_in_a.shape[0])],
        w_in_b=[_prep_in_ab(w_in_b[j]) for j in range(w_in_b.shape[0])],
        w_in_c=[_prep_in_c(w_in_c[j]) for j in range(w_in_c.shape[0])],
        w_uq_c=[_prep_uq(w_uq_c[j]) for j in range(w_uq_c.shape[0])],
        w_ukv_c=[_prep_ukv(w_ukv_c[j]) for j in range(w_ukv_c.shape[0])],
    )
    mem2 = mem_prompt.reshape(bp * n_mem, d)
    mem_k_p, mem_v_p = [], []
    for i in range(depth):
        mk, mv = norm_matmul(mem2, mem_norm_g[i], w_mem_kv[i].astype(BF16),
                             ((MEM_W, (F32,)), (MEM_W, (F32,))))
        mem_k_p.append(mk.reshape(bp, n_mem, MEM_W))
        mem_v_p.append(mv.reshape(bp, n_mem, MEM_W))

    y_p, na_p, nb_p, nc_p = _trunk(x_prompt, 0, mem_k_p, mem_v_p, None, wts)

    bs = x_sample.shape[0]
    mem_k_s = [cache_mem_k[i].reshape(bs, n_mem, MEM_W) for i in range(depth)]
    mem_v_s = [cache_mem_v[i].reshape(bs, n_mem, MEM_W) for i in range(depth)]
    past = (cache_a_k, cache_a_v, cache_b_k, cache_b_v, cache_c_lat, cache_c_rope)
    y_s, na_s, nb_s, nc_s = _trunk(x_sample, cache_b_k.shape[2], mem_k_s, mem_v_s, past, wts)

    stk = lambda lst, n: jnp.stack([s[n] for s in lst])
    heads4 = lambda lst: jnp.stack([a.reshape(bp, n_mem, mem_heads, HEAD_DIM) for a in lst])
    return (y_p, y_s,
            stk(na_p, 0), stk(na_p, 1), stk(na_s, 0), stk(na_s, 1),
            stk(nb_p, 0), stk(nb_p, 1), stk(nb_s, 0), stk(nb_s, 1),
            stk(nc_p, 0), stk(nc_p, 1), stk(nc_s, 0), stk(nc_s, 1),
            heads4(mem_k_p), heads4(mem_v_p))
```

```python
import functools
import math

import jax
import jax.numpy as jnp
from jax import lax
from jax.experimental import pallas as pl
from jax.experimental.pallas import tpu as pltpu

F32 = jnp.float32
BF16 = jnp.bfloat16

LANES = 128
VMEM_LIMIT = 56 * 1024 * 1024

EPS = 1e-6
NEG = -1e30
LOG2E = math.log2(math.e)
CHUNK = 64
HEAD_DIM = 64
Q_SCALE = LOG2E * HEAD_DIM ** -0.5
MEM_W = 256
A_HEADS = 12
A_PAST_ROWS = 512
A_BAND_CHUNKS = 8
A_REL_CLIP = 128
B_HEADS = 6
C_HEADS = 12
C_NOPE = 64
C_ROPE = 32
C_V = 64
C_Q_LORA = 384
C_KV_LORA = 256
ROPE_BASE = 10000.0
N_MIXERS = 3


def _params(n_grid):
    return pltpu.CompilerParams(dimension_semantics=("arbitrary",) * n_grid,
                                vmem_limit_bytes=VMEM_LIMIT)


def _row_tile(m, want):
    t = min(m, want)
    assert m % t == 0, (m, t)
    return t


def _rms(x, g):
    return x * lax.rsqrt(jnp.mean(x * x, axis=-1, keepdims=True) + EPS) * g


NT = (((1,), (1,)), ((), ()))


def _norm_matmul_kernel(x_ref, g_ref, w_ref, *rest, segs, has_t):
    h = _rms(x_ref[...], g_ref[...]).astype(BF16)
    out_refs = rest[1:] if has_t else rest
    off = n_out = 0
    for width, dtypes in segs:
        r = jnp.dot(h, w_ref[:, off:off + width], preferred_element_type=F32)
        for dt in dtypes:
            out_refs[n_out][...] = r.astype(dt)
            n_out += 1
        off += width
    if has_t:
        o_ref = out_refs[n_out]
        r = lax.dot_general(rest[0][...], h, NT, preferred_element_type=F32).astype(o_ref.dtype)
        blk = o_ref.shape[-1]
        for jb in range(o_ref.shape[0]):
            o_ref[jb] = r[:, jb * blk:(jb + 1) * blk]


def norm_matmul(x, g, w, segs, wt=None, t_blk=None, tm=512):
    m, d = x.shape
    tm = _row_tile(m, tm)
    n = w.shape[1]
    assert n == sum(wd for wd, _ in segs)
    row = lambda i: (i, 0)
    fixed = lambda i: (0, 0)
    in_specs = [pl.BlockSpec((tm, d), row), pl.BlockSpec((1, d), fixed), pl.BlockSpec((d, n), fixed)]
    args = [x, g.reshape(1, d).astype(F32), w]
    out_specs = [pl.BlockSpec((tm, wd), row) for wd, dts in segs for _ in dts]
    out_shape = [jax.ShapeDtypeStruct((m, wd), dt) for wd, dts in segs for dt in dts]
    if wt is not None:
        in_specs.append(pl.BlockSpec(wt.shape, fixed))
        args.append(wt)
        assert tm % t_blk == 0
        out_specs.append(pl.BlockSpec((tm // t_blk, wt.shape[0], t_blk), lambda i: (i, 0, 0)))
        out_shape.append(jax.ShapeDtypeStruct((m // t_blk, wt.shape[0], t_blk), BF16))
    return pl.pallas_call(
        functools.partial(_norm_matmul_kernel, segs=tuple(segs), has_t=wt is not None),
        grid=(m // tm,),
        in_specs=in_specs,
        out_specs=out_specs,
        out_shape=out_shape,
        compiler_params=_params(1),
    )(*args)


def _gated_out_kernel(br_ref, mo_ref, gate_ref, x_ref, w_ref, fg_ref, o_ref, *, final):
    nb = br_ref.shape[-1]
    gate = gate_ref[...]
    sg = gate * jax.nn.sigmoid(gate)
    y1 = (br_ref[...] * sg[:, :nb]).astype(BF16)
    y2 = (mo_ref[...] * sg[:, nb:]).astype(BF16)
    acc = jnp.dot(y1, w_ref[:nb, :], preferred_element_type=F32)
    acc += jnp.dot(y2, w_ref[nb:, :], preferred_element_type=F32)
    xn = x_ref[...] + acc
    o_ref[...] = _rms(xn, fg_ref[...]) if final else xn


def gated_out(br, mo, gate, x, w, final_g, final, tm=512):
    m, d = x.shape
    tm = _row_tile(m, tm)
    nb, nm, ng = br.shape[1], mo.shape[1], gate.shape[1]
    row = lambda i: (i, 0)
    fixed = lambda i: (0, 0)
    return pl.pallas_call(
        functools.partial(_gated_out_kernel, final=final),
        grid=(m // tm,),
        in_specs=[pl.BlockSpec((tm, nb), row), pl.BlockSpec((tm, nm), row), pl.BlockSpec((tm, ng), row),
                  pl.BlockSpec((tm, d), row), pl.BlockSpec((ng, d), fixed), pl.BlockSpec((1, d), fixed)],
        out_specs=pl.BlockSpec((tm, d), row),
        out_shape=jax.ShapeDtypeStruct((m, d), F32),
        compiler_params=_params(1),
    )(br, mo, gate, x, w, final_g.reshape(1, d).astype(F32))


def _half_masks():
    lane = lax.broadcasted_iota(jnp.int32, (1, LANES), 1)
    lo = lane < HEAD_DIM
    return lo, jnp.logical_not(lo)


def _attn_window_kernel(q_ref, k_ref, v_ref, *rest, tq, win, back):
    if len(rest) == 2:
        bias_ref, o_ref = rest
    else:
        bias_ref, (o_ref,) = None, rest
    tk_all = k_ref.shape[0]
    if win == tk_all:
        k = k_ref[...]
        v = v_ref[...]
    else:
        start = jnp.clip(pl.program_id(2) * tq - back, 0, tk_all - win)
        start = pl.multiple_of(start, CHUNK)
        k = k_ref[pl.ds(start, win), :]
        v = v_ref[pl.ds(start, win), :]
    k = k.astype(BF16)
    v = v.astype(BF16)
    q = q_ref[...]
    halves = _half_masks()
    outs = []
    for i in range(2):
        qm = jnp.where(halves[i], q, jnp.zeros_like(q))
        s = lax.dot_general(qm, k, (((1,), (1,)), ((), ())), preferred_element_type=F32)
        if bias_ref is not None:
            s = s + bias_ref[i]
        m = jnp.max(s, axis=-1, keepdims=True)
        p = jnp.exp2(s - m)
        l = jnp.sum(p, axis=-1, keepdims=True)
        o = jnp.dot(p.astype(BF16), v, preferred_element_type=F32)
        outs.append(o / l)
    o_ref[...] = jnp.where(halves[0], outs[0], outs[1])


def attn_window(q, k, v, bias, *, tq, win, back):
    bt, t_q, gw = q.shape
    g = gw // LANES
    t_k = k.shape[1]
    tq = _row_tile(t_q, tq)
    in_specs = [pl.BlockSpec((None, tq, LANES), lambda b, h, i: (b, i, h)),
                pl.BlockSpec((None, t_k, LANES), lambda b, h, i: (b, 0, h)),
                pl.BlockSpec((None, t_k, LANES), lambda b, h, i: (b, 0, h))]
    args = [q, k, v]
    if bias is not None:
        last = bias.shape[0] - 1
        in_specs.append(pl.BlockSpec((None, 2, tq, win), lambda b, h, i: (jnp.minimum(i, last), h, 0, 0)))
        args.append(bias)
    return pl.pallas_call(
        functools.partial(_attn_window_kernel, tq=tq, win=win, back=back),
        grid=(bt, g, t_q // tq),
        in_specs=in_specs,
        out_specs=pl.BlockSpec((None, tq, LANES), lambda b, h, i: (b, i, h)),
        out_shape=jax.ShapeDtypeStruct((bt, t_q, gw), F32),
        compiler_params=_params(3),
    )(*args)


def _attn_window_t_kernel(q_ref, k_ref, vt_ref, *rest, n_win, back_blocks):
    if len(rest) == 2:
        bias_ref, o_ref = rest
    else:
        bias_ref, (o_ref,) = None, rest
    n_blk, _, blk = vt_ref.shape
    if n_win == n_blk:
        b0 = 0
        k = k_ref[...]
    else:
        b0 = jnp.clip(pl.program_id(2) - back_blocks, 0, n_blk - n_win)
        k = k_ref[pl.ds(pl.multiple_of(b0 * blk, blk), n_win * blk), :]
    k = k.astype(BF16)
    q = q_ref[...]
    tq = q.shape[0]
    halves = _half_masks()
    q2 = jnp.concatenate([jnp.where(halves[i], q, jnp.zeros_like(q)) for i in range(2)], axis=0)
    st = lax.dot_general(k, q2, NT, preferred_element_type=F32)
    if bias_ref is not None:
        st = st + bias_ref[...]
    m = jnp.max(st, axis=0, keepdims=True)
    p = jnp.exp2(st - m)
    l = jnp.sum(p, axis=0, keepdims=True)
    pb = p.astype(BF16)
    o = None
    for j in range(n_win):
        oj = jnp.dot(vt_ref[b0 + j], pb[j * blk:(j + 1) * blk, :], preferred_element_type=F32)
        o = oj if o is None else o + oj
    o = o / l
    o_ref[...] = jnp.concatenate([o[:HEAD_DIM, :tq], o[HEAD_DIM:, tq:]], axis=0).T


def attn_window_t(q, k, vt, bias, *, tq, n_win, back_blocks):
    bt, t_q, gw = q.shape
    g = gw // LANES
    t_k = k.shape[1]
    blk = vt.shape[2]
    n_blk = t_k // blk
    assert vt.shape == (bt * n_blk, gw, blk), vt.shape
    tq = _row_tile(t_q, tq)
    in_specs = [pl.BlockSpec((None, tq, LANES), lambda b, h, i: (b, i, h)),
                pl.BlockSpec((None, t_k, LANES), lambda b, h, i: (b, 0, h)),
                pl.BlockSpec((n_blk, LANES, blk), lambda b, h, i: (b, h, 0))]
    args = [q, k, vt]
    if bias is not None:
        last = bias.shape[0] - 1
        in_specs.append(pl.BlockSpec((None, None, n_win * blk, 2 * tq),
                                     lambda b, h, i: (jnp.minimum(i, last), h, 0, 0)))
        args.append(bias)
    return pl.pallas_call(
        functools.partial(_attn_window_t_kernel, n_win=n_win, back_blocks=back_blocks),
        grid=(bt, g, t_q // tq),
        in_specs=in_specs,
        out_specs=pl.BlockSpec((None, tq, LANES), lambda b, h, i: (b, i, h)),
        out_shape=jax.ShapeDtypeStruct((bt, t_q, gw), F32),
        compiler_params=_params(3),
    )(*args)


def _flash_kernel(slope_ref, q_ref, km_ref, vm_ref, kt_ref, vt_ref, *rest,
                  mode, tq, tk, tt, n_main_static, q_pos0, lam_init):
    if mode == "diff":
        lam_ref, subg_ref, o_ref, m_scr, l_scr, acc_scr = rest
    else:
        o_ref, m_scr, l_scr, acc_scr = rest
    head = pl.program_id(1)
    qi = pl.program_id(2)
    q = q_ref[...]
    halves = _half_masks()
    if mode == "diff":
        qs = [jnp.where(halves[i], q, jnp.zeros_like(q)) for i in range(2)]
        slope = slope_ref[head]
    else:
        qs = [q[:, :LANES], q[:, LANES:]]
        slope = None

    m_scr[...] = jnp.full(m_scr.shape, NEG, F32)
    l_scr[...] = jnp.zeros(l_scr.shape, F32)
    acc_scr[...] = jnp.zeros(acc_scr.shape, F32)

    def update(i, s, shift, v):
        m_old = m_scr[i]
        m_new = jnp.maximum(m_old, jnp.max(s, axis=-1, keepdims=True) + shift)
        alpha = jnp.exp2(m_old - m_new)
        p = jnp.exp2(s - (m_new - shift))
        l_scr[i] = alpha * l_scr[i] + jnp.sum(p, axis=-1, keepdims=True)
        acc_scr[i] = alpha * acc_scr[i] + jnp.dot(p.astype(BF16), v, preferred_element_type=F32)
        m_scr[i] = m_new

    def k_of(kblk, i):
        return kblk if mode == "diff" else kblk[:, i * LANES:(i + 1) * LANES]

    nt = (((1,), (1,)), ((), ()))
    q_base = q_pos0 + qi * tq

    if mode == "diff":
        ri = lax.broadcasted_iota(jnp.int32, (tq, tk), 0)
        ci = lax.broadcasted_iota(jnp.int32, (tq, tk), 1)
        main_tile = slope * (ci - ri).astype(F32)

    def main_body(kb, carry):
        off = pl.multiple_of(kb * tk, tk)
        kblk = km_ref[pl.ds(off, tk), :].astype(BF16)
        vblk = vm_ref[pl.ds(off, tk), :].astype(BF16)
        for i in range(2):
            s = lax.dot_general(qs[i], k_of(kblk, i), nt, preferred_element_type=F32)
            if mode == "diff":
                shift = -slope * (q_base - kb * tk).astype(F32)
                update(i, s + main_tile, shift, vblk)
            else:
                update(i, s, 0.0, vblk)
        return carry

    n_main = qi if n_main_static is None else n_main_static
    lax.fori_loop(0, n_main, main_body, 0)

    ri = lax.broadcasted_iota(jnp.int32, (tq, tt), 0)
    ci = lax.broadcasted_iota(jnp.int32, (tq, tt), 1)
    valid = (ci // CHUNK) <= (ri // CHUNK)
    if mode == "diff":
        tail_tile = jnp.where(valid, -slope * jnp.abs(ri - ci).astype(F32), NEG)
    else:
        tail_tile = jnp.where(valid, 0.0, NEG).astype(F32)
    kblk = kt_ref[...].astype(BF16)
    vblk = vt_ref[...].astype(BF16)
    for i in range(2):
        s = lax.dot_general(qs[i], k_of(kblk, i), nt, preferred_element_type=F32)
        update(i, s + tail_tile, 0.0, vblk)

    o0 = acc_scr[0] / l_scr[0]
    o1 = acc_scr[1] / l_scr[1]
    if mode == "diff":
        lp = lam_ref[...]
        lam = (jnp.exp(jnp.sum(lp[0:1] * lp[1:2], axis=-1, keepdims=True))
               - jnp.exp(jnp.sum(lp[2:3] * lp[3:4], axis=-1, keepdims=True)) + lam_init)
        o = o0 - lam * o1
        o_ref[...] = _rms(o, subg_ref[...]) * (1.0 - lam_init)
    else:
        o_ref[...] = jnp.where(halves[0], o0, o1)


def flash_attention(q, k_main, v_main, k_tail, v_tail, *, mode, tq, tk, diagonal_tail, q_pos0,
                    lam_p=None, sub_g=None, lam_init=0.0):
    bt, t_q = q.shape[0], q.shape[1]
    tq = _row_tile(t_q, tq)
    qw = LANES if mode == "diff" else 2 * LANES
    groups = q.shape[2] // qw
    t_main = k_main.shape[1]
    tk = _row_tile(t_main, tk)
    if diagonal_tail:
        assert tk == tq
        tt, n_main_static = tq, None
        tail_idx = lambda b, h, i, *_: (b, i, h)
    else:
        tt, n_main_static = k_tail.shape[1], t_main // tk
        tail_idx = lambda b, h, i, *_: (b, 0, h)
    q_idx = lambda b, h, i, *_: (b, i, h)
    main_idx = lambda b, h, i, *_: (b, 0, h)
    in_specs = [pl.BlockSpec((None, tq, qw), q_idx),
                pl.BlockSpec((None, t_main, qw), main_idx),
                pl.BlockSpec((None, t_main, LANES), main_idx),
                pl.BlockSpec((None, tt, qw), tail_idx),
                pl.BlockSpec((None, tt, LANES), tail_idx)]
    args = [q, k_main, v_main, k_tail, v_tail]
    if mode == "diff":
        slopes = LOG2E * jnp.exp2(-8.0 * jnp.arange(1, groups + 1, dtype=F32) / groups)
        in_specs += [pl.BlockSpec(lam_p.shape, lambda b, h, i, *_: (0, 0)),
                     pl.BlockSpec((1, LANES), lambda b, h, i, *_: (0, 0))]
        args += [lam_p.astype(F32), sub_g.reshape(1, LANES).astype(F32)]
    else:
        slopes = jnp.zeros((groups,), F32)
    grid_spec = pltpu.PrefetchScalarGridSpec(
        num_scalar_prefetch=1,
        grid=(bt, groups, t_q // tq),
        in_specs=in_specs,
        out_specs=pl.BlockSpec((None, tq, LANES), q_idx),
        scratch_shapes=[pltpu.VMEM((2, tq, 1), F32), pltpu.VMEM((2, tq, 1), F32),
                        pltpu.VMEM((2, tq, LANES), F32)])
    return pl.pallas_call(
        functools.partial(_flash_kernel, mode=mode, tq=tq, tk=tk, tt=tt, n_main_static=n_main_static,
                          q_pos0=q_pos0, lam_init=lam_init),
        grid_spec=grid_spec,
        out_shape=jax.ShapeDtypeStruct((bt, t_q, groups * LANES), F32),
        compiler_params=_params(3),
    )(slopes, *args)


def _flash_prompt_kernel(slope_ref, q_ref, k_ref, vt_ref, *rest, mode, t, lam_init):
    if mode == "diff":
        lam_ref, subg_ref, o_ref, m_scr, l_scr, acc_scr, st_scr, msub_scr, alpha_scr = rest
    else:
        o_ref, m_scr, l_scr, acc_scr, st_scr, msub_scr, alpha_scr = rest
    head = pl.program_id(1)
    qi = pl.program_id(2)
    q = q_ref[...]
    halves = _half_masks()
    if mode == "diff":
        qs = [jnp.where(halves[i], q, jnp.zeros_like(q)) for i in range(2)]
        slope = slope_ref[head]
    else:
        qs = [q[:, :LANES], q[:, LANES:]]
    v_rows = acc_scr.shape[1]

    m_scr[...] = jnp.full(m_scr.shape, NEG, F32)
    l_scr[...] = jnp.zeros(l_scr.shape, F32)
    acc_scr[...] = jnp.zeros(acc_scr.shape, F32)


    def score(kb, tile, shift):
        off = pl.multiple_of(kb * t, t)
        kblk = k_ref[pl.ds(off, t), :]
        for i in range(2):
            ki = kblk if mode == "diff" else kblk[:, i * LANES:(i + 1) * LANES]
            st = lax.dot_general(ki, qs[i], NT, preferred_element_type=F32)
            if tile is not None:
                st = st + tile
            st_scr[i] = st
            m_old = m_scr[i]
            m_new = jnp.maximum(m_old, jnp.max(st, axis=0, keepdims=True) + shift)
            m_scr[i] = m_new
            msub_scr[i] = m_new - shift
            alpha_scr[i] = jnp.exp2(m_old - m_new)

    def accumulate(kb):
        vt = vt_ref[kb]
        for i in range(2):
            alpha = alpha_scr[i]
            vi = vt if mode == "diff" else vt[i * v_rows:(i + 1) * v_rows, :]
            p = jnp.exp2(st_scr[i] - msub_scr[i])
            l_scr[i] = alpha * l_scr[i] + jnp.sum(p, axis=0, keepdims=True)
            acc_scr[i] = alpha * acc_scr[i] + jnp.dot(vi, p.astype(BF16), preferred_element_type=F32)

    kj = lax.broadcasted_iota(jnp.int32, (t, t), 0)
    qj = lax.broadcasted_iota(jnp.int32, (t, t), 1)
    valid = (kj // CHUNK) <= (qj // CHUNK)
    first_is_diag = qi == 0
    if mode == "diff":
        main_tile = slope * (kj - qj).astype(F32)
        diag_tile = jnp.where(valid, -slope * jnp.abs(qj - kj).astype(F32), NEG)
        main_shift = lambda kb: -slope * ((qi - kb) * t).astype(F32)
        first_tile = jnp.where(first_is_diag, diag_tile, main_tile)
        first_shift = jnp.where(first_is_diag, 0.0, main_shift(0))
    else:
        main_tile = None
        diag_tile = jnp.where(valid, 0.0, NEG).astype(F32)
        main_shift = lambda kb: 0.0
        first_tile = jnp.where(jnp.logical_and(first_is_diag, jnp.logical_not(valid)), NEG, 0.0).astype(F32)
        first_shift = 0.0

    def main_body(kb, carry):
        accumulate(kb - 1)
        score(kb, main_tile, main_shift(kb))
        return carry

    def diag_body(kb, carry):
        accumulate(kb - 1)
        score(kb, diag_tile, 0.0)
        return carry

    score(0, first_tile, first_shift)
    lax.fori_loop(1, qi, main_body, 0)
    lax.fori_loop(qi, qi + jnp.minimum(qi, 1), diag_body, 0)
    accumulate(qi)

    o0 = acc_scr[0] / l_scr[0]
    o1 = acc_scr[1] / l_scr[1]
    if mode == "diff":
        lp = lam_ref[...]
        lam = (jnp.exp(jnp.sum(lp[0:1] * lp[1:2], axis=-1, keepdims=True))
               - jnp.exp(jnp.sum(lp[2:3] * lp[3:4], axis=-1, keepdims=True)) + lam_init)
        o = (o0 - lam * o1).T
        o_ref[...] = _rms(o, subg_ref[...]) * (1.0 - lam_init)
    else:
        o_ref[...] = jnp.concatenate([o0, o1], axis=0).T


def flash_prompt(q, k, vt, *, mode, t, lam_p=None, sub_g=None, lam_init=0.0):
    bt, t_all = q.shape[0], q.shape[1]
    n_blk = t_all // t
    qw = LANES if mode == "diff" else 2 * LANES
    groups = q.shape[2] // qw
    assert vt.shape == (bt * n_blk, groups * LANES, t), vt.shape
    q_idx = lambda b, h, i, *_: (b, i, h)
    in_specs = [pl.BlockSpec((None, t, qw), q_idx),
                pl.BlockSpec((None, t_all, qw), lambda b, h, i, *_: (b, 0, h)),
                pl.BlockSpec((n_blk, LANES, t), lambda b, h, i, *_: (b, h, 0))]
    args = [q, k, vt]
    if mode == "diff":
        slopes = LOG2E * jnp.exp2(-8.0 * jnp.arange(1, groups + 1, dtype=F32) / groups)
        in_specs += [pl.BlockSpec(lam_p.shape, lambda b, h, i, *_: (0, 0)),
                     pl.BlockSpec((1, LANES), lambda b, h, i, *_: (0, 0))]
        args += [lam_p.astype(F32), sub_g.reshape(1, LANES).astype(F32)]
        v_rows = LANES
    else:
        slopes = jnp.zeros((groups,), F32)
        v_rows = LANES // 2
    grid_spec = pltpu.PrefetchScalarGridSpec(
        num_scalar_prefetch=1,
        grid=(bt, groups, n_blk),
        in_specs=in_specs,
        out_specs=pl.BlockSpec((None, t, LANES), q_idx),
        scratch_shapes=[pltpu.VMEM((2, 1, t), F32), pltpu.VMEM((2, 1, t), F32),
                        pltpu.VMEM((2, v_rows, t), F32), pltpu.VMEM((2, t, t), F32),
                        pltpu.VMEM((2, 1, t), F32), pltpu.VMEM((2, 1, t), F32)])
    return pl.pallas_call(
        functools.partial(_flash_prompt_kernel, mode=mode, t=t, lam_init=lam_init),
        grid_spec=grid_spec,
        out_shape=jax.ShapeDtypeStruct((bt, t_all, groups * LANES), F32),
        compiler_params=_params(3),
    )(slopes, *args)


def _c_pre_kernel(cq_ref, ckv_ref, kra_ref, krb_ref, gq_ref, gkv_ref, wq_ref, wqr_ref,
                  cq_tab, sq_tab, ck_tab, sk_tab, q_out, lat_out, kr_out):
    qn = _rms(cq_ref[...], gq_ref[...]).astype(BF16)
    qc = jnp.dot(qn, wq_ref[...], preferred_element_type=F32)
    qr = jnp.dot(qn, wqr_ref[...], preferred_element_type=F32)
    cos, sin = cq_tab[...], sq_tab[...]
    for h in range(C_HEADS):
        sl = slice(h * LANES, (h + 1) * LANES)
        q_out[:, sl] = (qc[:, sl] * cos + qr[:, sl] * sin).astype(q_out.dtype)
    lat_out[...] = _rms(ckv_ref[...], gkv_ref[...])
    kr = kra_ref[...] * ck_tab[...] + krb_ref[...] * sk_tab[...]
    kr_out[...] = kr[:, :C_ROPE]


def c_pre(cq, ckv, kra, krb, gq, gkv, wq, wqr, tabs, t_len, tm=512):
    m = cq.shape[0]
    tm = _row_tile(t_len, tm)
    n_t = t_len // tm
    row = lambda i: (i, 0)
    fixed = lambda i: (0, 0)
    trow = lambda i: (i % n_t, 0)
    qw = wq.shape[1]
    return pl.pallas_call(
        _c_pre_kernel,
        grid=(m // tm,),
        in_specs=[pl.BlockSpec((tm, C_Q_LORA), row), pl.BlockSpec((tm, C_KV_LORA), row),
                  pl.BlockSpec((tm, LANES), row), pl.BlockSpec((tm, LANES), row),
                  pl.BlockSpec((1, C_Q_LORA), fixed), pl.BlockSpec((1, C_KV_LORA), fixed),
                  pl.BlockSpec(wq.shape, fixed), pl.BlockSpec(wqr.shape, fixed),
                  pl.BlockSpec((tm, LANES), trow), pl.BlockSpec((tm, LANES), trow),
                  pl.BlockSpec((tm, LANES), trow), pl.BlockSpec((tm, LANES), trow)],
        out_specs=[pl.BlockSpec((tm, qw), row), pl.BlockSpec((tm, C_KV_LORA), row),
                   pl.BlockSpec((tm, C_ROPE), row)],
        out_shape=[jax.ShapeDtypeStruct((m, qw), BF16), jax.ShapeDtypeStruct((m, C_KV_LORA), F32),
                   jax.ShapeDtypeStruct((m, C_ROPE), F32)],
        compiler_params=_params(1),
    )(cq, ckv, kra, krb, gq.reshape(1, -1).astype(F32), gkv.reshape(1, -1).astype(F32), wq, wqr, *tabs)


def _c_kv_kernel(lat_ref, kr_ref, wk_ref, wv_ref, place_ref, k_out, v_out, *, v_transposed):
    lat = lat_ref[...].astype(BF16)
    k = jnp.dot(lat, wk_ref[...], preferred_element_type=F32)
    k += jnp.dot(kr_ref[...].astype(BF16), place_ref[...], preferred_element_type=F32)
    k_out[...] = k.astype(k_out.dtype)
    if v_transposed:
        v = lax.dot_general(wv_ref[...], lat, NT, preferred_element_type=F32)
    else:
        v = jnp.dot(lat, wv_ref[...], preferred_element_type=F32)
    v_out[...] = v.astype(v_out.dtype)


def c_kv(lat, kr, wk, wv, place, v_transposed, tm=512):
    m = lat.shape[0]
    tm = _row_tile(m, tm)
    row = lambda i: (i, 0)
    fixed = lambda i: (0, 0)
    if v_transposed:
        n_v = wv.shape[0]
        v_spec = pl.BlockSpec((None, n_v, tm), lambda i: (i, 0, 0))
        v_shape = jax.ShapeDtypeStruct((m // tm, n_v, tm), BF16)
    else:
        n_v = wv.shape[1]
        v_spec = pl.BlockSpec((tm, n_v), row)
        v_shape = jax.ShapeDtypeStruct((m, n_v), BF16)
    return pl.pallas_call(
        functools.partial(_c_kv_kernel, v_transposed=v_transposed),
        grid=(m // tm,),
        in_specs=[pl.BlockSpec((tm, C_KV_LORA), row), pl.BlockSpec((tm, C_ROPE), row),
                  pl.BlockSpec(wk.shape, fixed), pl.BlockSpec(wv.shape, fixed), pl.BlockSpec(place.shape, fixed)],
        out_specs=[pl.BlockSpec((tm, wk.shape[1]), row), v_spec],
        out_shape=[jax.ShapeDtypeStruct((m, wk.shape[1]), BF16), v_shape],
        compiler_params=_params(1),
    )(lat, kr, wk, wv, place)


def _prep_in_ab(w):
    n_mix = w.shape[1] - MEM_W - w.shape[0]
    n_q = n_mix // 3
    scale = jnp.concatenate([jnp.full((n_q,), Q_SCALE, F32), jnp.ones((n_mix - n_q,), F32),
                             jnp.full((MEM_W,), Q_SCALE, F32), jnp.ones((w.shape[0],), F32)])
    return (w * scale[None, :]).astype(BF16)


def _prep_in_c(w):
    d = w.shape[0]
    o = C_Q_LORA + C_KV_LORA
    half = C_ROPE // 2
    kr = w[:, o:o + C_ROPE]
    rot = jnp.concatenate([-kr[:, half:], kr[:, :half]], axis=1)
    pad = jnp.zeros((d, LANES - C_ROPE), F32)
    mq = w[:, o + C_ROPE:o + C_ROPE + MEM_W] * (Q_SCALE)
    gate = w[:, o + C_ROPE + MEM_W:]
    return jnp.concatenate([w[:, :o], kr, pad, rot, pad, mq, gate], axis=1).astype(BF16)


def _prep_uq(w_uq):
    r = w_uq.shape[0]
    w = w_uq.reshape(r, C_HEADS, C_NOPE + C_ROPE)
    nope, rope = w[..., :C_NOPE], w[..., C_NOPE:]
    half = C_ROPE // 2
    rot = jnp.concatenate([-rope[..., half:], rope[..., :half]], axis=-1)
    pad = jnp.zeros((r, C_HEADS, LANES - C_NOPE - C_ROPE), F32)
    plain = jnp.concatenate([nope, rope, pad], axis=-1).reshape(r, C_HEADS * LANES)
    rotated = jnp.concatenate([jnp.zeros_like(nope), rot, pad], axis=-1).reshape(r, C_HEADS * LANES)
    return plain.astype(BF16), rotated.astype(BF16)


def _prep_ukv(w_ukv):
    r = w_ukv.shape[0]
    w = w_ukv.reshape(r, C_HEADS, C_NOPE + C_V)
    wk = jnp.concatenate([w[..., :C_NOPE], jnp.zeros((r, C_HEADS, LANES - C_NOPE), F32)], axis=-1)
    wv = w[..., C_NOPE:]
    eye = jnp.eye(C_ROPE, dtype=F32)
    place = jnp.concatenate([jnp.zeros((C_ROPE, C_NOPE), F32), eye,
                             jnp.zeros((C_ROPE, LANES - C_NOPE - C_ROPE), F32)], axis=1)
    place = jnp.tile(place, (1, C_HEADS))
    return (wk.reshape(r, C_HEADS * LANES).astype(BF16), wv.reshape(r, C_HEADS * C_V).astype(BF16),
            place.astype(BF16))


def _rope_tables(pos):
    half = C_ROPE // 2
    inv = jnp.exp(-math.log(ROPE_BASE) * jnp.arange(half, dtype=F32) * 2.0 / C_ROPE)
    ang = pos.astype(F32)[:, None] * inv[None, :]
    cos, sin = jnp.cos(ang), jnp.sin(ang)
    t = pos.shape[0]
    cos2 = jnp.concatenate([cos, cos], axis=1)
    sin2 = jnp.concatenate([sin, sin], axis=1)
    scale = LOG2E * (C_NOPE + C_ROPE) ** -0.5
    z = lambda n: jnp.zeros((t, n), F32)
    cq = jnp.concatenate([jnp.ones((t, C_NOPE), F32), cos2, z(LANES - C_NOPE - C_ROPE)], axis=1) * scale
    sq = jnp.concatenate([z(C_NOPE), sin2, z(LANES - C_NOPE - C_ROPE)], axis=1) * scale
    ck = jnp.concatenate([cos2, z(LANES - C_ROPE)], axis=1)
    sk = jnp.concatenate([sin2, z(LANES - C_ROPE)], axis=1)
    return cq, sq, ck, sk


def _band_bias_kernel(pos_ref, row_ref, o_ref, *, tq, win, n_valid, transposed):
    c = pl.program_id(0)
    q0, k0 = pos_ref[0, c], pos_ref[1, c]
    wp = row_ref.shape[-1]
    n_rows, n_cols = (win, tq) if transposed else (tq, win)
    rows = pltpu.roll(jnp.broadcast_to(row_ref[...], (n_rows, wp)), 0, 1, stride=1, stride_axis=0)
    ri = lax.broadcasted_iota(jnp.int32, (n_rows, n_cols), 0)
    ci = lax.broadcasted_iota(jnp.int32, (n_rows, n_cols), 1)
    qi, kj = (ci, ri) if transposed else (ri, ci)
    qp, kp = q0 + qi, k0 + kj
    qc, kc = qp // CHUNK, kp // CHUNK
    valid = (kp >= 0) & (kc <= qc) & (kc >= qc - A_BAND_CHUNKS) & (kj < n_valid)
    o_ref[...] = jnp.where(valid, rows[:, :n_cols], NEG)


def band_bias(rel_bias, q0, k0, *, tq, win, n_valid, transposed=False):
    n_cls = q0.shape[0]
    heads = rel_bias.shape[1]
    wp = -(-(tq + win) // LANES) * LANES
    mm = jnp.arange(wp, dtype=jnp.int32)
    mm = jnp.where(mm < (tq if transposed else win), mm, mm - wp)
    mm = mm if transposed else -mm
    rel = jnp.clip((q0 - k0)[:, None] + mm[None, :], -A_REL_CLIP, A_REL_CLIP) + A_REL_CLIP
    rows = jnp.moveaxis((LOG2E * rel_bias.astype(F32))[rel], -1, 1).reshape(n_cls, heads, 1, wp)
    pos = jnp.stack([q0, k0]).astype(jnp.int32)
    if transposed:
        out_spec = pl.BlockSpec((None, None, win, tq), lambda c, h, *_: (c, h // 2, 0, h % 2))
        out_shape = jax.ShapeDtypeStruct((n_cls, heads // 2, win, 2 * tq), F32)
    else:
        out_spec = pl.BlockSpec((None, None, tq, win), lambda c, h, *_: (c, h, 0, 0))
        out_shape = jax.ShapeDtypeStruct((n_cls, heads, tq, win), F32)
    grid_spec = pltpu.PrefetchScalarGridSpec(
        num_scalar_prefetch=1,
        grid=(n_cls, heads),
        in_specs=[pl.BlockSpec((None, None, 1, wp), lambda c, h, *_: (c, h, 0, 0))],
        out_specs=out_spec)
    return pl.pallas_call(
        functools.partial(_band_bias_kernel, tq=tq, win=win, n_valid=n_valid, transposed=transposed),
        grid_spec=grid_spec,
        out_shape=out_shape,
        compiler_params=_params(2),
    )(pos, rows)


A_TQ = 256
FLASH_T = 512
FAR = 1 << 24


def _trunk(x, pos0, mem_k, mem_v, past, wts):
    bt, t, d = x.shape
    m = bt * t
    depth = len(wts["w_out"])
    x2 = x.reshape(m, d)
    new_a, new_b, new_c = [], [], []
    pos = pos0 + jnp.arange(t, dtype=jnp.int32)
    one = lambda width, dt: (width, (dt,))
    for i in range(depth):
        kind, j = i % N_MIXERS, i // N_MIXERS
        if kind == 0:
            hd = A_HEADS * HEAD_DIM
            w_in = wts["w_in_a"][j]
            if past is None:
                tq = min(A_TQ, t)
                q, k, k16, v, mq, gate, vt = norm_matmul(
                    x2, wts["norm_g"][i], w_in,
                    (one(hd, BF16), (hd, (F32, BF16)), one(hd, F32), one(MEM_W, BF16), one(d, F32)),
                    wt=w_in[:, 2 * hd:3 * hd].T, t_blk=tq)
                k3, v3 = k.reshape(bt, t, hd), v.reshape(bt, t, hd)
                back_blocks = A_PAST_ROWS // tq
                n_win = min(back_blocks + 1, t // tq)
                win = n_win * tq
                n_cls = n_win
                q0 = jnp.arange(n_cls, dtype=jnp.int32) * tq
                k0 = jnp.clip(q0 - A_PAST_ROWS, 0, t - win)
                bias = band_bias(wts["rel_bias_a"][j], q0, k0, tq=tq, win=win, n_valid=win, transposed=True)
                br = attn_window_t(q.reshape(bt, t, hd), k16.reshape(bt, t, hd), vt, bias,
                                   tq=tq, n_win=n_win, back_blocks=back_blocks)
                keep = min(A_PAST_ROWS, t)
                new_a.append((k3[:, t - keep:].reshape(bt, keep, A_HEADS, HEAD_DIM),
                              v3[:, t - keep:].reshape(bt, keep, A_HEADS, HEAD_DIM)))
            else:
                q, k, v, mq, gate = norm_matmul(
                    x2, wts["norm_g"][i], w_in,
                    (one(hd, BF16), one(hd, F32), one(hd, F32), one(MEM_W, BF16), one(d, F32)))
                q3, k3, v3 = (a.reshape(bt, t, hd) for a in (q, k, v))
                ck, cv = past[0][j], past[1][j]
                rows = ck.shape[1]
                n_keys = rows + t
                pad = (-n_keys) % LANES
                zeros = jnp.zeros((bt, pad, hd), F32)
                kk = jnp.concatenate([ck.reshape(bt, rows, hd), k3, zeros], axis=1)
                vv = jnp.concatenate([cv.reshape(bt, rows, hd), v3, zeros], axis=1)
                start = jnp.full((1,), pos0, jnp.int32)
                bias = band_bias(wts["rel_bias_a"][j], start, start - rows, tq=t, win=n_keys + pad,
                                 n_valid=n_keys)
                br = attn_window(q3, kk, vv, bias, tq=t, win=n_keys + pad, back=0)
                new_a.append((k3.reshape(bt, t, A_HEADS, HEAD_DIM), v3.reshape(bt, t, A_HEADS, HEAD_DIM)))
        elif kind == 1:
            hd = B_HEADS * 2 * HEAD_DIM
            lam_init = 0.8 - 0.6 * math.exp(-0.3 * i)
            common = dict(mode="diff", lam_p=wts["lambda_b"][j], sub_g=wts["subln_g_b"][j], lam_init=lam_init)
            w_in = wts["w_in_b"][j]
            if past is None:
                q, k, k16, v, mq, gate, vt = norm_matmul(
                    x2, wts["norm_g"][i], w_in,
                    (one(hd, BF16), (hd, (F32, BF16)), one(hd, F32), one(MEM_W, BF16), one(d, F32)),
                    wt=w_in[:, 2 * hd:3 * hd].T, t_blk=FLASH_T, tm=FLASH_T)
                br = flash_prompt(q.reshape(bt, t, hd), k16.reshape(bt, t, hd), vt, t=FLASH_T, **common)
            else:
                q, k, v, mq, gate = norm_matmul(
                    x2, wts["norm_g"][i], w_in,
                    (one(hd, BF16), one(hd, F32), one(hd, F32), one(MEM_W, BF16), one(d, F32)))
                ck, cv = past[2][j], past[3][j]
                rows = ck.shape[1]
                br = flash_attention(q.reshape(bt, t, hd), ck.reshape(bt, rows, hd), cv.reshape(bt, rows, hd),
                                     k.reshape(bt, t, hd), v.reshape(bt, t, hd),
                                     tq=t, tk=FLASH_T, diagonal_tail=False, q_pos0=rows, **common)
            new_b.append((k.reshape(bt, t, B_HEADS, 2, HEAD_DIM), v.reshape(bt, t, B_HEADS, 2 * HEAD_DIM)))
        else:
            cq, ckv, kra, krb, mq, gate = norm_matmul(
                x2, wts["norm_g"][i], wts["w_in_c"][j],
                (one(C_Q_LORA, F32), one(C_KV_LORA, F32), one(LANES, F32), one(LANES, F32),
                 one(MEM_W, BF16), one(d, F32)))
            wq, wqr = wts["w_uq_c"][j]
            wk, wv, place = wts["w_ukv_c"][j]
            qcat, lat, kr = c_pre(cq, ckv, kra, krb, wts["q_norm_g_c"][j], wts["kv_norm_g_c"][j],
                                  wq, wqr, _rope_tables(pos), t)
            q3 = qcat.reshape(bt, t, -1)
            if past is None:
                kcat, vt = c_kv(lat, kr, wk, wv.T, place, v_transposed=True, tm=FLASH_T)
                br = flash_prompt(q3, kcat.reshape(bt, t, -1), vt, mode="mla", t=FLASH_T)
            else:
                kcat, vcat = c_kv(lat, kr, wk, wv, place, v_transposed=False)
                cl, cr = past[4][j], past[5][j]
                rows = cl.shape[1]
                kc_, vc_ = c_kv(cl.reshape(bt * rows, -1), cr.reshape(bt * rows, -1), wk, wv, place,
                                v_transposed=False)
                br = flash_attention(q3, kc_.reshape(bt, rows, -1), vc_.reshape(bt, rows, -1),
                                     kcat.reshape(bt, t, -1), vcat.reshape(bt, t, -1),
                                     mode="mla", tq=t, tk=FLASH_T, diagonal_tail=False, q_pos0=rows)
            new_c.append((lat.reshape(bt, t, -1), kr.reshape(bt, t, -1)))
        if past is None:
            mo = attn_window_t(mq.reshape(bt, t, MEM_W), mem_k[i], mem_v[i], None,
                               tq=min(FLASH_T, t), n_win=1, back_blocks=0)
        else:
            mo = attn_window(mq.reshape(bt, t, MEM_W), mem_k[i], mem_v[i], None,
                             tq=t, win=mem_k[i].shape[1], back=0)
        last = i == depth - 1
        x2 = gated_out(br.reshape(m, -1), mo.reshape(m, MEM_W), gate, x2, wts["w_out"][i],
                       wts["final_g"], final=last)
    return x2.reshape(bt, t, d), new_a, new_b, new_c


def kernel(x_prompt, x_sample, cache_a_k, cache_a_v, cache_b_k, cache_b_v, cache_c_lat, cache_c_rope,
           cache_mem_k, cache_mem_v, mem_prompt, norm_g, final_g, mem_norm_g, w_mem_kv, w_out, w_in_a,
           rel_bias_a, w_in_b, lambda_b, subln_g_b, w_in_c, q_norm_g_c, kv_norm_g_c, w_uq_c, w_ukv_c):
    depth = w_out.shape[0]
    bp, n_mem, d = mem_prompt.shape
    mem_heads = MEM_W // HEAD_DIM
    wts = dict(
        norm_g=norm_g, final_g=final_g, rel_bias_a=rel_bias_a, lambda_b=lambda_b, subln_g_b=subln_g_b,
        q_norm_g_c=q_norm_g_c, kv_norm_g_c=kv_norm_g_c,
        w_out=[w_out[i].astype(BF16) for i in range(depth)],
        w_in_a=[_prep_in_ab(w_in_a[j]) for j in range(w_in_a.shape[0])],
        w_in_b=[_prep_in_ab(w_in_b[j]) for j in range(w_in_b.shape[0])],
        w_in_c=[_prep_in_c(w_in_c[j]) for j in range(w_in_c.shape[0])],
        w_uq_c=[_prep_uq(w_uq_c[j]) for j in range(w_uq_c.shape[0])],
        w_ukv_c=[_prep_ukv(w_ukv_c[j]) for j in range(w_ukv_c.shape[0])],
    )
    mem2 = mem_prompt.reshape(bp * n_mem, d)
    mem_k_p, mem_v_p, mem_vt_p = [], [], []
    for i in range(depth):
        w_kv = w_mem_kv[i].astype(BF16)
        mk, mv, mvt = norm_matmul(mem2, mem_norm_g[i], w_kv, ((MEM_W, (F32,)), (MEM_W, (F32,))),
                                  wt=w_kv[:, MEM_W:].T, t_blk=n_mem, tm=n_mem)
        mem_k_p.append(mk.reshape(bp, n_mem, MEM_W))
        mem_v_p.append(mv.reshape(bp, n_mem, MEM_W))
        mem_vt_p.append(mvt)

    y_p, na_p, nb_p, nc_p = _trunk(x_prompt, 0, mem_k_p, mem_vt_p, None, wts)

    bs = x_sample.shape[0]
    mem_k_s = [cache_mem_k[i].reshape(bs, n_mem, MEM_W) for i in range(depth)]
    mem_v_s = [cache_mem_v[i].reshape(bs, n_mem, MEM_W) for i in range(depth)]
    past = (cache_a_k, cache_a_v, cache_b_k, cache_b_v, cache_c_lat, cache_c_rope)
    y_s, na_s, nb_s, nc_s = _trunk(x_sample, cache_b_k.shape[2], mem_k_s, mem_v_s, past, wts)

    stk = lambda lst, n: jnp.stack([s[n] for s in lst])
    heads4 = lambda lst: jnp.stack([a.reshape(bp, n_mem, mem_heads, HEAD_DIM) for a in lst])
    return (y_p, y_s,
            stk(na_p, 0), stk(na_p, 1), stk(na_s, 0), stk(na_s, 1),
            stk(nb_p, 0), stk(nb_p, 1), stk(nb_s, 0), stk(nb_s, 1),
            stk(nc_p, 0), stk(nc_p, 1), stk(nc_s, 0), stk(nc_s, 1),
            heads4(mem_k_p), heads4(mem_v_p))
```

```python
import functools
import math

import jax
import jax.numpy as jnp
from jax import lax
from jax.experimental import pallas as pl
from jax.experimental.pallas import tpu as pltpu

F32 = jnp.float32
BF16 = jnp.bfloat16

LANES = 128
VMEM_LIMIT = 56 * 1024 * 1024

EPS = 1e-6
NEG = -1e30
LOG2E = math.log2(math.e)
CHUNK = 64
HEAD_DIM = 64
Q_SCALE = LOG2E * HEAD_DIM ** -0.5
MEM_W = 256
A_HEADS = 12
A_PAST_ROWS = 512
A_BAND_CHUNKS = 8
A_REL_CLIP = 128
B_HEADS = 6
C_HEADS = 12
C_NOPE = 64
C_ROPE = 32
C_V = 64
C_Q_LORA = 384
C_KV_LORA = 256
ROPE_BASE = 10000.0
N_MIXERS = 3


def _params(n_grid):
    return pltpu.CompilerParams(dimension_semantics=("arbitrary",) * n_grid,
                                vmem_limit_bytes=VMEM_LIMIT)


def _row_tile(m, want):
    t = min(m, want)
    assert m % t == 0, (m, t)
    return t


def _rms(x, g):
    return x * lax.rsqrt(jnp.mean(x * x, axis=-1, keepdims=True) + EPS) * g


NT = (((1,), (1,)), ((), ()))


def _norm_matmul_kernel(x_ref, g_ref, w_ref, *rest, segs, has_t):
    h = _rms(x_ref[...], g_ref[...]).astype(BF16)
    out_refs = rest[1:] if has_t else rest
    off = n_out = 0
    for width, dtypes in segs:
        r = jnp.dot(h, w_ref[:, off:off + width], preferred_element_type=F32)
        for dt in dtypes:
            out_refs[n_out][...] = r.astype(dt)
            n_out += 1
        off += width
    if has_t:
        o_ref = out_refs[n_out]
        r = lax.dot_general(rest[0][...], h, NT, preferred_element_type=F32).astype(o_ref.dtype)
        blk = o_ref.shape[-1]
        for jb in range(o_ref.shape[0]):
            o_ref[jb] = r[:, jb * blk:(jb + 1) * blk]


def norm_matmul(x, g, w, segs, wt=None, t_blk=None, tm=512):
    m, d = x.shape
    tm = _row_tile(m, tm)
    n = w.shape[1]
    assert n == sum(wd for wd, _ in segs)
    row = lambda i: (i, 0)
    fixed = lambda i: (0, 0)
    in_specs = [pl.BlockSpec((tm, d), row), pl.BlockSpec((1, d), fixed), pl.BlockSpec((d, n), fixed)]
    args = [x, g.reshape(1, d).astype(F32), w]
    out_specs = [pl.BlockSpec((tm, wd), row) for wd, dts in segs for _ in dts]
    out_shape = [jax.ShapeDtypeStruct((m, wd), dt) for wd, dts in segs for dt in dts]
    if wt is not None:
        in_specs.append(pl.BlockSpec(wt.shape, fixed))
        args.append(wt)
        assert tm % t_blk == 0
        out_specs.append(pl.BlockSpec((tm // t_blk, wt.shape[0], t_blk), lambda i: (i, 0, 0)))
        out_shape.append(jax.ShapeDtypeStruct((m // t_blk, wt.shape[0], t_blk), BF16))
    return pl.pallas_call(
        functools.partial(_norm_matmul_kernel, segs=tuple(segs), has_t=wt is not None),
        grid=(m // tm,),
        in_specs=in_specs,
        out_specs=out_specs,
        out_shape=out_shape,
        compiler_params=_params(1),
    )(*args)


def _gated_out_kernel(br_ref, mo_ref, gate_ref, x_ref, w_ref, fg_ref, o_ref, *, final):
    nb = br_ref.shape[-1]
    gate = gate_ref[...]
    sg = gate * jax.nn.sigmoid(gate)
    y1 = (br_ref[...] * sg[:, :nb]).astype(BF16)
    y2 = (mo_ref[...] * sg[:, nb:]).astype(BF16)
    acc = jnp.dot(y1, w_ref[:nb, :], preferred_element_type=F32)
    acc += jnp.dot(y2, w_ref[nb:, :], preferred_element_type=F32)
    xn = x_ref[...] + acc
    o_ref[...] = _rms(xn, fg_ref[...]) if final else xn


def gated_out(br, mo, gate, x, w, final_g, final, tm=512):
    m, d = x.shape
    tm = _row_tile(m, tm)
    nb, nm, ng = br.shape[1], mo.shape[1], gate.shape[1]
    row = lambda i: (i, 0)
    fixed = lambda i: (0, 0)
    return pl.pallas_call(
        functools.partial(_gated_out_kernel, final=final),
        grid=(m // tm,),
        in_specs=[pl.BlockSpec((tm, nb), row), pl.BlockSpec((tm, nm), row), pl.BlockSpec((tm, ng), row),
                  pl.BlockSpec((tm, d), row), pl.BlockSpec((ng, d), fixed), pl.BlockSpec((1, d), fixed)],
        out_specs=pl.BlockSpec((tm, d), row),
        out_shape=jax.ShapeDtypeStruct((m, d), F32),
        compiler_params=_params(1),
    )(br, mo, gate, x, w, final_g.reshape(1, d).astype(F32))


def _half_masks():
    lane = lax.broadcasted_iota(jnp.int32, (1, LANES), 1)
    lo = lane < HEAD_DIM
    return lo, jnp.logical_not(lo)


def _attn_window_kernel(q_ref, k_ref, v_ref, *rest, tq, win, back):
    if len(rest) == 2:
        bias_ref, o_ref = rest
    else:
        bias_ref, (o_ref,) = None, rest
    tk_all = k_ref.shape[0]
    if win == tk_all:
        k = k_ref[...]
        v = v_ref[...]
    else:
        start = jnp.clip(pl.program_id(2) * tq - back, 0, tk_all - win)
        start = pl.multiple_of(start, CHUNK)
        k = k_ref[pl.ds(start, win), :]
        v = v_ref[pl.ds(start, win), :]
    k = k.astype(BF16)
    v = v.astype(BF16)
    q = q_ref[...]
    halves = _half_masks()
    outs = []
    for i in range(2):
        qm = jnp.where(halves[i], q, jnp.zeros_like(q))
        s = lax.dot_general(qm, k, (((1,), (1,)), ((), ())), preferred_element_type=F32)
        if bias_ref is not None:
            s = s + bias_ref[i]
        m = jnp.max(s, axis=-1, keepdims=True)
        p = jnp.exp2(s - m)
        l = jnp.sum(p, axis=-1, keepdims=True)
        o = jnp.dot(p.astype(BF16), v, preferred_element_type=F32)
        outs.append(o / l)
    o_ref[...] = jnp.where(halves[0], outs[0], outs[1])


def attn_window(q, k, v, bias, *, tq, win, back):
    bt, t_q, gw = q.shape
    g = gw // LANES
    t_k = k.shape[1]
    tq = _row_tile(t_q, tq)
    in_specs = [pl.BlockSpec((None, tq, LANES), lambda b, h, i: (b, i, h)),
                pl.BlockSpec((None, t_k, LANES), lambda b, h, i: (b, 0, h)),
                pl.BlockSpec((None, t_k, LANES), lambda b, h, i: (b, 0, h))]
    args = [q, k, v]
    if bias is not None:
        last = bias.shape[0] - 1
        in_specs.append(pl.BlockSpec((None, 2, tq, win), lambda b, h, i: (jnp.minimum(i, last), h, 0, 0)))
        args.append(bias)
    return pl.pallas_call(
        functools.partial(_attn_window_kernel, tq=tq, win=win, back=back),
        grid=(bt, g, t_q // tq),
        in_specs=in_specs,
        out_specs=pl.BlockSpec((None, tq, LANES), lambda b, h, i: (b, i, h)),
        out_shape=jax.ShapeDtypeStruct((bt, t_q, gw), F32),
        compiler_params=_params(3),
    )(*args)


def _attn_window_t_kernel(q_ref, k_ref, vt_ref, *rest, tq, n_sub, n_win, back_blocks):
    if len(rest) == 2:
        bias_ref, o_ref = rest
    else:
        bias_ref, (o_ref,) = None, rest
    n_blk, _, blk = vt_ref.shape
    halves = _half_masks()

    def score(sub):
        i = pl.program_id(2) * n_sub + sub
        if n_win == n_blk:
            b0 = 0
            k = k_ref[...]
        else:
            b0 = jnp.clip(i - back_blocks, 0, n_blk - n_win)
            k = k_ref[pl.ds(pl.multiple_of(b0 * blk, blk), n_win * blk), :]
        q = q_ref[sub * tq:(sub + 1) * tq, :]
        q2 = jnp.concatenate([jnp.where(halves[h], q, jnp.zeros_like(q)) for h in range(2)], axis=0)
        st = lax.dot_general(k.astype(BF16), q2, NT, preferred_element_type=F32)
        if bias_ref is not None:
            st = st + bias_ref[jnp.minimum(i, bias_ref.shape[0] - 1)]
        return b0, st, jnp.max(st, axis=0, keepdims=True)

    def accumulate(sub, b0, st, m):
        p = jnp.exp2(st - m)
        l = jnp.sum(p, axis=0, keepdims=True)
        pb = p.astype(BF16)
        o = None
        for j in range(n_win):
            oj = jnp.dot(vt_ref[b0 + j], pb[j * blk:(j + 1) * blk, :], preferred_element_type=F32)
            o = oj if o is None else o + oj
        o = o / l
        o_ref[sub * tq:(sub + 1) * tq, :] = jnp.concatenate([o[:HEAD_DIM, :tq], o[HEAD_DIM:, tq:]], axis=0).T

    staged = [score(sub) for sub in range(n_sub)]
    for sub in range(n_sub):
        accumulate(sub, *staged[sub])


def attn_window_t(q, k, vt, bias, *, tq, n_sub, n_win, back_blocks):
    bt, t_q, gw = q.shape
    g = gw // LANES
    t_k = k.shape[1]
    blk = vt.shape[2]
    n_blk = t_k // blk
    assert vt.shape == (bt * n_blk, gw, blk), vt.shape
    rows = n_sub * tq
    assert t_q % rows == 0
    in_specs = [pl.BlockSpec((None, rows, LANES), lambda b, h, i: (b, i, h)),
                pl.BlockSpec((None, t_k, LANES), lambda b, h, i: (b, 0, h)),
                pl.BlockSpec((n_blk, LANES, blk), lambda b, h, i: (b, h, 0))]
    args = [q, k, vt]
    if bias is not None:
        in_specs.append(pl.BlockSpec((bias.shape[0], None, n_win * blk, 2 * tq), lambda b, h, i: (0, h, 0, 0)))
        args.append(bias)
    return pl.pallas_call(
        functools.partial(_attn_window_t_kernel, tq=tq, n_sub=n_sub, n_win=n_win, back_blocks=back_blocks),
        grid=(bt, g, t_q // rows),
        in_specs=in_specs,
        out_specs=pl.BlockSpec((None, rows, LANES), lambda b, h, i: (b, i, h)),
        out_shape=jax.ShapeDtypeStruct((bt, t_q, gw), F32),
        compiler_params=_params(3),
    )(*args)


def _flash_kernel(slope_ref, q_ref, km_ref, vm_ref, kt_ref, vt_ref, *rest,
                  mode, tq, tk, tt, n_main_static, q_pos0, lam_init):
    if mode == "diff":
        lam_ref, subg_ref, o_ref, m_scr, l_scr, acc_scr = rest
    else:
        o_ref, m_scr, l_scr, acc_scr = rest
    head = pl.program_id(1)
    qi = pl.program_id(2)
    q = q_ref[...]
    halves = _half_masks()
    if mode == "diff":
        qs = [jnp.where(halves[i], q, jnp.zeros_like(q)) for i in range(2)]
        slope = slope_ref[head]
    else:
        qs = [q[:, :LANES], q[:, LANES:]]
        slope = None

    m_scr[...] = jnp.full(m_scr.shape, NEG, F32)
    l_scr[...] = jnp.zeros(l_scr.shape, F32)
    acc_scr[...] = jnp.zeros(acc_scr.shape, F32)

    def update(i, s, shift, v):
        m_old = m_scr[i]
        m_new = jnp.maximum(m_old, jnp.max(s, axis=-1, keepdims=True) + shift)
        alpha = jnp.exp2(m_old - m_new)
        p = jnp.exp2(s - (m_new - shift))
        l_scr[i] = alpha * l_scr[i] + jnp.sum(p, axis=-1, keepdims=True)
        acc_scr[i] = alpha * acc_scr[i] + jnp.dot(p.astype(BF16), v, preferred_element_type=F32)
        m_scr[i] = m_new

    def k_of(kblk, i):
        return kblk if mode == "diff" else kblk[:, i * LANES:(i + 1) * LANES]

    nt = (((1,), (1,)), ((), ()))
    q_base = q_pos0 + qi * tq

    if mode == "diff":
        ri = lax.broadcasted_iota(jnp.int32, (tq, tk), 0)
        ci = lax.broadcasted_iota(jnp.int32, (tq, tk), 1)
        main_tile = slope * (ci - ri).astype(F32)

    def main_body(kb, carry):
        off = pl.multiple_of(kb * tk, tk)
        kblk = km_ref[pl.ds(off, tk), :].astype(BF16)
        vblk = vm_ref[pl.ds(off, tk), :].astype(BF16)
        for i in range(2):
            s = lax.dot_general(qs[i], k_of(kblk, i), nt, preferred_element_type=F32)
            if mode == "diff":
                shift = -slope * (q_base - kb * tk).astype(F32)
                update(i, s + main_tile, shift, vblk)
            else:
                update(i, s, 0.0, vblk)
        return carry

    n_main = qi if n_main_static is None else n_main_static
    lax.fori_loop(0, n_main, main_body, 0)

    ri = lax.broadcasted_iota(jnp.int32, (tq, tt), 0)
    ci = lax.broadcasted_iota(jnp.int32, (tq, tt), 1)
    valid = (ci // CHUNK) <= (ri // CHUNK)
    if mode == "diff":
        tail_tile = jnp.where(valid, -slope * jnp.abs(ri - ci).astype(F32), NEG)
    else:
        tail_tile = jnp.where(valid, 0.0, NEG).astype(F32)
    kblk = kt_ref[...].astype(BF16)
    vblk = vt_ref[...].astype(BF16)
    for i in range(2):
        s = lax.dot_general(qs[i], k_of(kblk, i), nt, preferred_element_type=F32)
        update(i, s + tail_tile, 0.0, vblk)

    o0 = acc_scr[0] / l_scr[0]
    o1 = acc_scr[1] / l_scr[1]
    if mode == "diff":
        lp = lam_ref[...]
        lam = (jnp.exp(jnp.sum(lp[0:1] * lp[1:2], axis=-1, keepdims=True))
               - jnp.exp(jnp.sum(lp[2:3] * lp[3:4], axis=-1, keepdims=True)) + lam_init)
        o = o0 - lam * o1
        o_ref[...] = _rms(o, subg_ref[...]) * (1.0 - lam_init)
    else:
        o_ref[...] = jnp.where(halves[0], o0, o1)


def flash_attention(q, k_main, v_main, k_tail, v_tail, *, mode, tq, tk, diagonal_tail, q_pos0,
                    lam_p=None, sub_g=None, lam_init=0.0):
    bt, t_q = q.shape[0], q.shape[1]
    tq = _row_tile(t_q, tq)
    qw = LANES if mode == "diff" else 2 * LANES
    groups = q.shape[2] // qw
    t_main = k_main.shape[1]
    tk = _row_tile(t_main, tk)
    if diagonal_tail:
        assert tk == tq
        tt, n_main_static = tq, None
        tail_idx = lambda b, h, i, *_: (b, i, h)
    else:
        tt, n_main_static = k_tail.shape[1], t_main // tk
        tail_idx = lambda b, h, i, *_: (b, 0, h)
    q_idx = lambda b, h, i, *_: (b, i, h)
    main_idx = lambda b, h, i, *_: (b, 0, h)
    in_specs = [pl.BlockSpec((None, tq, qw), q_idx),
                pl.BlockSpec((None, t_main, qw), main_idx),
                pl.BlockSpec((None, t_main, LANES), main_idx),
                pl.BlockSpec((None, tt, qw), tail_idx),
                pl.BlockSpec((None, tt, LANES), tail_idx)]
    args = [q, k_main, v_main, k_tail, v_tail]
    if mode == "diff":
        slopes = LOG2E * jnp.exp2(-8.0 * jnp.arange(1, groups + 1, dtype=F32) / groups)
        in_specs += [pl.BlockSpec(lam_p.shape, lambda b, h, i, *_: (0, 0)),
                     pl.BlockSpec((1, LANES), lambda b, h, i, *_: (0, 0))]
        args += [lam_p.astype(F32), sub_g.reshape(1, LANES).astype(F32)]
    else:
        slopes = jnp.zeros((groups,), F32)
    grid_spec = pltpu.PrefetchScalarGridSpec(
        num_scalar_prefetch=1,
        grid=(bt, groups, t_q // tq),
        in_specs=in_specs,
        out_specs=pl.BlockSpec((None, tq, LANES), q_idx),
        scratch_shapes=[pltpu.VMEM((2, tq, 1), F32), pltpu.VMEM((2, tq, 1), F32),
                        pltpu.VMEM((2, tq, LANES), F32)])
    return pl.pallas_call(
        functools.partial(_flash_kernel, mode=mode, tq=tq, tk=tk, tt=tt, n_main_static=n_main_static,
                          q_pos0=q_pos0, lam_init=lam_init),
        grid_spec=grid_spec,
        out_shape=jax.ShapeDtypeStruct((bt, t_q, groups * LANES), F32),
        compiler_params=_params(3),
    )(slopes, *args)


def _flash_prompt_kernel(slope_ref, q_ref, k_ref, vt_ref, *rest, mode, t, lam_init):
    if mode == "diff":
        lam_ref, subg_ref, o_ref = rest[:3]
        rest = rest[3:]
    else:
        o_ref = rest[0]
        rest = rest[1:]
    m_scr, l_scr, acc_scr, main_tile_scr, diag_tile_scr = rest[:5]
    st_scrs, msub_scrs, alpha_scrs = rest[5:7], rest[7:9], rest[9:11]
    head = pl.program_id(1)
    qi = pl.program_id(2)
    q = q_ref[...]
    halves = _half_masks()
    if mode == "diff":
        qs = [jnp.where(halves[i], q, jnp.zeros_like(q)) for i in range(2)]
        slope = slope_ref[head]
    else:
        qs = [q[:, :LANES], q[:, LANES:]]
    v_rows = acc_scr.shape[1]

    m_scr[...] = jnp.full(m_scr.shape, NEG, F32)
    l_scr[...] = jnp.zeros(l_scr.shape, F32)
    acc_scr[...] = jnp.zeros(acc_scr.shape, F32)

    @pl.when(qi == 0)
    def _():
        kj = lax.broadcasted_iota(jnp.int32, (t, t), 0)
        qj = lax.broadcasted_iota(jnp.int32, (t, t), 1)
        valid = (kj // CHUNK) <= (qj // CHUNK)
        if mode == "diff":
            main_tile_scr[...] = slope * (kj - qj).astype(F32)
            diag_tile_scr[...] = jnp.where(valid, -slope * jnp.abs(qj - kj).astype(F32), NEG)
        else:
            diag_tile_scr[...] = jnp.where(valid, 0.0, NEG).astype(F32)


    def score(kb, diag, slot):
        off = pl.multiple_of(kb * t, t)
        kblk = k_ref[pl.ds(off, t), :]
        if mode == "diff" and not diag:
            shift = -slope * ((qi - kb) * t).astype(F32)
        else:
            shift = 0.0
        for i in range(2):
            ki = kblk if mode == "diff" else kblk[:, i * LANES:(i + 1) * LANES]
            st = lax.dot_general(ki, qs[i], NT, preferred_element_type=F32)
            if diag:
                st = st + diag_tile_scr[...]
            elif mode == "diff":
                st = st + main_tile_scr[...]
            st_scrs[slot][i] = st
            m_old = m_scr[i]
            m_new = jnp.maximum(m_old, jnp.max(st, axis=0, keepdims=True) + shift)
            m_scr[i] = m_new
            msub_scrs[slot][i] = m_new - shift
            alpha_scrs[slot][i] = jnp.exp2(m_old - m_new)

    def accumulate(kb, slot):
        vt = vt_ref[kb]
        for i in range(2):
            alpha = alpha_scrs[slot][i]
            vi = vt if mode == "diff" else vt[i * v_rows:(i + 1) * v_rows, :]
            p = jnp.exp2(st_scrs[slot][i] - msub_scrs[slot][i])
            l_scr[i] = alpha * l_scr[i] + jnp.sum(p, axis=0, keepdims=True)
            acc_scr[i] = alpha * acc_scr[i] + jnp.dot(vi, p.astype(BF16), preferred_element_type=F32)

    lead = jnp.logical_and(qi >= 2, qi % 2 == 0).astype(jnp.int32)

    @pl.when(lead == 1)
    def _():
        score(0, False, 0)
        accumulate(0, 0)

    @pl.when(qi == 0)
    def _():
        score(0, True, 0)

    @pl.when(qi >= 1)
    def _():
        score(lead, False, 0)

    def pair_body(j, carry):
        kb = lead + 2 * j
        score(kb + 1, False, 1)
        accumulate(kb, 0)
        score(kb + 2, False, 0)
        accumulate(kb + 1, 1)
        return carry

    lax.fori_loop(0, (qi - lead - 1) // 2, pair_body, 0)

    @pl.when(qi >= 1)
    def _():
        score(qi, True, 1)
        accumulate(qi - 1, 0)
        accumulate(qi, 1)

    @pl.when(qi == 0)
    def _():
        accumulate(0, 0)

    o0 = acc_scr[0] / l_scr[0]
    o1 = acc_scr[1] / l_scr[1]
    if mode == "diff":
        lp = lam_ref[...]
        lam = (jnp.exp(jnp.sum(lp[0:1] * lp[1:2], axis=-1, keepdims=True))
               - jnp.exp(jnp.sum(lp[2:3] * lp[3:4], axis=-1, keepdims=True)) + lam_init)
        o = (o0 - lam * o1).T
        o_ref[...] = _rms(o, subg_ref[...]) * (1.0 - lam_init)
    else:
        o_ref[...] = jnp.concatenate([o0, o1], axis=0).T


def flash_prompt(q, k, vt, *, mode, t, lam_p=None, sub_g=None, lam_init=0.0):
    bt, t_all = q.shape[0], q.shape[1]
    n_blk = t_all // t
    qw = LANES if mode == "diff" else 2 * LANES
    groups = q.shape[2] // qw
    assert vt.shape == (bt * n_blk, groups * LANES, t), vt.shape
    q_idx = lambda b, h, i, *_: (b, i, h)
    in_specs = [pl.BlockSpec((None, t, qw), q_idx),
                pl.BlockSpec((None, t_all, qw), lambda b, h, i, *_: (b, 0, h)),
                pl.BlockSpec((n_blk, LANES, t), lambda b, h, i, *_: (b, h, 0))]
    args = [q, k, vt]
    if mode == "diff":
        slopes = LOG2E * jnp.exp2(-8.0 * jnp.arange(1, groups + 1, dtype=F32) / groups)
        in_specs += [pl.BlockSpec(lam_p.shape, lambda b, h, i, *_: (0, 0)),
                     pl.BlockSpec((1, LANES), lambda b, h, i, *_: (0, 0))]
        args += [lam_p.astype(F32), sub_g.reshape(1, LANES).astype(F32)]
        v_rows = LANES
    else:
        slopes = jnp.zeros((groups,), F32)
        v_rows = LANES // 2
    stat = pltpu.VMEM((2, 1, t), F32)
    tile = pltpu.VMEM((t, t), F32)
    grid_spec = pltpu.PrefetchScalarGridSpec(
        num_scalar_prefetch=1,
        grid=(bt, groups, n_blk),
        in_specs=in_specs,
        out_specs=pl.BlockSpec((None, t, LANES), q_idx),
        scratch_shapes=[stat, stat, pltpu.VMEM((2, v_rows, t), F32), tile, tile,
                        pltpu.VMEM((2, t, t), F32), pltpu.VMEM((2, t, t), F32), stat, stat, stat, stat])
    return pl.pallas_call(
        functools.partial(_flash_prompt_kernel, mode=mode, t=t, lam_init=lam_init),
        grid_spec=grid_spec,
        out_shape=jax.ShapeDtypeStruct((bt, t_all, groups * LANES), F32),
        compiler_params=_params(3),
    )(slopes, *args)


def _c_pre_kernel(cq_ref, ckv_ref, kra_ref, krb_ref, gq_ref, gkv_ref, wq_ref, wqr_ref,
                  cq_tab, sq_tab, ck_tab, sk_tab, q_out, lat_out, kr_out):
    qn = _rms(cq_ref[...], gq_ref[...]).astype(BF16)
    qc = jnp.dot(qn, wq_ref[...], preferred_element_type=F32)
    qr = jnp.dot(qn, wqr_ref[...], preferred_element_type=F32)
    cos, sin = cq_tab[...], sq_tab[...]
    for h in range(C_HEADS):
        sl = slice(h * LANES, (h + 1) * LANES)
        q_out[:, sl] = (qc[:, sl] * cos + qr[:, sl] * sin).astype(q_out.dtype)
    lat_out[...] = _rms(ckv_ref[...], gkv_ref[...])
    kr = kra_ref[...] * ck_tab[...] + krb_ref[...] * sk_tab[...]
    kr_out[...] = kr[:, :C_ROPE]


def c_pre(cq, ckv, kra, krb, gq, gkv, wq, wqr, tabs, t_len, tm=512):
    m = cq.shape[0]
    tm = _row_tile(t_len, tm)
    n_t = t_len // tm
    row = lambda i: (i, 0)
    fixed = lambda i: (0, 0)
    trow = lambda i: (i % n_t, 0)
    qw = wq.shape[1]
    return pl.pallas_call(
        _c_pre_kernel,
        grid=(m // tm,),
        in_specs=[pl.BlockSpec((tm, C_Q_LORA), row), pl.BlockSpec((tm, C_KV_LORA), row),
                  pl.BlockSpec((tm, LANES), row), pl.BlockSpec((tm, LANES), row),
                  pl.BlockSpec((1, C_Q_LORA), fixed), pl.BlockSpec((1, C_KV_LORA), fixed),
                  pl.BlockSpec(wq.shape, fixed), pl.BlockSpec(wqr.shape, fixed),
                  pl.BlockSpec((tm, LANES), trow), pl.BlockSpec((tm, LANES), trow),
                  pl.BlockSpec((tm, LANES), trow), pl.BlockSpec((tm, LANES), trow)],
        out_specs=[pl.BlockSpec((tm, qw), row), pl.BlockSpec((tm, C_KV_LORA), row),
                   pl.BlockSpec((tm, C_ROPE), row)],
        out_shape=[jax.ShapeDtypeStruct((m, qw), BF16), jax.ShapeDtypeStruct((m, C_KV_LORA), F32),
                   jax.ShapeDtypeStruct((m, C_ROPE), F32)],
        compiler_params=_params(1),
    )(cq, ckv, kra, krb, gq.reshape(1, -1).astype(F32), gkv.reshape(1, -1).astype(F32), wq, wqr, *tabs)


def _c_kv_kernel(lat_ref, kr_ref, wk_ref, wv_ref, place_ref, k_out, v_out, *, v_transposed):
    lat = lat_ref[...].astype(BF16)
    k = jnp.dot(lat, wk_ref[...], preferred_element_type=F32)
    k += jnp.dot(kr_ref[...].astype(BF16), place_ref[...], preferred_element_type=F32)
    k_out[...] = k.astype(k_out.dtype)
    if v_transposed:
        v = lax.dot_general(wv_ref[...], lat, NT, preferred_element_type=F32)
    else:
        v = jnp.dot(lat, wv_ref[...], preferred_element_type=F32)
    v_out[...] = v.astype(v_out.dtype)


def c_kv(lat, kr, wk, wv, place, v_transposed, tm=512):
    m = lat.shape[0]
    tm = _row_tile(m, tm)
    row = lambda i: (i, 0)
    fixed = lambda i: (0, 0)
    if v_transposed:
        n_v = wv.shape[0]
        v_spec = pl.BlockSpec((None, n_v, tm), lambda i: (i, 0, 0))
        v_shape = jax.ShapeDtypeStruct((m // tm, n_v, tm), BF16)
    else:
        n_v = wv.shape[1]
        v_spec = pl.BlockSpec((tm, n_v), row)
        v_shape = jax.ShapeDtypeStruct((m, n_v), BF16)
    return pl.pallas_call(
        functools.partial(_c_kv_kernel, v_transposed=v_transposed),
        grid=(m // tm,),
        in_specs=[pl.BlockSpec((tm, C_KV_LORA), row), pl.BlockSpec((tm, C_ROPE), row),
                  pl.BlockSpec(wk.shape, fixed), pl.BlockSpec(wv.shape, fixed), pl.BlockSpec(place.shape, fixed)],
        out_specs=[pl.BlockSpec((tm, wk.shape[1]), row), v_spec],
        out_shape=[jax.ShapeDtypeStruct((m, wk.shape[1]), BF16), v_shape],
        compiler_params=_params(1),
    )(lat, kr, wk, wv, place)


def _prep_in_ab(w):
    n_mix = w.shape[1] - MEM_W - w.shape[0]
    n_q = n_mix // 3
    scale = jnp.concatenate([jnp.full((n_q,), Q_SCALE, F32), jnp.ones((n_mix - n_q,), F32),
                             jnp.full((MEM_W,), Q_SCALE, F32), jnp.ones((w.shape[0],), F32)])
    return (w * scale[None, :]).astype(BF16)


def _prep_in_c(w):
    d = w.shape[0]
    o = C_Q_LORA + C_KV_LORA
    half = C_ROPE // 2
    kr = w[:, o:o + C_ROPE]
    rot = jnp.concatenate([-kr[:, half:], kr[:, :half]], axis=1)
    pad = jnp.zeros((d, LANES - C_ROPE), F32)
    mq = w[:, o + C_ROPE:o + C_ROPE + MEM_W] * (Q_SCALE)
    gate = w[:, o + C_ROPE + MEM_W:]
    return jnp.concatenate([w[:, :o], kr, pad, rot, pad, mq, gate], axis=1).astype(BF16)


def _prep_uq(w_uq):
    r = w_uq.shape[0]
    w = w_uq.reshape(r, C_HEADS, C_NOPE + C_ROPE)
    nope, rope = w[..., :C_NOPE], w[..., C_NOPE:]
    half = C_ROPE // 2
    rot = jnp.concatenate([-rope[..., half:], rope[..., :half]], axis=-1)
    pad = jnp.zeros((r, C_HEADS, LANES - C_NOPE - C_ROPE), F32)
    plain = jnp.concatenate([nope, rope, pad], axis=-1).reshape(r, C_HEADS * LANES)
    rotated = jnp.concatenate([jnp.zeros_like(nope), rot, pad], axis=-1).reshape(r, C_HEADS * LANES)
    return plain.astype(BF16), rotated.astype(BF16)


def _prep_ukv(w_ukv):
    r = w_ukv.shape[0]
    w = w_ukv.reshape(r, C_HEADS, C_NOPE + C_V)
    wk = jnp.concatenate([w[..., :C_NOPE], jnp.zeros((r, C_HEADS, LANES - C_NOPE), F32)], axis=-1)
    wv = w[..., C_NOPE:]
    eye = jnp.eye(C_ROPE, dtype=F32)
    place = jnp.concatenate([jnp.zeros((C_ROPE, C_NOPE), F32), eye,
                             jnp.zeros((C_ROPE, LANES - C_NOPE - C_ROPE), F32)], axis=1)
    place = jnp.tile(place, (1, C_HEADS))
    return (wk.reshape(r, C_HEADS * LANES).astype(BF16), wv.reshape(r, C_HEADS * C_V).astype(BF16),
            place.astype(BF16))


def _rope_tables(pos):
    half = C_ROPE // 2
    inv = jnp.exp(-math.log(ROPE_BASE) * jnp.arange(half, dtype=F32) * 2.0 / C_ROPE)
    ang = pos.astype(F32)[:, None] * inv[None, :]
    cos, sin = jnp.cos(ang), jnp.sin(ang)
    t = pos.shape[0]
    cos2 = jnp.concatenate([cos, cos], axis=1)
    sin2 = jnp.concatenate([sin, sin], axis=1)
    scale = LOG2E * (C_NOPE + C_ROPE) ** -0.5
    z = lambda n: jnp.zeros((t, n), F32)
    cq = jnp.concatenate([jnp.ones((t, C_NOPE), F32), cos2, z(LANES - C_NOPE - C_ROPE)], axis=1) * scale
    sq = jnp.concatenate([z(C_NOPE), sin2, z(LANES - C_NOPE - C_ROPE)], axis=1) * scale
    ck = jnp.concatenate([cos2, z(LANES - C_ROPE)], axis=1)
    sk = jnp.concatenate([sin2, z(LANES - C_ROPE)], axis=1)
    return cq, sq, ck, sk


def _band_bias_kernel(pos_ref, row_ref, o_ref, *, tq, win, n_valid, transposed):
    c = pl.program_id(0)
    q0, k0 = pos_ref[0, c], pos_ref[1, c]
    wp = row_ref.shape[-1]
    n_rows, n_cols = (win, tq) if transposed else (tq, win)
    rows = pltpu.roll(jnp.broadcast_to(row_ref[...], (n_rows, wp)), 0, 1, stride=1, stride_axis=0)
    ri = lax.broadcasted_iota(jnp.int32, (n_rows, n_cols), 0)
    ci = lax.broadcasted_iota(jnp.int32, (n_rows, n_cols), 1)
    qi, kj = (ci, ri) if transposed else (ri, ci)
    qp, kp = q0 + qi, k0 + kj
    qc, kc = qp // CHUNK, kp // CHUNK
    valid = (kp >= 0) & (kc <= qc) & (kc >= qc - A_BAND_CHUNKS) & (kj < n_valid)
    o_ref[...] = jnp.where(valid, rows[:, :n_cols], NEG)


def band_bias(rel_bias, q0, k0, *, tq, win, n_valid, transposed=False):
    n_cls = q0.shape[0]
    heads = rel_bias.shape[1]
    wp = -(-(tq + win) // LANES) * LANES
    mm = jnp.arange(wp, dtype=jnp.int32)
    mm = jnp.where(mm < (tq if transposed else win), mm, mm - wp)
    mm = mm if transposed else -mm
    rel = jnp.clip((q0 - k0)[:, None] + mm[None, :], -A_REL_CLIP, A_REL_CLIP) + A_REL_CLIP
    rows = jnp.moveaxis((LOG2E * rel_bias.astype(F32))[rel], -1, 1).reshape(n_cls, heads, 1, wp)
    pos = jnp.stack([q0, k0]).astype(jnp.int32)
    if transposed:
        out_spec = pl.BlockSpec((None, None, win, tq), lambda c, h, *_: (c, h // 2, 0, h % 2))
        out_shape = jax.ShapeDtypeStruct((n_cls, heads // 2, win, 2 * tq), F32)
    else:
        out_spec = pl.BlockSpec((None, None, tq, win), lambda c, h, *_: (c, h, 0, 0))
        out_shape = jax.ShapeDtypeStruct((n_cls, heads, tq, win), F32)
    grid_spec = pltpu.PrefetchScalarGridSpec(
        num_scalar_prefetch=1,
        grid=(n_cls, heads),
        in_specs=[pl.BlockSpec((None, None, 1, wp), lambda c, h, *_: (c, h, 0, 0))],
        out_specs=out_spec)
    return pl.pallas_call(
        functools.partial(_band_bias_kernel, tq=tq, win=win, n_valid=n_valid, transposed=transposed),
        grid_spec=grid_spec,
        out_shape=out_shape,
        compiler_params=_params(2),
    )(pos, rows)


A_TQ = 256
FLASH_T = 512
FAR = 1 << 24


def _trunk(x, pos0, mem_k, mem_v, past, wts):
    bt, t, d = x.shape
    m = bt * t
    depth = len(wts["w_out"])
    x2 = x.reshape(m, d)
    new_a, new_b, new_c = [], [], []
    pos = pos0 + jnp.arange(t, dtype=jnp.int32)
    one = lambda width, dt: (width, (dt,))
    for i in range(depth):
        kind, j = i % N_MIXERS, i // N_MIXERS
        if kind == 0:
            hd = A_HEADS * HEAD_DIM
            w_in = wts["w_in_a"][j]
            if past is None:
                tq = min(A_TQ, t)
                q, k, k16, v, mq, gate, vt = norm_matmul(
                    x2, wts["norm_g"][i], w_in,
                    (one(hd, BF16), (hd, (F32, BF16)), one(hd, F32), one(MEM_W, BF16), one(d, F32)),
                    wt=w_in[:, 2 * hd:3 * hd].T, t_blk=tq)
                k3, v3 = k.reshape(bt, t, hd), v.reshape(bt, t, hd)
                back_blocks = A_PAST_ROWS // tq
                n_win = min(back_blocks + 1, t // tq)
                win = n_win * tq
                n_cls = n_win
                q0 = jnp.arange(n_cls, dtype=jnp.int32) * tq
                k0 = jnp.clip(q0 - A_PAST_ROWS, 0, t - win)
                bias = band_bias(wts["rel_bias_a"][j], q0, k0, tq=tq, win=win, n_valid=win, transposed=True)
                br = attn_window_t(q.reshape(bt, t, hd), k16.reshape(bt, t, hd), vt, bias,
                                   tq=tq, n_sub=2, n_win=n_win, back_blocks=back_blocks)
                keep = min(A_PAST_ROWS, t)
                new_a.append((k3[:, t - keep:].reshape(bt, keep, A_HEADS, HEAD_DIM),
                              v3[:, t - keep:].reshape(bt, keep, A_HEADS, HEAD_DIM)))
            else:
                q, k, v, mq, gate = norm_matmul(
                    x2, wts["norm_g"][i], w_in,
                    (one(hd, BF16), one(hd, F32), one(hd, F32), one(MEM_W, BF16), one(d, F32)))
                q3, k3, v3 = (a.reshape(bt, t, hd) for a in (q, k, v))
                ck, cv = past[0][j], past[1][j]
                rows = ck.shape[1]
                n_keys = rows + t
                pad = (-n_keys) % LANES
                zeros = jnp.zeros((bt, pad, hd), F32)
                kk = jnp.concatenate([ck.reshape(bt, rows, hd), k3, zeros], axis=1)
                vv = jnp.concatenate([cv.reshape(bt, rows, hd), v3, zeros], axis=1)
                start = jnp.full((1,), pos0, jnp.int32)
                bias = band_bias(wts["rel_bias_a"][j], start, start - rows, tq=t, win=n_keys + pad,
                                 n_valid=n_keys)
                br = attn_window(q3, kk, vv, bias, tq=t, win=n_keys + pad, back=0)
                new_a.append((k3.reshape(bt, t, A_HEADS, HEAD_DIM), v3.reshape(bt, t, A_HEADS, HEAD_DIM)))
        elif kind == 1:
            hd = B_HEADS * 2 * HEAD_DIM
            lam_init = 0.8 - 0.6 * math.exp(-0.3 * i)
            common = dict(mode="diff", lam_p=wts["lambda_b"][j], sub_g=wts["subln_g_b"][j], lam_init=lam_init)
            w_in = wts["w_in_b"][j]
            if past is None:
                q, k, k16, v, mq, gate, vt = norm_matmul(
                    x2, wts["norm_g"][i], w_in,
                    (one(hd, BF16), (hd, (F32, BF16)), one(hd, F32), one(MEM_W, BF16), one(d, F32)),
                    wt=w_in[:, 2 * hd:3 * hd].T, t_blk=FLASH_T, tm=FLASH_T)
                br = flash_prompt(q.reshape(bt, t, hd), k16.reshape(bt, t, hd), vt, t=FLASH_T, **common)
            else:
                q, k, v, mq, gate = norm_matmul(
                    x2, wts["norm_g"][i], w_in,
                    (one(hd, BF16), one(hd, F32), one(hd, F32), one(MEM_W, BF16), one(d, F32)))
                ck, cv = past[2][j], past[3][j]
                rows = ck.shape[1]
                br = flash_attention(q.reshape(bt, t, hd), ck.reshape(bt, rows, hd), cv.reshape(bt, rows, hd),
                                     k.reshape(bt, t, hd), v.reshape(bt, t, hd),
                                     tq=t, tk=FLASH_T, diagonal_tail=False, q_pos0=rows, **common)
            new_b.append((k.reshape(bt, t, B_HEADS, 2, HEAD_DIM), v.reshape(bt, t, B_HEADS, 2 * HEAD_DIM)))
        else:
            cq, ckv, kra, krb, mq, gate = norm_matmul(
                x2, wts["norm_g"][i], wts["w_in_c"][j],
                (one(C_Q_LORA, F32), one(C_KV_LORA, F32), one(LANES, F32), one(LANES, F32),
                 one(MEM_W, BF16), one(d, F32)))
            wq, wqr = wts["w_uq_c"][j]
            wk, wv, place = wts["w_ukv_c"][j]
            qcat, lat, kr = c_pre(cq, ckv, kra, krb, wts["q_norm_g_c"][j], wts["kv_norm_g_c"][j],
                                  wq, wqr, _rope_tables(pos), t)
            q3 = qcat.reshape(bt, t, -1)
            if past is None:
                kcat, vt = c_kv(lat, kr, wk, wv.T, place, v_transposed=True, tm=FLASH_T)
                br = flash_prompt(q3, kcat.reshape(bt, t, -1), vt, mode="mla", t=FLASH_T)
            else:
                kcat, vcat = c_kv(lat, kr, wk, wv, place, v_transposed=False)
                cl, cr = past[4][j], past[5][j]
                rows = cl.shape[1]
                kc_, vc_ = c_kv(cl.reshape(bt * rows, -1), cr.reshape(bt * rows, -1), wk, wv, place,
                                v_transposed=False)
                br = flash_attention(q3, kc_.reshape(bt, rows, -1), vc_.reshape(bt, rows, -1),
                                     kcat.reshape(bt, t, -1), vcat.reshape(bt, t, -1),
                                     mode="mla", tq=t, tk=FLASH_T, diagonal_tail=False, q_pos0=rows)
            new_c.append((lat.reshape(bt, t, -1), kr.reshape(bt, t, -1)))
        if past is None:
            mo = attn_window_t(mq.reshape(bt, t, MEM_W), mem_k[i], mem_v[i], None,
                               tq=min(FLASH_T, t) // 2, n_sub=2, n_win=1, back_blocks=0)
        else:
            mo = attn_window(mq.reshape(bt, t, MEM_W), mem_k[i], mem_v[i], None,
                             tq=t, win=mem_k[i].shape[1], back=0)
        last = i == depth - 1
        x2 = gated_out(br.reshape(m, -1), mo.reshape(m, MEM_W), gate, x2, wts["w_out"][i],
                       wts["final_g"], final=last)
    return x2.reshape(bt, t, d), new_a, new_b, new_c


def kernel(x_prompt, x_sample, cache_a_k, cache_a_v, cache_b_k, cache_b_v, cache_c_lat, cache_c_rope,
           cache_mem_k, cache_mem_v, mem_prompt, norm_g, final_g, mem_norm_g, w_mem_kv, w_out, w_in_a,
           rel_bias_a, w_in_b, lambda_b, subln_g_b, w_in_c, q_norm_g_c, kv_norm_g_c, w_uq_c, w_ukv_c):
    depth = w_out.shape[0]
    bp, n_mem, d = mem_prompt.shape
    mem_heads = MEM_W // HEAD_DIM
    wts = dict(
        norm_g=norm_g, final_g=final_g, rel_bias_a=rel_bias_a, lambda_b=lambda_b, subln_g_b=subln_g_b,
        q_norm_g_c=q_norm_g_c, kv_norm_g_c=kv_norm_g_c,
        w_out=[w_out[i].astype(BF16) for i in range(depth)],
        w_in_a=[_prep_in_ab(w_in_a[j]) for j in range(w_in_a.shape[0])],
        w_in_b=[_prep_in_ab(w_in_b[j]) for j in range(w_in_b.shape[0])],
        w_in_c=[_prep_in_c(w_in_c[j]) for j in range(w_in_c.shape[0])],
        w_uq_c=[_prep_uq(w_uq_c[j]) for j in range(w_uq_c.shape[0])],
        w_ukv_c=[_prep_ukv(w_ukv_c[j]) for j in range(w_ukv_c.shape[0])],
    )
    mem2 = mem_prompt.reshape(bp * n_mem, d)
    mem_k_p, mem_v_p, mem_vt_p = [], [], []
    for i in range(depth):
        w_kv = w_mem_kv[i].astype(BF16)
        mk, mv, mvt = norm_matmul(mem2, mem_norm_g[i], w_kv, ((MEM_W, (F32,)), (MEM_W, (F32,))),
                                  wt=w_kv[:, MEM_W:].T, t_blk=n_mem, tm=n_mem)
        mem_k_p.append(mk.reshape(bp, n_mem, MEM_W))
        mem_v_p.append(mv.reshape(bp, n_mem, MEM_W))
        mem_vt_p.append(mvt)

    y_p, na_p, nb_p, nc_p = _trunk(x_prompt, 0, mem_k_p, mem_vt_p, None, wts)

    bs = x_sample.shape[0]
    mem_k_s = [cache_mem_k[i].reshape(bs, n_mem, MEM_W) for i in range(depth)]
    mem_v_s = [cache_mem_v[i].reshape(bs, n_mem, MEM_W) for i in range(depth)]
    past = (cache_a_k, cache_a_v, cache_b_k, cache_b_v, cache_c_lat, cache_c_rope)
    y_s, na_s, nb_s, nc_s = _trunk(x_sample, cache_b_k.shape[2], mem_k_s, mem_v_s, past, wts)

    stk = lambda lst, n: jnp.stack([s[n] for s in lst])
    heads4 = lambda lst: jnp.stack([a.reshape(bp, n_mem, mem_heads, HEAD_DIM) for a in lst])
    return (y_p, y_s,
            stk(na_p, 0), stk(na_p, 1), stk(na_s, 0), stk(na_s, 1),
            stk(nb_p, 0), stk(nb_p, 1), stk(nb_s, 0), stk(nb_s, 1),
            stk(nc_p, 0), stk(nc_p, 1), stk(nc_s, 0), stk(nc_s, 1),
            heads4(mem_k_p), heads4(mem_v_p))
```

```python
import functools
import math

import jax
import jax.numpy as jnp
from jax import lax
from jax.experimental import pallas as pl
from jax.experimental.pallas import tpu as pltpu

F32 = jnp.float32
BF16 = jnp.bfloat16

LANES = 128
VMEM_LIMIT = 56 * 1024 * 1024

EPS = 1e-6
NEG = -1e30
LOG2E = math.log2(math.e)
CHUNK = 64
HEAD_DIM = 64
Q_SCALE = LOG2E * HEAD_DIM ** -0.5
MEM_W = 256
A_HEADS = 12
A_PAST_ROWS = 512
A_BAND_CHUNKS = 8
A_REL_CLIP = 128
B_HEADS = 6
C_HEADS = 12
C_NOPE = 64
C_ROPE = 32
C_V = 64
C_Q_LORA = 384
C_KV_LORA = 256
ROPE_BASE = 10000.0
N_MIXERS = 3


def _params(n_grid):
    return pltpu.CompilerParams(dimension_semantics=("arbitrary",) * n_grid,
                                vmem_limit_bytes=VMEM_LIMIT)


def _row_tile(m, want):
    t = min(m, want)
    assert m % t == 0, (m, t)
    return t


def _rms(x, g):
    return x * lax.rsqrt(jnp.mean(x * x, axis=-1, keepdims=True) + EPS) * g


NT = (((1,), (1,)), ((), ()))


def _norm_matmul_kernel(x_ref, g_ref, w_ref, *rest, segs, has_t):
    h = _rms(x_ref[...], g_ref[...]).astype(BF16)
    out_refs = rest[1:] if has_t else rest
    off = n_out = 0
    for width, dtypes in segs:
        r = jnp.dot(h, w_ref[:, off:off + width], preferred_element_type=F32)
        for dt in dtypes:
            out_refs[n_out][...] = r.astype(dt)
            n_out += 1
        off += width
    if has_t:
        o_ref = out_refs[n_out]
        r = lax.dot_general(rest[0][...], h, NT, preferred_element_type=F32).astype(o_ref.dtype)
        blk = o_ref.shape[-1]
        for jb in range(o_ref.shape[0]):
            o_ref[jb] = r[:, jb * blk:(jb + 1) * blk]


def norm_matmul(x, g, w, segs, wt=None, t_blk=None, tm=512):
    m, d = x.shape
    tm = _row_tile(m, tm)
    n = w.shape[1]
    assert n == sum(wd for wd, _ in segs)
    row = lambda i: (i, 0)
    fixed = lambda i: (0, 0)
    in_specs = [pl.BlockSpec((tm, d), row), pl.BlockSpec((1, d), fixed), pl.BlockSpec((d, n), fixed)]
    args = [x, g.reshape(1, d).astype(F32), w]
    out_specs = [pl.BlockSpec((tm, wd), row) for wd, dts in segs for _ in dts]
    out_shape = [jax.ShapeDtypeStruct((m, wd), dt) for wd, dts in segs for dt in dts]
    if wt is not None:
        in_specs.append(pl.BlockSpec(wt.shape, fixed))
        args.append(wt)
        assert tm % t_blk == 0
        out_specs.append(pl.BlockSpec((tm // t_blk, wt.shape[0], t_blk), lambda i: (i, 0, 0)))
        out_shape.append(jax.ShapeDtypeStruct((m // t_blk, wt.shape[0], t_blk), BF16))
    return pl.pallas_call(
        functools.partial(_norm_matmul_kernel, segs=tuple(segs), has_t=wt is not None),
        grid=(m // tm,),
        in_specs=in_specs,
        out_specs=out_specs,
        out_shape=out_shape,
        compiler_params=_params(1),
    )(*args)


def _gated_out_kernel(br_ref, mo_ref, gate_ref, x_ref, w_ref, fg_ref, o_ref, *, final):
    nb = br_ref.shape[-1]
    gate = gate_ref[...].astype(F32)
    sg = gate * jax.nn.sigmoid(gate)
    y1 = (br_ref[...].astype(F32) * sg[:, :nb]).astype(BF16)
    y2 = (mo_ref[...].astype(F32) * sg[:, nb:]).astype(BF16)
    acc = jnp.dot(y1, w_ref[:nb, :], preferred_element_type=F32)
    acc += jnp.dot(y2, w_ref[nb:, :], preferred_element_type=F32)
    xn = x_ref[...] + acc
    o_ref[...] = _rms(xn, fg_ref[...]) if final else xn


def gated_out(br, mo, gate, x, w, final_g, final, tm=512):
    m, d = x.shape
    tm = _row_tile(m, tm)
    nb, nm, ng = br.shape[1], mo.shape[1], gate.shape[1]
    row = lambda i: (i, 0)
    fixed = lambda i: (0, 0)
    return pl.pallas_call(
        functools.partial(_gated_out_kernel, final=final),
        grid=(m // tm,),
        in_specs=[pl.BlockSpec((tm, nb), row), pl.BlockSpec((tm, nm), row), pl.BlockSpec((tm, ng), row),
                  pl.BlockSpec((tm, d), row), pl.BlockSpec((ng, d), fixed), pl.BlockSpec((1, d), fixed)],
        out_specs=pl.BlockSpec((tm, d), row),
        out_shape=jax.ShapeDtypeStruct((m, d), F32),
        compiler_params=_params(1),
    )(br, mo, gate, x, w, final_g.reshape(1, d).astype(F32))


def _half_masks():
    lane = lax.broadcasted_iota(jnp.int32, (1, LANES), 1)
    lo = lane < HEAD_DIM
    return lo, jnp.logical_not(lo)


def _attn_window_kernel(q_ref, k_ref, v_ref, *rest, tq, win, back):
    if len(rest) == 2:
        bias_ref, o_ref = rest
    else:
        bias_ref, (o_ref,) = None, rest
    tk_all = k_ref.shape[0]
    if win == tk_all:
        k = k_ref[...]
        v = v_ref[...]
    else:
        start = jnp.clip(pl.program_id(2) * tq - back, 0, tk_all - win)
        start = pl.multiple_of(start, CHUNK)
        k = k_ref[pl.ds(start, win), :]
        v = v_ref[pl.ds(start, win), :]
    k = k.astype(BF16)
    v = v.astype(BF16)
    q = q_ref[...]
    halves = _half_masks()
    outs = []
    for i in range(2):
        qm = jnp.where(halves[i], q, jnp.zeros_like(q))
        s = lax.dot_general(qm, k, (((1,), (1,)), ((), ())), preferred_element_type=F32)
        if bias_ref is not None:
            s = s + bias_ref[i]
        m = jnp.max(s, axis=-1, keepdims=True)
        p = jnp.exp2(s - m)
        l = jnp.sum(p, axis=-1, keepdims=True)
        o = jnp.dot(p.astype(BF16), v, preferred_element_type=F32)
        outs.append(o / l)
    o_ref[...] = jnp.where(halves[0], outs[0], outs[1]).astype(o_ref.dtype)


def attn_window(q, k, v, bias, *, tq, win, back):
    bt, t_q, gw = q.shape
    g = gw // LANES
    t_k = k.shape[1]
    tq = _row_tile(t_q, tq)
    in_specs = [pl.BlockSpec((None, tq, LANES), lambda b, h, i: (b, i, h)),
                pl.BlockSpec((None, t_k, LANES), lambda b, h, i: (b, 0, h)),
                pl.BlockSpec((None, t_k, LANES), lambda b, h, i: (b, 0, h))]
    args = [q, k, v]
    if bias is not None:
        last = bias.shape[0] - 1
        in_specs.append(pl.BlockSpec((None, 2, tq, win), lambda b, h, i: (jnp.minimum(i, last), h, 0, 0)))
        args.append(bias)
    return pl.pallas_call(
        functools.partial(_attn_window_kernel, tq=tq, win=win, back=back),
        grid=(bt, g, t_q // tq),
        in_specs=in_specs,
        out_specs=pl.BlockSpec((None, tq, LANES), lambda b, h, i: (b, i, h)),
        out_shape=jax.ShapeDtypeStruct((bt, t_q, gw), BF16),
        compiler_params=_params(3),
    )(*args)


def _attn_window_t_kernel(q_ref, k_ref, vt_ref, *rest, tq, n_sub, n_win, back_blocks):
    if len(rest) == 2:
        bias_ref, o_ref = rest
    else:
        bias_ref, (o_ref,) = None, rest
    n_blk, _, blk = vt_ref.shape
    halves = _half_masks()

    def score(sub):
        i = pl.program_id(2) * n_sub + sub
        if n_win == n_blk:
            b0 = 0
            k = k_ref[...]
        else:
            b0 = jnp.clip(i - back_blocks, 0, n_blk - n_win)
            k = k_ref[pl.ds(pl.multiple_of(b0 * blk, blk), n_win * blk), :]
        q = q_ref[sub * tq:(sub + 1) * tq, :]
        q2 = jnp.concatenate([jnp.where(halves[h], q, jnp.zeros_like(q)) for h in range(2)], axis=0)
        st = lax.dot_general(k.astype(BF16), q2, NT, preferred_element_type=F32)
        if bias_ref is not None:
            st = st + bias_ref[jnp.minimum(i, bias_ref.shape[0] - 1)]
        return b0, st, jnp.max(st, axis=0, keepdims=True)

    def accumulate(sub, b0, st, m):
        p = jnp.exp2(st - m)
        l = jnp.sum(p, axis=0, keepdims=True)
        pb = p.astype(BF16)
        o = None
        for j in range(n_win):
            oj = jnp.dot(vt_ref[b0 + j], pb[j * blk:(j + 1) * blk, :], preferred_element_type=F32)
            o = oj if o is None else o + oj
        o = o / l
        o = jnp.concatenate([o[:HEAD_DIM, :tq], o[HEAD_DIM:, tq:]], axis=0).T
        o_ref[sub * tq:(sub + 1) * tq, :] = o.astype(o_ref.dtype)

    staged = [score(sub) for sub in range(n_sub)]
    for sub in range(n_sub):
        accumulate(sub, *staged[sub])


def attn_window_t(q, k, vt, bias, *, tq, n_sub, n_win, back_blocks):
    bt, t_q, gw = q.shape
    g = gw // LANES
    t_k = k.shape[1]
    blk = vt.shape[2]
    n_blk = t_k // blk
    assert vt.shape == (bt * n_blk, gw, blk), vt.shape
    rows = n_sub * tq
    assert t_q % rows == 0
    in_specs = [pl.BlockSpec((None, rows, LANES), lambda b, h, i: (b, i, h)),
                pl.BlockSpec((None, t_k, LANES), lambda b, h, i: (b, 0, h)),
                pl.BlockSpec((n_blk, LANES, blk), lambda b, h, i: (b, h, 0))]
    args = [q, k, vt]
    if bias is not None:
        in_specs.append(pl.BlockSpec((bias.shape[0], None, n_win * blk, 2 * tq), lambda b, h, i: (0, h, 0, 0)))
        args.append(bias)
    return pl.pallas_call(
        functools.partial(_attn_window_t_kernel, tq=tq, n_sub=n_sub, n_win=n_win, back_blocks=back_blocks),
        grid=(bt, g, t_q // rows),
        in_specs=in_specs,
        out_specs=pl.BlockSpec((None, rows, LANES), lambda b, h, i: (b, i, h)),
        out_shape=jax.ShapeDtypeStruct((bt, t_q, gw), BF16),
        compiler_params=_params(3),
    )(*args)


def _flash_kernel(slope_ref, q_ref, km_ref, vm_ref, kt_ref, vt_ref, *rest,
                  mode, tq, tk, tt, n_main_static, q_pos0, lam_init):
    if mode == "diff":
        lam_ref, subg_ref, o_ref, m_scr, l_scr, acc_scr = rest
    else:
        o_ref, m_scr, l_scr, acc_scr = rest
    head = pl.program_id(1)
    qi = pl.program_id(2)
    q = q_ref[...]
    halves = _half_masks()
    if mode == "diff":
        qs = [jnp.where(halves[i], q, jnp.zeros_like(q)) for i in range(2)]
        slope = slope_ref[head]
    else:
        qs = [q[:, :LANES], q[:, LANES:]]
        slope = None

    m_scr[...] = jnp.full(m_scr.shape, NEG, F32)
    l_scr[...] = jnp.zeros(l_scr.shape, F32)
    acc_scr[...] = jnp.zeros(acc_scr.shape, F32)

    def update(i, s, shift, v):
        m_old = m_scr[i]
        m_new = jnp.maximum(m_old, jnp.max(s, axis=-1, keepdims=True) + shift)
        alpha = jnp.exp2(m_old - m_new)
        p = jnp.exp2(s - (m_new - shift))
        l_scr[i] = alpha * l_scr[i] + jnp.sum(p, axis=-1, keepdims=True)
        acc_scr[i] = alpha * acc_scr[i] + jnp.dot(p.astype(BF16), v, preferred_element_type=F32)
        m_scr[i] = m_new

    def k_of(kblk, i):
        return kblk if mode == "diff" else kblk[:, i * LANES:(i + 1) * LANES]

    nt = (((1,), (1,)), ((), ()))
    q_base = q_pos0 + qi * tq

    if mode == "diff":
        ri = lax.broadcasted_iota(jnp.int32, (tq, tk), 0)
        ci = lax.broadcasted_iota(jnp.int32, (tq, tk), 1)
        main_tile = slope * (ci - ri).astype(F32)

    def main_body(kb, carry):
        off = pl.multiple_of(kb * tk, tk)
        kblk = km_ref[pl.ds(off, tk), :].astype(BF16)
        vblk = vm_ref[pl.ds(off, tk), :].astype(BF16)
        for i in range(2):
            s = lax.dot_general(qs[i], k_of(kblk, i), nt, preferred_element_type=F32)
            if mode == "diff":
                shift = -slope * (q_base - kb * tk).astype(F32)
                update(i, s + main_tile, shift, vblk)
            else:
                update(i, s, 0.0, vblk)
        return carry

    n_main = qi if n_main_static is None else n_main_static
    lax.fori_loop(0, n_main, main_body, 0)

    ri = lax.broadcasted_iota(jnp.int32, (tq, tt), 0)
    ci = lax.broadcasted_iota(jnp.int32, (tq, tt), 1)
    valid = (ci // CHUNK) <= (ri // CHUNK)
    if mode == "diff":
        tail_tile = jnp.where(valid, -slope * jnp.abs(ri - ci).astype(F32), NEG)
    else:
        tail_tile = jnp.where(valid, 0.0, NEG).astype(F32)
    kblk = kt_ref[...].astype(BF16)
    vblk = vt_ref[...].astype(BF16)
    for i in range(2):
        s = lax.dot_general(qs[i], k_of(kblk, i), nt, preferred_element_type=F32)
        update(i, s + tail_tile, 0.0, vblk)

    o0 = acc_scr[0] / l_scr[0]
    o1 = acc_scr[1] / l_scr[1]
    if mode == "diff":
        lp = lam_ref[...]
        lam = (jnp.exp(jnp.sum(lp[0:1] * lp[1:2], axis=-1, keepdims=True))
               - jnp.exp(jnp.sum(lp[2:3] * lp[3:4], axis=-1, keepdims=True)) + lam_init)
        o = o0 - lam * o1
        o_ref[...] = (_rms(o, subg_ref[...]) * (1.0 - lam_init)).astype(o_ref.dtype)
    else:
        o_ref[...] = jnp.where(halves[0], o0, o1).astype(o_ref.dtype)


def flash_attention(q, k_main, v_main, k_tail, v_tail, *, mode, tq, tk, diagonal_tail, q_pos0,
                    lam_p=None, sub_g=None, lam_init=0.0):
    bt, t_q = q.shape[0], q.shape[1]
    tq = _row_tile(t_q, tq)
    qw = LANES if mode == "diff" else 2 * LANES
    groups = q.shape[2] // qw
    t_main = k_main.shape[1]
    tk = _row_tile(t_main, tk)
    if diagonal_tail:
        assert tk == tq
        tt, n_main_static = tq, None
        tail_idx = lambda b, h, i, *_: (b, i, h)
    else:
        tt, n_main_static = k_tail.shape[1], t_main // tk
        tail_idx = lambda b, h, i, *_: (b, 0, h)
    q_idx = lambda b, h, i, *_: (b, i, h)
    main_idx = lambda b, h, i, *_: (b, 0, h)
    in_specs = [pl.BlockSpec((None, tq, qw), q_idx),
                pl.BlockSpec((None, t_main, qw), main_idx),
                pl.BlockSpec((None, t_main, LANES), main_idx),
                pl.BlockSpec((None, tt, qw), tail_idx),
                pl.BlockSpec((None, tt, LANES), tail_idx)]
    args = [q, k_main, v_main, k_tail, v_tail]
    if mode == "diff":
        slopes = LOG2E * jnp.exp2(-8.0 * jnp.arange(1, groups + 1, dtype=F32) / groups)
        in_specs += [pl.BlockSpec(lam_p.shape, lambda b, h, i, *_: (0, 0)),
                     pl.BlockSpec((1, LANES), lambda b, h, i, *_: (0, 0))]
        args += [lam_p.astype(F32), sub_g.reshape(1, LANES).astype(F32)]
    else:
        slopes = jnp.zeros((groups,), F32)
    grid_spec = pltpu.PrefetchScalarGridSpec(
        num_scalar_prefetch=1,
        grid=(bt, groups, t_q // tq),
        in_specs=in_specs,
        out_specs=pl.BlockSpec((None, tq, LANES), q_idx),
        scratch_shapes=[pltpu.VMEM((2, tq, 1), F32), pltpu.VMEM((2, tq, 1), F32),
                        pltpu.VMEM((2, tq, LANES), F32)])
    return pl.pallas_call(
        functools.partial(_flash_kernel, mode=mode, tq=tq, tk=tk, tt=tt, n_main_static=n_main_static,
                          q_pos0=q_pos0, lam_init=lam_init),
        grid_spec=grid_spec,
        out_shape=jax.ShapeDtypeStruct((bt, t_q, groups * LANES), BF16),
        compiler_params=_params(3),
    )(slopes, *args)


def _flash_prompt_kernel(slope_ref, q_ref, k_ref, vt_ref, *rest, mode, t, lam_init):
    if mode == "diff":
        lam_ref, subg_ref, o_ref = rest[:3]
        rest = rest[3:]
    else:
        o_ref = rest[0]
        rest = rest[1:]
    m_scr, l_scr, acc_scr, main_tile_scr, diag_tile_scr = rest[:5]
    st_scrs, msub_scrs, alpha_scrs = rest[5:7], rest[7:9], rest[9:11]
    head = pl.program_id(1)
    qi = pl.program_id(2)
    q = q_ref[...]
    halves = _half_masks()
    if mode == "diff":
        qs = [jnp.where(halves[i], q, jnp.zeros_like(q)) for i in range(2)]
        slope = slope_ref[head]
    else:
        qs = [q[:, :LANES], q[:, LANES:]]
    v_rows = acc_scr.shape[1]

    m_scr[...] = jnp.full(m_scr.shape, NEG, F32)
    l_scr[...] = jnp.zeros(l_scr.shape, F32)
    acc_scr[...] = jnp.zeros(acc_scr.shape, F32)

    @pl.when(qi == 0)
    def _():
        kj = lax.broadcasted_iota(jnp.int32, (t, t), 0)
        qj = lax.broadcasted_iota(jnp.int32, (t, t), 1)
        valid = (kj // CHUNK) <= (qj // CHUNK)
        if mode == "diff":
            main_tile_scr[...] = slope * (kj - qj).astype(F32)
            diag_tile_scr[...] = jnp.where(valid, -slope * jnp.abs(qj - kj).astype(F32), NEG)
        else:
            diag_tile_scr[...] = jnp.where(valid, 0.0, NEG).astype(F32)


    def score(kb, diag, slot):
        off = pl.multiple_of(kb * t, t)
        kblk = k_ref[pl.ds(off, t), :]
        if mode == "diff" and not diag:
            shift = -slope * ((qi - kb) * t).astype(F32)
        else:
            shift = 0.0
        for i in range(2):
            ki = kblk if mode == "diff" else kblk[:, i * LANES:(i + 1) * LANES]
            st = lax.dot_general(ki, qs[i], NT, preferred_element_type=F32)
            if diag:
                st = st + diag_tile_scr[...]
            elif mode == "diff":
                st = st + main_tile_scr[...]
            st_scrs[slot][i] = st
            m_old = m_scr[i]
            m_new = jnp.maximum(m_old, jnp.max(st, axis=0, keepdims=True) + shift)
            m_scr[i] = m_new
            msub_scrs[slot][i] = m_new - shift
            alpha_scrs[slot][i] = jnp.exp2(m_old - m_new)

    def accumulate(kb, slot):
        vt = vt_ref[kb]
        for i in range(2):
            alpha = alpha_scrs[slot][i]
            vi = vt if mode == "diff" else vt[i * v_rows:(i + 1) * v_rows, :]
            p = jnp.exp2(st_scrs[slot][i] - msub_scrs[slot][i])
            l_scr[i] = alpha * l_scr[i] + jnp.sum(p, axis=0, keepdims=True)
            acc_scr[i] = alpha * acc_scr[i] + jnp.dot(vi, p.astype(BF16), preferred_element_type=F32)

    lead = jnp.logical_and(qi >= 2, qi % 2 == 0).astype(jnp.int32)

    @pl.when(lead == 1)
    def _():
        score(0, False, 0)
        accumulate(0, 0)

    @pl.when(qi == 0)
    def _():
        score(0, True, 0)

    @pl.when(qi >= 1)
    def _():
        score(lead, False, 0)

    def pair(kb):
        score(kb + 1, False, 1)
        accumulate(kb, 0)
        score(kb + 2, False, 0)
        accumulate(kb + 1, 1)

    def quad_body(j, carry):
        pair(lead + 4 * j)
        pair(lead + 4 * j + 2)
        return carry

    def pair_body(j, carry):
        pair(lead + 2 * j)
        return carry

    n_pairs = jnp.maximum((qi - lead - 1) // 2, 0)
    lax.fori_loop(0, n_pairs // 2, quad_body, 0)
    lax.fori_loop(n_pairs - n_pairs % 2, n_pairs, pair_body, 0)

    @pl.when(qi >= 1)
    def _():
        score(qi, True, 1)
        accumulate(qi - 1, 0)
        accumulate(qi, 1)

    @pl.when(qi == 0)
    def _():
        accumulate(0, 0)

    o0 = acc_scr[0] / l_scr[0]
    o1 = acc_scr[1] / l_scr[1]
    if mode == "diff":
        lp = lam_ref[...]
        lam = (jnp.exp(jnp.sum(lp[0:1] * lp[1:2], axis=-1, keepdims=True))
               - jnp.exp(jnp.sum(lp[2:3] * lp[3:4], axis=-1, keepdims=True)) + lam_init)
        o = (o0 - lam * o1).T
        o_ref[...] = (_rms(o, subg_ref[...]) * (1.0 - lam_init)).astype(o_ref.dtype)
    else:
        o_ref[...] = jnp.concatenate([o0, o1], axis=0).T.astype(o_ref.dtype)


def flash_prompt(q, k, vt, *, mode, t, lam_p=None, sub_g=None, lam_init=0.0):
    bt, t_all = q.shape[0], q.shape[1]
    n_blk = t_all // t
    qw = LANES if mode == "diff" else 2 * LANES
    groups = q.shape[2] // qw
    assert vt.shape == (bt * n_blk, groups * LANES, t), vt.shape
    q_idx = lambda b, h, i, *_: (b, i, h)
    in_specs = [pl.BlockSpec((None, t, qw), q_idx),
                pl.BlockSpec((None, t_all, qw), lambda b, h, i, *_: (b, 0, h)),
                pl.BlockSpec((n_blk, LANES, t), lambda b, h, i, *_: (b, h, 0))]
    args = [q, k, vt]
    if mode == "diff":
        slopes = LOG2E * jnp.exp2(-8.0 * jnp.arange(1, groups + 1, dtype=F32) / groups)
        in_specs += [pl.BlockSpec(lam_p.shape, lambda b, h, i, *_: (0, 0)),
                     pl.BlockSpec((1, LANES), lambda b, h, i, *_: (0, 0))]
        args += [lam_p.astype(F32), sub_g.reshape(1, LANES).astype(F32)]
        v_rows = LANES
    else:
        slopes = jnp.zeros((groups,), F32)
        v_rows = LANES // 2
    stat = pltpu.VMEM((2, 1, t), F32)
    tile = pltpu.VMEM((t, t), F32)
    grid_spec = pltpu.PrefetchScalarGridSpec(
        num_scalar_prefetch=1,
        grid=(bt, groups, n_blk),
        in_specs=in_specs,
        out_specs=pl.BlockSpec((None, t, LANES), q_idx),
        scratch_shapes=[stat, stat, pltpu.VMEM((2, v_rows, t), F32), tile, tile,
                        pltpu.VMEM((2, t, t), F32), pltpu.VMEM((2, t, t), F32), stat, stat, stat, stat])
    return pl.pallas_call(
        functools.partial(_flash_prompt_kernel, mode=mode, t=t, lam_init=lam_init),
        grid_spec=grid_spec,
        out_shape=jax.ShapeDtypeStruct((bt, t_all, groups * LANES), BF16),
        compiler_params=_params(3),
    )(slopes, *args)


def _c_pre_kernel(cq_ref, ckv_ref, kra_ref, krb_ref, gq_ref, gkv_ref, wq_ref, wqr_ref,
                  cq_tab, sq_tab, ck_tab, sk_tab, q_out, lat_out, kr_out):
    qn = _rms(cq_ref[...], gq_ref[...]).astype(BF16)
    qc = jnp.dot(qn, wq_ref[...], preferred_element_type=F32)
    qr = jnp.dot(qn, wqr_ref[...], preferred_element_type=F32)
    cos, sin = cq_tab[...], sq_tab[...]
    for h in range(C_HEADS):
        sl = slice(h * LANES, (h + 1) * LANES)
        q_out[:, sl] = (qc[:, sl] * cos + qr[:, sl] * sin).astype(q_out.dtype)
    lat_out[...] = _rms(ckv_ref[...], gkv_ref[...])
    kr = kra_ref[...] * ck_tab[...] + krb_ref[...] * sk_tab[...]
    kr_out[...] = kr[:, :C_ROPE]


def c_pre(cq, ckv, kra, krb, gq, gkv, wq, wqr, tabs, t_len, tm=512):
    m = cq.shape[0]
    tm = _row_tile(t_len, tm)
    n_t = t_len // tm
    row = lambda i: (i, 0)
    fixed = lambda i: (0, 0)
    trow = lambda i: (i % n_t, 0)
    qw = wq.shape[1]
    return pl.pallas_call(
        _c_pre_kernel,
        grid=(m // tm,),
        in_specs=[pl.BlockSpec((tm, C_Q_LORA), row), pl.BlockSpec((tm, C_KV_LORA), row),
                  pl.BlockSpec((tm, LANES), row), pl.BlockSpec((tm, LANES), row),
                  pl.BlockSpec((1, C_Q_LORA), fixed), pl.BlockSpec((1, C_KV_LORA), fixed),
                  pl.BlockSpec(wq.shape, fixed), pl.BlockSpec(wqr.shape, fixed),
                  pl.BlockSpec((tm, LANES), trow), pl.BlockSpec((tm, LANES), trow),
                  pl.BlockSpec((tm, LANES), trow), pl.BlockSpec((tm, LANES), trow)],
        out_specs=[pl.BlockSpec((tm, qw), row), pl.BlockSpec((tm, C_KV_LORA), row),
                   pl.BlockSpec((tm, C_ROPE), row)],
        out_shape=[jax.ShapeDtypeStruct((m, qw), BF16), jax.ShapeDtypeStruct((m, C_KV_LORA), F32),
                   jax.ShapeDtypeStruct((m, C_ROPE), F32)],
        compiler_params=_params(1),
    )(cq, ckv, kra, krb, gq.reshape(1, -1).astype(F32), gkv.reshape(1, -1).astype(F32), wq, wqr, *tabs)


def _c_kv_kernel(lat_ref, kr_ref, wk_ref, wv_ref, place_ref, k_out, v_out, *, v_transposed):
    lat = lat_ref[...].astype(BF16)
    k = jnp.dot(lat, wk_ref[...], preferred_element_type=F32)
    k += jnp.dot(kr_ref[...].astype(BF16), place_ref[...], preferred_element_type=F32)
    k_out[...] = k.astype(k_out.dtype)
    if v_transposed:
        v = lax.dot_general(wv_ref[...], lat, NT, preferred_element_type=F32)
    else:
        v = jnp.dot(lat, wv_ref[...], preferred_element_type=F32)
    v_out[...] = v.astype(v_out.dtype)


def c_kv(lat, kr, wk, wv, place, v_transposed, tm=512):
    m = lat.shape[0]
    tm = _row_tile(m, tm)
    row = lambda i: (i, 0)
    fixed = lambda i: (0, 0)
    if v_transposed:
        n_v = wv.shape[0]
        v_spec = pl.BlockSpec((None, n_v, tm), lambda i: (i, 0, 0))
        v_shape = jax.ShapeDtypeStruct((m // tm, n_v, tm), BF16)
    else:
        n_v = wv.shape[1]
        v_spec = pl.BlockSpec((tm, n_v), row)
        v_shape = jax.ShapeDtypeStruct((m, n_v), BF16)
    return pl.pallas_call(
        functools.partial(_c_kv_kernel, v_transposed=v_transposed),
        grid=(m // tm,),
        in_specs=[pl.BlockSpec((tm, C_KV_LORA), row), pl.BlockSpec((tm, C_ROPE), row),
                  pl.BlockSpec(wk.shape, fixed), pl.BlockSpec(wv.shape, fixed), pl.BlockSpec(place.shape, fixed)],
        out_specs=[pl.BlockSpec((tm, wk.shape[1]), row), v_spec],
        out_shape=[jax.ShapeDtypeStruct((m, wk.shape[1]), BF16), v_shape],
        compiler_params=_params(1),
    )(lat, kr, wk, wv, place)


def _prep_in_ab(w):
    n_mix = w.shape[1] - MEM_W - w.shape[0]
    n_q = n_mix // 3
    scale = jnp.concatenate([jnp.full((n_q,), Q_SCALE, F32), jnp.ones((n_mix - n_q,), F32),
                             jnp.full((MEM_W,), Q_SCALE, F32), jnp.ones((w.shape[0],), F32)])
    return (w * scale[None, :]).astype(BF16)


def _prep_in_c(w):
    d = w.shape[0]
    o = C_Q_LORA + C_KV_LORA
    half = C_ROPE // 2
    kr = w[:, o:o + C_ROPE]
    rot = jnp.concatenate([-kr[:, half:], kr[:, :half]], axis=1)
    pad = jnp.zeros((d, LANES - C_ROPE), F32)
    mq = w[:, o + C_ROPE:o + C_ROPE + MEM_W] * (Q_SCALE)
    gate = w[:, o + C_ROPE + MEM_W:]
    return jnp.concatenate([w[:, :o], kr, pad, rot, pad, mq, gate], axis=1).astype(BF16)


def _prep_uq(w_uq):
    r = w_uq.shape[0]
    w = w_uq.reshape(r, C_HEADS, C_NOPE + C_ROPE)
    nope, rope = w[..., :C_NOPE], w[..., C_NOPE:]
    half = C_ROPE // 2
    rot = jnp.concatenate([-rope[..., half:], rope[..., :half]], axis=-1)
    pad = jnp.zeros((r, C_HEADS, LANES - C_NOPE - C_ROPE), F32)
    plain = jnp.concatenate([nope, rope, pad], axis=-1).reshape(r, C_HEADS * LANES)
    rotated = jnp.concatenate([jnp.zeros_like(nope), rot, pad], axis=-1).reshape(r, C_HEADS * LANES)
    return plain.astype(BF16), rotated.astype(BF16)


def _prep_ukv(w_ukv):
    r = w_ukv.shape[0]
    w = w_ukv.reshape(r, C_HEADS, C_NOPE + C_V)
    wk = jnp.concatenate([w[..., :C_NOPE], jnp.zeros((r, C_HEADS, LANES - C_NOPE), F32)], axis=-1)
    wv = w[..., C_NOPE:]
    eye = jnp.eye(C_ROPE, dtype=F32)
    place = jnp.concatenate([jnp.zeros((C_ROPE, C_NOPE), F32), eye,
                             jnp.zeros((C_ROPE, LANES - C_NOPE - C_ROPE), F32)], axis=1)
    place = jnp.tile(place, (1, C_HEADS))
    return (wk.reshape(r, C_HEADS * LANES).astype(BF16), wv.reshape(r, C_HEADS * C_V).astype(BF16),
            place.astype(BF16))


def _rope_tables(pos):
    half = C_ROPE // 2
    inv = jnp.exp(-math.log(ROPE_BASE) * jnp.arange(half, dtype=F32) * 2.0 / C_ROPE)
    ang = pos.astype(F32)[:, None] * inv[None, :]
    cos, sin = jnp.cos(ang), jnp.sin(ang)
    t = pos.shape[0]
    cos2 = jnp.concatenate([cos, cos], axis=1)
    sin2 = jnp.concatenate([sin, sin], axis=1)
    scale = LOG2E * (C_NOPE + C_ROPE) ** -0.5
    z = lambda n: jnp.zeros((t, n), F32)
    cq = jnp.concatenate([jnp.ones((t, C_NOPE), F32), cos2, z(LANES - C_NOPE - C_ROPE)], axis=1) * scale
    sq = jnp.concatenate([z(C_NOPE), sin2, z(LANES - C_NOPE - C_ROPE)], axis=1) * scale
    ck = jnp.concatenate([cos2, z(LANES - C_ROPE)], axis=1)
    sk = jnp.concatenate([sin2, z(LANES - C_ROPE)], axis=1)
    return cq, sq, ck, sk


def _band_bias_kernel(pos_ref, row_ref, o_ref, *, tq, win, n_valid, transposed):
    c = pl.program_id(0)
    q0, k0 = pos_ref[0, c], pos_ref[1, c]
    wp = row_ref.shape[-1]
    n_rows, n_cols = (win, tq) if transposed else (tq, win)
    rows = pltpu.roll(jnp.broadcast_to(row_ref[...], (n_rows, wp)), 0, 1, stride=1, stride_axis=0)
    ri = lax.broadcasted_iota(jnp.int32, (n_rows, n_cols), 0)
    ci = lax.broadcasted_iota(jnp.int32, (n_rows, n_cols), 1)
    qi, kj = (ci, ri) if transposed else (ri, ci)
    qp, kp = q0 + qi, k0 + kj
    qc, kc = qp // CHUNK, kp // CHUNK
    valid = (kp >= 0) & (kc <= qc) & (kc >= qc - A_BAND_CHUNKS) & (kj < n_valid)
    o_ref[...] = jnp.where(valid, rows[:, :n_cols], NEG)


def band_bias(rel_bias, q0, k0, *, tq, win, n_valid, transposed=False):
    n_cls = q0.shape[0]
    heads = rel_bias.shape[1]
    wp = -(-(tq + win) // LANES) * LANES
    mm = jnp.arange(wp, dtype=jnp.int32)
    mm = jnp.where(mm < (tq if transposed else win), mm, mm - wp)
    mm = mm if transposed else -mm
    rel = jnp.clip((q0 - k0)[:, None] + mm[None, :], -A_REL_CLIP, A_REL_CLIP) + A_REL_CLIP
    rows = jnp.moveaxis((LOG2E * rel_bias.astype(F32))[rel], -1, 1).reshape(n_cls, heads, 1, wp)
    pos = jnp.stack([q0, k0]).astype(jnp.int32)
    if transposed:
        out_spec = pl.BlockSpec((None, None, win, tq), lambda c, h, *_: (c, h // 2, 0, h % 2))
        out_shape = jax.ShapeDtypeStruct((n_cls, heads // 2, win, 2 * tq), F32)
    else:
        out_spec = pl.BlockSpec((None, None, tq, win), lambda c, h, *_: (c, h, 0, 0))
        out_shape = jax.ShapeDtypeStruct((n_cls, heads, tq, win), F32)
    grid_spec = pltpu.PrefetchScalarGridSpec(
        num_scalar_prefetch=1,
        grid=(n_cls, heads),
        in_specs=[pl.BlockSpec((None, None, 1, wp), lambda c, h, *_: (c, h, 0, 0))],
        out_specs=out_spec)
    return pl.pallas_call(
        functools.partial(_band_bias_kernel, tq=tq, win=win, n_valid=n_valid, transposed=transposed),
        grid_spec=grid_spec,
        out_shape=out_shape,
        compiler_params=_params(2),
    )(pos, rows)


A_TQ = 256
A_SUB = 4
FLASH_T = 512
FAR = 1 << 24


def _trunk(x, pos0, mem_k, mem_v, past, wts):
    bt, t, d = x.shape
    m = bt * t
    depth = len(wts["w_out"])
    x2 = x.reshape(m, d)
    new_a, new_b, new_c = [], [], []
    pos = pos0 + jnp.arange(t, dtype=jnp.int32)
    one = lambda width, dt: (width, (dt,))
    for i in range(depth):
        kind, j = i % N_MIXERS, i // N_MIXERS
        if kind == 0:
            hd = A_HEADS * HEAD_DIM
            w_in = wts["w_in_a"][j]
            if past is None:
                tq = min(A_TQ, t)
                q, k, k16, v, mq, gate, vt = norm_matmul(
                    x2, wts["norm_g"][i], w_in,
                    (one(hd, BF16), (hd, (F32, BF16)), one(hd, F32), one(MEM_W, BF16), one(d, BF16)),
                    wt=w_in[:, 2 * hd:3 * hd].T, t_blk=tq)
                k3, v3 = k.reshape(bt, t, hd), v.reshape(bt, t, hd)
                back_blocks = A_PAST_ROWS // tq
                n_win = min(back_blocks + 1, t // tq)
                win = n_win * tq
                n_cls = n_win
                q0 = jnp.arange(n_cls, dtype=jnp.int32) * tq
                k0 = jnp.clip(q0 - A_PAST_ROWS, 0, t - win)
                bias = band_bias(wts["rel_bias_a"][j], q0, k0, tq=tq, win=win, n_valid=win, transposed=True)
                br = attn_window_t(q.reshape(bt, t, hd), k16.reshape(bt, t, hd), vt, bias,
                                   tq=tq, n_sub=min(A_SUB, t // tq), n_win=n_win, back_blocks=back_blocks)
                keep = min(A_PAST_ROWS, t)
                new_a.append((k3[:, t - keep:].reshape(bt, keep, A_HEADS, HEAD_DIM),
                              v3[:, t - keep:].reshape(bt, keep, A_HEADS, HEAD_DIM)))
            else:
                q, k, v, mq, gate = norm_matmul(
                    x2, wts["norm_g"][i], w_in,
                    (one(hd, BF16), one(hd, F32), one(hd, F32), one(MEM_W, BF16), one(d, BF16)))
                q3, k3, v3 = (a.reshape(bt, t, hd) for a in (q, k, v))
                ck, cv = past[0][j], past[1][j]
                rows = ck.shape[1]
                n_keys = rows + t
                pad = (-n_keys) % LANES
                zeros = jnp.zeros((bt, pad, hd), F32)
                kk = jnp.concatenate([ck.reshape(bt, rows, hd), k3, zeros], axis=1)
                vv = jnp.concatenate([cv.reshape(bt, rows, hd), v3, zeros], axis=1)
                start = jnp.full((1,), pos0, jnp.int32)
                bias = band_bias(wts["rel_bias_a"][j], start, start - rows, tq=t, win=n_keys + pad,
                                 n_valid=n_keys)
                br = attn_window(q3, kk, vv, bias, tq=t, win=n_keys + pad, back=0)
                new_a.append((k3.reshape(bt, t, A_HEADS, HEAD_DIM), v3.reshape(bt, t, A_HEADS, HEAD_DIM)))
        elif kind == 1:
            hd = B_HEADS * 2 * HEAD_DIM
            lam_init = 0.8 - 0.6 * math.exp(-0.3 * i)
            common = dict(mode="diff", lam_p=wts["lambda_b"][j], sub_g=wts["subln_g_b"][j], lam_init=lam_init)
            w_in = wts["w_in_b"][j]
            if past is None:
                q, k, k16, v, mq, gate, vt = norm_matmul(
                    x2, wts["norm_g"][i], w_in,
                    (one(hd, BF16), (hd, (F32, BF16)), one(hd, F32), one(MEM_W, BF16), one(d, BF16)),
                    wt=w_in[:, 2 * hd:3 * hd].T, t_blk=FLASH_T, tm=FLASH_T)
                br = flash_prompt(q.reshape(bt, t, hd), k16.reshape(bt, t, hd), vt, t=FLASH_T, **common)
            else:
                q, k, v, mq, gate = norm_matmul(
                    x2, wts["norm_g"][i], w_in,
                    (one(hd, BF16), one(hd, F32), one(hd, F32), one(MEM_W, BF16), one(d, BF16)))
                ck, cv = past[2][j], past[3][j]
                rows = ck.shape[1]
                br = flash_attention(q.reshape(bt, t, hd), ck.reshape(bt, rows, hd), cv.reshape(bt, rows, hd),
                                     k.reshape(bt, t, hd), v.reshape(bt, t, hd),
                                     tq=t, tk=FLASH_T, diagonal_tail=False, q_pos0=rows, **common)
            new_b.append((k.reshape(bt, t, B_HEADS, 2, HEAD_DIM), v.reshape(bt, t, B_HEADS, 2 * HEAD_DIM)))
        else:
            cq, ckv, kra, krb, mq, gate = norm_matmul(
                x2, wts["norm_g"][i], wts["w_in_c"][j],
                (one(C_Q_LORA, F32), one(C_KV_LORA, F32), one(LANES, F32), one(LANES, F32),
                 one(MEM_W, BF16), one(d, BF16)))
            wq, wqr = wts["w_uq_c"][j]
            wk, wv, place = wts["w_ukv_c"][j]
            qcat, lat, kr = c_pre(cq, ckv, kra, krb, wts["q_norm_g_c"][j], wts["kv_norm_g_c"][j],
                                  wq, wqr, _rope_tables(pos), t)
            q3 = qcat.reshape(bt, t, -1)
            if past is None:
                kcat, vt = c_kv(lat, kr, wk, wv.T, place, v_transposed=True, tm=FLASH_T)
                br = flash_prompt(q3, kcat.reshape(bt, t, -1), vt, mode="mla", t=FLASH_T)
            else:
                kcat, vcat = c_kv(lat, kr, wk, wv, place, v_transposed=False)
                cl, cr = past[4][j], past[5][j]
                rows = cl.shape[1]
                kc_, vc_ = c_kv(cl.reshape(bt * rows, -1), cr.reshape(bt * rows, -1), wk, wv, place,
                                v_transposed=False)
                br = flash_attention(q3, kc_.reshape(bt, rows, -1), vc_.reshape(bt, rows, -1),
                                     kcat.reshape(bt, t, -1), vcat.reshape(bt, t, -1),
                                     mode="mla", tq=t, tk=FLASH_T, diagonal_tail=False, q_pos0=rows)
            new_c.append((lat.reshape(bt, t, -1), kr.reshape(bt, t, -1)))
        if past is None:
            mo = attn_window_t(mq.reshape(bt, t, MEM_W), mem_k[i], mem_v[i], None,
                               tq=min(FLASH_T, t) // 2, n_sub=2, n_win=1, back_blocks=0)
        else:
            mo = attn_window(mq.reshape(bt, t, MEM_W), mem_k[i], mem_v[i], None,
                             tq=t, win=mem_k[i].shape[1], back=0)
        last = i == depth - 1
        x2 = gated_out(br.reshape(m, -1), mo.reshape(m, MEM_W), gate, x2, wts["w_out"][i],
                       wts["final_g"], final=last)
    return x2.reshape(bt, t, d), new_a, new_b, new_c


def kernel(x_prompt, x_sample, cache_a_k, cache_a_v, cache_b_k, cache_b_v, cache_c_lat, cache_c_rope,
           cache_mem_k, cache_mem_v, mem_prompt, norm_g, final_g, mem_norm_g, w_mem_kv, w_out, w_in_a,
           rel_bias_a, w_in_b, lambda_b, subln_g_b, w_in_c, q_norm_g_c, kv_norm_g_c, w_uq_c, w_ukv_c):
    depth = w_out.shape[0]
    bp, n_mem, d = mem_prompt.shape
    mem_heads = MEM_W // HEAD_DIM
    wts = dict(
        norm_g=norm_g, final_g=final_g, rel_bias_a=rel_bias_a, lambda_b=lambda_b, subln_g_b=subln_g_b,
        q_norm_g_c=q_norm_g_c, kv_norm_g_c=kv_norm_g_c,
        w_out=[w_out[i].astype(BF16) for i in range(depth)],
        w_in_a=[_prep_in_ab(w_in_a[j]) for j in range(w_in_a.shape[0])],
        w_in_b=[_prep_in_ab(w_in_b[j]) for j in range(w_in_b.shape[0])],
        w_in_c=[_prep_in_c(w_in_c[j]) for j in range(w_in_c.shape[0])],
        w_uq_c=[_prep_uq(w_uq_c[j]) for j in range(w_uq_c.shape[0])],
        w_ukv_c=[_prep_ukv(w_ukv_c[j]) for j in range(w_ukv_c.shape[0])],
    )
    mem2 = mem_prompt.reshape(bp * n_mem, d)
    mem_k_p, mem_v_p, mem_vt_p = [], [], []
    for i in range(depth):
        w_kv = w_mem_kv[i].astype(BF16)
        mk, mv, mvt = norm_matmul(mem2, mem_norm_g[i], w_kv, ((MEM_W, (F32,)), (MEM_W, (F32,))),
                                  wt=w_kv[:, MEM_W:].T, t_blk=n_mem, tm=n_mem)
        mem_k_p.append(mk.reshape(bp, n_mem, MEM_W))
        mem_v_p.append(mv.reshape(bp, n_mem, MEM_W))
        mem_vt_p.append(mvt)

    y_p, na_p, nb_p, nc_p = _trunk(x_prompt, 0, mem_k_p, mem_vt_p, None, wts)

    bs = x_sample.shape[0]
    mem_k_s = [cache_mem_k[i].reshape(bs, n_mem, MEM_W) for i in range(depth)]
    mem_v_s = [cache_mem_v[i].reshape(bs, n_mem, MEM_W) for i in range(depth)]
    past = (cache_a_k, cache_a_v, cache_b_k, cache_b_v, cache_c_lat, cache_c_rope)
    y_s, na_s, nb_s, nc_s = _trunk(x_sample, cache_b_k.shape[2], mem_k_s, mem_v_s, past, wts)

    stk = lambda lst, n: jnp.stack([s[n] for s in lst])
    heads4 = lambda lst: jnp.stack([a.reshape(bp, n_mem, mem_heads, HEAD_DIM) for a in lst])
    return (y_p, y_s,
            stk(na_p, 0), stk(na_p, 1), stk(na_s, 0), stk(na_s, 1),
            stk(nb_p, 0), stk(nb_p, 1), stk(nb_s, 0), stk(nb_s, 1),
            stk(nc_p, 0), stk(nc_p, 1), stk(nc_s, 0), stk(nc_s, 1),
            heads4(mem_k_p), heads4(mem_v_p))
```

```python
import functools
import math

import jax
import jax.numpy as jnp
from jax import lax
from jax.experimental import pallas as pl
from jax.experimental.pallas import tpu as pltpu

F32 = jnp.float32
BF16 = jnp.bfloat16

LANES = 128
VMEM_LIMIT = 56 * 1024 * 1024

EPS = 1e-6
NEG = -1e30
LOG2E = math.log2(math.e)
CHUNK = 64
HEAD_DIM = 64
Q_SCALE = LOG2E * HEAD_DIM ** -0.5
MEM_W = 256
A_HEADS = 12
A_PAST_ROWS = 512
A_BAND_CHUNKS = 8
A_REL_CLIP = 128
B_HEADS = 6
C_HEADS = 12
C_NOPE = 64
C_ROPE = 32
C_V = 64
C_Q_LORA = 384
C_KV_LORA = 256
ROPE_BASE = 10000.0
N_MIXERS = 3


def _params(n_grid):
    return pltpu.CompilerParams(dimension_semantics=("arbitrary",) * n_grid,
                                vmem_limit_bytes=VMEM_LIMIT)


def _row_tile(m, want):
    t = min(m, want)
    assert m % t == 0, (m, t)
    return t


def _rms(x, g):
    return x * lax.rsqrt(jnp.mean(x * x, axis=-1, keepdims=True) + EPS) * g


NT = (((1,), (1,)), ((), ()))


def _norm_matmul_kernel(x_ref, g_ref, w_ref, *rest, segs, has_t):
    h = _rms(x_ref[...], g_ref[...]).astype(BF16)
    out_refs = rest[1:] if has_t else rest
    off = n_out = 0
    for width, dtypes in segs:
        r = jnp.dot(h, w_ref[:, off:off + width], preferred_element_type=F32)
        for dt in dtypes:
            out_refs[n_out][...] = r.astype(dt)
            n_out += 1
        off += width
    if has_t:
        o_ref = out_refs[n_out]
        r = lax.dot_general(rest[0][...], h, NT, preferred_element_type=F32).astype(o_ref.dtype)
        blk = o_ref.shape[-1]
        for jb in range(o_ref.shape[0]):
            o_ref[jb] = r[:, jb * blk:(jb + 1) * blk]


def norm_matmul(x, g, w, segs, wt=None, t_blk=None, tm=512):
    m, d = x.shape
    tm = _row_tile(m, tm)
    n = w.shape[1]
    assert n == sum(wd for wd, _ in segs)
    row = lambda i: (i, 0)
    fixed = lambda i: (0, 0)
    in_specs = [pl.BlockSpec((tm, d), row), pl.BlockSpec((1, d), fixed), pl.BlockSpec((d, n), fixed)]
    args = [x, g.reshape(1, d).astype(F32), w]
    out_specs = [pl.BlockSpec((tm, wd), row) for wd, dts in segs for _ in dts]
    out_shape = [jax.ShapeDtypeStruct((m, wd), dt) for wd, dts in segs for dt in dts]
    if wt is not None:
        in_specs.append(pl.BlockSpec(wt.shape, fixed))
        args.append(wt)
        assert tm % t_blk == 0
        out_specs.append(pl.BlockSpec((tm // t_blk, wt.shape[0], t_blk), lambda i: (i, 0, 0)))
        out_shape.append(jax.ShapeDtypeStruct((m // t_blk, wt.shape[0], t_blk), BF16))
    return pl.pallas_call(
        functools.partial(_norm_matmul_kernel, segs=tuple(segs), has_t=wt is not None),
        grid=(m // tm,),
        in_specs=in_specs,
        out_specs=out_specs,
        out_shape=out_shape,
        compiler_params=_params(1),
    )(*args)


def _gated_out_kernel(br_ref, mq_ref, gate_ref, x_ref, w_ref, fg_ref, mk_ref, mvt_ref, o_ref, *, final):
    nb = br_ref.shape[-1]
    tm = x_ref.shape[0]
    halves = _half_masks()
    mq = mq_ref[...]
    mem = []
    for pair in range(mq.shape[1] // LANES):
        cols = slice(pair * LANES, (pair + 1) * LANES)
        q = mq[:, cols]
        q2 = jnp.concatenate([jnp.where(halves[h], q, jnp.zeros_like(q)) for h in range(2)], axis=0)
        st = lax.dot_general(mk_ref[:, cols].astype(BF16), q2, NT, preferred_element_type=F32)
        p = jnp.exp2(st - jnp.max(st, axis=0, keepdims=True))
        l = jnp.sum(p, axis=0, keepdims=True)
        o = jnp.dot(mvt_ref[cols, :], p.astype(BF16), preferred_element_type=F32) / l
        mem.append(jnp.concatenate([o[:HEAD_DIM, :tm], o[HEAD_DIM:, tm:]], axis=0).T)
    mo = jnp.concatenate(mem, axis=1)
    gate = gate_ref[...].astype(F32)
    sg = gate * jax.nn.sigmoid(gate)
    y1 = (br_ref[...].astype(F32) * sg[:, :nb]).astype(BF16)
    y2 = (mo * sg[:, nb:]).astype(BF16)
    acc = jnp.dot(y1, w_ref[:nb, :], preferred_element_type=F32)
    acc += jnp.dot(y2, w_ref[nb:, :], preferred_element_type=F32)
    xn = x_ref[...] + acc
    o_ref[...] = _rms(xn, fg_ref[...]) if final else xn


def gated_out(br, mq, gate, x, w, final_g, mem_k, mem_vt, final, tm=512):
    m, d = x.shape
    t_len = m // mem_k.shape[0]
    tm = _row_tile(t_len, tm)
    n_t = t_len // tm
    nb, nm, ng = br.shape[1], mq.shape[1], gate.shape[1]
    row = lambda i: (i, 0)
    fixed = lambda i: (0, 0)
    per_batch = lambda i: (i // n_t, 0, 0)
    return pl.pallas_call(
        functools.partial(_gated_out_kernel, final=final),
        grid=(m // tm,),
        in_specs=[pl.BlockSpec((tm, nb), row), pl.BlockSpec((tm, nm), row), pl.BlockSpec((tm, ng), row),
                  pl.BlockSpec((tm, d), row), pl.BlockSpec((ng, d), fixed), pl.BlockSpec((1, d), fixed),
                  pl.BlockSpec((None,) + mem_k.shape[1:], per_batch),
                  pl.BlockSpec((None,) + mem_vt.shape[1:], per_batch)],
        out_specs=pl.BlockSpec((tm, d), row),
        out_shape=jax.ShapeDtypeStruct((m, d), F32),
        compiler_params=_params(1),
    )(br, mq, gate, x, w, final_g.reshape(1, d).astype(F32), mem_k, mem_vt)


def _half_masks():
    lane = lax.broadcasted_iota(jnp.int32, (1, LANES), 1)
    lo = lane < HEAD_DIM
    return lo, jnp.logical_not(lo)


def _attn_window_kernel(q_ref, k_ref, v_ref, *rest, tq, win, back):
    if len(rest) == 2:
        bias_ref, o_ref = rest
    else:
        bias_ref, (o_ref,) = None, rest
    tk_all = k_ref.shape[0]
    if win == tk_all:
        k = k_ref[...]
        v = v_ref[...]
    else:
        start = jnp.clip(pl.program_id(2) * tq - back, 0, tk_all - win)
        start = pl.multiple_of(start, CHUNK)
        k = k_ref[pl.ds(start, win), :]
        v = v_ref[pl.ds(start, win), :]
    k = k.astype(BF16)
    v = v.astype(BF16)
    q = q_ref[...]
    halves = _half_masks()
    outs = []
    for i in range(2):
        qm = jnp.where(halves[i], q, jnp.zeros_like(q))
        s = lax.dot_general(qm, k, (((1,), (1,)), ((), ())), preferred_element_type=F32)
        if bias_ref is not None:
            s = s + bias_ref[i]
        m = jnp.max(s, axis=-1, keepdims=True)
        p = jnp.exp2(s - m)
        l = jnp.sum(p, axis=-1, keepdims=True)
        o = jnp.dot(p.astype(BF16), v, preferred_element_type=F32)
        outs.append(o / l)
    o_ref[...] = jnp.where(halves[0], outs[0], outs[1]).astype(o_ref.dtype)


def attn_window(q, k, v, bias, *, tq, win, back):
    bt, t_q, gw = q.shape
    g = gw // LANES
    t_k = k.shape[1]
    tq = _row_tile(t_q, tq)
    in_specs = [pl.BlockSpec((None, tq, LANES), lambda b, h, i: (b, i, h)),
                pl.BlockSpec((None, t_k, LANES), lambda b, h, i: (b, 0, h)),
                pl.BlockSpec((None, t_k, LANES), lambda b, h, i: (b, 0, h))]
    args = [q, k, v]
    if bias is not None:
        last = bias.shape[0] - 1
        in_specs.append(pl.BlockSpec((None, 2, tq, win), lambda b, h, i: (jnp.minimum(i, last), h, 0, 0)))
        args.append(bias)
    return pl.pallas_call(
        functools.partial(_attn_window_kernel, tq=tq, win=win, back=back),
        grid=(bt, g, t_q // tq),
        in_specs=in_specs,
        out_specs=pl.BlockSpec((None, tq, LANES), lambda b, h, i: (b, i, h)),
        out_shape=jax.ShapeDtypeStruct((bt, t_q, gw), BF16),
        compiler_params=_params(3),
    )(*args)


def _attn_window_t_kernel(q_ref, k_ref, vt_ref, *rest, tq, n_sub, n_win, back_blocks):
    if len(rest) == 2:
        bias_ref, o_ref = rest
    else:
        bias_ref, (o_ref,) = None, rest
    n_blk, _, blk = vt_ref.shape
    halves = _half_masks()

    def score(sub):
        i = pl.program_id(2) * n_sub + sub
        if n_win == n_blk:
            b0 = 0
            k = k_ref[...]
        else:
            b0 = jnp.clip(i - back_blocks, 0, n_blk - n_win)
            k = k_ref[pl.ds(pl.multiple_of(b0 * blk, blk), n_win * blk), :]
        q = q_ref[sub * tq:(sub + 1) * tq, :]
        q2 = jnp.concatenate([jnp.where(halves[h], q, jnp.zeros_like(q)) for h in range(2)], axis=0)
        st = lax.dot_general(k.astype(BF16), q2, NT, preferred_element_type=F32)
        if bias_ref is not None:
            st = st + bias_ref[jnp.minimum(i, bias_ref.shape[0] - 1)]
        return b0, st, jnp.max(st, axis=0, keepdims=True)

    def accumulate(sub, b0, st, m):
        p = jnp.exp2(st - m)
        l = jnp.sum(p, axis=0, keepdims=True)
        pb = p.astype(BF16)
        o = None
        for j in range(n_win):
            oj = jnp.dot(vt_ref[b0 + j], pb[j * blk:(j + 1) * blk, :], preferred_element_type=F32)
            o = oj if o is None else o + oj
        o = o / l
        o = jnp.concatenate([o[:HEAD_DIM, :tq], o[HEAD_DIM:, tq:]], axis=0).T
        o_ref[sub * tq:(sub + 1) * tq, :] = o.astype(o_ref.dtype)

    staged = [score(sub) for sub in range(n_sub)]
    for sub in range(n_sub):
        accumulate(sub, *staged[sub])


def attn_window_t(q, k, vt, bias, *, tq, n_sub, n_win, back_blocks):
    bt, t_q, gw = q.shape
    g = gw // LANES
    t_k = k.shape[1]
    blk = vt.shape[2]
    n_blk = t_k // blk
    assert vt.shape == (bt * n_blk, gw, blk), vt.shape
    rows = n_sub * tq
    assert t_q % rows == 0
    in_specs = [pl.BlockSpec((None, rows, LANES), lambda b, h, i: (b, i, h)),
                pl.BlockSpec((None, t_k, LANES), lambda b, h, i: (b, 0, h)),
                pl.BlockSpec((n_blk, LANES, blk), lambda b, h, i: (b, h, 0))]
    args = [q, k, vt]
    if bias is not None:
        in_specs.append(pl.BlockSpec((bias.shape[0], None, n_win * blk, 2 * tq), lambda b, h, i: (0, h, 0, 0)))
        args.append(bias)
    return pl.pallas_call(
        functools.partial(_attn_window_t_kernel, tq=tq, n_sub=n_sub, n_win=n_win, back_blocks=back_blocks),
        grid=(bt, g, t_q // rows),
        in_specs=in_specs,
        out_specs=pl.BlockSpec((None, rows, LANES), lambda b, h, i: (b, i, h)),
        out_shape=jax.ShapeDtypeStruct((bt, t_q, gw), BF16),
        compiler_params=_params(3),
    )(*args)


def _flash_kernel(slope_ref, q_ref, km_ref, vm_ref, kt_ref, vt_ref, *rest,
                  mode, tq, tk, tt, n_main_static, q_pos0, lam_init):
    if mode == "diff":
        lam_ref, subg_ref, o_ref, m_scr, l_scr, acc_scr = rest
    else:
        o_ref, m_scr, l_scr, acc_scr = rest
    head = pl.program_id(1)
    qi = pl.program_id(2)
    q = q_ref[...]
    halves = _half_masks()
    if mode == "diff":
        qs = [jnp.where(halves[i], q, jnp.zeros_like(q)) for i in range(2)]
        slope = slope_ref[head]
    else:
        qs = [q[:, :LANES], q[:, LANES:]]
        slope = None

    m_scr[...] = jnp.full(m_scr.shape, NEG, F32)
    l_scr[...] = jnp.zeros(l_scr.shape, F32)
    acc_scr[...] = jnp.zeros(acc_scr.shape, F32)

    def update(i, s, shift, v):
        m_old = m_scr[i]
        m_new = jnp.maximum(m_old, jnp.max(s, axis=-1, keepdims=True) + shift)
        alpha = jnp.exp2(m_old - m_new)
        p = jnp.exp2(s - (m_new - shift))
        l_scr[i] = alpha * l_scr[i] + jnp.sum(p, axis=-1, keepdims=True)
        acc_scr[i] = alpha * acc_scr[i] + jnp.dot(p.astype(BF16), v, preferred_element_type=F32)
        m_scr[i] = m_new

    def k_of(kblk, i):
        return kblk if mode == "diff" else kblk[:, i * LANES:(i + 1) * LANES]

    nt = (((1,), (1,)), ((), ()))
    q_base = q_pos0 + qi * tq

    if mode == "diff":
        ri = lax.broadcasted_iota(jnp.int32, (tq, tk), 0)
        ci = lax.broadcasted_iota(jnp.int32, (tq, tk), 1)
        main_tile = slope * (ci - ri).astype(F32)

    def main_body(kb, carry):
        off = pl.multiple_of(kb * tk, tk)
        kblk = km_ref[pl.ds(off, tk), :].astype(BF16)
        vblk = vm_ref[pl.ds(off, tk), :].astype(BF16)
        for i in range(2):
            s = lax.dot_general(qs[i], k_of(kblk, i), nt, preferred_element_type=F32)
            if mode == "diff":
                shift = -slope * (q_base - kb * tk).astype(F32)
                update(i, s + main_tile, shift, vblk)
            else:
                update(i, s, 0.0, vblk)
        return carry

    n_main = qi if n_main_static is None else n_main_static
    lax.fori_loop(0, n_main, main_body, 0)

    ri = lax.broadcasted_iota(jnp.int32, (tq, tt), 0)
    ci = lax.broadcasted_iota(jnp.int32, (tq, tt), 1)
    valid = (ci // CHUNK) <= (ri // CHUNK)
    if mode == "diff":
        tail_tile = jnp.where(valid, -slope * jnp.abs(ri - ci).astype(F32), NEG)
    else:
        tail_tile = jnp.where(valid, 0.0, NEG).astype(F32)
    kblk = kt_ref[...].astype(BF16)
    vblk = vt_ref[...].astype(BF16)
    for i in range(2):
        s = lax.dot_general(qs[i], k_of(kblk, i), nt, preferred_element_type=F32)
        update(i, s + tail_tile, 0.0, vblk)

    o0 = acc_scr[0] / l_scr[0]
    o1 = acc_scr[1] / l_scr[1]
    if mode == "diff":
        lp = lam_ref[...]
        lam = (jnp.exp(jnp.sum(lp[0:1] * lp[1:2], axis=-1, keepdims=True))
               - jnp.exp(jnp.sum(lp[2:3] * lp[3:4], axis=-1, keepdims=True)) + lam_init)
        o = o0 - lam * o1
        o_ref[...] = (_rms(o, subg_ref[...]) * (1.0 - lam_init)).astype(o_ref.dtype)
    else:
        o_ref[...] = jnp.where(halves[0], o0, o1).astype(o_ref.dtype)


def flash_attention(q, k_main, v_main, k_tail, v_tail, *, mode, tq, tk, diagonal_tail, q_pos0,
                    lam_p=None, sub_g=None, lam_init=0.0):
    bt, t_q = q.shape[0], q.shape[1]
    tq = _row_tile(t_q, tq)
    qw = LANES if mode == "diff" else 2 * LANES
    groups = q.shape[2] // qw
    t_main = k_main.shape[1]
    tk = _row_tile(t_main, tk)
    if diagonal_tail:
        assert tk == tq
        tt, n_main_static = tq, None
        tail_idx = lambda b, h, i, *_: (b, i, h)
    else:
        tt, n_main_static = k_tail.shape[1], t_main // tk
        tail_idx = lambda b, h, i, *_: (b, 0, h)
    q_idx = lambda b, h, i, *_: (b, i, h)
    main_idx = lambda b, h, i, *_: (b, 0, h)
    in_specs = [pl.BlockSpec((None, tq, qw), q_idx),
                pl.BlockSpec((None, t_main, qw), main_idx),
                pl.BlockSpec((None, t_main, LANES), main_idx),
                pl.BlockSpec((None, tt, qw), tail_idx),
                pl.BlockSpec((None, tt, LANES), tail_idx)]
    args = [q, k_main, v_main, k_tail, v_tail]
    if mode == "diff":
        slopes = LOG2E * jnp.exp2(-8.0 * jnp.arange(1, groups + 1, dtype=F32) / groups)
        in_specs += [pl.BlockSpec(lam_p.shape, lambda b, h, i, *_: (0, 0)),
                     pl.BlockSpec((1, LANES), lambda b, h, i, *_: (0, 0))]
        args += [lam_p.astype(F32), sub_g.reshape(1, LANES).astype(F32)]
    else:
        slopes = jnp.zeros((groups,), F32)
    grid_spec = pltpu.PrefetchScalarGridSpec(
        num_scalar_prefetch=1,
        grid=(bt, groups, t_q // tq),
        in_specs=in_specs,
        out_specs=pl.BlockSpec((None, tq, LANES), q_idx),
        scratch_shapes=[pltpu.VMEM((2, tq, 1), F32), pltpu.VMEM((2, tq, 1), F32),
                        pltpu.VMEM((2, tq, LANES), F32)])
    return pl.pallas_call(
        functools.partial(_flash_kernel, mode=mode, tq=tq, tk=tk, tt=tt, n_main_static=n_main_static,
                          q_pos0=q_pos0, lam_init=lam_init),
        grid_spec=grid_spec,
        out_shape=jax.ShapeDtypeStruct((bt, t_q, groups * LANES), BF16),
        compiler_params=_params(3),
    )(slopes, *args)


def _flash_prompt_kernel(slope_ref, q_ref, k_ref, vt_ref, *rest, mode, t, lam_init):
    if mode == "diff":
        lam_ref, subg_ref, o_ref = rest[:3]
        rest = rest[3:]
    else:
        o_ref = rest[0]
        rest = rest[1:]
    m_scr, l_scr, acc_scr, main_tile_scr, diag_tile_scr = rest[:5]
    st_scrs, msub_scrs, alpha_scrs = rest[5:7], rest[7:9], rest[9:11]
    head = pl.program_id(1)
    qi = pl.program_id(2)
    q = q_ref[...]
    halves = _half_masks()
    if mode == "diff":
        qs = [jnp.where(halves[i], q, jnp.zeros_like(q)) for i in range(2)]
        slope = slope_ref[head]
    else:
        qs = [q[:, :LANES], q[:, LANES:]]
    v_rows = acc_scr.shape[1]

    m_scr[...] = jnp.full(m_scr.shape, NEG, F32)
    l_scr[...] = jnp.zeros(l_scr.shape, F32)
    acc_scr[...] = jnp.zeros(acc_scr.shape, F32)

    @pl.when(qi == 0)
    def _():
        kj = lax.broadcasted_iota(jnp.int32, (t, t), 0)
        qj = lax.broadcasted_iota(jnp.int32, (t, t), 1)
        valid = (kj // CHUNK) <= (qj // CHUNK)
        if mode == "diff":
            main_tile_scr[...] = slope * (kj - qj).astype(F32)
            diag_tile_scr[...] = jnp.where(valid, -slope * jnp.abs(qj - kj).astype(F32), NEG)
        else:
            diag_tile_scr[...] = jnp.where(valid, 0.0, NEG).astype(F32)


    def score(kb, diag, slot):
        off = pl.multiple_of(kb * t, t)
        kblk = k_ref[pl.ds(off, t), :]
        if mode == "diff" and not diag:
            shift = -slope * ((qi - kb) * t).astype(F32)
        else:
            shift = 0.0
        for i in range(2):
            ki = kblk if mode == "diff" else kblk[:, i * LANES:(i + 1) * LANES]
            st = lax.dot_general(ki, qs[i], NT, preferred_element_type=F32)
            if diag:
                st = st + diag_tile_scr[...]
            elif mode == "diff":
                st = st + main_tile_scr[...]
            st_scrs[slot][i] = st
            m_old = m_scr[i]
            m_new = jnp.maximum(m_old, jnp.max(st, axis=0, keepdims=True) + shift)
            m_scr[i] = m_new
            msub_scrs[slot][i] = m_new - shift
            alpha_scrs[slot][i] = jnp.exp2(m_old - m_new)

    def accumulate(kb, slot):
        vt = vt_ref[kb]
        for i in range(2):
            alpha = alpha_scrs[slot][i]
            vi = vt if mode == "diff" else vt[i * v_rows:(i + 1) * v_rows, :]
            p = jnp.exp2(st_scrs[slot][i] - msub_scrs[slot][i])
            l_scr[i] = alpha * l_scr[i] + jnp.sum(p, axis=0, keepdims=True)
            acc_scr[i] = alpha * acc_scr[i] + jnp.dot(vi, p.astype(BF16), preferred_element_type=F32)

    lead = jnp.logical_and(qi >= 2, qi % 2 == 0).astype(jnp.int32)

    @pl.when(lead == 1)
    def _():
        score(0, False, 0)
        accumulate(0, 0)

    @pl.when(qi == 0)
    def _():
        score(0, True, 0)

    @pl.when(qi >= 1)
    def _():
        score(lead, False, 0)

    def pair(kb):
        score(kb + 1, False, 1)
        accumulate(kb, 0)
        score(kb + 2, False, 0)
        accumulate(kb + 1, 1)

    def quad_body(j, carry):
        pair(lead + 4 * j)
        pair(lead + 4 * j + 2)
        return carry

    def pair_body(j, carry):
        pair(lead + 2 * j)
        return carry

    n_pairs = jnp.maximum((qi - lead - 1) // 2, 0)
    lax.fori_loop(0, n_pairs // 2, quad_body, 0)
    lax.fori_loop(n_pairs - n_pairs % 2, n_pairs, pair_body, 0)

    @pl.when(qi >= 1)
    def _():
        score(qi, True, 1)
        accumulate(qi - 1, 0)
        accumulate(qi, 1)

    @pl.when(qi == 0)
    def _():
        accumulate(0, 0)

    o0 = acc_scr[0] / l_scr[0]
    o1 = acc_scr[1] / l_scr[1]
    if mode == "diff":
        lp = lam_ref[...]
        lam = (jnp.exp(jnp.sum(lp[0:1] * lp[1:2], axis=-1, keepdims=True))
               - jnp.exp(jnp.sum(lp[2:3] * lp[3:4], axis=-1, keepdims=True)) + lam_init)
        o = (o0 - lam * o1).T
        o_ref[...] = (_rms(o, subg_ref[...]) * (1.0 - lam_init)).astype(o_ref.dtype)
    else:
        o_ref[...] = jnp.concatenate([o0, o1], axis=0).T.astype(o_ref.dtype)


def flash_prompt(q, k, vt, *, mode, t, lam_p=None, sub_g=None, lam_init=0.0):
    bt, t_all = q.shape[0], q.shape[1]
    n_blk = t_all // t
    qw = LANES if mode == "diff" else 2 * LANES
    groups = q.shape[2] // qw
    assert vt.shape == (bt * n_blk, groups * LANES, t), vt.shape
    q_idx = lambda b, h, i, *_: (b, i, h)
    in_specs = [pl.BlockSpec((None, t, qw), q_idx),
                pl.BlockSpec((None, t_all, qw), lambda b, h, i, *_: (b, 0, h)),
                pl.BlockSpec((n_blk, LANES, t), lambda b, h, i, *_: (b, h, 0))]
    args = [q, k, vt]
    if mode == "diff":
        slopes = LOG2E * jnp.exp2(-8.0 * jnp.arange(1, groups + 1, dtype=F32) / groups)
        in_specs += [pl.BlockSpec(lam_p.shape, lambda b, h, i, *_: (0, 0)),
                     pl.BlockSpec((1, LANES), lambda b, h, i, *_: (0, 0))]
        args += [lam_p.astype(F32), sub_g.reshape(1, LANES).astype(F32)]
        v_rows = LANES
    else:
        slopes = jnp.zeros((groups,), F32)
        v_rows = LANES // 2
    stat = pltpu.VMEM((2, 1, t), F32)
    tile = pltpu.VMEM((t, t), F32)
    grid_spec = pltpu.PrefetchScalarGridSpec(
        num_scalar_prefetch=1,
        grid=(bt, groups, n_blk),
        in_specs=in_specs,
        out_specs=pl.BlockSpec((None, t, LANES), q_idx),
        scratch_shapes=[stat, stat, pltpu.VMEM((2, v_rows, t), F32), tile, tile,
                        pltpu.VMEM((2, t, t), F32), pltpu.VMEM((2, t, t), F32), stat, stat, stat, stat])
    return pl.pallas_call(
        functools.partial(_flash_prompt_kernel, mode=mode, t=t, lam_init=lam_init),
        grid_spec=grid_spec,
        out_shape=jax.ShapeDtypeStruct((bt, t_all, groups * LANES), BF16),
        compiler_params=_params(3),
    )(slopes, *args)


def _c_pre_kernel(cq_ref, ckv_ref, kra_ref, krb_ref, gq_ref, gkv_ref, wq_ref, wqr_ref,
                  cq_tab, sq_tab, ck_tab, sk_tab, q_out, lat_out, kr_out):
    qn = _rms(cq_ref[...], gq_ref[...]).astype(BF16)
    qc = jnp.dot(qn, wq_ref[...], preferred_element_type=F32)
    qr = jnp.dot(qn, wqr_ref[...], preferred_element_type=F32)
    cos, sin = cq_tab[...], sq_tab[...]
    for h in range(C_HEADS):
        sl = slice(h * LANES, (h + 1) * LANES)
        q_out[:, sl] = (qc[:, sl] * cos + qr[:, sl] * sin).astype(q_out.dtype)
    lat_out[...] = _rms(ckv_ref[...], gkv_ref[...])
    kr = kra_ref[...] * ck_tab[...] + krb_ref[...] * sk_tab[...]
    kr_out[...] = kr[:, :C_ROPE]


def c_pre(cq, ckv, kra, krb, gq, gkv, wq, wqr, tabs, t_len, tm=512):
    m = cq.shape[0]
    tm = _row_tile(t_len, tm)
    n_t = t_len // tm
    row = lambda i: (i, 0)
    fixed = lambda i: (0, 0)
    trow = lambda i: (i % n_t, 0)
    qw = wq.shape[1]
    return pl.pallas_call(
        _c_pre_kernel,
        grid=(m // tm,),
        in_specs=[pl.BlockSpec((tm, C_Q_LORA), row), pl.BlockSpec((tm, C_KV_LORA), row),
                  pl.BlockSpec((tm, LANES), row), pl.BlockSpec((tm, LANES), row),
                  pl.BlockSpec((1, C_Q_LORA), fixed), pl.BlockSpec((1, C_KV_LORA), fixed),
                  pl.BlockSpec(wq.shape, fixed), pl.BlockSpec(wqr.shape, fixed),
                  pl.BlockSpec((tm, LANES), trow), pl.BlockSpec((tm, LANES), trow),
                  pl.BlockSpec((tm, LANES), trow), pl.BlockSpec((tm, LANES), trow)],
        out_specs=[pl.BlockSpec((tm, qw), row), pl.BlockSpec((tm, C_KV_LORA), row),
                   pl.BlockSpec((tm, C_ROPE), row)],
        out_shape=[jax.ShapeDtypeStruct((m, qw), BF16), jax.ShapeDtypeStruct((m, C_KV_LORA), F32),
                   jax.ShapeDtypeStruct((m, C_ROPE), F32)],
        compiler_params=_params(1),
    )(cq, ckv, kra, krb, gq.reshape(1, -1).astype(F32), gkv.reshape(1, -1).astype(F32), wq, wqr, *tabs)


def _c_kv_kernel(lat_ref, kr_ref, wk_ref, wv_ref, place_ref, k_out, v_out, *, v_transposed):
    lat = lat_ref[...].astype(BF16)
    k = jnp.dot(lat, wk_ref[...], preferred_element_type=F32)
    k += jnp.dot(kr_ref[...].astype(BF16), place_ref[...], preferred_element_type=F32)
    k_out[...] = k.astype(k_out.dtype)
    if v_transposed:
        v = lax.dot_general(wv_ref[...], lat, NT, preferred_element_type=F32)
    else:
        v = jnp.dot(lat, wv_ref[...], preferred_element_type=F32)
    v_out[...] = v.astype(v_out.dtype)


def c_kv(lat, kr, wk, wv, place, v_transposed, tm=512):
    m = lat.shape[0]
    tm = _row_tile(m, tm)
    row = lambda i: (i, 0)
    fixed = lambda i: (0, 0)
    if v_transposed:
        n_v = wv.shape[0]
        v_spec = pl.BlockSpec((None, n_v, tm), lambda i: (i, 0, 0))
        v_shape = jax.ShapeDtypeStruct((m // tm, n_v, tm), BF16)
    else:
        n_v = wv.shape[1]
        v_spec = pl.BlockSpec((tm, n_v), row)
        v_shape = jax.ShapeDtypeStruct((m, n_v), BF16)
    return pl.pallas_call(
        functools.partial(_c_kv_kernel, v_transposed=v_transposed),
        grid=(m // tm,),
        in_specs=[pl.BlockSpec((tm, C_KV_LORA), row), pl.BlockSpec((tm, C_ROPE), row),
                  pl.BlockSpec(wk.shape, fixed), pl.BlockSpec(wv.shape, fixed), pl.BlockSpec(place.shape, fixed)],
        out_specs=[pl.BlockSpec((tm, wk.shape[1]), row), v_spec],
        out_shape=[jax.ShapeDtypeStruct((m, wk.shape[1]), BF16), v_shape],
        compiler_params=_params(1),
    )(lat, kr, wk, wv, place)


def _prep_in_ab(w):
    n_mix = w.shape[1] - MEM_W - w.shape[0]
    n_q = n_mix // 3
    scale = jnp.concatenate([jnp.full((n_q,), Q_SCALE, F32), jnp.ones((n_mix - n_q,), F32),
                             jnp.full((MEM_W,), Q_SCALE, F32), jnp.ones((w.shape[0],), F32)])
    return (w * scale[None, :]).astype(BF16)


def _prep_in_c(w):
    d = w.shape[0]
    o = C_Q_LORA + C_KV_LORA
    half = C_ROPE // 2
    kr = w[:, o:o + C_ROPE]
    rot = jnp.concatenate([-kr[:, half:], kr[:, :half]], axis=1)
    pad = jnp.zeros((d, LANES - C_ROPE), F32)
    mq = w[:, o + C_ROPE:o + C_ROPE + MEM_W] * (Q_SCALE)
    gate = w[:, o + C_ROPE + MEM_W:]
    return jnp.concatenate([w[:, :o], kr, pad, rot, pad, mq, gate], axis=1).astype(BF16)


def _prep_uq(w_uq):
    r = w_uq.shape[0]
    w = w_uq.reshape(r, C_HEADS, C_NOPE + C_ROPE)
    nope, rope = w[..., :C_NOPE], w[..., C_NOPE:]
    half = C_ROPE // 2
    rot = jnp.concatenate([-rope[..., half:], rope[..., :half]], axis=-1)
    pad = jnp.zeros((r, C_HEADS, LANES - C_NOPE - C_ROPE), F32)
    plain = jnp.concatenate([nope, rope, pad], axis=-1).reshape(r, C_HEADS * LANES)
    rotated = jnp.concatenate([jnp.zeros_like(nope), rot, pad], axis=-1).reshape(r, C_HEADS * LANES)
    return plain.astype(BF16), rotated.astype(BF16)


def _prep_ukv(w_ukv):
    r = w_ukv.shape[0]
    w = w_ukv.reshape(r, C_HEADS, C_NOPE + C_V)
    wk = jnp.concatenate([w[..., :C_NOPE], jnp.zeros((r, C_HEADS, LANES - C_NOPE), F32)], axis=-1)
    wv = w[..., C_NOPE:]
    eye = jnp.eye(C_ROPE, dtype=F32)
    place = jnp.concatenate([jnp.zeros((C_ROPE, C_NOPE), F32), eye,
                             jnp.zeros((C_ROPE, LANES - C_NOPE - C_ROPE), F32)], axis=1)
    place = jnp.tile(place, (1, C_HEADS))
    return (wk.reshape(r, C_HEADS * LANES).astype(BF16), wv.reshape(r, C_HEADS * C_V).astype(BF16),
            place.astype(BF16))


def _rope_tables(pos):
    half = C_ROPE // 2
    inv = jnp.exp(-math.log(ROPE_BASE) * jnp.arange(half, dtype=F32) * 2.0 / C_ROPE)
    ang = pos.astype(F32)[:, None] * inv[None, :]
    cos, sin = jnp.cos(ang), jnp.sin(ang)
    t = pos.shape[0]
    cos2 = jnp.concatenate([cos, cos], axis=1)
    sin2 = jnp.concatenate([sin, sin], axis=1)
    scale = LOG2E * (C_NOPE + C_ROPE) ** -0.5
    z = lambda n: jnp.zeros((t, n), F32)
    cq = jnp.concatenate([jnp.ones((t, C_NOPE), F32), cos2, z(LANES - C_NOPE - C_ROPE)], axis=1) * scale
    sq = jnp.concatenate([z(C_NOPE), sin2, z(LANES - C_NOPE - C_ROPE)], axis=1) * scale
    ck = jnp.concatenate([cos2, z(LANES - C_ROPE)], axis=1)
    sk = jnp.concatenate([sin2, z(LANES - C_ROPE)], axis=1)
    return cq, sq, ck, sk


def _band_bias_kernel(pos_ref, row_ref, o_ref, *, tq, win, n_valid, transposed):
    c = pl.program_id(0)
    q0, k0 = pos_ref[0, c], pos_ref[1, c]
    wp = row_ref.shape[-1]
    n_rows, n_cols = (win, tq) if transposed else (tq, win)
    rows = pltpu.roll(jnp.broadcast_to(row_ref[...], (n_rows, wp)), 0, 1, stride=1, stride_axis=0)
    ri = lax.broadcasted_iota(jnp.int32, (n_rows, n_cols), 0)
    ci = lax.broadcasted_iota(jnp.int32, (n_rows, n_cols), 1)
    qi, kj = (ci, ri) if transposed else (ri, ci)
    qp, kp = q0 + qi, k0 + kj
    qc, kc = qp // CHUNK, kp // CHUNK
    valid = (kp >= 0) & (kc <= qc) & (kc >= qc - A_BAND_CHUNKS) & (kj < n_valid)
    o_ref[...] = jnp.where(valid, rows[:, :n_cols], NEG)


def band_bias(rel_bias, q0, k0, *, tq, win, n_valid, transposed=False):
    n_cls = q0.shape[0]
    heads = rel_bias.shape[1]
    wp = -(-(tq + win) // LANES) * LANES
    mm = jnp.arange(wp, dtype=jnp.int32)
    mm = jnp.where(mm < (tq if transposed else win), mm, mm - wp)
    mm = mm if transposed else -mm
    rel = jnp.clip((q0 - k0)[:, None] + mm[None, :], -A_REL_CLIP, A_REL_CLIP) + A_REL_CLIP
    rows = jnp.moveaxis((LOG2E * rel_bias.astype(F32))[rel], -1, 1).reshape(n_cls, heads, 1, wp)
    pos = jnp.stack([q0, k0]).astype(jnp.int32)
    if transposed:
        out_spec = pl.BlockSpec((None, None, win, tq), lambda c, h, *_: (c, h // 2, 0, h % 2))
        out_shape = jax.ShapeDtypeStruct((n_cls, heads // 2, win, 2 * tq), F32)
    else:
        out_spec = pl.BlockSpec((None, None, tq, win), lambda c, h, *_: (c, h, 0, 0))
        out_shape = jax.ShapeDtypeStruct((n_cls, heads, tq, win), F32)
    grid_spec = pltpu.PrefetchScalarGridSpec(
        num_scalar_prefetch=1,
        grid=(n_cls, heads),
        in_specs=[pl.BlockSpec((None, None, 1, wp), lambda c, h, *_: (c, h, 0, 0))],
        out_specs=out_spec)
    return pl.pallas_call(
        functools.partial(_band_bias_kernel, tq=tq, win=win, n_valid=n_valid, transposed=transposed),
        grid_spec=grid_spec,
        out_shape=out_shape,
        compiler_params=_params(2),
    )(pos, rows)


A_TQ = 256
A_SUB = 4
FLASH_T = 512
FAR = 1 << 24


def _trunk(x, pos0, mem_k, mem_vt, past, wts):
    bt, t, d = x.shape
    m = bt * t
    depth = len(wts["w_out"])
    x2 = x.reshape(m, d)
    new_a, new_b, new_c = [], [], []
    pos = pos0 + jnp.arange(t, dtype=jnp.int32)
    one = lambda width, dt: (width, (dt,))
    for i in range(depth):
        kind, j = i % N_MIXERS, i // N_MIXERS
        if kind == 0:
            hd = A_HEADS * HEAD_DIM
            w_in = wts["w_in_a"][j]
            if past is None:
                tq = min(A_TQ, t)
                q, k, k16, v, mq, gate, vt = norm_matmul(
                    x2, wts["norm_g"][i], w_in,
                    (one(hd, BF16), (hd, (F32, BF16)), one(hd, F32), one(MEM_W, BF16), one(d, BF16)),
                    wt=w_in[:, 2 * hd:3 * hd].T, t_blk=tq)
                k3, v3 = k.reshape(bt, t, hd), v.reshape(bt, t, hd)
                back_blocks = A_PAST_ROWS // tq
                n_win = min(back_blocks + 1, t // tq)
                win = n_win * tq
                n_cls = n_win
                q0 = jnp.arange(n_cls, dtype=jnp.int32) * tq
                k0 = jnp.clip(q0 - A_PAST_ROWS, 0, t - win)
                bias = band_bias(wts["rel_bias_a"][j], q0, k0, tq=tq, win=win, n_valid=win, transposed=True)
                br = attn_window_t(q.reshape(bt, t, hd), k16.reshape(bt, t, hd), vt, bias,
                                   tq=tq, n_sub=min(A_SUB, t // tq), n_win=n_win, back_blocks=back_blocks)
                keep = min(A_PAST_ROWS, t)
                new_a.append((k3[:, t - keep:].reshape(bt, keep, A_HEADS, HEAD_DIM),
                              v3[:, t - keep:].reshape(bt, keep, A_HEADS, HEAD_DIM)))
            else:
                q, k, v, mq, gate = norm_matmul(
                    x2, wts["norm_g"][i], w_in,
                    (one(hd, BF16), one(hd, F32), one(hd, F32), one(MEM_W, BF16), one(d, BF16)))
                q3, k3, v3 = (a.reshape(bt, t, hd) for a in (q, k, v))
                ck, cv = past[0][j], past[1][j]
                rows = ck.shape[1]
                n_keys = rows + t
                pad = (-n_keys) % LANES
                zeros = jnp.zeros((bt, pad, hd), F32)
                kk = jnp.concatenate([ck.reshape(bt, rows, hd), k3, zeros], axis=1)
                vv = jnp.concatenate([cv.reshape(bt, rows, hd), v3, zeros], axis=1)
                start = jnp.full((1,), pos0, jnp.int32)
                bias = band_bias(wts["rel_bias_a"][j], start, start - rows, tq=t, win=n_keys + pad,
                                 n_valid=n_keys)
                br = attn_window(q3, kk, vv, bias, tq=t, win=n_keys + pad, back=0)
                new_a.append((k3.reshape(bt, t, A_HEADS, HEAD_DIM), v3.reshape(bt, t, A_HEADS, HEAD_DIM)))
        elif kind == 1:
            hd = B_HEADS * 2 * HEAD_DIM
            lam_init = 0.8 - 0.6 * math.exp(-0.3 * i)
            common = dict(mode="diff", lam_p=wts["lambda_b"][j], sub_g=wts["subln_g_b"][j], lam_init=lam_init)
            w_in = wts["w_in_b"][j]
            if past is None:
                q, k, k16, v, mq, gate, vt = norm_matmul(
                    x2, wts["norm_g"][i], w_in,
                    (one(hd, BF16), (hd, (F32, BF16)), one(hd, F32), one(MEM_W, BF16), one(d, BF16)),
                    wt=w_in[:, 2 * hd:3 * hd].T, t_blk=FLASH_T, tm=FLASH_T)
                br = flash_prompt(q.reshape(bt, t, hd), k16.reshape(bt, t, hd), vt, t=FLASH_T, **common)
            else:
                q, k, v, mq, gate = norm_matmul(
                    x2, wts["norm_g"][i], w_in,
                    (one(hd, BF16), one(hd, F32), one(hd, F32), one(MEM_W, BF16), one(d, BF16)))
                ck, cv = past[2][j], past[3][j]
                rows = ck.shape[1]
                br = flash_attention(q.reshape(bt, t, hd), ck.reshape(bt, rows, hd), cv.reshape(bt, rows, hd),
                                     k.reshape(bt, t, hd), v.reshape(bt, t, hd),
                                     tq=t, tk=FLASH_T, diagonal_tail=False, q_pos0=rows, **common)
            new_b.append((k.reshape(bt, t, B_HEADS, 2, HEAD_DIM), v.reshape(bt, t, B_HEADS, 2 * HEAD_DIM)))
        else:
            cq, ckv, kra, krb, mq, gate = norm_matmul(
                x2, wts["norm_g"][i], wts["w_in_c"][j],
                (one(C_Q_LORA, F32), one(C_KV_LORA, F32), one(LANES, F32), one(LANES, F32),
                 one(MEM_W, BF16), one(d, BF16)))
            wq, wqr = wts["w_uq_c"][j]
            wk, wv, place = wts["w_ukv_c"][j]
            qcat, lat, kr = c_pre(cq, ckv, kra, krb, wts["q_norm_g_c"][j], wts["kv_norm_g_c"][j],
                                  wq, wqr, _rope_tables(pos), t)
            q3 = qcat.reshape(bt, t, -1)
            if past is None:
                kcat, vt = c_kv(lat, kr, wk, wv.T, place, v_transposed=True, tm=FLASH_T)
                br = flash_prompt(q3, kcat.reshape(bt, t, -1), vt, mode="mla", t=FLASH_T)
            else:
                kcat, vcat = c_kv(lat, kr, wk, wv, place, v_transposed=False)
                cl, cr = past[4][j], past[5][j]
                rows = cl.shape[1]
                kc_, vc_ = c_kv(cl.reshape(bt * rows, -1), cr.reshape(bt * rows, -1), wk, wv, place,
                                v_transposed=False)
                br = flash_attention(q3, kc_.reshape(bt, rows, -1), vc_.reshape(bt, rows, -1),
                                     kcat.reshape(bt, t, -1), vcat.reshape(bt, t, -1),
                                     mode="mla", tq=t, tk=FLASH_T, diagonal_tail=False, q_pos0=rows)
            new_c.append((lat.reshape(bt, t, -1), kr.reshape(bt, t, -1)))
        last = i == depth - 1
        x2 = gated_out(br.reshape(m, -1), mq, gate, x2, wts["w_out"][i], wts["final_g"],
                       mem_k[i], mem_vt[i], final=last)
    return x2.reshape(bt, t, d), new_a, new_b, new_c


def kernel(x_prompt, x_sample, cache_a_k, cache_a_v, cache_b_k, cache_b_v, cache_c_lat, cache_c_rope,
           cache_mem_k, cache_mem_v, mem_prompt, norm_g, final_g, mem_norm_g, w_mem_kv, w_out, w_in_a,
           rel_bias_a, w_in_b, lambda_b, subln_g_b, w_in_c, q_norm_g_c, kv_norm_g_c, w_uq_c, w_ukv_c):
    depth = w_out.shape[0]
    bp, n_mem, d = mem_prompt.shape
    mem_heads = MEM_W // HEAD_DIM
    wts = dict(
        norm_g=norm_g, final_g=final_g, rel_bias_a=rel_bias_a, lambda_b=lambda_b, subln_g_b=subln_g_b,
        q_norm_g_c=q_norm_g_c, kv_norm_g_c=kv_norm_g_c,
        w_out=[w_out[i].astype(BF16) for i in range(depth)],
        w_in_a=[_prep_in_ab(w_in_a[j]) for j in range(w_in_a.shape[0])],
        w_in_b=[_prep_in_ab(w_in_b[j]) for j in range(w_in_b.shape[0])],
        w_in_c=[_prep_in_c(w_in_c[j]) for j in range(w_in_c.shape[0])],
        w_uq_c=[_prep_uq(w_uq_c[j]) for j in range(w_uq_c.shape[0])],
        w_ukv_c=[_prep_ukv(w_ukv_c[j]) for j in range(w_ukv_c.shape[0])],
    )
    mem2 = mem_prompt.reshape(bp * n_mem, d)
    mem_k_p, mem_v_p, mem_vt_p = [], [], []
    for i in range(depth):
        w_kv = w_mem_kv[i].astype(BF16)
        mk, mv, mvt = norm_matmul(mem2, mem_norm_g[i], w_kv, ((MEM_W, (F32,)), (MEM_W, (F32,))),
                                  wt=w_kv[:, MEM_W:].T, t_blk=n_mem, tm=n_mem)
        mem_k_p.append(mk.reshape(bp, n_mem, MEM_W))
        mem_v_p.append(mv.reshape(bp, n_mem, MEM_W))
        mem_vt_p.append(mvt)

    y_p, na_p, nb_p, nc_p = _trunk(x_prompt, 0, mem_k_p, mem_vt_p, None, wts)

    bs = x_sample.shape[0]
    mem_k_s = [cache_mem_k[i].reshape(bs, n_mem, MEM_W) for i in range(depth)]
    mem_vt_s = [jnp.swapaxes(cache_mem_v[i].reshape(bs, n_mem, MEM_W), 1, 2).astype(BF16) for i in range(depth)]
    past = (cache_a_k, cache_a_v, cache_b_k, cache_b_v, cache_c_lat, cache_c_rope)
    y_s, na_s, nb_s, nc_s = _trunk(x_sample, cache_b_k.shape[2], mem_k_s, mem_vt_s, past, wts)

    stk = lambda lst, n: jnp.stack([s[n] for s in lst])
    heads4 = lambda lst: jnp.stack([a.reshape(bp, n_mem, mem_heads, HEAD_DIM) for a in lst])
    return (y_p, y_s,
            stk(na_p, 0), stk(na_p, 1), stk(na_s, 0), stk(na_s, 1),
            stk(nb_p, 0), stk(nb_p, 1), stk(nb_s, 0), stk(nb_s, 1),
            stk(nc_p, 0), stk(nc_p, 1), stk(nc_s, 0), stk(nc_s, 1),
            heads4(mem_k_p), heads4(mem_v_p))
```

```python
import functools
import itertools
import math

import jax
import jax.numpy as jnp
from jax import lax
from jax.experimental import pallas as pl
from jax.experimental.pallas import tpu as pltpu

F32 = jnp.float32
BF16 = jnp.bfloat16

LANES = 128
VMEM_LIMIT = 56 * 1024 * 1024

EPS = 1e-6
NEG = -1e30
LOG2E = math.log2(math.e)
CHUNK = 64
HEAD_DIM = 64
Q_SCALE = LOG2E * HEAD_DIM ** -0.5
MEM_W = 256
A_HEADS = 12
A_PAST_ROWS = 512
A_BAND_CHUNKS = 8
A_REL_CLIP = 128
B_HEADS = 6
C_HEADS = 12
C_NOPE = 64
C_ROPE = 32
C_V = 64
C_Q_LORA = 384
C_KV_LORA = 256
ROPE_BASE = 10000.0
N_MIXERS = 3


def _params(n_grid):
    return pltpu.CompilerParams(dimension_semantics=("arbitrary",) * n_grid,
                                vmem_limit_bytes=VMEM_LIMIT)


def _row_tile(m, want):
    t = min(m, want)
    assert m % t == 0, (m, t)
    return t


def _rms(x, g):
    return x * lax.rsqrt(jnp.mean(x * x, axis=-1, keepdims=True) + EPS) * g


NT = (((1,), (1,)), ((), ()))


def _norm_matmul_kernel(x_ref, g_ref, w_ref, *rest, segs, has_t):
    h = _rms(x_ref[...], g_ref[...]).astype(BF16)
    out_refs = rest[1:] if has_t else rest
    off = n_out = 0
    for width, dtypes in segs:
        r = jnp.dot(h, w_ref[:, off:off + width], preferred_element_type=F32)
        for dt in dtypes:
            o_ref = out_refs[n_out]
            if len(o_ref.shape) == 2:
                o_ref[...] = r.astype(dt)
            else:
                w = o_ref.shape[-1]
                for c, ix in enumerate(itertools.product(*[range(n) for n in o_ref.shape[1:-1]])):
                    o_ref[(slice(None),) + ix + (slice(None),)] = r[:, c * w:(c + 1) * w].astype(dt)
            n_out += 1
        off += width
    if has_t:
        o_ref = out_refs[n_out]
        r = lax.dot_general(rest[0][...], h, NT, preferred_element_type=F32).astype(o_ref.dtype)
        blk = o_ref.shape[-1]
        for jb in range(o_ref.shape[0]):
            o_ref[jb] = r[:, jb * blk:(jb + 1) * blk]


def norm_matmul(x, g, w, segs, wt=None, t_blk=None, tm=512, split=None):
    m, d = x.shape
    tm = _row_tile(m, tm)
    n = w.shape[1]
    assert n == sum(wd for wd, _ in segs)
    row = lambda i: (i, 0)
    fixed = lambda i: (0, 0)
    in_specs = [pl.BlockSpec((tm, d), row), pl.BlockSpec((1, d), fixed), pl.BlockSpec((d, n), fixed)]
    args = [x, g.reshape(1, d).astype(F32), w]
    out_specs = [pl.BlockSpec((tm, wd), row) for wd, dts in segs for _ in dts]
    out_shape = [jax.ShapeDtypeStruct((m, wd), dt) for wd, dts in segs for dt in dts]
    for pos, tail in (split or {}).items():
        assert math.prod(tail) == out_shape[pos].shape[1]
        out_specs[pos] = pl.BlockSpec((tm,) + tail, lambda i, n=len(tail): (i,) + (0,) * n)
        out_shape[pos] = jax.ShapeDtypeStruct((m,) + tail, out_shape[pos].dtype)
    if wt is not None:
        in_specs.append(pl.BlockSpec(wt.shape, fixed))
        args.append(wt)
        assert tm % t_blk == 0
        out_specs.append(pl.BlockSpec((tm // t_blk, wt.shape[0], t_blk), lambda i: (i, 0, 0)))
        out_shape.append(jax.ShapeDtypeStruct((m // t_blk, wt.shape[0], t_blk), BF16))
    return pl.pallas_call(
        functools.partial(_norm_matmul_kernel, segs=tuple(segs), has_t=wt is not None),
        grid=(m // tm,),
        in_specs=in_specs,
        out_specs=out_specs,
        out_shape=out_shape,
        compiler_params=_params(1),
    )(*args)


def _gated_out_kernel(br_ref, mq_ref, gate_ref, x_ref, w_ref, fg_ref, mk_ref, mvt_ref, o_ref, *, final):
    nb = br_ref.shape[-1]
    tm = x_ref.shape[0]
    halves = _half_masks()
    mq = mq_ref[...]
    mem = []
    for pair in range(mq.shape[1] // LANES):
        cols = slice(pair * LANES, (pair + 1) * LANES)
        q = mq[:, cols]
        q2 = jnp.concatenate([jnp.where(halves[h], q, jnp.zeros_like(q)) for h in range(2)], axis=0)
        st = lax.dot_general(mk_ref[:, cols].astype(BF16), q2, NT, preferred_element_type=F32)
        p = jnp.exp2(st - jnp.max(st, axis=0, keepdims=True))
        l = jnp.sum(p, axis=0, keepdims=True)
        o = jnp.dot(mvt_ref[cols, :], p.astype(BF16), preferred_element_type=F32) / l
        mem.append(jnp.concatenate([o[:HEAD_DIM, :tm], o[HEAD_DIM:, tm:]], axis=0).T)
    mo = jnp.concatenate(mem, axis=1)
    gate = gate_ref[...].astype(F32)
    sg = gate * jax.nn.sigmoid(gate)
    y1 = (br_ref[...].astype(F32) * sg[:, :nb]).astype(BF16)
    y2 = (mo * sg[:, nb:]).astype(BF16)
    acc = jnp.dot(y1, w_ref[:nb, :], preferred_element_type=F32)
    acc += jnp.dot(y2, w_ref[nb:, :], preferred_element_type=F32)
    xn = x_ref[...] + acc
    o_ref[...] = _rms(xn, fg_ref[...]) if final else xn


def gated_out(br, mq, gate, x, w, final_g, mem_k, mem_vt, final, tm=512):
    m, d = x.shape
    t_len = m // mem_k.shape[0]
    tm = _row_tile(t_len, tm)
    n_t = t_len // tm
    nb, nm, ng = br.shape[1], mq.shape[1], gate.shape[1]
    row = lambda i: (i, 0)
    fixed = lambda i: (0, 0)
    per_batch = lambda i: (i // n_t, 0, 0)
    return pl.pallas_call(
        functools.partial(_gated_out_kernel, final=final),
        grid=(m // tm,),
        in_specs=[pl.BlockSpec((tm, nb), row), pl.BlockSpec((tm, nm), row), pl.BlockSpec((tm, ng), row),
                  pl.BlockSpec((tm, d), row), pl.BlockSpec((ng, d), fixed), pl.BlockSpec((1, d), fixed),
                  pl.BlockSpec((None,) + mem_k.shape[1:], per_batch),
                  pl.BlockSpec((None,) + mem_vt.shape[1:], per_batch)],
        out_specs=pl.BlockSpec((tm, d), row),
        out_shape=jax.ShapeDtypeStruct((m, d), F32),
        compiler_params=_params(1),
    )(br, mq, gate, x, w, final_g.reshape(1, d).astype(F32), mem_k, mem_vt)


def _half_masks():
    lane = lax.broadcasted_iota(jnp.int32, (1, LANES), 1)
    lo = lane < HEAD_DIM
    return lo, jnp.logical_not(lo)


def _attn_window_kernel(q_ref, k_ref, v_ref, *rest, tq, win, back):
    if len(rest) == 2:
        bias_ref, o_ref = rest
    else:
        bias_ref, (o_ref,) = None, rest
    tk_all = k_ref.shape[0]
    if win == tk_all:
        k = k_ref[...]
        v = v_ref[...]
    else:
        start = jnp.clip(pl.program_id(2) * tq - back, 0, tk_all - win)
        start = pl.multiple_of(start, CHUNK)
        k = k_ref[pl.ds(start, win), :]
        v = v_ref[pl.ds(start, win), :]
    k = k.astype(BF16)
    v = v.astype(BF16)
    q = q_ref[...]
    halves = _half_masks()
    outs = []
    for i in range(2):
        qm = jnp.where(halves[i], q, jnp.zeros_like(q))
        s = lax.dot_general(qm, k, (((1,), (1,)), ((), ())), preferred_element_type=F32)
        if bias_ref is not None:
            s = s + bias_ref[i]
        m = jnp.max(s, axis=-1, keepdims=True)
        p = jnp.exp2(s - m)
        l = jnp.sum(p, axis=-1, keepdims=True)
        o = jnp.dot(p.astype(BF16), v, preferred_element_type=F32)
        outs.append(o / l)
    o_ref[...] = jnp.where(halves[0], outs[0], outs[1]).astype(o_ref.dtype)


def attn_window(q, k, v, bias, *, tq, win, back):
    bt, t_q, gw = q.shape
    g = gw // LANES
    t_k = k.shape[1]
    tq = _row_tile(t_q, tq)
    in_specs = [pl.BlockSpec((None, tq, LANES), lambda b, h, i: (b, i, h)),
                pl.BlockSpec((None, t_k, LANES), lambda b, h, i: (b, 0, h)),
                pl.BlockSpec((None, t_k, LANES), lambda b, h, i: (b, 0, h))]
    args = [q, k, v]
    if bias is not None:
        last = bias.shape[0] - 1
        in_specs.append(pl.BlockSpec((None, 2, tq, win), lambda b, h, i: (jnp.minimum(i, last), h, 0, 0)))
        args.append(bias)
    return pl.pallas_call(
        functools.partial(_attn_window_kernel, tq=tq, win=win, back=back),
        grid=(bt, g, t_q // tq),
        in_specs=in_specs,
        out_specs=pl.BlockSpec((None, tq, LANES), lambda b, h, i: (b, i, h)),
        out_shape=jax.ShapeDtypeStruct((bt, t_q, gw), BF16),
        compiler_params=_params(3),
    )(*args)


def _attn_window_t_kernel(q_ref, k_ref, vt_ref, *rest, tq, n_sub, n_win, back_blocks):
    if len(rest) == 2:
        bias_ref, o_ref = rest
    else:
        bias_ref, (o_ref,) = None, rest
    n_blk, _, blk = vt_ref.shape
    halves = _half_masks()

    def score(sub):
        i = pl.program_id(2) * n_sub + sub
        if n_win == n_blk:
            b0 = 0
            k = k_ref[...]
        else:
            b0 = jnp.clip(i - back_blocks, 0, n_blk - n_win)
            k = k_ref[pl.ds(pl.multiple_of(b0 * blk, blk), n_win * blk), :]
        q = q_ref[sub * tq:(sub + 1) * tq, :]
        q2 = jnp.concatenate([jnp.where(halves[h], q, jnp.zeros_like(q)) for h in range(2)], axis=0)
        st = lax.dot_general(k.astype(BF16), q2, NT, preferred_element_type=F32)
        if bias_ref is not None:
            st = st + bias_ref[jnp.minimum(i, bias_ref.shape[0] - 1)]
        return b0, st, jnp.max(st, axis=0, keepdims=True)

    def accumulate(sub, b0, st, m):
        p = jnp.exp2(st - m)
        l = jnp.sum(p, axis=0, keepdims=True)
        pb = p.astype(BF16)
        o = None
        for j in range(n_win):
            oj = jnp.dot(vt_ref[b0 + j], pb[j * blk:(j + 1) * blk, :], preferred_element_type=F32)
            o = oj if o is None else o + oj
        o = o / l
        o = jnp.concatenate([o[:HEAD_DIM, :tq], o[HEAD_DIM:, tq:]], axis=0).T
        o_ref[sub * tq:(sub + 1) * tq, :] = o.astype(o_ref.dtype)

    staged = [score(sub) for sub in range(n_sub)]
    for sub in range(n_sub):
        accumulate(sub, *staged[sub])


def attn_window_t(q, k, vt, bias, *, tq, n_sub, n_win, back_blocks):
    bt, t_q, gw = q.shape
    g = gw // LANES
    t_k = k.shape[1]
    blk = vt.shape[2]
    n_blk = t_k // blk
    assert vt.shape == (bt * n_blk, gw, blk), vt.shape
    rows = n_sub * tq
    assert t_q % rows == 0
    in_specs = [pl.BlockSpec((None, rows, LANES), lambda b, h, i: (b, i, h)),
                pl.BlockSpec((None, t_k, LANES), lambda b, h, i: (b, 0, h)),
                pl.BlockSpec((n_blk, LANES, blk), lambda b, h, i: (b, h, 0))]
    args = [q, k, vt]
    if bias is not None:
        in_specs.append(pl.BlockSpec((bias.shape[0], None, n_win * blk, 2 * tq), lambda b, h, i: (0, h, 0, 0)))
        args.append(bias)
    return pl.pallas_call(
        functools.partial(_attn_window_t_kernel, tq=tq, n_sub=n_sub, n_win=n_win, back_blocks=back_blocks),
        grid=(bt, g, t_q // rows),
        in_specs=in_specs,
        out_specs=pl.BlockSpec((None, rows, LANES), lambda b, h, i: (b, i, h)),
        out_shape=jax.ShapeDtypeStruct((bt, t_q, gw), BF16),
        compiler_params=_params(3),
    )(*args)


def _flash_kernel(slope_ref, q_ref, km_ref, vm_ref, kt_ref, vt_ref, *rest,
                  mode, tq, tk, tt, n_main_static, q_pos0, lam_init):
    if mode == "diff":
        lam_ref, subg_ref, o_ref, m_scr, l_scr, acc_scr = rest
    else:
        o_ref, m_scr, l_scr, acc_scr = rest
    head = pl.program_id(1)
    qi = pl.program_id(2)
    q = q_ref[...]
    halves = _half_masks()
    if mode == "diff":
        qs = [jnp.where(halves[i], q, jnp.zeros_like(q)) for i in range(2)]
        slope = slope_ref[head]
    else:
        qs = [q[:, :LANES], q[:, LANES:]]
        slope = None

    m_scr[...] = jnp.full(m_scr.shape, NEG, F32)
    l_scr[...] = jnp.zeros(l_scr.shape, F32)
    acc_scr[...] = jnp.zeros(acc_scr.shape, F32)

    def update(i, s, shift, v):
        m_old = m_scr[i]
        m_new = jnp.maximum(m_old, jnp.max(s, axis=-1, keepdims=True) + shift)
        alpha = jnp.exp2(m_old - m_new)
        p = jnp.exp2(s - (m_new - shift))
        l_scr[i] = alpha * l_scr[i] + jnp.sum(p, axis=-1, keepdims=True)
        acc_scr[i] = alpha * acc_scr[i] + jnp.dot(p.astype(BF16), v, preferred_element_type=F32)
        m_scr[i] = m_new

    def k_of(kblk, i):
        return kblk if mode == "diff" else kblk[:, i * LANES:(i + 1) * LANES]

    nt = (((1,), (1,)), ((), ()))
    q_base = q_pos0 + qi * tq

    if mode == "diff":
        ri = lax.broadcasted_iota(jnp.int32, (tq, tk), 0)
        ci = lax.broadcasted_iota(jnp.int32, (tq, tk), 1)
        main_tile = slope * (ci - ri).astype(F32)

    def main_body(kb, carry):
        off = pl.multiple_of(kb * tk, tk)
        kblk = km_ref[pl.ds(off, tk), :].astype(BF16)
        vblk = vm_ref[pl.ds(off, tk), :].astype(BF16)
        for i in range(2):
            s = lax.dot_general(qs[i], k_of(kblk, i), nt, preferred_element_type=F32)
            if mode == "diff":
                shift = -slope * (q_base - kb * tk).astype(F32)
                update(i, s + main_tile, shift, vblk)
            else:
                update(i, s, 0.0, vblk)
        return carry

    n_main = qi if n_main_static is None else n_main_static
    lax.fori_loop(0, n_main, main_body, 0)

    ri = lax.broadcasted_iota(jnp.int32, (tq, tt), 0)
    ci = lax.broadcasted_iota(jnp.int32, (tq, tt), 1)
    valid = (ci // CHUNK) <= (ri // CHUNK)
    if mode == "diff":
        tail_tile = jnp.where(valid, -slope * jnp.abs(ri - ci).astype(F32), NEG)
    else:
        tail_tile = jnp.where(valid, 0.0, NEG).astype(F32)
    kblk = kt_ref[...].astype(BF16)
    vblk = vt_ref[...].astype(BF16)
    for i in range(2):
        s = lax.dot_general(qs[i], k_of(kblk, i), nt, preferred_element_type=F32)
        update(i, s + tail_tile, 0.0, vblk)

    o0 = acc_scr[0] / l_scr[0]
    o1 = acc_scr[1] / l_scr[1]
    if mode == "diff":
        lp = lam_ref[...]
        lam = (jnp.exp(jnp.sum(lp[0:1] * lp[1:2], axis=-1, keepdims=True))
               - jnp.exp(jnp.sum(lp[2:3] * lp[3:4], axis=-1, keepdims=True)) + lam_init)
        o = o0 - lam * o1
        o_ref[...] = (_rms(o, subg_ref[...]) * (1.0 - lam_init)).astype(o_ref.dtype)
    else:
        o_ref[...] = jnp.where(halves[0], o0, o1).astype(o_ref.dtype)


def flash_attention(q, k_main, v_main, k_tail, v_tail, *, mode, tq, tk, diagonal_tail, q_pos0,
                    lam_p=None, sub_g=None, lam_init=0.0):
    bt, t_q = q.shape[0], q.shape[1]
    tq = _row_tile(t_q, tq)
    qw = LANES if mode == "diff" else 2 * LANES
    groups = q.shape[2] // qw
    t_main = k_main.shape[1]
    tk = _row_tile(t_main, tk)
    if diagonal_tail:
        assert tk == tq
        tt, n_main_static = tq, None
        tail_idx = lambda b, h, i, *_: (b, i, h)
    else:
        tt, n_main_static = k_tail.shape[1], t_main // tk
        tail_idx = lambda b, h, i, *_: (b, 0, h)
    q_idx = lambda b, h, i, *_: (b, i, h)
    main_idx = lambda b, h, i, *_: (b, 0, h)
    in_specs = [pl.BlockSpec((None, tq, qw), q_idx),
                pl.BlockSpec((None, t_main, qw), main_idx),
                pl.BlockSpec((None, t_main, LANES), main_idx),
                pl.BlockSpec((None, tt, qw), tail_idx),
                pl.BlockSpec((None, tt, LANES), tail_idx)]
    args = [q, k_main, v_main, k_tail, v_tail]
    if mode == "diff":
        slopes = LOG2E * jnp.exp2(-8.0 * jnp.arange(1, groups + 1, dtype=F32) / groups)
        in_specs += [pl.BlockSpec(lam_p.shape, lambda b, h, i, *_: (0, 0)),
                     pl.BlockSpec((1, LANES), lambda b, h, i, *_: (0, 0))]
        args += [lam_p.astype(F32), sub_g.reshape(1, LANES).astype(F32)]
    else:
        slopes = jnp.zeros((groups,), F32)
    grid_spec = pltpu.PrefetchScalarGridSpec(
        num_scalar_prefetch=1,
        grid=(bt, groups, t_q // tq),
        in_specs=in_specs,
        out_specs=pl.BlockSpec((None, tq, LANES), q_idx),
        scratch_shapes=[pltpu.VMEM((2, tq, 1), F32), pltpu.VMEM((2, tq, 1), F32),
                        pltpu.VMEM((2, tq, LANES), F32)])
    return pl.pallas_call(
        functools.partial(_flash_kernel, mode=mode, tq=tq, tk=tk, tt=tt, n_main_static=n_main_static,
                          q_pos0=q_pos0, lam_init=lam_init),
        grid_spec=grid_spec,
        out_shape=jax.ShapeDtypeStruct((bt, t_q, groups * LANES), BF16),
        compiler_params=_params(3),
    )(slopes, *args)


def _flash_prompt_kernel(slope_ref, q_ref, k_ref, vt_ref, *rest, mode, t, lam_init):
    if mode == "diff":
        lam_ref, subg_ref, o_ref = rest[:3]
        rest = rest[3:]
    else:
        o_ref = rest[0]
        rest = rest[1:]
    m_scr, acc_scr, main_tile_scr, diag_tile_scr = rest[:4]
    st_scrs, msub_scrs, alpha_scrs = rest[4:6], rest[6:8], rest[8:10]
    head = pl.program_id(1)
    qi = pl.program_id(2)
    q = q_ref[...]
    halves = _half_masks()
    if mode == "diff":
        qs = [jnp.where(halves[i], q, jnp.zeros_like(q)) for i in range(2)]
        slope = slope_ref[head]
    else:
        qs = [q[:, :LANES], q[:, LANES:]]
    v_rows = acc_scr.shape[1] - ONES_ROWS
    ones = jnp.ones((ONES_ROWS, t), BF16)

    m_scr[...] = jnp.full(m_scr.shape, NEG, F32)
    acc_scr[...] = jnp.zeros(acc_scr.shape, F32)

    @pl.when(qi == 0)
    def _():
        kj = lax.broadcasted_iota(jnp.int32, (t, t), 0)
        qj = lax.broadcasted_iota(jnp.int32, (t, t), 1)
        valid = (kj // CHUNK) <= (qj // CHUNK)
        if mode == "diff":
            main_tile_scr[...] = slope * (kj - qj).astype(F32)
            diag_tile_scr[...] = jnp.where(valid, -slope * jnp.abs(qj - kj).astype(F32), NEG)
        else:
            diag_tile_scr[...] = jnp.where(valid, 0.0, NEG).astype(F32)


    def score(kb, diag, slot):
        off = pl.multiple_of(kb * t, t)
        kblk = k_ref[pl.ds(off, t), :]
        if mode == "diff" and not diag:
            shift = -slope * ((qi - kb) * t).astype(F32)
        else:
            shift = 0.0
        for i in range(2):
            ki = kblk if mode == "diff" else kblk[:, i * LANES:(i + 1) * LANES]
            st = lax.dot_general(ki, qs[i], NT, preferred_element_type=F32)
            if diag:
                st = st + diag_tile_scr[...]
            elif mode == "diff":
                st = st + main_tile_scr[...]
            st_scrs[slot][i] = st
            m_old = m_scr[i]
            m_new = jnp.maximum(m_old, jnp.max(st, axis=0, keepdims=True) + shift)
            m_scr[i] = m_new
            msub_scrs[slot][i] = m_new - shift
            alpha_scrs[slot][i] = jnp.exp2(m_old - m_new)

    def accumulate(kb, slot):
        vt = vt_ref[kb]
        for i in range(2):
            alpha = alpha_scrs[slot][i]
            vi = vt if mode == "diff" else vt[i * v_rows:(i + 1) * v_rows, :]
            vi = jnp.concatenate([vi, ones], axis=0)
            p = jnp.exp2(st_scrs[slot][i] - msub_scrs[slot][i])
            acc_scr[i] = alpha * acc_scr[i] + jnp.dot(vi, p.astype(BF16), preferred_element_type=F32)

    lead = jnp.logical_and(qi >= 2, qi % 2 == 0).astype(jnp.int32)

    @pl.when(lead == 1)
    def _():
        score(0, False, 0)
        accumulate(0, 0)

    @pl.when(qi == 0)
    def _():
        score(0, True, 0)

    @pl.when(qi >= 1)
    def _():
        score(lead, False, 0)

    def pair(kb):
        score(kb + 1, False, 1)
        accumulate(kb, 0)
        score(kb + 2, False, 0)
        accumulate(kb + 1, 1)

    def quad_body(j, carry):
        pair(lead + 4 * j)
        pair(lead + 4 * j + 2)
        return carry

    def pair_body(j, carry):
        pair(lead + 2 * j)
        return carry

    n_pairs = jnp.maximum((qi - lead - 1) // 2, 0)
    lax.fori_loop(0, n_pairs // 2, quad_body, 0)
    lax.fori_loop(n_pairs - n_pairs % 2, n_pairs, pair_body, 0)

    @pl.when(qi >= 1)
    def _():
        score(qi, True, 1)
        accumulate(qi - 1, 0)
        accumulate(qi, 1)

    @pl.when(qi == 0)
    def _():
        accumulate(0, 0)

    o0 = acc_scr[0, :v_rows] / acc_scr[0, v_rows:v_rows + 1]
    o1 = acc_scr[1, :v_rows] / acc_scr[1, v_rows:v_rows + 1]
    if mode == "diff":
        lp = lam_ref[...]
        lam = (jnp.exp(jnp.sum(lp[0:1] * lp[1:2], axis=-1, keepdims=True))
               - jnp.exp(jnp.sum(lp[2:3] * lp[3:4], axis=-1, keepdims=True)) + lam_init)
        o = (o0 - lam * o1).T
        o_ref[...] = (_rms(o, subg_ref[...]) * (1.0 - lam_init)).astype(o_ref.dtype)
    else:
        o_ref[...] = jnp.concatenate([o0, o1], axis=0).T.astype(o_ref.dtype)


def flash_prompt(q, k, vt, *, mode, t, lam_p=None, sub_g=None, lam_init=0.0):
    bt, t_all = q.shape[0], q.shape[1]
    n_blk = t_all // t
    qw = LANES if mode == "diff" else 2 * LANES
    groups = q.shape[2] // qw
    assert vt.shape == (bt * n_blk, groups * LANES, t), vt.shape
    q_idx = lambda b, h, i, *_: (b, i, h)
    in_specs = [pl.BlockSpec((None, t, qw), q_idx),
                pl.BlockSpec((None, t_all, qw), lambda b, h, i, *_: (b, 0, h)),
                pl.BlockSpec((n_blk, LANES, t), lambda b, h, i, *_: (b, h, 0))]
    args = [q, k, vt]
    if mode == "diff":
        slopes = LOG2E * jnp.exp2(-8.0 * jnp.arange(1, groups + 1, dtype=F32) / groups)
        in_specs += [pl.BlockSpec(lam_p.shape, lambda b, h, i, *_: (0, 0)),
                     pl.BlockSpec((1, LANES), lambda b, h, i, *_: (0, 0))]
        args += [lam_p.astype(F32), sub_g.reshape(1, LANES).astype(F32)]
        v_rows = LANES
    else:
        slopes = jnp.zeros((groups,), F32)
        v_rows = LANES // 2
    stat = pltpu.VMEM((2, 1, t), F32)
    tile = pltpu.VMEM((t, t), F32)
    grid_spec = pltpu.PrefetchScalarGridSpec(
        num_scalar_prefetch=1,
        grid=(bt, groups, n_blk),
        in_specs=in_specs,
        out_specs=pl.BlockSpec((None, t, LANES), q_idx),
        scratch_shapes=[stat, pltpu.VMEM((2, v_rows + ONES_ROWS, t), F32), tile, tile,
                        pltpu.VMEM((2, t, t), F32), pltpu.VMEM((2, t, t), F32), stat, stat, stat, stat])
    return pl.pallas_call(
        functools.partial(_flash_prompt_kernel, mode=mode, t=t, lam_init=lam_init),
        grid_spec=grid_spec,
        out_shape=jax.ShapeDtypeStruct((bt, t_all, groups * LANES), BF16),
        compiler_params=_params(3),
    )(slopes, *args)


def _c_pre_kernel(cq_ref, ckv_ref, kra_ref, krb_ref, gq_ref, gkv_ref, wq_ref, wqr_ref,
                  cq_tab, sq_tab, ck_tab, sk_tab, q_out, lat_out, kr_out):
    qn = _rms(cq_ref[...], gq_ref[...]).astype(BF16)
    qc = jnp.dot(qn, wq_ref[...], preferred_element_type=F32)
    qr = jnp.dot(qn, wqr_ref[...], preferred_element_type=F32)
    cos, sin = cq_tab[...], sq_tab[...]
    for h in range(C_HEADS):
        sl = slice(h * LANES, (h + 1) * LANES)
        q_out[:, sl] = (qc[:, sl] * cos + qr[:, sl] * sin).astype(q_out.dtype)
    lat_out[...] = _rms(ckv_ref[...], gkv_ref[...])
    kr = kra_ref[...] * ck_tab[...] + krb_ref[...] * sk_tab[...]
    kr_out[...] = kr[:, :C_ROPE]


def c_pre(cq, ckv, kra, krb, gq, gkv, wq, wqr, tabs, t_len, tm=512):
    m = cq.shape[0]
    tm = _row_tile(t_len, tm)
    n_t = t_len // tm
    row = lambda i: (i, 0)
    fixed = lambda i: (0, 0)
    trow = lambda i: (i % n_t, 0)
    qw = wq.shape[1]
    return pl.pallas_call(
        _c_pre_kernel,
        grid=(m // tm,),
        in_specs=[pl.BlockSpec((tm, C_Q_LORA), row), pl.BlockSpec((tm, C_KV_LORA), row),
                  pl.BlockSpec((tm, LANES), row), pl.BlockSpec((tm, LANES), row),
                  pl.BlockSpec((1, C_Q_LORA), fixed), pl.BlockSpec((1, C_KV_LORA), fixed),
                  pl.BlockSpec(wq.shape, fixed), pl.BlockSpec(wqr.shape, fixed),
                  pl.BlockSpec((tm, LANES), trow), pl.BlockSpec((tm, LANES), trow),
                  pl.BlockSpec((tm, LANES), trow), pl.BlockSpec((tm, LANES), trow)],
        out_specs=[pl.BlockSpec((tm, qw), row), pl.BlockSpec((tm, C_KV_LORA), row),
                   pl.BlockSpec((tm, C_ROPE), row)],
        out_shape=[jax.ShapeDtypeStruct((m, qw), BF16), jax.ShapeDtypeStruct((m, C_KV_LORA), F32),
                   jax.ShapeDtypeStruct((m, C_ROPE), F32)],
        compiler_params=_params(1),
    )(cq, ckv, kra, krb, gq.reshape(1, -1).astype(F32), gkv.reshape(1, -1).astype(F32), wq, wqr, *tabs)


def _c_kv_kernel(lat_ref, kr_ref, wk_ref, wv_ref, place_ref, k_out, v_out, *, v_transposed):
    lat = lat_ref[...].astype(BF16)
    k = jnp.dot(lat, wk_ref[...], preferred_element_type=F32)
    k += jnp.dot(kr_ref[...].astype(BF16), place_ref[...], preferred_element_type=F32)
    k_out[...] = k.astype(k_out.dtype)
    if v_transposed:
        v = lax.dot_general(wv_ref[...], lat, NT, preferred_element_type=F32)
    else:
        v = jnp.dot(lat, wv_ref[...], preferred_element_type=F32)
    v_out[...] = v.astype(v_out.dtype)


def c_kv(lat, kr, wk, wv, place, v_transposed, tm=512):
    m = lat.shape[0]
    tm = _row_tile(m, tm)
    row = lambda i: (i, 0)
    fixed = lambda i: (0, 0)
    if v_transposed:
        n_v = wv.shape[0]
        v_spec = pl.BlockSpec((None, n_v, tm), lambda i: (i, 0, 0))
        v_shape = jax.ShapeDtypeStruct((m // tm, n_v, tm), BF16)
    else:
        n_v = wv.shape[1]
        v_spec = pl.BlockSpec((tm, n_v), row)
        v_shape = jax.ShapeDtypeStruct((m, n_v), BF16)
    return pl.pallas_call(
        functools.partial(_c_kv_kernel, v_transposed=v_transposed),
        grid=(m // tm,),
        in_specs=[pl.BlockSpec((tm, C_KV_LORA), row), pl.BlockSpec((tm, C_ROPE), row),
                  pl.BlockSpec(wk.shape, fixed), pl.BlockSpec(wv.shape, fixed), pl.BlockSpec(place.shape, fixed)],
        out_specs=[pl.BlockSpec((tm, wk.shape[1]), row), v_spec],
        out_shape=[jax.ShapeDtypeStruct((m, wk.shape[1]), BF16), v_shape],
        compiler_params=_params(1),
    )(lat, kr, wk, wv, place)


def _prep_in_ab(w):
    n_mix = w.shape[1] - MEM_W - w.shape[0]
    n_q = n_mix // 3
    scale = jnp.concatenate([jnp.full((n_q,), Q_SCALE, F32), jnp.ones((n_mix - n_q,), F32),
                             jnp.full((MEM_W,), Q_SCALE, F32), jnp.ones((w.shape[0],), F32)])
    return (w * scale[None, :]).astype(BF16)


def _prep_in_c(w):
    d = w.shape[0]
    o = C_Q_LORA + C_KV_LORA
    half = C_ROPE // 2
    kr = w[:, o:o + C_ROPE]
    rot = jnp.concatenate([-kr[:, half:], kr[:, :half]], axis=1)
    pad = jnp.zeros((d, LANES - C_ROPE), F32)
    mq = w[:, o + C_ROPE:o + C_ROPE + MEM_W] * (Q_SCALE)
    gate = w[:, o + C_ROPE + MEM_W:]
    return jnp.concatenate([w[:, :o], kr, pad, rot, pad, mq, gate], axis=1).astype(BF16)


def _prep_uq(w_uq):
    r = w_uq.shape[0]
    w = w_uq.reshape(r, C_HEADS, C_NOPE + C_ROPE)
    nope, rope = w[..., :C_NOPE], w[..., C_NOPE:]
    half = C_ROPE // 2
    rot = jnp.concatenate([-rope[..., half:], rope[..., :half]], axis=-1)
    pad = jnp.zeros((r, C_HEADS, LANES - C_NOPE - C_ROPE), F32)
    plain = jnp.concatenate([nope, rope, pad], axis=-1).reshape(r, C_HEADS * LANES)
    rotated = jnp.concatenate([jnp.zeros_like(nope), rot, pad], axis=-1).reshape(r, C_HEADS * LANES)
    return plain.astype(BF16), rotated.astype(BF16)


def _prep_ukv(w_ukv):
    r = w_ukv.shape[0]
    w = w_ukv.reshape(r, C_HEADS, C_NOPE + C_V)
    wk = jnp.concatenate([w[..., :C_NOPE], jnp.zeros((r, C_HEADS, LANES - C_NOPE), F32)], axis=-1)
    wv = w[..., C_NOPE:]
    eye = jnp.eye(C_ROPE, dtype=F32)
    place = jnp.concatenate([jnp.zeros((C_ROPE, C_NOPE), F32), eye,
                             jnp.zeros((C_ROPE, LANES - C_NOPE - C_ROPE), F32)], axis=1)
    place = jnp.tile(place, (1, C_HEADS))
    return (wk.reshape(r, C_HEADS * LANES).astype(BF16), wv.reshape(r, C_HEADS * C_V).astype(BF16),
            place.astype(BF16))


def _rope_tables(pos):
    half = C_ROPE // 2
    inv = jnp.exp(-math.log(ROPE_BASE) * jnp.arange(half, dtype=F32) * 2.0 / C_ROPE)
    ang = pos.astype(F32)[:, None] * inv[None, :]
    cos, sin = jnp.cos(ang), jnp.sin(ang)
    t = pos.shape[0]
    cos2 = jnp.concatenate([cos, cos], axis=1)
    sin2 = jnp.concatenate([sin, sin], axis=1)
    scale = LOG2E * (C_NOPE + C_ROPE) ** -0.5
    z = lambda n: jnp.zeros((t, n), F32)
    cq = jnp.concatenate([jnp.ones((t, C_NOPE), F32), cos2, z(LANES - C_NOPE - C_ROPE)], axis=1) * scale
    sq = jnp.concatenate([z(C_NOPE), sin2, z(LANES - C_NOPE - C_ROPE)], axis=1) * scale
    ck = jnp.concatenate([cos2, z(LANES - C_ROPE)], axis=1)
    sk = jnp.concatenate([sin2, z(LANES - C_ROPE)], axis=1)
    return cq, sq, ck, sk


def _band_bias_kernel(pos_ref, row_ref, o_ref, *, tq, win, n_valid, transposed):
    c = pl.program_id(0)
    q0, k0 = pos_ref[0, c], pos_ref[1, c]
    wp = row_ref.shape[-1]
    n_rows, n_cols = (win, tq) if transposed else (tq, win)
    rows = pltpu.roll(jnp.broadcast_to(row_ref[...], (n_rows, wp)), 0, 1, stride=1, stride_axis=0)
    ri = lax.broadcasted_iota(jnp.int32, (n_rows, n_cols), 0)
    ci = lax.broadcasted_iota(jnp.int32, (n_rows, n_cols), 1)
    qi, kj = (ci, ri) if transposed else (ri, ci)
    qp, kp = q0 + qi, k0 + kj
    qc, kc = qp // CHUNK, kp // CHUNK
    valid = (kp >= 0) & (kc <= qc) & (kc >= qc - A_BAND_CHUNKS) & (kj < n_valid)
    o_ref[...] = jnp.where(valid, rows[:, :n_cols], NEG)


def band_bias(rel_bias, q0, k0, *, tq, win, n_valid, transposed=False):
    n_cls = q0.shape[0]
    heads = rel_bias.shape[1]
    wp = -(-(tq + win) // LANES) * LANES
    mm = jnp.arange(wp, dtype=jnp.int32)
    mm = jnp.where(mm < (tq if transposed else win), mm, mm - wp)
    mm = mm if transposed else -mm
    rel = jnp.clip((q0 - k0)[:, None] + mm[None, :], -A_REL_CLIP, A_REL_CLIP) + A_REL_CLIP
    rows = jnp.moveaxis((LOG2E * rel_bias.astype(F32))[rel], -1, 1).reshape(n_cls, heads, 1, wp)
    pos = jnp.stack([q0, k0]).astype(jnp.int32)
    if transposed:
        out_spec = pl.BlockSpec((None, None, win, tq), lambda c, h, *_: (c, h // 2, 0, h % 2))
        out_shape = jax.ShapeDtypeStruct((n_cls, heads // 2, win, 2 * tq), F32)
    else:
        out_spec = pl.BlockSpec((None, None, tq, win), lambda c, h, *_: (c, h, 0, 0))
        out_shape = jax.ShapeDtypeStruct((n_cls, heads, tq, win), F32)
    grid_spec = pltpu.PrefetchScalarGridSpec(
        num_scalar_prefetch=1,
        grid=(n_cls, heads),
        in_specs=[pl.BlockSpec((None, None, 1, wp), lambda c, h, *_: (c, h, 0, 0))],
        out_specs=out_spec)
    return pl.pallas_call(
        functools.partial(_band_bias_kernel, tq=tq, win=win, n_valid=n_valid, transposed=transposed),
        grid_spec=grid_spec,
        out_shape=out_shape,
        compiler_params=_params(2),
    )(pos, rows)


A_TQ = 256
A_SUB = 4
FLASH_T = 512
ONES_ROWS = 16
FAR = 1 << 24


def _trunk(x, pos0, mem_k, mem_vt, past, wts):
    bt, t, d = x.shape
    m = bt * t
    depth = len(wts["w_out"])
    x2 = x.reshape(m, d)
    new_a, new_b, new_c = [], [], []
    pos = pos0 + jnp.arange(t, dtype=jnp.int32)
    one = lambda width, dt: (width, (dt,))
    for i in range(depth):
        kind, j = i % N_MIXERS, i // N_MIXERS
        if kind == 0:
            hd = A_HEADS * HEAD_DIM
            w_in = wts["w_in_a"][j]
            if past is None:
                tq = min(A_TQ, t)
                q, k, k16, v, mq, gate, vt = norm_matmul(
                    x2, wts["norm_g"][i], w_in,
                    (one(hd, BF16), (hd, (F32, BF16)), one(hd, F32), one(MEM_W, BF16), one(d, BF16)),
                    wt=w_in[:, 2 * hd:3 * hd].T, t_blk=tq)
                k3, v3 = k.reshape(bt, t, hd), v.reshape(bt, t, hd)
                back_blocks = A_PAST_ROWS // tq
                n_win = min(back_blocks + 1, t // tq)
                win = n_win * tq
                n_cls = n_win
                q0 = jnp.arange(n_cls, dtype=jnp.int32) * tq
                k0 = jnp.clip(q0 - A_PAST_ROWS, 0, t - win)
                bias = band_bias(wts["rel_bias_a"][j], q0, k0, tq=tq, win=win, n_valid=win, transposed=True)
                br = attn_window_t(q.reshape(bt, t, hd), k16.reshape(bt, t, hd), vt, bias,
                                   tq=tq, n_sub=min(A_SUB, t // tq), n_win=n_win, back_blocks=back_blocks)
                keep = min(A_PAST_ROWS, t)
                new_a.append((k3[:, t - keep:].reshape(bt, keep, A_HEADS, HEAD_DIM),
                              v3[:, t - keep:].reshape(bt, keep, A_HEADS, HEAD_DIM)))
            else:
                q, k, v, mq, gate = norm_matmul(
                    x2, wts["norm_g"][i], w_in,
                    (one(hd, BF16), one(hd, F32), one(hd, F32), one(MEM_W, BF16), one(d, BF16)))
                q3, k3, v3 = (a.reshape(bt, t, hd) for a in (q, k, v))
                ck, cv = past[0][j], past[1][j]
                rows = ck.shape[1]
                n_keys = rows + t
                pad = (-n_keys) % LANES
                zeros = jnp.zeros((bt, pad, hd), F32)
                kk = jnp.concatenate([ck.reshape(bt, rows, hd), k3, zeros], axis=1)
                vv = jnp.concatenate([cv.reshape(bt, rows, hd), v3, zeros], axis=1)
                start = jnp.full((1,), pos0, jnp.int32)
                bias = band_bias(wts["rel_bias_a"][j], start, start - rows, tq=t, win=n_keys + pad,
                                 n_valid=n_keys)
                br = attn_window(q3, kk, vv, bias, tq=t, win=n_keys + pad, back=0)
                new_a.append((k3.reshape(bt, t, A_HEADS, HEAD_DIM), v3.reshape(bt, t, A_HEADS, HEAD_DIM)))
        elif kind == 1:
            hd = B_HEADS * 2 * HEAD_DIM
            lam_init = 0.8 - 0.6 * math.exp(-0.3 * i)
            common = dict(mode="diff", lam_p=wts["lambda_b"][j], sub_g=wts["subln_g_b"][j], lam_init=lam_init)
            w_in = wts["w_in_b"][j]
            if past is None:
                q, k, k16, v, mq, gate, vt = norm_matmul(
                    x2, wts["norm_g"][i], w_in,
                    (one(hd, BF16), (hd, (F32, BF16)), one(hd, F32), one(MEM_W, BF16), one(d, BF16)),
                    wt=w_in[:, 2 * hd:3 * hd].T, t_blk=FLASH_T, tm=FLASH_T,
                    split={1: (B_HEADS, 2, HEAD_DIM), 3: (B_HEADS, 2 * HEAD_DIM)})
                br = flash_prompt(q.reshape(bt, t, hd), k16.reshape(bt, t, hd), vt, t=FLASH_T, **common)
            else:
                q, k, v, mq, gate = norm_matmul(
                    x2, wts["norm_g"][i], w_in,
                    (one(hd, BF16), one(hd, F32), one(hd, F32), one(MEM_W, BF16), one(d, BF16)))
                ck, cv = past[2][j], past[3][j]
                rows = ck.shape[1]
                br = flash_attention(q.reshape(bt, t, hd), ck.reshape(bt, rows, hd), cv.reshape(bt, rows, hd),
                                     k.reshape(bt, t, hd), v.reshape(bt, t, hd),
                                     tq=t, tk=FLASH_T, diagonal_tail=False, q_pos0=rows, **common)
            new_b.append((k.reshape(bt, t, B_HEADS, 2, HEAD_DIM), v.reshape(bt, t, B_HEADS, 2 * HEAD_DIM)))
        else:
            cq, ckv, kra, krb, mq, gate = norm_matmul(
                x2, wts["norm_g"][i], wts["w_in_c"][j],
                (one(C_Q_LORA, F32), one(C_KV_LORA, F32), one(LANES, F32), one(LANES, F32),
                 one(MEM_W, BF16), one(d, BF16)))
            wq, wqr = wts["w_uq_c"][j]
            wk, wv, place = wts["w_ukv_c"][j]
            qcat, lat, kr = c_pre(cq, ckv, kra, krb, wts["q_norm_g_c"][j], wts["kv_norm_g_c"][j],
                                  wq, wqr, _rope_tables(pos), t)
            q3 = qcat.reshape(bt, t, -1)
            if past is None:
                kcat, vt = c_kv(lat, kr, wk, wv.T, place, v_transposed=True, tm=FLASH_T)
                br = flash_prompt(q3, kcat.reshape(bt, t, -1), vt, mode="mla", t=FLASH_T)
            else:
                kcat, vcat = c_kv(lat, kr, wk, wv, place, v_transposed=False)
                cl, cr = past[4][j], past[5][j]
                rows = cl.shape[1]
                kc_, vc_ = c_kv(cl.reshape(bt * rows, -1), cr.reshape(bt * rows, -1), wk, wv, place,
                                v_transposed=False)
                br = flash_attention(q3, kc_.reshape(bt, rows, -1), vc_.reshape(bt, rows, -1),
                                     kcat.reshape(bt, t, -1), vcat.reshape(bt, t, -1),
                                     mode="mla", tq=t, tk=FLASH_T, diagonal_tail=False, q_pos0=rows)
            new_c.append((lat.reshape(bt, t, -1), kr.reshape(bt, t, -1)))
        last = i == depth - 1
        x2 = gated_out(br.reshape(m, -1), mq, gate, x2, wts["w_out"][i], wts["final_g"],
                       mem_k[i], mem_vt[i], final=last)
    return x2.reshape(bt, t, d), new_a, new_b, new_c


def kernel(x_prompt, x_sample, cache_a_k, cache_a_v, cache_b_k, cache_b_v, cache_c_lat, cache_c_rope,
           cache_mem_k, cache_mem_v, mem_prompt, norm_g, final_g, mem_norm_g, w_mem_kv, w_out, w_in_a,
           rel_bias_a, w_in_b, lambda_b, subln_g_b, w_in_c, q_norm_g_c, kv_norm_g_c, w_uq_c, w_ukv_c):
    depth = w_out.shape[0]
    bp, n_mem, d = mem_prompt.shape
    mem_heads = MEM_W // HEAD_DIM
    wts = dict(
        norm_g=norm_g, final_g=final_g, rel_bias_a=rel_bias_a, lambda_b=lambda_b, subln_g_b=subln_g_b,
        q_norm_g_c=q_norm_g_c, kv_norm_g_c=kv_norm_g_c,
        w_out=[w_out[i].astype(BF16) for i in range(depth)],
        w_in_a=[_prep_in_ab(w_in_a[j]) for j in range(w_in_a.shape[0])],
        w_in_b=[_prep_in_ab(w_in_b[j]) for j in range(w_in_b.shape[0])],
        w_in_c=[_prep_in_c(w_in_c[j]) for j in range(w_in_c.shape[0])],
        w_uq_c=[_prep_uq(w_uq_c[j]) for j in range(w_uq_c.shape[0])],
        w_ukv_c=[_prep_ukv(w_ukv_c[j]) for j in range(w_ukv_c.shape[0])],
    )
    mem2 = mem_prompt.reshape(bp * n_mem, d)
    mem_k_p, mem_v_p, mem_vt_p = [], [], []
    for i in range(depth):
        w_kv = w_mem_kv[i].astype(BF16)
        mk, mv, mvt = norm_matmul(mem2, mem_norm_g[i], w_kv, ((MEM_W, (F32,)), (MEM_W, (F32,))),
                                  wt=w_kv[:, MEM_W:].T, t_blk=n_mem, tm=n_mem)
        mem_k_p.append(mk.reshape(bp, n_mem, MEM_W))
        mem_v_p.append(mv.reshape(bp, n_mem, MEM_W))
        mem_vt_p.append(mvt)

    y_p, na_p, nb_p, nc_p = _trunk(x_prompt, 0, mem_k_p, mem_vt_p, None, wts)

    bs = x_sample.shape[0]
    mem_k_s = [cache_mem_k[i].reshape(bs, n_mem, MEM_W) for i in range(depth)]
    mem_vt_s = [jnp.swapaxes(cache_mem_v[i].reshape(bs, n_mem, MEM_W), 1, 2).astype(BF16) for i in range(depth)]
    past = (cache_a_k, cache_a_v, cache_b_k, cache_b_v, cache_c_lat, cache_c_rope)
    y_s, na_s, nb_s, nc_s = _trunk(x_sample, cache_b_k.shape[2], mem_k_s, mem_vt_s, past, wts)

    stk = lambda lst, n: jnp.stack([s[n] for s in lst])
    heads4 = lambda lst: jnp.stack([a.reshape(bp, n_mem, mem_heads, HEAD_DIM) for a in lst])
    return (y_p, y_s,
            stk(na_p, 0), stk(na_p, 1), stk(na_s, 0), stk(na_s, 1),
            stk(nb_p, 0), stk(nb_p, 1), stk(nb_s, 0), stk(nb_s, 1),
            stk(nc_p, 0), stk(nc_p, 1), stk(nc_s, 0), stk(nc_s, 1),
            heads4(mem_k_p), heads4(mem_v_p))
```

```python
import functools
import itertools
import math

import jax
import jax.numpy as jnp
from jax import lax
from jax.experimental import pallas as pl
from jax.experimental.pallas import tpu as pltpu

F32 = jnp.float32
BF16 = jnp.bfloat16

LANES = 128
VMEM_LIMIT = 56 * 1024 * 1024

EPS = 1e-6
NEG = -1e30
LOG2E = math.log2(math.e)
CHUNK = 64
HEAD_DIM = 64
Q_SCALE = LOG2E * HEAD_DIM ** -0.5
MEM_W = 256
A_HEADS = 12
A_PAST_ROWS = 512
A_BAND_CHUNKS = 8
A_REL_CLIP = 128
B_HEADS = 6
C_HEADS = 12
C_NOPE = 64
C_ROPE = 32
C_V = 64
C_Q_LORA = 384
C_KV_LORA = 256
ROPE_BASE = 10000.0
N_MIXERS = 3


def _params(n_grid):
    return pltpu.CompilerParams(dimension_semantics=("arbitrary",) * n_grid,
                                vmem_limit_bytes=VMEM_LIMIT)


def _row_tile(m, want):
    t = min(m, want)
    assert m % t == 0, (m, t)
    return t


def _rms(x, g):
    return x * lax.rsqrt(jnp.mean(x * x, axis=-1, keepdims=True) + EPS) * g


NT = (((1,), (1,)), ((), ()))


def _norm_matmul_kernel(x_ref, g_ref, w_ref, *rest, segs, has_t):
    h = _rms(x_ref[...], g_ref[...]).astype(BF16)
    out_refs = rest[1:] if has_t else rest
    off = n_out = 0
    for width, dtypes in segs:
        r = jnp.dot(h, w_ref[:, off:off + width], preferred_element_type=F32)
        for dt in dtypes:
            o_ref = out_refs[n_out]
            if len(o_ref.shape) == 2:
                o_ref[...] = r.astype(dt)
            else:
                w = o_ref.shape[-1]
                for c, ix in enumerate(itertools.product(*[range(n) for n in o_ref.shape[1:-1]])):
                    o_ref[(slice(None),) + ix + (slice(None),)] = r[:, c * w:(c + 1) * w].astype(dt)
            n_out += 1
        off += width
    if has_t:
        o_ref = out_refs[n_out]
        r = lax.dot_general(rest[0][...], h, NT, preferred_element_type=F32).astype(o_ref.dtype)
        blk = o_ref.shape[-1]
        for jb in range(o_ref.shape[0]):
            o_ref[jb] = r[:, jb * blk:(jb + 1) * blk]


def norm_matmul(x, g, w, segs, wt=None, t_blk=None, tm=512, split=None):
    m, d = x.shape
    tm = _row_tile(m, tm)
    n = w.shape[1]
    assert n == sum(wd for wd, _ in segs)
    row = lambda i: (i, 0)
    fixed = lambda i: (0, 0)
    in_specs = [pl.BlockSpec((tm, d), row), pl.BlockSpec((1, d), fixed), pl.BlockSpec((d, n), fixed)]
    args = [x, g.reshape(1, d).astype(F32), w]
    out_specs = [pl.BlockSpec((tm, wd), row) for wd, dts in segs for _ in dts]
    out_shape = [jax.ShapeDtypeStruct((m, wd), dt) for wd, dts in segs for dt in dts]
    for pos, tail in (split or {}).items():
        assert math.prod(tail) == out_shape[pos].shape[1]
        out_specs[pos] = pl.BlockSpec((tm,) + tail, lambda i, n=len(tail): (i,) + (0,) * n)
        out_shape[pos] = jax.ShapeDtypeStruct((m,) + tail, out_shape[pos].dtype)
    if wt is not None:
        in_specs.append(pl.BlockSpec(wt.shape, fixed))
        args.append(wt)
        assert tm % t_blk == 0
        out_specs.append(pl.BlockSpec((tm // t_blk, wt.shape[0], t_blk), lambda i: (i, 0, 0)))
        out_shape.append(jax.ShapeDtypeStruct((m // t_blk, wt.shape[0], t_blk), BF16))
    return pl.pallas_call(
        functools.partial(_norm_matmul_kernel, segs=tuple(segs), has_t=wt is not None),
        grid=(m // tm,),
        in_specs=in_specs,
        out_specs=out_specs,
        out_shape=out_shape,
        compiler_params=_params(1),
    )(*args)


def _gated_out_kernel(br_ref, mq_ref, gate_ref, x_ref, w_ref, fg_ref, mk_ref, mvt_ref, o_ref, *, final):
    nb = br_ref.shape[-1]
    tm = x_ref.shape[0]
    halves = _half_masks()
    mq = mq_ref[...]
    mem = []
    for pair in range(mq.shape[1] // LANES):
        cols = slice(pair * LANES, (pair + 1) * LANES)
        q = mq[:, cols]
        q2 = jnp.concatenate([jnp.where(halves[h], q, jnp.zeros_like(q)) for h in range(2)], axis=0)
        st = lax.dot_general(mk_ref[:, cols].astype(BF16), q2, NT, preferred_element_type=F32)
        p = jnp.exp2(st - jnp.max(st, axis=0, keepdims=True))
        l = jnp.sum(p, axis=0, keepdims=True)
        o = jnp.dot(mvt_ref[cols, :], p.astype(BF16), preferred_element_type=F32) / l
        mem.append(jnp.concatenate([o[:HEAD_DIM, :tm], o[HEAD_DIM:, tm:]], axis=0).T)
    mo = jnp.concatenate(mem, axis=1)
    gate = gate_ref[...].astype(F32)
    sg = gate * jax.nn.sigmoid(gate)
    y1 = (br_ref[...].astype(F32) * sg[:, :nb]).astype(BF16)
    y2 = (mo * sg[:, nb:]).astype(BF16)
    acc = jnp.dot(y1, w_ref[:nb, :], preferred_element_type=F32)
    acc += jnp.dot(y2, w_ref[nb:, :], preferred_element_type=F32)
    xn = x_ref[...] + acc
    o_ref[...] = _rms(xn, fg_ref[...]) if final else xn


def gated_out(br, mq, gate, x, w, final_g, mem_k, mem_vt, final, tm=512):
    m, d = x.shape
    t_len = m // mem_k.shape[0]
    tm = _row_tile(t_len, tm)
    n_t = t_len // tm
    nb, nm, ng = br.shape[1], mq.shape[1], gate.shape[1]
    row = lambda i: (i, 0)
    fixed = lambda i: (0, 0)
    per_batch = lambda i: (i // n_t, 0, 0)
    return pl.pallas_call(
        functools.partial(_gated_out_kernel, final=final),
        grid=(m // tm,),
        in_specs=[pl.BlockSpec((tm, nb), row), pl.BlockSpec((tm, nm), row), pl.BlockSpec((tm, ng), row),
                  pl.BlockSpec((tm, d), row), pl.BlockSpec((ng, d), fixed), pl.BlockSpec((1, d), fixed),
                  pl.BlockSpec((None,) + mem_k.shape[1:], per_batch),
                  pl.BlockSpec((None,) + mem_vt.shape[1:], per_batch)],
        out_specs=pl.BlockSpec((tm, d), row),
        out_shape=jax.ShapeDtypeStruct((m, d), F32),
        compiler_params=_params(1),
    )(br, mq, gate, x, w, final_g.reshape(1, d).astype(F32), mem_k, mem_vt)


def _half_masks():
    lane = lax.broadcasted_iota(jnp.int32, (1, LANES), 1)
    lo = lane < HEAD_DIM
    return lo, jnp.logical_not(lo)


def _attn_window_kernel(q_ref, k_ref, v_ref, *rest, tq, win, back):
    if len(rest) == 2:
        bias_ref, o_ref = rest
    else:
        bias_ref, (o_ref,) = None, rest
    tk_all = k_ref.shape[0]
    if win == tk_all:
        k = k_ref[...]
        v = v_ref[...]
    else:
        start = jnp.clip(pl.program_id(2) * tq - back, 0, tk_all - win)
        start = pl.multiple_of(start, CHUNK)
        k = k_ref[pl.ds(start, win), :]
        v = v_ref[pl.ds(start, win), :]
    k = k.astype(BF16)
    v = v.astype(BF16)
    q = q_ref[...]
    halves = _half_masks()
    outs = []
    for i in range(2):
        qm = jnp.where(halves[i], q, jnp.zeros_like(q))
        s = lax.dot_general(qm, k, (((1,), (1,)), ((), ())), preferred_element_type=F32)
        if bias_ref is not None:
            s = s + bias_ref[i]
        m = jnp.max(s, axis=-1, keepdims=True)
        p = jnp.exp2(s - m)
        l = jnp.sum(p, axis=-1, keepdims=True)
        o = jnp.dot(p.astype(BF16), v, preferred_element_type=F32)
        outs.append(o / l)
    o_ref[...] = jnp.where(halves[0], outs[0], outs[1]).astype(o_ref.dtype)


def attn_window(q, k, v, bias, *, tq, win, back):
    bt, t_q, gw = q.shape
    g = gw // LANES
    t_k = k.shape[1]
    tq = _row_tile(t_q, tq)
    in_specs = [pl.BlockSpec((None, tq, LANES), lambda b, h, i: (b, i, h)),
                pl.BlockSpec((None, t_k, LANES), lambda b, h, i: (b, 0, h)),
                pl.BlockSpec((None, t_k, LANES), lambda b, h, i: (b, 0, h))]
    args = [q, k, v]
    if bias is not None:
        last = bias.shape[0] - 1
        in_specs.append(pl.BlockSpec((None, 2, tq, win), lambda b, h, i: (jnp.minimum(i, last), h, 0, 0)))
        args.append(bias)
    return pl.pallas_call(
        functools.partial(_attn_window_kernel, tq=tq, win=win, back=back),
        grid=(bt, g, t_q // tq),
        in_specs=in_specs,
        out_specs=pl.BlockSpec((None, tq, LANES), lambda b, h, i: (b, i, h)),
        out_shape=jax.ShapeDtypeStruct((bt, t_q, gw), BF16),
        compiler_params=_params(3),
    )(*args)


def _attn_window_t_kernel(q_ref, k_ref, vt_ref, *rest, tq, n_sub, n_win, back_blocks):
    if len(rest) == 2:
        bias_ref, o_ref = rest
    else:
        bias_ref, (o_ref,) = None, rest
    n_blk, _, blk = vt_ref.shape
    halves = _half_masks()

    def score(sub):
        i = pl.program_id(2) * n_sub + sub
        if n_win == n_blk:
            b0 = 0
            k = k_ref[...]
        else:
            b0 = jnp.clip(i - back_blocks, 0, n_blk - n_win)
            k = k_ref[pl.ds(pl.multiple_of(b0 * blk, blk), n_win * blk), :]
        q = q_ref[sub * tq:(sub + 1) * tq, :]
        q2 = jnp.concatenate([jnp.where(halves[h], q, jnp.zeros_like(q)) for h in range(2)], axis=0)
        st = lax.dot_general(k.astype(BF16), q2, NT, preferred_element_type=F32)
        if bias_ref is not None:
            st = st + bias_ref[jnp.minimum(i, bias_ref.shape[0] - 1)]
        return b0, st, jnp.max(st, axis=0, keepdims=True)

    def accumulate(sub, b0, st, m):
        p = jnp.exp2(st - m)
        l = jnp.sum(p, axis=0, keepdims=True)
        pb = p.astype(BF16)
        o = None
        for j in range(n_win):
            oj = jnp.dot(vt_ref[b0 + j], pb[j * blk:(j + 1) * blk, :], preferred_element_type=F32)
            o = oj if o is None else o + oj
        o = o / l
        o = jnp.concatenate([o[:HEAD_DIM, :tq], o[HEAD_DIM:, tq:]], axis=0).T
        o_ref[sub * tq:(sub + 1) * tq, :] = o.astype(o_ref.dtype)

    staged = [score(sub) for sub in range(n_sub)]
    for sub in range(n_sub):
        accumulate(sub, *staged[sub])


def attn_window_t(q, k, vt, bias, *, tq, n_sub, n_win, back_blocks):
    bt, t_q, gw = q.shape
    g = gw // LANES
    t_k = k.shape[1]
    blk = vt.shape[2]
    n_blk = t_k // blk
    assert vt.shape == (bt * n_blk, gw, blk), vt.shape
    rows = n_sub * tq
    assert t_q % rows == 0
    in_specs = [pl.BlockSpec((None, rows, LANES), lambda b, h, i: (b, i, h)),
                pl.BlockSpec((None, t_k, LANES), lambda b, h, i: (b, 0, h)),
                pl.BlockSpec((n_blk, LANES, blk), lambda b, h, i: (b, h, 0))]
    args = [q, k, vt]
    if bias is not None:
        in_specs.append(pl.BlockSpec((bias.shape[0], None, n_win * blk, 2 * tq), lambda b, h, i: (0, h, 0, 0)))
        args.append(bias)
    return pl.pallas_call(
        functools.partial(_attn_window_t_kernel, tq=tq, n_sub=n_sub, n_win=n_win, back_blocks=back_blocks),
        grid=(bt, g, t_q // rows),
        in_specs=in_specs,
        out_specs=pl.BlockSpec((None, rows, LANES), lambda b, h, i: (b, i, h)),
        out_shape=jax.ShapeDtypeStruct((bt, t_q, gw), BF16),
        compiler_params=_params(3),
    )(*args)


def _flash_kernel(slope_ref, q_ref, km_ref, vm_ref, kt_ref, vt_ref, *rest,
                  mode, tq, tk, tt, n_main_static, q_pos0, lam_init):
    if mode == "diff":
        lam_ref, subg_ref, o_ref, m_scr, l_scr, acc_scr = rest
    else:
        o_ref, m_scr, l_scr, acc_scr = rest
    head = pl.program_id(1)
    qi = pl.program_id(2)
    q = q_ref[...]
    halves = _half_masks()
    if mode == "diff":
        qs = [jnp.where(halves[i], q, jnp.zeros_like(q)) for i in range(2)]
        slope = slope_ref[head]
    else:
        qs = [q[:, :LANES], q[:, LANES:]]
        slope = None

    m_scr[...] = jnp.full(m_scr.shape, NEG, F32)
    l_scr[...] = jnp.zeros(l_scr.shape, F32)
    acc_scr[...] = jnp.zeros(acc_scr.shape, F32)

    def update(i, s, shift, v):
        m_old = m_scr[i]
        m_new = jnp.maximum(m_old, jnp.max(s, axis=-1, keepdims=True) + shift)
        alpha = jnp.exp2(m_old - m_new)
        p = jnp.exp2(s - (m_new - shift))
        l_scr[i] = alpha * l_scr[i] + jnp.sum(p, axis=-1, keepdims=True)
        acc_scr[i] = alpha * acc_scr[i] + jnp.dot(p.astype(BF16), v, preferred_element_type=F32)
        m_scr[i] = m_new

    def k_of(kblk, i):
        return kblk if mode == "diff" else kblk[:, i * LANES:(i + 1) * LANES]

    nt = (((1,), (1,)), ((), ()))
    q_base = q_pos0 + qi * tq

    if mode == "diff":
        ri = lax.broadcasted_iota(jnp.int32, (tq, tk), 0)
        ci = lax.broadcasted_iota(jnp.int32, (tq, tk), 1)
        main_tile = slope * (ci - ri).astype(F32)

    def main_body(kb, carry):
        off = pl.multiple_of(kb * tk, tk)
        kblk = km_ref[pl.ds(off, tk), :].astype(BF16)
        vblk = vm_ref[pl.ds(off, tk), :].astype(BF16)
        for i in range(2):
            s = lax.dot_general(qs[i], k_of(kblk, i), nt, preferred_element_type=F32)
            if mode == "diff":
                shift = -slope * (q_base - kb * tk).astype(F32)
                update(i, s + main_tile, shift, vblk)
            else:
                update(i, s, 0.0, vblk)
        return carry

    n_main = qi if n_main_static is None else n_main_static
    lax.fori_loop(0, n_main, main_body, 0)

    ri = lax.broadcasted_iota(jnp.int32, (tq, tt), 0)
    ci = lax.broadcasted_iota(jnp.int32, (tq, tt), 1)
    valid = (ci // CHUNK) <= (ri // CHUNK)
    if mode == "diff":
        tail_tile = jnp.where(valid, -slope * jnp.abs(ri - ci).astype(F32), NEG)
    else:
        tail_tile = jnp.where(valid, 0.0, NEG).astype(F32)
    kblk = kt_ref[...].astype(BF16)
    vblk = vt_ref[...].astype(BF16)
    for i in range(2):
        s = lax.dot_general(qs[i], k_of(kblk, i), nt, preferred_element_type=F32)
        update(i, s + tail_tile, 0.0, vblk)

    o0 = acc_scr[0] / l_scr[0]
    o1 = acc_scr[1] / l_scr[1]
    if mode == "diff":
        lp = lam_ref[...]
        lam = (jnp.exp(jnp.sum(lp[0:1] * lp[1:2], axis=-1, keepdims=True))
               - jnp.exp(jnp.sum(lp[2:3] * lp[3:4], axis=-1, keepdims=True)) + lam_init)
        o = o0 - lam * o1
        o_ref[...] = (_rms(o, subg_ref[...]) * (1.0 - lam_init)).astype(o_ref.dtype)
    else:
        o_ref[...] = jnp.where(halves[0], o0, o1).astype(o_ref.dtype)


def flash_attention(q, k_main, v_main, k_tail, v_tail, *, mode, tq, tk, diagonal_tail, q_pos0,
                    lam_p=None, sub_g=None, lam_init=0.0):
    bt, t_q = q.shape[0], q.shape[1]
    tq = _row_tile(t_q, tq)
    qw = LANES if mode == "diff" else 2 * LANES
    groups = q.shape[2] // qw
    t_main = k_main.shape[1]
    tk = _row_tile(t_main, tk)
    if diagonal_tail:
        assert tk == tq
        tt, n_main_static = tq, None
        tail_idx = lambda b, h, i, *_: (b, i, h)
    else:
        tt, n_main_static = k_tail.shape[1], t_main // tk
        tail_idx = lambda b, h, i, *_: (b, 0, h)
    q_idx = lambda b, h, i, *_: (b, i, h)
    main_idx = lambda b, h, i, *_: (b, 0, h)
    in_specs = [pl.BlockSpec((None, tq, qw), q_idx),
                pl.BlockSpec((None, t_main, qw), main_idx),
                pl.BlockSpec((None, t_main, LANES), main_idx),
                pl.BlockSpec((None, tt, qw), tail_idx),
                pl.BlockSpec((None, tt, LANES), tail_idx)]
    args = [q, k_main, v_main, k_tail, v_tail]
    if mode == "diff":
        slopes = LOG2E * jnp.exp2(-8.0 * jnp.arange(1, groups + 1, dtype=F32) / groups)
        in_specs += [pl.BlockSpec(lam_p.shape, lambda b, h, i, *_: (0, 0)),
                     pl.BlockSpec((1, LANES), lambda b, h, i, *_: (0, 0))]
        args += [lam_p.astype(F32), sub_g.reshape(1, LANES).astype(F32)]
    else:
        slopes = jnp.zeros((groups,), F32)
    grid_spec = pltpu.PrefetchScalarGridSpec(
        num_scalar_prefetch=1,
        grid=(bt, groups, t_q // tq),
        in_specs=in_specs,
        out_specs=pl.BlockSpec((None, tq, LANES), q_idx),
        scratch_shapes=[pltpu.VMEM((2, tq, 1), F32), pltpu.VMEM((2, tq, 1), F32),
                        pltpu.VMEM((2, tq, LANES), F32)])
    return pl.pallas_call(
        functools.partial(_flash_kernel, mode=mode, tq=tq, tk=tk, tt=tt, n_main_static=n_main_static,
                          q_pos0=q_pos0, lam_init=lam_init),
        grid_spec=grid_spec,
        out_shape=jax.ShapeDtypeStruct((bt, t_q, groups * LANES), BF16),
        compiler_params=_params(3),
    )(slopes, *args)


def _flash_prompt_kernel(slope_ref, q_ref, k_ref, vt_ref, *rest, mode, t, lam_init):
    if mode == "diff":
        lam_ref, subg_ref, o_ref = rest[:3]
        rest = rest[3:]
    else:
        o_ref = rest[0]
        rest = rest[1:]
    m_scr, acc_scr, main_tile_scr, diag_tile_scr = rest[:4]
    st_scrs, msub_scrs, alpha_scrs = rest[4:6], rest[6:8], rest[8:10]
    head = pl.program_id(1)
    qi = pl.program_id(2)
    q = q_ref[...]
    halves = _half_masks()
    if mode == "diff":
        qs = [jnp.where(halves[i], q, jnp.zeros_like(q)) for i in range(2)]
        slope = slope_ref[head]
    else:
        qs = [q[:, :LANES], q[:, LANES:]]
    v_rows = acc_scr.shape[1] - ONES_ROWS
    ones = jnp.ones((ONES_ROWS, t), BF16)

    m_scr[...] = jnp.full(m_scr.shape, NEG, F32)
    acc_scr[...] = jnp.zeros(acc_scr.shape, F32)

    @pl.when(qi == 0)
    def _():
        kj = lax.broadcasted_iota(jnp.int32, (t, t), 0)
        qj = lax.broadcasted_iota(jnp.int32, (t, t), 1)
        valid = (kj // CHUNK) <= (qj // CHUNK)
        if mode == "diff":
            main_tile_scr[...] = slope * (kj - qj).astype(F32)
            diag_tile_scr[...] = jnp.where(valid, -slope * jnp.abs(qj - kj).astype(F32), NEG)
        else:
            diag_tile_scr[...] = jnp.where(valid, 0.0, NEG).astype(F32)


    def score(kb, diag, slot):
        off = pl.multiple_of(kb * t, t)
        kblk = k_ref[pl.ds(off, t), :]
        if mode == "diff" and not diag:
            shift = -slope * ((qi - kb) * t).astype(F32)
        else:
            shift = 0.0
        for i in range(2):
            ki = kblk if mode == "diff" else kblk[:, i * LANES:(i + 1) * LANES]
            st = lax.dot_general(ki, qs[i], NT, preferred_element_type=F32)
            if diag:
                st = st + diag_tile_scr[...]
            elif mode == "diff":
                st = st + main_tile_scr[...]
            st_scrs[slot][i] = st
            m_old = m_scr[i]
            m_new = jnp.maximum(m_old, jnp.max(st, axis=0, keepdims=True) + shift)
            m_scr[i] = m_new
            msub_scrs[slot][i] = m_new - shift
            alpha_scrs[slot][i] = jnp.exp2(m_old - m_new)

    def accumulate(kb, slot):
        vt = vt_ref[kb]
        for i in range(2):
            alpha = alpha_scrs[slot][i]
            vi = vt if mode == "diff" else vt[i * v_rows:(i + 1) * v_rows, :]
            vi = jnp.concatenate([vi, ones], axis=0)
            p = jnp.exp2(st_scrs[slot][i] - msub_scrs[slot][i])
            acc_scr[i] = alpha * acc_scr[i] + jnp.dot(vi, p.astype(BF16), preferred_element_type=F32)

    lead = jnp.logical_and(qi >= 2, qi % 2 == 0).astype(jnp.int32)

    @pl.when(lead == 1)
    def _():
        score(0, False, 0)
        accumulate(0, 0)

    @pl.when(qi == 0)
    def _():
        score(0, True, 0)

    @pl.when(qi >= 1)
    def _():
        score(lead, False, 0)

    def pair(kb):
        score(kb + 1, False, 1)
        accumulate(kb, 0)
        score(kb + 2, False, 0)
        accumulate(kb + 1, 1)

    def quad_body(j, carry):
        pair(lead + 4 * j)
        pair(lead + 4 * j + 2)
        return carry

    def pair_body(j, carry):
        pair(lead + 2 * j)
        return carry

    n_pairs = jnp.maximum((qi - lead - 1) // 2, 0)
    lax.fori_loop(0, n_pairs // 2, quad_body, 0)
    lax.fori_loop(n_pairs - n_pairs % 2, n_pairs, pair_body, 0)

    @pl.when(qi >= 1)
    def _():
        score(qi, True, 1)
        accumulate(qi - 1, 0)
        accumulate(qi, 1)

    @pl.when(qi == 0)
    def _():
        accumulate(0, 0)

    o0 = acc_scr[0, :v_rows] / acc_scr[0, v_rows:v_rows + 1]
    o1 = acc_scr[1, :v_rows] / acc_scr[1, v_rows:v_rows + 1]
    if mode == "diff":
        lp = lam_ref[...]
        lam = (jnp.exp(jnp.sum(lp[0:1] * lp[1:2], axis=-1, keepdims=True))
               - jnp.exp(jnp.sum(lp[2:3] * lp[3:4], axis=-1, keepdims=True)) + lam_init)
        o = (o0 - lam * o1).T
        o_ref[...] = (_rms(o, subg_ref[...]) * (1.0 - lam_init)).astype(o_ref.dtype)
    else:
        o_ref[...] = jnp.concatenate([o0, o1], axis=0).T.astype(o_ref.dtype)


def flash_prompt(q, k, vt, *, mode, t, lam_p=None, sub_g=None, lam_init=0.0):
    bt, t_all = q.shape[0], q.shape[1]
    n_blk = t_all // t
    qw = LANES if mode == "diff" else 2 * LANES
    groups = q.shape[2] // qw
    assert vt.shape == (bt * n_blk, groups * LANES, t), vt.shape
    q_idx = lambda b, h, i, *_: (b, i, h)
    in_specs = [pl.BlockSpec((None, t, qw), q_idx),
                pl.BlockSpec((None, t_all, qw), lambda b, h, i, *_: (b, 0, h)),
                pl.BlockSpec((n_blk, LANES, t), lambda b, h, i, *_: (b, h, 0))]
    args = [q, k, vt]
    if mode == "diff":
        slopes = LOG2E * jnp.exp2(-8.0 * jnp.arange(1, groups + 1, dtype=F32) / groups)
        in_specs += [pl.BlockSpec(lam_p.shape, lambda b, h, i, *_: (0, 0)),
                     pl.BlockSpec((1, LANES), lambda b, h, i, *_: (0, 0))]
        args += [lam_p.astype(F32), sub_g.reshape(1, LANES).astype(F32)]
        v_rows = LANES
    else:
        slopes = jnp.zeros((groups,), F32)
        v_rows = LANES // 2
    stat = pltpu.VMEM((2, 1, t), F32)
    tile = pltpu.VMEM((t, t), F32)
    grid_spec = pltpu.PrefetchScalarGridSpec(
        num_scalar_prefetch=1,
        grid=(bt, groups, n_blk),
        in_specs=in_specs,
        out_specs=pl.BlockSpec((None, t, LANES), q_idx),
        scratch_shapes=[stat, pltpu.VMEM((2, v_rows + ONES_ROWS, t), F32), tile, tile,
                        pltpu.VMEM((2, t, t), F32), pltpu.VMEM((2, t, t), F32), stat, stat, stat, stat])
    return pl.pallas_call(
        functools.partial(_flash_prompt_kernel, mode=mode, t=t, lam_init=lam_init),
        grid_spec=grid_spec,
        out_shape=jax.ShapeDtypeStruct((bt, t_all, groups * LANES), BF16),
        compiler_params=_params(3),
    )(slopes, *args)


def _c_pre_kernel(cq_ref, ckv_ref, kra_ref, krb_ref, gq_ref, gkv_ref, wq_ref, wqr_ref,
                  cq_tab, sq_tab, ck_tab, sk_tab, q_out, lat_out, kr_out):
    qn = _rms(cq_ref[...], gq_ref[...]).astype(BF16)
    qc = jnp.dot(qn, wq_ref[...], preferred_element_type=F32)
    qr = jnp.dot(qn, wqr_ref[...], preferred_element_type=F32)
    cos, sin = cq_tab[...], sq_tab[...]
    for h in range(C_HEADS):
        sl = slice(h * LANES, (h + 1) * LANES)
        q_out[:, sl] = (qc[:, sl] * cos + qr[:, sl] * sin).astype(q_out.dtype)
    lat_out[...] = _rms(ckv_ref[...], gkv_ref[...])
    kr = kra_ref[...] * ck_tab[...] + krb_ref[...] * sk_tab[...]
    kr_out[...] = kr


def c_pre(cq, ckv, kra, krb, gq, gkv, wq, wqr, tabs, t_len, tm=512):
    m = cq.shape[0]
    tm = _row_tile(t_len, tm)
    n_t = t_len // tm
    row = lambda i: (i, 0)
    fixed = lambda i: (0, 0)
    trow = lambda i: (i % n_t, 0)
    qw = wq.shape[1]
    return pl.pallas_call(
        _c_pre_kernel,
        grid=(m // tm,),
        in_specs=[pl.BlockSpec((tm, C_Q_LORA), row), pl.BlockSpec((tm, C_KV_LORA), row),
                  pl.BlockSpec((tm, LANES), row), pl.BlockSpec((tm, LANES), row),
                  pl.BlockSpec((1, C_Q_LORA), fixed), pl.BlockSpec((1, C_KV_LORA), fixed),
                  pl.BlockSpec(wq.shape, fixed), pl.BlockSpec(wqr.shape, fixed),
                  pl.BlockSpec((tm, LANES), trow), pl.BlockSpec((tm, LANES), trow),
                  pl.BlockSpec((tm, LANES), trow), pl.BlockSpec((tm, LANES), trow)],
        out_specs=[pl.BlockSpec((tm, qw), row), pl.BlockSpec((tm, C_KV_LORA), row),
                   pl.BlockSpec((tm, LANES), row)],
        out_shape=[jax.ShapeDtypeStruct((m, qw), BF16), jax.ShapeDtypeStruct((m, C_KV_LORA), F32),
                   jax.ShapeDtypeStruct((m, LANES), F32)],
        compiler_params=_params(1),
    )(cq, ckv, kra, krb, gq.reshape(1, -1).astype(F32), gkv.reshape(1, -1).astype(F32), wq, wqr, *tabs)


def _c_kv_kernel(lat_ref, kr_ref, wk_ref, wv_ref, place_ref, k_out, v_out, *, v_transposed):
    lat = lat_ref[...].astype(BF16)
    k = jnp.dot(lat, wk_ref[...], preferred_element_type=F32)
    k += jnp.dot(kr_ref[...].astype(BF16), place_ref[...], preferred_element_type=F32)
    k_out[...] = k.astype(k_out.dtype)
    if v_transposed:
        v = lax.dot_general(wv_ref[...], lat, NT, preferred_element_type=F32)
    else:
        v = jnp.dot(lat, wv_ref[...], preferred_element_type=F32)
    v_out[...] = v.astype(v_out.dtype)


def c_kv(lat, kr, wk, wv, place, v_transposed, tm=512):
    m = lat.shape[0]
    tm = _row_tile(m, tm)
    row = lambda i: (i, 0)
    fixed = lambda i: (0, 0)
    if v_transposed:
        n_v = wv.shape[0]
        v_spec = pl.BlockSpec((None, n_v, tm), lambda i: (i, 0, 0))
        v_shape = jax.ShapeDtypeStruct((m // tm, n_v, tm), BF16)
    else:
        n_v = wv.shape[1]
        v_spec = pl.BlockSpec((tm, n_v), row)
        v_shape = jax.ShapeDtypeStruct((m, n_v), BF16)
    return pl.pallas_call(
        functools.partial(_c_kv_kernel, v_transposed=v_transposed),
        grid=(m // tm,),
        in_specs=[pl.BlockSpec((tm, C_KV_LORA), row), pl.BlockSpec((tm, LANES), row),
                  pl.BlockSpec(wk.shape, fixed), pl.BlockSpec(wv.shape, fixed), pl.BlockSpec(place.shape, fixed)],
        out_specs=[pl.BlockSpec((tm, wk.shape[1]), row), v_spec],
        out_shape=[jax.ShapeDtypeStruct((m, wk.shape[1]), BF16), v_shape],
        compiler_params=_params(1),
    )(lat, kr, wk, wv, place)


def _prep_in_ab(w):
    n_mix = w.shape[1] - MEM_W - w.shape[0]
    n_q = n_mix // 3
    scale = jnp.concatenate([jnp.full((n_q,), Q_SCALE, F32), jnp.ones((n_mix - n_q,), F32),
                             jnp.full((MEM_W,), Q_SCALE, F32), jnp.ones((w.shape[0],), F32)])
    return (w * scale[None, :]).astype(BF16)


def _prep_in_c(w):
    d = w.shape[0]
    o = C_Q_LORA + C_KV_LORA
    half = C_ROPE // 2
    kr = w[:, o:o + C_ROPE]
    rot = jnp.concatenate([-kr[:, half:], kr[:, :half]], axis=1)
    pad = jnp.zeros((d, LANES - C_ROPE), F32)
    mq = w[:, o + C_ROPE:o + C_ROPE + MEM_W] * (Q_SCALE)
    gate = w[:, o + C_ROPE + MEM_W:]
    return jnp.concatenate([w[:, :o], kr, pad, rot, pad, mq, gate], axis=1).astype(BF16)


def _prep_uq(w_uq):
    r = w_uq.shape[0]
    w = w_uq.reshape(r, C_HEADS, C_NOPE + C_ROPE)
    nope, rope = w[..., :C_NOPE], w[..., C_NOPE:]
    half = C_ROPE // 2
    rot = jnp.concatenate([-rope[..., half:], rope[..., :half]], axis=-1)
    pad = jnp.zeros((r, C_HEADS, LANES - C_NOPE - C_ROPE), F32)
    plain = jnp.concatenate([nope, rope, pad], axis=-1).reshape(r, C_HEADS * LANES)
    rotated = jnp.concatenate([jnp.zeros_like(nope), rot, pad], axis=-1).reshape(r, C_HEADS * LANES)
    return plain.astype(BF16), rotated.astype(BF16)


def _prep_ukv(w_ukv):
    r = w_ukv.shape[0]
    w = w_ukv.reshape(r, C_HEADS, C_NOPE + C_V)
    wk = jnp.concatenate([w[..., :C_NOPE], jnp.zeros((r, C_HEADS, LANES - C_NOPE), F32)], axis=-1)
    wv = w[..., C_NOPE:]
    eye = jnp.eye(C_ROPE, dtype=F32)
    place = jnp.concatenate([jnp.zeros((C_ROPE, C_NOPE), F32), eye,
                             jnp.zeros((C_ROPE, LANES - C_NOPE - C_ROPE), F32)], axis=1)
    place = jnp.tile(place, (1, C_HEADS))
    place = jnp.concatenate([place, jnp.zeros((LANES - C_ROPE, place.shape[1]), F32)], axis=0)
    return (wk.reshape(r, C_HEADS * LANES).astype(BF16), wv.reshape(r, C_HEADS * C_V).astype(BF16),
            place.astype(BF16))


def _rope_tables(pos):
    half = C_ROPE // 2
    inv = jnp.exp(-math.log(ROPE_BASE) * jnp.arange(half, dtype=F32) * 2.0 / C_ROPE)
    ang = pos.astype(F32)[:, None] * inv[None, :]
    cos, sin = jnp.cos(ang), jnp.sin(ang)
    t = pos.shape[0]
    cos2 = jnp.concatenate([cos, cos], axis=1)
    sin2 = jnp.concatenate([sin, sin], axis=1)
    scale = LOG2E * (C_NOPE + C_ROPE) ** -0.5
    z = lambda n: jnp.zeros((t, n), F32)
    cq = jnp.concatenate([jnp.ones((t, C_NOPE), F32), cos2, z(LANES - C_NOPE - C_ROPE)], axis=1) * scale
    sq = jnp.concatenate([z(C_NOPE), sin2, z(LANES - C_NOPE - C_ROPE)], axis=1) * scale
    ck = jnp.concatenate([cos2, z(LANES - C_ROPE)], axis=1)
    sk = jnp.concatenate([sin2, z(LANES - C_ROPE)], axis=1)
    return cq, sq, ck, sk


def _band_bias_kernel(pos_ref, row_ref, o_ref, *, tq, win, n_valid, transposed):
    c = pl.program_id(0)
    q0, k0 = pos_ref[0, c], pos_ref[1, c]
    wp = row_ref.shape[-1]
    n_rows, n_cols = (win, tq) if transposed else (tq, win)
    rows = pltpu.roll(jnp.broadcast_to(row_ref[...], (n_rows, wp)), 0, 1, stride=1, stride_axis=0)
    ri = lax.broadcasted_iota(jnp.int32, (n_rows, n_cols), 0)
    ci = lax.broadcasted_iota(jnp.int32, (n_rows, n_cols), 1)
    qi, kj = (ci, ri) if transposed else (ri, ci)
    qp, kp = q0 + qi, k0 + kj
    qc, kc = qp // CHUNK, kp // CHUNK
    valid = (kp >= 0) & (kc <= qc) & (kc >= qc - A_BAND_CHUNKS) & (kj < n_valid)
    o_ref[...] = jnp.where(valid, rows[:, :n_cols], NEG)


def band_bias(rel_bias, q0, k0, *, tq, win, n_valid, transposed=False):
    n_cls = q0.shape[0]
    heads = rel_bias.shape[1]
    wp = -(-(tq + win) // LANES) * LANES
    mm = jnp.arange(wp, dtype=jnp.int32)
    mm = jnp.where(mm < (tq if transposed else win), mm, mm - wp)
    mm = mm if transposed else -mm
    rel = jnp.clip((q0 - k0)[:, None] + mm[None, :], -A_REL_CLIP, A_REL_CLIP) + A_REL_CLIP
    rows = jnp.moveaxis((LOG2E * rel_bias.astype(F32))[rel], -1, 1).reshape(n_cls, heads, 1, wp)
    pos = jnp.stack([q0, k0]).astype(jnp.int32)
    if transposed:
        out_spec = pl.BlockSpec((None, None, win, tq), lambda c, h, *_: (c, h // 2, 0, h % 2))
        out_shape = jax.ShapeDtypeStruct((n_cls, heads // 2, win, 2 * tq), F32)
    else:
        out_spec = pl.BlockSpec((None, None, tq, win), lambda c, h, *_: (c, h, 0, 0))
        out_shape = jax.ShapeDtypeStruct((n_cls, heads, tq, win), F32)
    grid_spec = pltpu.PrefetchScalarGridSpec(
        num_scalar_prefetch=1,
        grid=(n_cls, heads),
        in_specs=[pl.BlockSpec((None, None, 1, wp), lambda c, h, *_: (c, h, 0, 0))],
        out_specs=out_spec)
    return pl.pallas_call(
        functools.partial(_band_bias_kernel, tq=tq, win=win, n_valid=n_valid, transposed=transposed),
        grid_spec=grid_spec,
        out_shape=out_shape,
        compiler_params=_params(2),
    )(pos, rows)


A_TQ = 256
A_SUB = 4
FLASH_T = 512
ONES_ROWS = 16


def _trunk(x, pos0, mem_k, mem_vt, past, wts):
    bt, t, d = x.shape
    m = bt * t
    depth = len(wts["w_out"])
    x2 = x.reshape(m, d)
    new_a, new_b, new_c = [], [], []
    pos = pos0 + jnp.arange(t, dtype=jnp.int32)
    one = lambda width, dt: (width, (dt,))
    for i in range(depth):
        kind, j = i % N_MIXERS, i // N_MIXERS
        if kind == 0:
            hd = A_HEADS * HEAD_DIM
            w_in = wts["w_in_a"][j]
            if past is None:
                tq = min(A_TQ, t)
                q, k, k16, v, mq, gate, vt = norm_matmul(
                    x2, wts["norm_g"][i], w_in,
                    (one(hd, BF16), (hd, (F32, BF16)), one(hd, F32), one(MEM_W, BF16), one(d, BF16)),
                    wt=w_in[:, 2 * hd:3 * hd].T, t_blk=tq)
                k3, v3 = k.reshape(bt, t, hd), v.reshape(bt, t, hd)
                back_blocks = A_PAST_ROWS // tq
                n_win = min(back_blocks + 1, t // tq)
                win = n_win * tq
                n_cls = n_win
                q0 = jnp.arange(n_cls, dtype=jnp.int32) * tq
                k0 = jnp.clip(q0 - A_PAST_ROWS, 0, t - win)
                bias = band_bias(wts["rel_bias_a"][j], q0, k0, tq=tq, win=win, n_valid=win, transposed=True)
                br = attn_window_t(q.reshape(bt, t, hd), k16.reshape(bt, t, hd), vt, bias,
                                   tq=tq, n_sub=min(A_SUB, t // tq), n_win=n_win, back_blocks=back_blocks)
                keep = min(A_PAST_ROWS, t)
                new_a.append((k3[:, t - keep:].reshape(bt, keep, A_HEADS, HEAD_DIM),
                              v3[:, t - keep:].reshape(bt, keep, A_HEADS, HEAD_DIM)))
            else:
                q, k, v, mq, gate = norm_matmul(
                    x2, wts["norm_g"][i], w_in,
                    (one(hd, BF16), one(hd, F32), one(hd, F32), one(MEM_W, BF16), one(d, BF16)))
                q3, k3, v3 = (a.reshape(bt, t, hd) for a in (q, k, v))
                ck, cv = past[0][j], past[1][j]
                rows = ck.shape[1]
                n_keys = rows + t
                pad = (-n_keys) % LANES
                zeros = jnp.zeros((bt, pad, hd), F32)
                kk = jnp.concatenate([ck.reshape(bt, rows, hd), k3, zeros], axis=1)
                vv = jnp.concatenate([cv.reshape(bt, rows, hd), v3, zeros], axis=1)
                start = jnp.full((1,), pos0, jnp.int32)
                bias = band_bias(wts["rel_bias_a"][j], start, start - rows, tq=t, win=n_keys + pad,
                                 n_valid=n_keys)
                br = attn_window(q3, kk, vv, bias, tq=t, win=n_keys + pad, back=0)
                new_a.append((k3.reshape(bt, t, A_HEADS, HEAD_DIM), v3.reshape(bt, t, A_HEADS, HEAD_DIM)))
        elif kind == 1:
            hd = B_HEADS * 2 * HEAD_DIM
            lam_init = 0.8 - 0.6 * math.exp(-0.3 * i)
            common = dict(mode="diff", lam_p=wts["lambda_b"][j], sub_g=wts["subln_g_b"][j], lam_init=lam_init)
            w_in = wts["w_in_b"][j]
            if past is None:
                q, k, k16, v, mq, gate, vt = norm_matmul(
                    x2, wts["norm_g"][i], w_in,
                    (one(hd, BF16), (hd, (F32, BF16)), one(hd, F32), one(MEM_W, BF16), one(d, BF16)),
                    wt=w_in[:, 2 * hd:3 * hd].T, t_blk=FLASH_T, tm=FLASH_T,
                    split={1: (B_HEADS, 2, HEAD_DIM), 3: (B_HEADS, 2 * HEAD_DIM)})
                br = flash_prompt(q.reshape(bt, t, hd), k16.reshape(bt, t, hd), vt, t=FLASH_T, **common)
            else:
                q, k, v, mq, gate = norm_matmul(
                    x2, wts["norm_g"][i], w_in,
                    (one(hd, BF16), one(hd, F32), one(hd, F32), one(MEM_W, BF16), one(d, BF16)))
                ck, cv = past[2][j], past[3][j]
                rows = ck.shape[1]
                br = flash_attention(q.reshape(bt, t, hd), ck.reshape(bt, rows, hd), cv.reshape(bt, rows, hd),
                                     k.reshape(bt, t, hd), v.reshape(bt, t, hd),
                                     tq=t, tk=FLASH_T, diagonal_tail=False, q_pos0=rows, **common)
            new_b.append((k.reshape(bt, t, B_HEADS, 2, HEAD_DIM), v.reshape(bt, t, B_HEADS, 2 * HEAD_DIM)))
        else:
            cq, ckv, kra, krb, mq, gate = norm_matmul(
                x2, wts["norm_g"][i], wts["w_in_c"][j],
                (one(C_Q_LORA, F32), one(C_KV_LORA, F32), one(LANES, F32), one(LANES, F32),
                 one(MEM_W, BF16), one(d, BF16)))
            wq, wqr = wts["w_uq_c"][j]
            wk, wv, place = wts["w_ukv_c"][j]
            qcat, lat, kr = c_pre(cq, ckv, kra, krb, wts["q_norm_g_c"][j], wts["kv_norm_g_c"][j],
                                  wq, wqr, _rope_tables(pos), t)
            q3 = qcat.reshape(bt, t, -1)
            if past is None:
                kcat, vt = c_kv(lat, kr, wk, wv.T, place, v_transposed=True, tm=FLASH_T)
                br = flash_prompt(q3, kcat.reshape(bt, t, -1), vt, mode="mla", t=FLASH_T)
            else:
                kcat, vcat = c_kv(lat, kr, wk, wv, place, v_transposed=False)
                cl, cr = past[4][j], past[5][j]
                rows = cl.shape[1]
                cr = jnp.pad(cr.reshape(bt * rows, -1), ((0, 0), (0, LANES - C_ROPE)))
                kc_, vc_ = c_kv(cl.reshape(bt * rows, -1), cr, wk, wv, place, v_transposed=False)
                br = flash_attention(q3, kc_.reshape(bt, rows, -1), vc_.reshape(bt, rows, -1),
                                     kcat.reshape(bt, t, -1), vcat.reshape(bt, t, -1),
                                     mode="mla", tq=t, tk=FLASH_T, diagonal_tail=False, q_pos0=rows)
            new_c.append((lat.reshape(bt, t, -1), kr[:, :C_ROPE].reshape(bt, t, -1)))
        last = i == depth - 1
        x2 = gated_out(br.reshape(m, -1), mq, gate, x2, wts["w_out"][i], wts["final_g"],
                       mem_k[i], mem_vt[i], final=last)
    return x2.reshape(bt, t, d), new_a, new_b, new_c


def kernel(x_prompt, x_sample, cache_a_k, cache_a_v, cache_b_k, cache_b_v, cache_c_lat, cache_c_rope,
           cache_mem_k, cache_mem_v, mem_prompt, norm_g, final_g, mem_norm_g, w_mem_kv, w_out, w_in_a,
           rel_bias_a, w_in_b, lambda_b, subln_g_b, w_in_c, q_norm_g_c, kv_norm_g_c, w_uq_c, w_ukv_c):
    depth = w_out.shape[0]
    bp, n_mem, d = mem_prompt.shape
    mem_heads = MEM_W // HEAD_DIM
    wts = dict(
        norm_g=norm_g, final_g=final_g, rel_bias_a=rel_bias_a, lambda_b=lambda_b, subln_g_b=subln_g_b,
        q_norm_g_c=q_norm_g_c, kv_norm_g_c=kv_norm_g_c,
        w_out=[w_out[i].astype(BF16) for i in range(depth)],
        w_in_a=[_prep_in_ab(w_in_a[j]) for j in range(w_in_a.shape[0])],
        w_in_b=[_prep_in_ab(w_in_b[j]) for j in range(w_in_b.shape[0])],
        w_in_c=[_prep_in_c(w_in_c[j]) for j in range(w_in_c.shape[0])],
        w_uq_c=[_prep_uq(w_uq_c[j]) for j in range(w_uq_c.shape[0])],
        w_ukv_c=[_prep_ukv(w_ukv_c[j]) for j in range(w_ukv_c.shape[0])],
    )
    mem2 = mem_prompt.reshape(bp * n_mem, d)
    mem_k_p, mem_v_p, mem_vt_p = [], [], []
    for i in range(depth):
        w_kv = w_mem_kv[i].astype(BF16)
        mk, mv, mvt = norm_matmul(mem2, mem_norm_g[i], w_kv, ((MEM_W, (F32,)), (MEM_W, (F32,))),
                                  wt=w_kv[:, MEM_W:].T, t_blk=n_mem, tm=n_mem)
        mem_k_p.append(mk.reshape(bp, n_mem, MEM_W))
        mem_v_p.append(mv.reshape(bp, n_mem, MEM_W))
        mem_vt_p.append(mvt)

    y_p, na_p, nb_p, nc_p = _trunk(x_prompt, 0, mem_k_p, mem_vt_p, None, wts)

    bs = x_sample.shape[0]
    mem_k_s = [cache_mem_k[i].reshape(bs, n_mem, MEM_W) for i in range(depth)]
    mem_vt_s = [jnp.swapaxes(cache_mem_v[i].reshape(bs, n_mem, MEM_W), 1, 2).astype(BF16) for i in range(depth)]
    past = (cache_a_k, cache_a_v, cache_b_k, cache_b_v, cache_c_lat, cache_c_rope)
    y_s, na_s, nb_s, nc_s = _trunk(x_sample, cache_b_k.shape[2], mem_k_s, mem_vt_s, past, wts)

    stk = lambda lst, n: jnp.stack([s[n] for s in lst])
    heads4 = lambda lst: jnp.stack([a.reshape(bp, n_mem, mem_heads, HEAD_DIM) for a in lst])
    return (y_p, y_s,
            stk(na_p, 0), stk(na_p, 1), stk(na_s, 0), stk(na_s, 1),
            stk(nb_p, 0), stk(nb_p, 1), stk(nb_s, 0), stk(nb_s, 1),
            stk(nc_p, 0), stk(nc_p, 1), stk(nc_s, 0), stk(nc_s, 1),
            heads4(mem_k_p), heads4(mem_v_p))
```

```python
import functools
import itertools
import math

import jax
import jax.numpy as jnp
from jax import lax
from jax.experimental import pallas as pl
from jax.experimental.pallas import tpu as pltpu

F32 = jnp.float32
BF16 = jnp.bfloat16

LANES = 128
VMEM_LIMIT = 56 * 1024 * 1024

EPS = 1e-6
NEG = -1e30
LOG2E = math.log2(math.e)
CHUNK = 64
HEAD_DIM = 64
Q_SCALE = LOG2E * HEAD_DIM ** -0.5
MEM_W = 256
A_HEADS = 12
A_PAST_ROWS = 512
A_BAND_CHUNKS = 8
A_REL_CLIP = 128
B_HEADS = 6
C_HEADS = 12
C_NOPE = 64
C_ROPE = 32
C_V = 64
C_Q_LORA = 384
C_KV_LORA = 256
ROPE_BASE = 10000.0
N_MIXERS = 3


def _params(n_grid):
    return pltpu.CompilerParams(dimension_semantics=("arbitrary",) * n_grid,
                                vmem_limit_bytes=VMEM_LIMIT)


def _row_tile(m, want):
    t = min(m, want)
    assert m % t == 0, (m, t)
    return t


def _rms(x, g):
    return x * lax.rsqrt(jnp.mean(x * x, axis=-1, keepdims=True) + EPS) * g


NT = (((1,), (1,)), ((), ()))


def _store_blocks(o_ref, rt):
    blk = o_ref.shape[-1]
    for jb in range(o_ref.shape[0]):
        o_ref[jb] = rt[:, jb * blk:(jb + 1) * blk]


def _norm_matmul_kernel(x_ref, g_ref, w_ref, *rest, segs, has_t, t_seg):
    h = _rms(x_ref[...], g_ref[...]).astype(BF16)
    out_refs = rest[1:] if has_t else rest
    off = n_out = 0
    for si, (width, dtypes) in enumerate(segs):
        r = jnp.dot(h, w_ref[:, off:off + width], preferred_element_type=F32)
        if si == t_seg:
            _store_blocks(out_refs[-1], r.T.astype(BF16))
        for dt in dtypes:
            o_ref = out_refs[n_out]
            if len(o_ref.shape) == 2:
                o_ref[...] = r.astype(dt)
            else:
                w = o_ref.shape[-1]
                for c, ix in enumerate(itertools.product(*[range(n) for n in o_ref.shape[1:-1]])):
                    o_ref[(slice(None),) + ix + (slice(None),)] = r[:, c * w:(c + 1) * w].astype(dt)
            n_out += 1
        off += width
    if has_t:
        _store_blocks(out_refs[n_out], lax.dot_general(rest[0][...], h, NT, preferred_element_type=F32).astype(BF16))


def norm_matmul(x, g, w, segs, wt=None, t_seg=None, t_blk=None, tm=512, split=None):
    m, d = x.shape
    tm = _row_tile(m, tm)
    n = w.shape[1]
    assert n == sum(wd for wd, _ in segs)
    row = lambda i: (i, 0)
    fixed = lambda i: (0, 0)
    in_specs = [pl.BlockSpec((tm, d), row), pl.BlockSpec((1, d), fixed), pl.BlockSpec((d, n), fixed)]
    args = [x, g.reshape(1, d).astype(F32), w]
    out_specs = [pl.BlockSpec((tm, wd), row) for wd, dts in segs for _ in dts]
    out_shape = [jax.ShapeDtypeStruct((m, wd), dt) for wd, dts in segs for dt in dts]
    for pos, tail in (split or {}).items():
        assert math.prod(tail) == out_shape[pos].shape[1]
        out_specs[pos] = pl.BlockSpec((tm,) + tail, lambda i, n=len(tail): (i,) + (0,) * n)
        out_shape[pos] = jax.ShapeDtypeStruct((m,) + tail, out_shape[pos].dtype)
    assert wt is None or t_seg is None
    if wt is not None:
        in_specs.append(pl.BlockSpec(wt.shape, fixed))
        args.append(wt)
    if wt is not None or t_seg is not None:
        n_t = wt.shape[0] if wt is not None else segs[t_seg][0]
        assert tm % t_blk == 0
        out_specs.append(pl.BlockSpec((tm // t_blk, n_t, t_blk), lambda i: (i, 0, 0)))
        out_shape.append(jax.ShapeDtypeStruct((m // t_blk, n_t, t_blk), BF16))
    return pl.pallas_call(
        functools.partial(_norm_matmul_kernel, segs=tuple(segs), has_t=wt is not None, t_seg=t_seg),
        grid=(m // tm,),
        in_specs=in_specs,
        out_specs=out_specs,
        out_shape=out_shape,
        compiler_params=_params(1),
    )(*args)


def _gated_out_kernel(br_ref, mq_ref, gate_ref, x_ref, w_ref, fg_ref, mk_ref, mvt_ref, o_ref, *, final):
    nb = br_ref.shape[-1]
    tm = x_ref.shape[0]
    halves = _half_masks()
    mq = mq_ref[...]
    mem = []
    for pair in range(mq.shape[1] // LANES):
        cols = slice(pair * LANES, (pair + 1) * LANES)
        q = mq[:, cols]
        q2 = jnp.concatenate([jnp.where(halves[h], q, jnp.zeros_like(q)) for h in range(2)], axis=0)
        st = lax.dot_general(mk_ref[:, cols].astype(BF16), q2, NT, preferred_element_type=F32)
        p = jnp.exp2(st - jnp.max(st, axis=0, keepdims=True))
        l = jnp.sum(p, axis=0, keepdims=True)
        o = jnp.dot(mvt_ref[cols, :], p.astype(BF16), preferred_element_type=F32) / l
        mem.append(jnp.concatenate([o[:HEAD_DIM, :tm], o[HEAD_DIM:, tm:]], axis=0).T)
    mo = jnp.concatenate(mem, axis=1)
    gate = gate_ref[...].astype(F32)
    sg = gate * jax.nn.sigmoid(gate)
    y1 = (br_ref[...].astype(F32) * sg[:, :nb]).astype(BF16)
    y2 = (mo * sg[:, nb:]).astype(BF16)
    acc = jnp.dot(y1, w_ref[:nb, :], preferred_element_type=F32)
    acc += jnp.dot(y2, w_ref[nb:, :], preferred_element_type=F32)
    xn = x_ref[...] + acc
    o_ref[...] = _rms(xn, fg_ref[...]) if final else xn


def gated_out(br, mq, gate, x, w, final_g, mem_k, mem_vt, final, tm=512):
    m, d = x.shape
    t_len = m // mem_k.shape[0]
    tm = _row_tile(t_len, tm)
    n_t = t_len // tm
    nb, nm, ng = br.shape[1], mq.shape[1], gate.shape[1]
    row = lambda i: (i, 0)
    fixed = lambda i: (0, 0)
    per_batch = lambda i: (i // n_t, 0, 0)
    return pl.pallas_call(
        functools.partial(_gated_out_kernel, final=final),
        grid=(m // tm,),
        in_specs=[pl.BlockSpec((tm, nb), row), pl.BlockSpec((tm, nm), row), pl.BlockSpec((tm, ng), row),
                  pl.BlockSpec((tm, d), row), pl.BlockSpec((ng, d), fixed), pl.BlockSpec((1, d), fixed),
                  pl.BlockSpec((None,) + mem_k.shape[1:], per_batch),
                  pl.BlockSpec((None,) + mem_vt.shape[1:], per_batch)],
        out_specs=pl.BlockSpec((tm, d), row),
        out_shape=jax.ShapeDtypeStruct((m, d), F32),
        compiler_params=_params(1),
    )(br, mq, gate, x, w, final_g.reshape(1, d).astype(F32), mem_k, mem_vt)


def _half_masks():
    lane = lax.broadcasted_iota(jnp.int32, (1, LANES), 1)
    lo = lane < HEAD_DIM
    return lo, jnp.logical_not(lo)


def _attn_window_kernel(q_ref, k_ref, v_ref, *rest, tq, win, back):
    if len(rest) == 2:
        bias_ref, o_ref = rest
    else:
        bias_ref, (o_ref,) = None, rest
    tk_all = k_ref.shape[0]
    if win == tk_all:
        k = k_ref[...]
        v = v_ref[...]
    else:
        start = jnp.clip(pl.program_id(2) * tq - back, 0, tk_all - win)
        start = pl.multiple_of(start, CHUNK)
        k = k_ref[pl.ds(start, win), :]
        v = v_ref[pl.ds(start, win), :]
    k = k.astype(BF16)
    v = v.astype(BF16)
    q = q_ref[...]
    halves = _half_masks()
    outs = []
    for i in range(2):
        qm = jnp.where(halves[i], q, jnp.zeros_like(q))
        s = lax.dot_general(qm, k, (((1,), (1,)), ((), ())), preferred_element_type=F32)
        if bias_ref is not None:
            s = s + bias_ref[i]
        m = jnp.max(s, axis=-1, keepdims=True)
        p = jnp.exp2(s - m)
        l = jnp.sum(p, axis=-1, keepdims=True)
        o = jnp.dot(p.astype(BF16), v, preferred_element_type=F32)
        outs.append(o / l)
    o_ref[...] = jnp.where(halves[0], outs[0], outs[1]).astype(o_ref.dtype)


def attn_window(q, k, v, bias, *, tq, win, back):
    bt, t_q, gw = q.shape
    g = gw // LANES
    t_k = k.shape[1]
    tq = _row_tile(t_q, tq)
    in_specs = [pl.BlockSpec((None, tq, LANES), lambda b, h, i: (b, i, h)),
                pl.BlockSpec((None, t_k, LANES), lambda b, h, i: (b, 0, h)),
                pl.BlockSpec((None, t_k, LANES), lambda b, h, i: (b, 0, h))]
    args = [q, k, v]
    if bias is not None:
        last = bias.shape[0] - 1
        in_specs.append(pl.BlockSpec((None, 2, tq, win), lambda b, h, i: (jnp.minimum(i, last), h, 0, 0)))
        args.append(bias)
    return pl.pallas_call(
        functools.partial(_attn_window_kernel, tq=tq, win=win, back=back),
        grid=(bt, g, t_q // tq),
        in_specs=in_specs,
        out_specs=pl.BlockSpec((None, tq, LANES), lambda b, h, i: (b, i, h)),
        out_shape=jax.ShapeDtypeStruct((bt, t_q, gw), BF16),
        compiler_params=_params(3),
    )(*args)


def _attn_window_t_kernel(q_ref, k_ref, vt_ref, *rest, tq, n_sub, n_win, back_blocks):
    if len(rest) == 2:
        bias_ref, o_ref = rest
    else:
        bias_ref, (o_ref,) = None, rest
    n_blk, _, blk = vt_ref.shape
    halves = _half_masks()

    def score(sub):
        i = pl.program_id(2) * n_sub + sub
        if n_win == n_blk:
            b0 = 0
            k = k_ref[...]
        else:
            b0 = jnp.clip(i - back_blocks, 0, n_blk - n_win)
            k = k_ref[pl.ds(pl.multiple_of(b0 * blk, blk), n_win * blk), :]
        q = q_ref[sub * tq:(sub + 1) * tq, :]
        q2 = jnp.concatenate([jnp.where(halves[h], q, jnp.zeros_like(q)) for h in range(2)], axis=0)
        st = lax.dot_general(k.astype(BF16), q2, NT, preferred_element_type=F32)
        if bias_ref is not None:
            st = st + bias_ref[jnp.minimum(i, bias_ref.shape[0] - 1)]
        return b0, st, jnp.max(st, axis=0, keepdims=True)

    def accumulate(sub, b0, st, m):
        p = jnp.exp2(st - m)
        l = jnp.sum(p, axis=0, keepdims=True)
        pb = p.astype(BF16)
        o = None
        for j in range(n_win):
            oj = jnp.dot(vt_ref[b0 + j], pb[j * blk:(j + 1) * blk, :], preferred_element_type=F32)
            o = oj if o is None else o + oj
        o = o / l
        o = jnp.concatenate([o[:HEAD_DIM, :tq], o[HEAD_DIM:, tq:]], axis=0).T
        o_ref[sub * tq:(sub + 1) * tq, :] = o.astype(o_ref.dtype)

    staged = [score(sub) for sub in range(n_sub)]
    for sub in range(n_sub):
        accumulate(sub, *staged[sub])


def attn_window_t(q, k, vt, bias, *, tq, n_sub, n_win, back_blocks):
    bt, t_q, gw = q.shape
    g = gw // LANES
    t_k = k.shape[1]
    blk = vt.shape[2]
    n_blk = t_k // blk
    assert vt.shape == (bt * n_blk, gw, blk), vt.shape
    rows = n_sub * tq
    assert t_q % rows == 0
    in_specs = [pl.BlockSpec((None, rows, LANES), lambda b, h, i: (b, i, h)),
                pl.BlockSpec((None, t_k, LANES), lambda b, h, i: (b, 0, h)),
                pl.BlockSpec((n_blk, LANES, blk), lambda b, h, i: (b, h, 0))]
    args = [q, k, vt]
    if bias is not None:
        in_specs.append(pl.BlockSpec((bias.shape[0], None, n_win * blk, 2 * tq), lambda b, h, i: (0, h, 0, 0)))
        args.append(bias)
    return pl.pallas_call(
        functools.partial(_attn_window_t_kernel, tq=tq, n_sub=n_sub, n_win=n_win, back_blocks=back_blocks),
        grid=(bt, g, t_q // rows),
        in_specs=in_specs,
        out_specs=pl.BlockSpec((None, rows, LANES), lambda b, h, i: (b, i, h)),
        out_shape=jax.ShapeDtypeStruct((bt, t_q, gw), BF16),
        compiler_params=_params(3),
    )(*args)


def _flash_kernel(slope_ref, q_ref, km_ref, vm_ref, kt_ref, vt_ref, *rest,
                  mode, tq, tk, tt, n_main_static, q_pos0, lam_init):
    if mode == "diff":
        lam_ref, subg_ref, o_ref, m_scr, l_scr, acc_scr = rest
    else:
        o_ref, m_scr, l_scr, acc_scr = rest
    head = pl.program_id(1)
    qi = pl.program_id(2)
    q = q_ref[...]
    halves = _half_masks()
    if mode == "diff":
        qs = [jnp.where(halves[i], q, jnp.zeros_like(q)) for i in range(2)]
        slope = slope_ref[head]
    else:
        qs = [q[:, :LANES], q[:, LANES:]]
        slope = None

    m_scr[...] = jnp.full(m_scr.shape, NEG, F32)
    l_scr[...] = jnp.zeros(l_scr.shape, F32)
    acc_scr[...] = jnp.zeros(acc_scr.shape, F32)

    def update(i, s, shift, v):
        m_old = m_scr[i]
        m_new = jnp.maximum(m_old, jnp.max(s, axis=-1, keepdims=True) + shift)
        alpha = jnp.exp2(m_old - m_new)
        p = jnp.exp2(s - (m_new - shift))
        l_scr[i] = alpha * l_scr[i] + jnp.sum(p, axis=-1, keepdims=True)
        acc_scr[i] = alpha * acc_scr[i] + jnp.dot(p.astype(BF16), v, preferred_element_type=F32)
        m_scr[i] = m_new

    def k_of(kblk, i):
        return kblk if mode == "diff" else kblk[:, i * LANES:(i + 1) * LANES]

    nt = (((1,), (1,)), ((), ()))
    q_base = q_pos0 + qi * tq

    if mode == "diff":
        ri = lax.broadcasted_iota(jnp.int32, (tq, tk), 0)
        ci = lax.broadcasted_iota(jnp.int32, (tq, tk), 1)
        main_tile = slope * (ci - ri).astype(F32)

    def main_body(kb, carry):
        off = pl.multiple_of(kb * tk, tk)
        kblk = km_ref[pl.ds(off, tk), :].astype(BF16)
        vblk = vm_ref[pl.ds(off, tk), :].astype(BF16)
        for i in range(2):
            s = lax.dot_general(qs[i], k_of(kblk, i), nt, preferred_element_type=F32)
            if mode == "diff":
                shift = -slope * (q_base - kb * tk).astype(F32)
                update(i, s + main_tile, shift, vblk)
            else:
                update(i, s, 0.0, vblk)
        return carry

    n_main = qi if n_main_static is None else n_main_static
    lax.fori_loop(0, n_main, main_body, 0)

    ri = lax.broadcasted_iota(jnp.int32, (tq, tt), 0)
    ci = lax.broadcasted_iota(jnp.int32, (tq, tt), 1)
    valid = (ci // CHUNK) <= (ri // CHUNK)
    if mode == "diff":
        tail_tile = jnp.where(valid, -slope * jnp.abs(ri - ci).astype(F32), NEG)
    else:
        tail_tile = jnp.where(valid, 0.0, NEG).astype(F32)
    kblk = kt_ref[...].astype(BF16)
    vblk = vt_ref[...].astype(BF16)
    for i in range(2):
        s = lax.dot_general(qs[i], k_of(kblk, i), nt, preferred_element_type=F32)
        update(i, s + tail_tile, 0.0, vblk)

    o0 = acc_scr[0] / l_scr[0]
    o1 = acc_scr[1] / l_scr[1]
    if mode == "diff":
        lp = lam_ref[...]
        lam = (jnp.exp(jnp.sum(lp[0:1] * lp[1:2], axis=-1, keepdims=True))
               - jnp.exp(jnp.sum(lp[2:3] * lp[3:4], axis=-1, keepdims=True)) + lam_init)
        o = o0 - lam * o1
        o_ref[...] = (_rms(o, subg_ref[...]) * (1.0 - lam_init)).astype(o_ref.dtype)
    else:
        o_ref[...] = jnp.where(halves[0], o0, o1).astype(o_ref.dtype)


def flash_attention(q, k_main, v_main, k_tail, v_tail, *, mode, tq, tk, diagonal_tail, q_pos0,
                    lam_p=None, sub_g=None, lam_init=0.0):
    bt, t_q = q.shape[0], q.shape[1]
    tq = _row_tile(t_q, tq)
    qw = LANES if mode == "diff" else 2 * LANES
    groups = q.shape[2] // qw
    t_main = k_main.shape[1]
    tk = _row_tile(t_main, tk)
    if diagonal_tail:
        assert tk == tq
        tt, n_main_static = tq, None
        tail_idx = lambda b, h, i, *_: (b, i, h)
    else:
        tt, n_main_static = k_tail.shape[1], t_main // tk
        tail_idx = lambda b, h, i, *_: (b, 0, h)
    q_idx = lambda b, h, i, *_: (b, i, h)
    main_idx = lambda b, h, i, *_: (b, 0, h)
    in_specs = [pl.BlockSpec((None, tq, qw), q_idx),
                pl.BlockSpec((None, t_main, qw), main_idx),
                pl.BlockSpec((None, t_main, LANES), main_idx),
                pl.BlockSpec((None, tt, qw), tail_idx),
                pl.BlockSpec((None, tt, LANES), tail_idx)]
    args = [q, k_main, v_main, k_tail, v_tail]
    if mode == "diff":
        slopes = LOG2E * jnp.exp2(-8.0 * jnp.arange(1, groups + 1, dtype=F32) / groups)
        in_specs += [pl.BlockSpec(lam_p.shape, lambda b, h, i, *_: (0, 0)),
                     pl.BlockSpec((1, LANES), lambda b, h, i, *_: (0, 0))]
        args += [lam_p.astype(F32), sub_g.reshape(1, LANES).astype(F32)]
    else:
        slopes = jnp.zeros((groups,), F32)
    grid_spec = pltpu.PrefetchScalarGridSpec(
        num_scalar_prefetch=1,
        grid=(bt, groups, t_q // tq),
        in_specs=in_specs,
        out_specs=pl.BlockSpec((None, tq, LANES), q_idx),
        scratch_shapes=[pltpu.VMEM((2, tq, 1), F32), pltpu.VMEM((2, tq, 1), F32),
                        pltpu.VMEM((2, tq, LANES), F32)])
    return pl.pallas_call(
        functools.partial(_flash_kernel, mode=mode, tq=tq, tk=tk, tt=tt, n_main_static=n_main_static,
                          q_pos0=q_pos0, lam_init=lam_init),
        grid_spec=grid_spec,
        out_shape=jax.ShapeDtypeStruct((bt, t_q, groups * LANES), BF16),
        compiler_params=_params(3),
    )(slopes, *args)


def _flash_prompt_kernel(slope_ref, q_ref, k_ref, vt_ref, *rest, mode, t, lam_init):
    if mode == "diff":
        lam_ref, subg_ref, o_ref = rest[:3]
        rest = rest[3:]
    else:
        o_ref = rest[0]
        rest = rest[1:]
    m_scr, acc_scr, main_tile_scr, diag_tile_scr = rest[:4]
    st_scrs, msub_scrs, alpha_scrs = rest[4:6], rest[6:8], rest[8:10]
    head = pl.program_id(1)
    qi = pl.program_id(2)
    q = q_ref[...]
    halves = _half_masks()
    if mode == "diff":
        qs = [jnp.where(halves[i], q, jnp.zeros_like(q)) for i in range(2)]
        slope = slope_ref[head]
    else:
        qs = [q[:, :LANES], q[:, LANES:]]
    v_rows = acc_scr.shape[1] - ONES_ROWS
    ones = jnp.ones((ONES_ROWS, t), BF16)

    m_scr[...] = jnp.full(m_scr.shape, NEG, F32)
    acc_scr[...] = jnp.zeros(acc_scr.shape, F32)

    @pl.when(qi == 0)
    def _():
        kj = lax.broadcasted_iota(jnp.int32, (t, t), 0)
        qj = lax.broadcasted_iota(jnp.int32, (t, t), 1)
        valid = (kj // CHUNK) <= (qj // CHUNK)
        if mode == "diff":
            main_tile_scr[...] = slope * (kj - qj).astype(F32)
            diag_tile_scr[...] = jnp.where(valid, -slope * jnp.abs(qj - kj).astype(F32), NEG)
        else:
            diag_tile_scr[...] = jnp.where(valid, 0.0, NEG).astype(F32)


    def score(kb, diag, slot):
        off = pl.multiple_of(kb * t, t)
        kblk = k_ref[pl.ds(off, t), :]
        if mode == "diff" and not diag:
            shift = -slope * ((qi - kb) * t).astype(F32)
        else:
            shift = 0.0
        for i in range(2):
            ki = kblk if mode == "diff" else kblk[:, i * LANES:(i + 1) * LANES]
            st = lax.dot_general(ki, qs[i], NT, preferred_element_type=F32)
            if diag:
                st = st + diag_tile_scr[...]
            elif mode == "diff":
                st = st + main_tile_scr[...]
            st_scrs[slot][i] = st
            m_old = m_scr[i]
            m_new = jnp.maximum(m_old, jnp.max(st, axis=0, keepdims=True) + shift)
            m_scr[i] = m_new
            msub_scrs[slot][i] = m_new - shift
            alpha_scrs[slot][i] = jnp.exp2(m_old - m_new)

    def accumulate(kb, slot):
        vt = vt_ref[kb]
        for i in range(2):
            alpha = alpha_scrs[slot][i]
            vi = vt if mode == "diff" else vt[i * v_rows:(i + 1) * v_rows, :]
            vi = jnp.concatenate([vi, ones], axis=0)
            p = jnp.exp2(st_scrs[slot][i] - msub_scrs[slot][i])
            acc_scr[i] = alpha * acc_scr[i] + jnp.dot(vi, p.astype(BF16), preferred_element_type=F32)

    lead = jnp.logical_and(qi >= 2, qi % 2 == 0).astype(jnp.int32)

    @pl.when(lead == 1)
    def _():
        score(0, False, 0)
        accumulate(0, 0)

    @pl.when(qi == 0)
    def _():
        score(0, True, 0)

    @pl.when(qi >= 1)
    def _():
        score(lead, False, 0)

    def pair(kb):
        score(kb + 1, False, 1)
        accumulate(kb, 0)
        score(kb + 2, False, 0)
        accumulate(kb + 1, 1)

    def quad_body(j, carry):
        pair(lead + 4 * j)
        pair(lead + 4 * j + 2)
        return carry

    def pair_body(j, carry):
        pair(lead + 2 * j)
        return carry

    n_pairs = jnp.maximum((qi - lead - 1) // 2, 0)
    lax.fori_loop(0, n_pairs // 2, quad_body, 0)
    lax.fori_loop(n_pairs - n_pairs % 2, n_pairs, pair_body, 0)

    @pl.when(qi >= 1)
    def _():
        score(qi, True, 1)
        accumulate(qi - 1, 0)
        accumulate(qi, 1)

    @pl.when(qi == 0)
    def _():
        accumulate(0, 0)

    o0 = acc_scr[0, :v_rows] / acc_scr[0, v_rows:v_rows + 1]
    o1 = acc_scr[1, :v_rows] / acc_scr[1, v_rows:v_rows + 1]
    if mode == "diff":
        lp = lam_ref[...]
        lam = (jnp.exp(jnp.sum(lp[0:1] * lp[1:2], axis=-1, keepdims=True))
               - jnp.exp(jnp.sum(lp[2:3] * lp[3:4], axis=-1, keepdims=True)) + lam_init)
        o = (o0 - lam * o1).T
        o_ref[...] = (_rms(o, subg_ref[...]) * (1.0 - lam_init)).astype(o_ref.dtype)
    else:
        o_ref[...] = jnp.concatenate([o0, o1], axis=0).T.astype(o_ref.dtype)


def flash_prompt(q, k, vt, *, mode, t, lam_p=None, sub_g=None, lam_init=0.0):
    bt, t_all = q.shape[0], q.shape[1]
    n_blk = t_all // t
    qw = LANES if mode == "diff" else 2 * LANES
    groups = q.shape[2] // qw
    assert vt.shape == (bt * n_blk, groups * LANES, t), vt.shape
    q_idx = lambda b, h, i, *_: (b, i, h)
    in_specs = [pl.BlockSpec((None, t, qw), q_idx),
                pl.BlockSpec((None, t_all, qw), lambda b, h, i, *_: (b, 0, h)),
                pl.BlockSpec((n_blk, LANES, t), lambda b, h, i, *_: (b, h, 0))]
    args = [q, k, vt]
    if mode == "diff":
        slopes = LOG2E * jnp.exp2(-8.0 * jnp.arange(1, groups + 1, dtype=F32) / groups)
        in_specs += [pl.BlockSpec(lam_p.shape, lambda b, h, i, *_: (0, 0)),
                     pl.BlockSpec((1, LANES), lambda b, h, i, *_: (0, 0))]
        args += [lam_p.astype(F32), sub_g.reshape(1, LANES).astype(F32)]
        v_rows = LANES
    else:
        slopes = jnp.zeros((groups,), F32)
        v_rows = LANES // 2
    stat = pltpu.VMEM((2, 1, t), F32)
    tile = pltpu.VMEM((t, t), F32)
    grid_spec = pltpu.PrefetchScalarGridSpec(
        num_scalar_prefetch=1,
        grid=(bt, groups, n_blk),
        in_specs=in_specs,
        out_specs=pl.BlockSpec((None, t, LANES), q_idx),
        scratch_shapes=[stat, pltpu.VMEM((2, v_rows + ONES_ROWS, t), F32), tile, tile,
                        pltpu.VMEM((2, t, t), F32), pltpu.VMEM((2, t, t), F32), stat, stat, stat, stat])
    return pl.pallas_call(
        functools.partial(_flash_prompt_kernel, mode=mode, t=t, lam_init=lam_init),
        grid_spec=grid_spec,
        out_shape=jax.ShapeDtypeStruct((bt, t_all, groups * LANES), BF16),
        compiler_params=_params(3),
    )(slopes, *args)


def _c_pre_kernel(cq_ref, ckv_ref, kra_ref, krb_ref, gq_ref, gkv_ref, wq_ref, wqr_ref,
                  cq_tab, sq_tab, ck_tab, sk_tab, q_out, lat_out, kr_out):
    qn = _rms(cq_ref[...], gq_ref[...]).astype(BF16)
    qc = jnp.dot(qn, wq_ref[...], preferred_element_type=F32)
    qr = jnp.dot(qn, wqr_ref[...], preferred_element_type=F32)
    cos, sin = cq_tab[...], sq_tab[...]
    for h in range(C_HEADS):
        sl = slice(h * LANES, (h + 1) * LANES)
        q_out[:, sl] = (qc[:, sl] * cos + qr[:, sl] * sin).astype(q_out.dtype)
    lat_out[...] = _rms(ckv_ref[...], gkv_ref[...])
    kr = kra_ref[...] * ck_tab[...] + krb_ref[...] * sk_tab[...]
    kr_out[...] = kr


def c_pre(cq, ckv, kra, krb, gq, gkv, wq, wqr, tabs, t_len, tm=512):
    m = cq.shape[0]
    tm = _row_tile(t_len, tm)
    n_t = t_len // tm
    row = lambda i: (i, 0)
    fixed = lambda i: (0, 0)
    trow = lambda i: (i % n_t, 0)
    qw = wq.shape[1]
    return pl.pallas_call(
        _c_pre_kernel,
        grid=(m // tm,),
        in_specs=[pl.BlockSpec((tm, C_Q_LORA), row), pl.BlockSpec((tm, C_KV_LORA), row),
                  pl.BlockSpec((tm, LANES), row), pl.BlockSpec((tm, LANES), row),
                  pl.BlockSpec((1, C_Q_LORA), fixed), pl.BlockSpec((1, C_KV_LORA), fixed),
                  pl.BlockSpec(wq.shape, fixed), pl.BlockSpec(wqr.shape, fixed),
                  pl.BlockSpec((tm, LANES), trow), pl.BlockSpec((tm, LANES), trow),
                  pl.BlockSpec((tm, LANES), trow), pl.BlockSpec((tm, LANES), trow)],
        out_specs=[pl.BlockSpec((tm, qw), row), pl.BlockSpec((tm, C_KV_LORA), row),
                   pl.BlockSpec((tm, LANES), row)],
        out_shape=[jax.ShapeDtypeStruct((m, qw), BF16), jax.ShapeDtypeStruct((m, C_KV_LORA), F32),
                   jax.ShapeDtypeStruct((m, LANES), F32)],
        compiler_params=_params(1),
    )(cq, ckv, kra, krb, gq.reshape(1, -1).astype(F32), gkv.reshape(1, -1).astype(F32), wq, wqr, *tabs)


def _c_kv_kernel(lat_ref, kr_ref, wk_ref, wv_ref, place_ref, k_out, v_out, *, v_transposed):
    lat = lat_ref[...].astype(BF16)
    k = jnp.dot(lat, wk_ref[...], preferred_element_type=F32)
    k += jnp.dot(kr_ref[...].astype(BF16), place_ref[...], preferred_element_type=F32)
    k_out[...] = k.astype(k_out.dtype)
    if v_transposed:
        v = lax.dot_general(wv_ref[...], lat, NT, preferred_element_type=F32)
    else:
        v = jnp.dot(lat, wv_ref[...], preferred_element_type=F32)
    v_out[...] = v.astype(v_out.dtype)


def c_kv(lat, kr, wk, wv, place, v_transposed, tm=512):
    m = lat.shape[0]
    tm = _row_tile(m, tm)
    row = lambda i: (i, 0)
    fixed = lambda i: (0, 0)
    if v_transposed:
        n_v = wv.shape[0]
        v_spec = pl.BlockSpec((None, n_v, tm), lambda i: (i, 0, 0))
        v_shape = jax.ShapeDtypeStruct((m // tm, n_v, tm), BF16)
    else:
        n_v = wv.shape[1]
        v_spec = pl.BlockSpec((tm, n_v), row)
        v_shape = jax.ShapeDtypeStruct((m, n_v), BF16)
    return pl.pallas_call(
        functools.partial(_c_kv_kernel, v_transposed=v_transposed),
        grid=(m // tm,),
        in_specs=[pl.BlockSpec((tm, C_KV_LORA), row), pl.BlockSpec((tm, LANES), row),
                  pl.BlockSpec(wk.shape, fixed), pl.BlockSpec(wv.shape, fixed), pl.BlockSpec(place.shape, fixed)],
        out_specs=[pl.BlockSpec((tm, wk.shape[1]), row), v_spec],
        out_shape=[jax.ShapeDtypeStruct((m, wk.shape[1]), BF16), v_shape],
        compiler_params=_params(1),
    )(lat, kr, wk, wv, place)


def _prep_in_ab(w):
    n_mix = w.shape[1] - MEM_W - w.shape[0]
    n_q = n_mix // 3
    scale = jnp.concatenate([jnp.full((n_q,), Q_SCALE, F32), jnp.ones((n_mix - n_q,), F32),
                             jnp.full((MEM_W,), Q_SCALE, F32), jnp.ones((w.shape[0],), F32)])
    return (w * scale[None, :]).astype(BF16)


def _prep_in_c(w):
    d = w.shape[0]
    o = C_Q_LORA + C_KV_LORA
    half = C_ROPE // 2
    kr = w[:, o:o + C_ROPE]
    rot = jnp.concatenate([-kr[:, half:], kr[:, :half]], axis=1)
    pad = jnp.zeros((d, LANES - C_ROPE), F32)
    mq = w[:, o + C_ROPE:o + C_ROPE + MEM_W] * (Q_SCALE)
    gate = w[:, o + C_ROPE + MEM_W:]
    return jnp.concatenate([w[:, :o], kr, pad, rot, pad, mq, gate], axis=1).astype(BF16)


def _prep_uq(w_uq):
    r = w_uq.shape[0]
    w = w_uq.reshape(r, C_HEADS, C_NOPE + C_ROPE)
    nope, rope = w[..., :C_NOPE], w[..., C_NOPE:]
    half = C_ROPE // 2
    rot = jnp.concatenate([-rope[..., half:], rope[..., :half]], axis=-1)
    pad = jnp.zeros((r, C_HEADS, LANES - C_NOPE - C_ROPE), F32)
    plain = jnp.concatenate([nope, rope, pad], axis=-1).reshape(r, C_HEADS * LANES)
    rotated = jnp.concatenate([jnp.zeros_like(nope), rot, pad], axis=-1).reshape(r, C_HEADS * LANES)
    return plain.astype(BF16), rotated.astype(BF16)


def _prep_ukv(w_ukv):
    r = w_ukv.shape[0]
    w = w_ukv.reshape(r, C_HEADS, C_NOPE + C_V)
    wk = jnp.concatenate([w[..., :C_NOPE], jnp.zeros((r, C_HEADS, LANES - C_NOPE), F32)], axis=-1)
    wv = w[..., C_NOPE:]
    eye = jnp.eye(C_ROPE, dtype=F32)
    place = jnp.concatenate([jnp.zeros((C_ROPE, C_NOPE), F32), eye,
                             jnp.zeros((C_ROPE, LANES - C_NOPE - C_ROPE), F32)], axis=1)
    place = jnp.tile(place, (1, C_HEADS))
    place = jnp.concatenate([place, jnp.zeros((LANES - C_ROPE, place.shape[1]), F32)], axis=0)
    return (wk.reshape(r, C_HEADS * LANES).astype(BF16), wv.reshape(r, C_HEADS * C_V).astype(BF16),
            place.astype(BF16))


def _rope_tables(pos):
    half = C_ROPE // 2
    inv = jnp.exp(-math.log(ROPE_BASE) * jnp.arange(half, dtype=F32) * 2.0 / C_ROPE)
    ang = pos.astype(F32)[:, None] * inv[None, :]
    cos, sin = jnp.cos(ang), jnp.sin(ang)
    t = pos.shape[0]
    cos2 = jnp.concatenate([cos, cos], axis=1)
    sin2 = jnp.concatenate([sin, sin], axis=1)
    scale = LOG2E * (C_NOPE + C_ROPE) ** -0.5
    z = lambda n: jnp.zeros((t, n), F32)
    cq = jnp.concatenate([jnp.ones((t, C_NOPE), F32), cos2, z(LANES - C_NOPE - C_ROPE)], axis=1) * scale
    sq = jnp.concatenate([z(C_NOPE), sin2, z(LANES - C_NOPE - C_ROPE)], axis=1) * scale
    ck = jnp.concatenate([cos2, z(LANES - C_ROPE)], axis=1)
    sk = jnp.concatenate([sin2, z(LANES - C_ROPE)], axis=1)
    return cq, sq, ck, sk


def _band_bias_kernel(pos_ref, row_ref, o_ref, *, tq, win, n_valid, transposed):
    c = pl.program_id(0)
    q0, k0 = pos_ref[0, c], pos_ref[1, c]
    wp = row_ref.shape[-1]
    n_rows, n_cols = (win, tq) if transposed else (tq, win)
    rows = pltpu.roll(jnp.broadcast_to(row_ref[...], (n_rows, wp)), 0, 1, stride=1, stride_axis=0)
    ri = lax.broadcasted_iota(jnp.int32, (n_rows, n_cols), 0)
    ci = lax.broadcasted_iota(jnp.int32, (n_rows, n_cols), 1)
    qi, kj = (ci, ri) if transposed else (ri, ci)
    qp, kp = q0 + qi, k0 + kj
    qc, kc = qp // CHUNK, kp // CHUNK
    valid = (kp >= 0) & (kc <= qc) & (kc >= qc - A_BAND_CHUNKS) & (kj < n_valid)
    o_ref[...] = jnp.where(valid, rows[:, :n_cols], NEG)


def band_bias(rel_bias, q0, k0, *, tq, win, n_valid, transposed=False):
    n_cls = q0.shape[0]
    heads = rel_bias.shape[1]
    wp = -(-(tq + win) // LANES) * LANES
    mm = jnp.arange(wp, dtype=jnp.int32)
    mm = jnp.where(mm < (tq if transposed else win), mm, mm - wp)
    mm = mm if transposed else -mm
    rel = jnp.clip((q0 - k0)[:, None] + mm[None, :], -A_REL_CLIP, A_REL_CLIP) + A_REL_CLIP
    rows = jnp.moveaxis((LOG2E * rel_bias.astype(F32))[rel], -1, 1).reshape(n_cls, heads, 1, wp)
    pos = jnp.stack([q0, k0]).astype(jnp.int32)
    if transposed:
        out_spec = pl.BlockSpec((None, None, win, tq), lambda c, h, *_: (c, h // 2, 0, h % 2))
        out_shape = jax.ShapeDtypeStruct((n_cls, heads // 2, win, 2 * tq), F32)
    else:
        out_spec = pl.BlockSpec((None, None, tq, win), lambda c, h, *_: (c, h, 0, 0))
        out_shape = jax.ShapeDtypeStruct((n_cls, heads, tq, win), F32)
    grid_spec = pltpu.PrefetchScalarGridSpec(
        num_scalar_prefetch=1,
        grid=(n_cls, heads),
        in_specs=[pl.BlockSpec((None, None, 1, wp), lambda c, h, *_: (c, h, 0, 0))],
        out_specs=out_spec)
    return pl.pallas_call(
        functools.partial(_band_bias_kernel, tq=tq, win=win, n_valid=n_valid, transposed=transposed),
        grid_spec=grid_spec,
        out_shape=out_shape,
        compiler_params=_params(2),
    )(pos, rows)


A_TQ = 256
A_SUB = 4
FLASH_T = 512
ONES_ROWS = 16


def _trunk(x, pos0, mem_k, mem_vt, past, wts):
    bt, t, d = x.shape
    m = bt * t
    depth = len(wts["w_out"])
    x2 = x.reshape(m, d)
    new_a, new_b, new_c = [], [], []
    pos = pos0 + jnp.arange(t, dtype=jnp.int32)
    one = lambda width, dt: (width, (dt,))
    for i in range(depth):
        kind, j = i % N_MIXERS, i // N_MIXERS
        if kind == 0:
            hd = A_HEADS * HEAD_DIM
            w_in = wts["w_in_a"][j]
            if past is None:
                tq = min(A_TQ, t)
                q, k, k16, v, mq, gate, vt = norm_matmul(
                    x2, wts["norm_g"][i], w_in,
                    (one(hd, BF16), (hd, (F32, BF16)), one(hd, F32), one(MEM_W, BF16), one(d, BF16)),
                    t_seg=2, t_blk=tq)
                k3, v3 = k.reshape(bt, t, hd), v.reshape(bt, t, hd)
                back_blocks = A_PAST_ROWS // tq
                n_win = min(back_blocks + 1, t // tq)
                win = n_win * tq
                n_cls = n_win
                q0 = jnp.arange(n_cls, dtype=jnp.int32) * tq
                k0 = jnp.clip(q0 - A_PAST_ROWS, 0, t - win)
                bias = band_bias(wts["rel_bias_a"][j], q0, k0, tq=tq, win=win, n_valid=win, transposed=True)
                br = attn_window_t(q.reshape(bt, t, hd), k16.reshape(bt, t, hd), vt, bias,
                                   tq=tq, n_sub=min(A_SUB, t // tq), n_win=n_win, back_blocks=back_blocks)
                keep = min(A_PAST_ROWS, t)
                new_a.append((k3[:, t - keep:].reshape(bt, keep, A_HEADS, HEAD_DIM),
                              v3[:, t - keep:].reshape(bt, keep, A_HEADS, HEAD_DIM)))
            else:
                q, k, v, mq, gate = norm_matmul(
                    x2, wts["norm_g"][i], w_in,
                    (one(hd, BF16), one(hd, F32), one(hd, F32), one(MEM_W, BF16), one(d, BF16)))
                q3, k3, v3 = (a.reshape(bt, t, hd) for a in (q, k, v))
                ck, cv = past[0][j], past[1][j]
                rows = ck.shape[1]
                n_keys = rows + t
                pad = (-n_keys) % LANES
                zeros = jnp.zeros((bt, pad, hd), F32)
                kk = jnp.concatenate([ck.reshape(bt, rows, hd), k3, zeros], axis=1)
                vv = jnp.concatenate([cv.reshape(bt, rows, hd), v3, zeros], axis=1)
                start = jnp.full((1,), pos0, jnp.int32)
                bias = band_bias(wts["rel_bias_a"][j], start, start - rows, tq=t, win=n_keys + pad,
                                 n_valid=n_keys)
                br = attn_window(q3, kk, vv, bias, tq=t, win=n_keys + pad, back=0)
                new_a.append((k3.reshape(bt, t, A_HEADS, HEAD_DIM), v3.reshape(bt, t, A_HEADS, HEAD_DIM)))
        elif kind == 1:
            hd = B_HEADS * 2 * HEAD_DIM
            lam_init = 0.8 - 0.6 * math.exp(-0.3 * i)
            common = dict(mode="diff", lam_p=wts["lambda_b"][j], sub_g=wts["subln_g_b"][j], lam_init=lam_init)
            w_in = wts["w_in_b"][j]
            if past is None:
                q, k, k16, v, mq, gate, vt = norm_matmul(
                    x2, wts["norm_g"][i], w_in,
                    (one(hd, BF16), (hd, (F32, BF16)), one(hd, F32), one(MEM_W, BF16), one(d, BF16)),
                    wt=w_in[:, 2 * hd:3 * hd].T, t_blk=FLASH_T, tm=FLASH_T,
                    split={1: (B_HEADS, 2, HEAD_DIM), 3: (B_HEADS, 2 * HEAD_DIM)})
                br = flash_prompt(q.reshape(bt, t, hd), k16.reshape(bt, t, hd), vt, t=FLASH_T, **common)
            else:
                q, k, v, mq, gate = norm_matmul(
                    x2, wts["norm_g"][i], w_in,
                    (one(hd, BF16), one(hd, F32), one(hd, F32), one(MEM_W, BF16), one(d, BF16)))
                ck, cv = past[2][j], past[3][j]
                rows = ck.shape[1]
                br = flash_attention(q.reshape(bt, t, hd), ck.reshape(bt, rows, hd), cv.reshape(bt, rows, hd),
                                     k.reshape(bt, t, hd), v.reshape(bt, t, hd),
                                     tq=t, tk=FLASH_T, diagonal_tail=False, q_pos0=rows, **common)
            new_b.append((k.reshape(bt, t, B_HEADS, 2, HEAD_DIM), v.reshape(bt, t, B_HEADS, 2 * HEAD_DIM)))
        else:
            cq, ckv, kra, krb, mq, gate = norm_matmul(
                x2, wts["norm_g"][i], wts["w_in_c"][j],
                (one(C_Q_LORA, F32), one(C_KV_LORA, F32), one(LANES, F32), one(LANES, F32),
                 one(MEM_W, BF16), one(d, BF16)))
            wq, wqr = wts["w_uq_c"][j]
            wk, wv, place = wts["w_ukv_c"][j]
            qcat, lat, kr = c_pre(cq, ckv, kra, krb, wts["q_norm_g_c"][j], wts["kv_norm_g_c"][j],
                                  wq, wqr, _rope_tables(pos), t)
            q3 = qcat.reshape(bt, t, -1)
            if past is None:
                kcat, vt = c_kv(lat, kr, wk, wv.T, place, v_transposed=True, tm=FLASH_T)
                br = flash_prompt(q3, kcat.reshape(bt, t, -1), vt, mode="mla", t=FLASH_T)
            else:
                kcat, vcat = c_kv(lat, kr, wk, wv, place, v_transposed=False)
                cl, cr = past[4][j], past[5][j]
                rows = cl.shape[1]
                cr = jnp.pad(cr.reshape(bt * rows, -1), ((0, 0), (0, LANES - C_ROPE)))
                kc_, vc_ = c_kv(cl.reshape(bt * rows, -1), cr, wk, wv, place, v_transposed=False)
                br = flash_attention(q3, kc_.reshape(bt, rows, -1), vc_.reshape(bt, rows, -1),
                                     kcat.reshape(bt, t, -1), vcat.reshape(bt, t, -1),
                                     mode="mla", tq=t, tk=FLASH_T, diagonal_tail=False, q_pos0=rows)
            new_c.append((lat.reshape(bt, t, -1), kr[:, :C_ROPE].reshape(bt, t, -1)))
        last = i == depth - 1
        x2 = gated_out(br.reshape(m, -1), mq, gate, x2, wts["w_out"][i], wts["final_g"],
                       mem_k[i], mem_vt[i], final=last)
    return x2.reshape(bt, t, d), new_a, new_b, new_c


def kernel(x_prompt, x_sample, cache_a_k, cache_a_v, cache_b_k, cache_b_v, cache_c_lat, cache_c_rope,
           cache_mem_k, cache_mem_v, mem_prompt, norm_g, final_g, mem_norm_g, w_mem_kv, w_out, w_in_a,
           rel_bias_a, w_in_b, lambda_b, subln_g_b, w_in_c, q_norm_g_c, kv_norm_g_c, w_uq_c, w_ukv_c):
    depth = w_out.shape[0]
    bp, n_mem, d = mem_prompt.shape
    mem_heads = MEM_W // HEAD_DIM
    wts = dict(
        norm_g=norm_g, final_g=final_g, rel_bias_a=rel_bias_a, lambda_b=lambda_b, subln_g_b=subln_g_b,
        q_norm_g_c=q_norm_g_c, kv_norm_g_c=kv_norm_g_c,
        w_out=[w_out[i].astype(BF16) for i in range(depth)],
        w_in_a=[_prep_in_ab(w_in_a[j]) for j in range(w_in_a.shape[0])],
        w_in_b=[_prep_in_ab(w_in_b[j]) for j in range(w_in_b.shape[0])],
        w_in_c=[_prep_in_c(w_in_c[j]) for j in range(w_in_c.shape[0])],
        w_uq_c=[_prep_uq(w_uq_c[j]) for j in range(w_uq_c.shape[0])],
        w_ukv_c=[_prep_ukv(w_ukv_c[j]) for j in range(w_ukv_c.shape[0])],
    )
    mem2 = mem_prompt.reshape(bp * n_mem, d)
    mem_k_p, mem_v_p, mem_vt_p = [], [], []
    for i in range(depth):
        w_kv = w_mem_kv[i].astype(BF16)
        mk, mv, mvt = norm_matmul(mem2, mem_norm_g[i], w_kv, ((MEM_W, (F32,)), (MEM_W, (F32,))),
                                  wt=w_kv[:, MEM_W:].T, t_blk=n_mem, tm=n_mem)
        mem_k_p.append(mk.reshape(bp, n_mem, MEM_W))
        mem_v_p.append(mv.reshape(bp, n_mem, MEM_W))
        mem_vt_p.append(mvt)

    y_p, na_p, nb_p, nc_p = _trunk(x_prompt, 0, mem_k_p, mem_vt_p, None, wts)

    bs = x_sample.shape[0]
    mem_k_s = [cache_mem_k[i].reshape(bs, n_mem, MEM_W) for i in range(depth)]
    mem_vt_s = [jnp.swapaxes(cache_mem_v[i].reshape(bs, n_mem, MEM_W), 1, 2).astype(BF16) for i in range(depth)]
    past = (cache_a_k, cache_a_v, cache_b_k, cache_b_v, cache_c_lat, cache_c_rope)
    y_s, na_s, nb_s, nc_s = _trunk(x_sample, cache_b_k.shape[2], mem_k_s, mem_vt_s, past, wts)

    stk = lambda lst, n: jnp.stack([s[n] for s in lst])
    heads4 = lambda lst: jnp.stack([a.reshape(bp, n_mem, mem_heads, HEAD_DIM) for a in lst])
    return (y_p, y_s,
            stk(na_p, 0), stk(na_p, 1), stk(na_s, 0), stk(na_s, 1),
            stk(nb_p, 0), stk(nb_p, 1), stk(nb_s, 0), stk(nb_s, 1),
            stk(nc_p, 0), stk(nc_p, 1), stk(nc_s, 0), stk(nc_s, 1),
            heads4(mem_k_p), heads4(mem_v_p))
```

```python
import functools
import itertools
import math

import jax
import jax.numpy as jnp
from jax import lax
from jax.experimental import pallas as pl
from jax.experimental.pallas import tpu as pltpu

F32 = jnp.float32
BF16 = jnp.bfloat16

LANES = 128
VMEM_LIMIT = 56 * 1024 * 1024

EPS = 1e-6
NEG = -1e30
LOG2E = math.log2(math.e)
CHUNK = 64
HEAD_DIM = 64
Q_SCALE = LOG2E * HEAD_DIM ** -0.5
MEM_W = 256
A_HEADS = 12
A_PAST_ROWS = 512
A_BAND_CHUNKS = 8
A_REL_CLIP = 128
B_HEADS = 6
C_HEADS = 12
C_NOPE = 64
C_ROPE = 32
C_V = 64
C_Q_LORA = 384
C_KV_LORA = 256
ROPE_BASE = 10000.0
N_MIXERS = 3


def _params(n_grid):
    return pltpu.CompilerParams(dimension_semantics=("arbitrary",) * n_grid,
                                vmem_limit_bytes=VMEM_LIMIT)


def _row_tile(m, want):
    t = min(m, want)
    assert m % t == 0, (m, t)
    return t


def _rms(x, g):
    return x * lax.rsqrt(jnp.mean(x * x, axis=-1, keepdims=True) + EPS) * g


NT = (((1,), (1,)), ((), ()))


def _store_blocks(o_ref, rt):
    blk = o_ref.shape[-1]
    for jb in range(o_ref.shape[0]):
        o_ref[jb] = rt[:, jb * blk:(jb + 1) * blk]


def _norm_matmul_kernel(x_ref, g_ref, w_ref, *rest, segs, has_t, t_seg):
    h = _rms(x_ref[...], g_ref[...]).astype(BF16)
    out_refs = rest[1:] if has_t else rest
    off = n_out = 0
    for si, (width, dtypes) in enumerate(segs):
        r = jnp.dot(h, w_ref[:, off:off + width], preferred_element_type=F32)
        if si == t_seg:
            _store_blocks(out_refs[-1], r.T.astype(BF16))
        for dt in dtypes:
            o_ref = out_refs[n_out]
            if len(o_ref.shape) == 2:
                o_ref[...] = r.astype(dt)
            else:
                w = o_ref.shape[-1]
                for c, ix in enumerate(itertools.product(*[range(n) for n in o_ref.shape[1:-1]])):
                    o_ref[(slice(None),) + ix + (slice(None),)] = r[:, c * w:(c + 1) * w].astype(dt)
            n_out += 1
        off += width
    if has_t:
        _store_blocks(out_refs[n_out], lax.dot_general(rest[0][...], h, NT, preferred_element_type=F32).astype(BF16))


def norm_matmul(x, g, w, segs, wt=None, t_seg=None, t_blk=None, tm=512, split=None):
    m, d = x.shape
    tm = _row_tile(m, tm)
    n = w.shape[1]
    assert n == sum(wd for wd, _ in segs)
    row = lambda i: (i, 0)
    fixed = lambda i: (0, 0)
    in_specs = [pl.BlockSpec((tm, d), row), pl.BlockSpec((1, d), fixed), pl.BlockSpec((d, n), fixed)]
    args = [x, g.reshape(1, d).astype(F32), w]
    out_specs = [pl.BlockSpec((tm, wd), row) for wd, dts in segs for _ in dts]
    out_shape = [jax.ShapeDtypeStruct((m, wd), dt) for wd, dts in segs for dt in dts]
    for pos, tail in (split or {}).items():
        assert math.prod(tail) == out_shape[pos].shape[1]
        out_specs[pos] = pl.BlockSpec((tm,) + tail, lambda i, n=len(tail): (i,) + (0,) * n)
        out_shape[pos] = jax.ShapeDtypeStruct((m,) + tail, out_shape[pos].dtype)
    assert wt is None or t_seg is None
    if wt is not None:
        in_specs.append(pl.BlockSpec(wt.shape, fixed))
        args.append(wt)
    if wt is not None or t_seg is not None:
        n_t = wt.shape[0] if wt is not None else segs[t_seg][0]
        assert tm % t_blk == 0
        out_specs.append(pl.BlockSpec((tm // t_blk, n_t, t_blk), lambda i: (i, 0, 0)))
        out_shape.append(jax.ShapeDtypeStruct((m // t_blk, n_t, t_blk), BF16))
    return pl.pallas_call(
        functools.partial(_norm_matmul_kernel, segs=tuple(segs), has_t=wt is not None, t_seg=t_seg),
        grid=(m // tm,),
        in_specs=in_specs,
        out_specs=out_specs,
        out_shape=out_shape,
        compiler_params=_params(1),
    )(*args)


def _gated_out_kernel(br_ref, mq_ref, gate_ref, x_ref, w_ref, fg_ref, mk_ref, mvt_ref, o_ref, *, final):
    nb = br_ref.shape[-1]
    tm = x_ref.shape[0]
    halves = _half_masks()
    mq = mq_ref[...]
    mem = []
    for pair in range(mq.shape[1] // LANES):
        cols = slice(pair * LANES, (pair + 1) * LANES)
        q = mq[:, cols]
        q2 = jnp.concatenate([jnp.where(halves[h], q, jnp.zeros_like(q)) for h in range(2)], axis=0)
        st = lax.dot_general(mk_ref[:, cols].astype(BF16), q2, NT, preferred_element_type=F32)
        p = jnp.exp2(st - jnp.max(st, axis=0, keepdims=True))
        l = jnp.sum(p, axis=0, keepdims=True)
        o = jnp.dot(mvt_ref[cols, :], p.astype(BF16), preferred_element_type=F32) / l
        mem.append(jnp.concatenate([o[:HEAD_DIM, :tm], o[HEAD_DIM:, tm:]], axis=0).T)
    mo = jnp.concatenate(mem, axis=1)
    gate = gate_ref[...].astype(F32)
    sg = gate * jax.nn.sigmoid(gate)
    y1 = (br_ref[...].astype(F32) * sg[:, :nb]).astype(BF16)
    y2 = (mo * sg[:, nb:]).astype(BF16)
    acc = jnp.dot(y1, w_ref[:nb, :], preferred_element_type=F32)
    acc += jnp.dot(y2, w_ref[nb:, :], preferred_element_type=F32)
    xn = x_ref[...] + acc
    o_ref[...] = _rms(xn, fg_ref[...]) if final else xn


def gated_out(br, mq, gate, x, w, final_g, mem_k, mem_vt, final, tm=512):
    m, d = x.shape
    t_len = m // mem_k.shape[0]
    tm = _row_tile(t_len, tm)
    n_t = t_len // tm
    nb, nm, ng = br.shape[1], mq.shape[1], gate.shape[1]
    row = lambda i: (i, 0)
    fixed = lambda i: (0, 0)
    per_batch = lambda i: (i // n_t, 0, 0)
    return pl.pallas_call(
        functools.partial(_gated_out_kernel, final=final),
        grid=(m // tm,),
        in_specs=[pl.BlockSpec((tm, nb), row), pl.BlockSpec((tm, nm), row), pl.BlockSpec((tm, ng), row),
                  pl.BlockSpec((tm, d), row), pl.BlockSpec((ng, d), fixed), pl.BlockSpec((1, d), fixed),
                  pl.BlockSpec((None,) + mem_k.shape[1:], per_batch),
                  pl.BlockSpec((None,) + mem_vt.shape[1:], per_batch)],
        out_specs=pl.BlockSpec((tm, d), row),
        out_shape=jax.ShapeDtypeStruct((m, d), F32),
        compiler_params=_params(1),
    )(br, mq, gate, x, w, final_g.reshape(1, d).astype(F32), mem_k, mem_vt)


def _half_masks():
    lane = lax.broadcasted_iota(jnp.int32, (1, LANES), 1)
    lo = lane < HEAD_DIM
    return lo, jnp.logical_not(lo)


def _attn_window_kernel(q_ref, k_ref, v_ref, *rest, tq, win, back):
    if len(rest) == 2:
        bias_ref, o_ref = rest
    else:
        bias_ref, (o_ref,) = None, rest
    tk_all = k_ref.shape[0]
    if win == tk_all:
        k = k_ref[...]
        v = v_ref[...]
    else:
        start = jnp.clip(pl.program_id(2) * tq - back, 0, tk_all - win)
        start = pl.multiple_of(start, CHUNK)
        k = k_ref[pl.ds(start, win), :]
        v = v_ref[pl.ds(start, win), :]
    k = k.astype(BF16)
    v = v.astype(BF16)
    q = q_ref[...]
    halves = _half_masks()
    outs = []
    for i in range(2):
        qm = jnp.where(halves[i], q, jnp.zeros_like(q))
        s = lax.dot_general(qm, k, (((1,), (1,)), ((), ())), preferred_element_type=F32)
        if bias_ref is not None:
            s = s + bias_ref[i]
        m = jnp.max(s, axis=-1, keepdims=True)
        p = jnp.exp2(s - m)
        l = jnp.sum(p, axis=-1, keepdims=True)
        o = jnp.dot(p.astype(BF16), v, preferred_element_type=F32)
        outs.append(o / l)
    o_ref[...] = jnp.where(halves[0], outs[0], outs[1]).astype(o_ref.dtype)


def attn_window(q, k, v, bias, *, tq, win, back):
    bt, t_q, gw = q.shape
    g = gw // LANES
    t_k = k.shape[1]
    tq = _row_tile(t_q, tq)
    in_specs = [pl.BlockSpec((None, tq, LANES), lambda b, h, i: (b, i, h)),
                pl.BlockSpec((None, t_k, LANES), lambda b, h, i: (b, 0, h)),
                pl.BlockSpec((None, t_k, LANES), lambda b, h, i: (b, 0, h))]
    args = [q, k, v]
    if bias is not None:
        last = bias.shape[0] - 1
        in_specs.append(pl.BlockSpec((None, 2, tq, win), lambda b, h, i: (jnp.minimum(i, last), h, 0, 0)))
        args.append(bias)
    return pl.pallas_call(
        functools.partial(_attn_window_kernel, tq=tq, win=win, back=back),
        grid=(bt, g, t_q // tq),
        in_specs=in_specs,
        out_specs=pl.BlockSpec((None, tq, LANES), lambda b, h, i: (b, i, h)),
        out_shape=jax.ShapeDtypeStruct((bt, t_q, gw), BF16),
        compiler_params=_params(3),
    )(*args)


def _attn_window_t_kernel(q_ref, k_ref, vt_ref, *rest, tq, n_sub, n_win, back_blocks):
    if len(rest) == 2:
        bias_ref, o_ref = rest
    else:
        bias_ref, (o_ref,) = None, rest
    n_blk, _, blk = vt_ref.shape
    halves = _half_masks()

    def score(sub):
        i = pl.program_id(2) * n_sub + sub
        if n_win == n_blk:
            b0 = 0
            k = k_ref[...]
        else:
            b0 = jnp.clip(i - back_blocks, 0, n_blk - n_win)
            k = k_ref[pl.ds(pl.multiple_of(b0 * blk, blk), n_win * blk), :]
        q = q_ref[sub * tq:(sub + 1) * tq, :]
        q2 = jnp.concatenate([jnp.where(halves[h], q, jnp.zeros_like(q)) for h in range(2)], axis=0)
        st = lax.dot_general(k.astype(BF16), q2, NT, preferred_element_type=F32)
        if bias_ref is not None:
            st = st + bias_ref[jnp.minimum(i, bias_ref.shape[0] - 1)]
        return b0, st, jnp.max(st, axis=0, keepdims=True)

    def accumulate(sub, b0, st, m):
        p = jnp.exp2(st - m)
        l = jnp.sum(p, axis=0, keepdims=True)
        pb = p.astype(BF16)
        o = None
        for j in range(n_win):
            oj = jnp.dot(vt_ref[b0 + j], pb[j * blk:(j + 1) * blk, :], preferred_element_type=F32)
            o = oj if o is None else o + oj
        o = o / l
        o = jnp.concatenate([o[:HEAD_DIM, :tq], o[HEAD_DIM:, tq:]], axis=0).T
        o_ref[sub * tq:(sub + 1) * tq, :] = o.astype(o_ref.dtype)

    staged = [score(sub) for sub in range(n_sub)]
    for sub in range(n_sub):
        accumulate(sub, *staged[sub])


def attn_window_t(q, k, vt, bias, *, tq, n_sub, n_win, back_blocks):
    bt, t_q, gw = q.shape
    g = gw // LANES
    t_k = k.shape[1]
    blk = vt.shape[2]
    n_blk = t_k // blk
    assert vt.shape == (bt * n_blk, gw, blk), vt.shape
    rows = n_sub * tq
    assert t_q % rows == 0
    in_specs = [pl.BlockSpec((None, rows, LANES), lambda b, h, i: (b, i, h)),
                pl.BlockSpec((None, t_k, LANES), lambda b, h, i: (b, 0, h)),
                pl.BlockSpec((n_blk, LANES, blk), lambda b, h, i: (b, h, 0))]
    args = [q, k, vt]
    if bias is not None:
        in_specs.append(pl.BlockSpec((bias.shape[0], None, n_win * blk, 2 * tq), lambda b, h, i: (0, h, 0, 0)))
        args.append(bias)
    return pl.pallas_call(
        functools.partial(_attn_window_t_kernel, tq=tq, n_sub=n_sub, n_win=n_win, back_blocks=back_blocks),
        grid=(bt, g, t_q // rows),
        in_specs=in_specs,
        out_specs=pl.BlockSpec((None, rows, LANES), lambda b, h, i: (b, i, h)),
        out_shape=jax.ShapeDtypeStruct((bt, t_q, gw), BF16),
        compiler_params=_params(3),
    )(*args)


def _flash_kernel(slope_ref, q_ref, km_ref, vm_ref, kt_ref, vt_ref, *rest,
                  mode, tq, tk, tt, n_main, q_pos0, lam_init):
    if mode == "diff":
        lam_ref, subg_ref, o_ref, m_scr, l_scr, acc_scr = rest
    else:
        o_ref, m_scr, l_scr, acc_scr = rest
    head = pl.program_id(1)
    qi = pl.program_id(2)
    q = q_ref[...]
    halves = _half_masks()
    if mode == "diff":
        qs = [jnp.where(halves[i], q, jnp.zeros_like(q)) for i in range(2)]
        slope = slope_ref[head]
    else:
        qs = [q[:, :LANES], q[:, LANES:]]
        slope = None

    m_scr[...] = jnp.full(m_scr.shape, NEG, F32)
    l_scr[...] = jnp.zeros(l_scr.shape, F32)
    acc_scr[...] = jnp.zeros(acc_scr.shape, F32)

    def update(i, s, shift, v):
        m_old = m_scr[i]
        m_new = jnp.maximum(m_old, jnp.max(s, axis=-1, keepdims=True) + shift)
        alpha = jnp.exp2(m_old - m_new)
        p = jnp.exp2(s - (m_new - shift))
        l_scr[i] = alpha * l_scr[i] + jnp.sum(p, axis=-1, keepdims=True)
        acc_scr[i] = alpha * acc_scr[i] + jnp.dot(p.astype(BF16), v, preferred_element_type=F32)
        m_scr[i] = m_new

    def k_of(kblk, i):
        return kblk if mode == "diff" else kblk[:, i * LANES:(i + 1) * LANES]

    nt = (((1,), (1,)), ((), ()))
    q_base = q_pos0 + qi * tq

    if mode == "diff":
        ri = lax.broadcasted_iota(jnp.int32, (tq, tk), 0)
        ci = lax.broadcasted_iota(jnp.int32, (tq, tk), 1)
        main_tile = slope * (ci - ri).astype(F32)

    def main_body(kb, carry):
        off = pl.multiple_of(kb * tk, tk)
        kblk = km_ref[pl.ds(off, tk), :].astype(BF16)
        vblk = vm_ref[pl.ds(off, tk), :].astype(BF16)
        for i in range(2):
            s = lax.dot_general(qs[i], k_of(kblk, i), nt, preferred_element_type=F32)
            if mode == "diff":
                shift = -slope * (q_base - kb * tk).astype(F32)
                update(i, s + main_tile, shift, vblk)
            else:
                update(i, s, 0.0, vblk)
        return carry

    lax.fori_loop(0, n_main, main_body, 0)

    ri = lax.broadcasted_iota(jnp.int32, (tq, tt), 0)
    ci = lax.broadcasted_iota(jnp.int32, (tq, tt), 1)
    valid = (ci // CHUNK) <= (ri // CHUNK)
    if mode == "diff":
        tail_tile = jnp.where(valid, -slope * jnp.abs(ri - ci).astype(F32), NEG)
    else:
        tail_tile = jnp.where(valid, 0.0, NEG).astype(F32)
    kblk = kt_ref[...].astype(BF16)
    vblk = vt_ref[...].astype(BF16)
    for i in range(2):
        s = lax.dot_general(qs[i], k_of(kblk, i), nt, preferred_element_type=F32)
        update(i, s + tail_tile, 0.0, vblk)

    o0 = acc_scr[0] / l_scr[0]
    o1 = acc_scr[1] / l_scr[1]
    if mode == "diff":
        lp = lam_ref[...]
        lam = (jnp.exp(jnp.sum(lp[0:1] * lp[1:2], axis=-1, keepdims=True))
               - jnp.exp(jnp.sum(lp[2:3] * lp[3:4], axis=-1, keepdims=True)) + lam_init)
        o = o0 - lam * o1
        o_ref[...] = (_rms(o, subg_ref[...]) * (1.0 - lam_init)).astype(o_ref.dtype)
    else:
        o_ref[...] = jnp.where(halves[0], o0, o1).astype(o_ref.dtype)


def flash_attention(q, k_main, v_main, k_tail, v_tail, *, mode, tq, tk, q_pos0,
                    lam_p=None, sub_g=None, lam_init=0.0):
    bt, t_q = q.shape[0], q.shape[1]
    tq = _row_tile(t_q, tq)
    qw = LANES if mode == "diff" else 2 * LANES
    groups = q.shape[2] // qw
    t_main = k_main.shape[1]
    tk = _row_tile(t_main, tk)
    tt, n_main = k_tail.shape[1], t_main // tk
    tail_idx = lambda b, h, i, *_: (b, 0, h)
    q_idx = lambda b, h, i, *_: (b, i, h)
    main_idx = lambda b, h, i, *_: (b, 0, h)
    in_specs = [pl.BlockSpec((None, tq, qw), q_idx),
                pl.BlockSpec((None, t_main, qw), main_idx),
                pl.BlockSpec((None, t_main, LANES), main_idx),
                pl.BlockSpec((None, tt, qw), tail_idx),
                pl.BlockSpec((None, tt, LANES), tail_idx)]
    args = [q, k_main, v_main, k_tail, v_tail]
    if mode == "diff":
        slopes = LOG2E * jnp.exp2(-8.0 * jnp.arange(1, groups + 1, dtype=F32) / groups)
        in_specs += [pl.BlockSpec(lam_p.shape, lambda b, h, i, *_: (0, 0)),
                     pl.BlockSpec((1, LANES), lambda b, h, i, *_: (0, 0))]
        args += [lam_p.astype(F32), sub_g.reshape(1, LANES).astype(F32)]
    else:
        slopes = jnp.zeros((groups,), F32)
    grid_spec = pltpu.PrefetchScalarGridSpec(
        num_scalar_prefetch=1,
        grid=(bt, groups, t_q // tq),
        in_specs=in_specs,
        out_specs=pl.BlockSpec((None, tq, LANES), q_idx),
        scratch_shapes=[pltpu.VMEM((2, tq, 1), F32), pltpu.VMEM((2, tq, 1), F32),
                        pltpu.VMEM((2, tq, LANES), F32)])
    return pl.pallas_call(
        functools.partial(_flash_kernel, mode=mode, tq=tq, tk=tk, tt=tt, n_main=n_main,
                          q_pos0=q_pos0, lam_init=lam_init),
        grid_spec=grid_spec,
        out_shape=jax.ShapeDtypeStruct((bt, t_q, groups * LANES), BF16),
        compiler_params=_params(3),
    )(slopes, *args)


def _flash_prompt_kernel(slope_ref, q_ref, k_ref, vt_ref, *rest, mode, t, lam_init):
    if mode == "diff":
        lam_ref, subg_ref, o_ref = rest[:3]
        rest = rest[3:]
    else:
        o_ref = rest[0]
        rest = rest[1:]
    m_scr, acc_scr, main_tile_scr, diag_tile_scr = rest[:4]
    st_scrs, msub_scrs, alpha_scrs = rest[4:6], rest[6:8], rest[8:10]
    head = pl.program_id(1)
    qi = pl.program_id(2)
    q = q_ref[...]
    halves = _half_masks()
    if mode == "diff":
        qs = [jnp.where(halves[i], q, jnp.zeros_like(q)) for i in range(2)]
        slope = slope_ref[head]
    else:
        qs = [q[:, :LANES], q[:, LANES:]]
    v_rows = acc_scr.shape[1] - ONES_ROWS
    ones = jnp.ones((ONES_ROWS, t), BF16)

    m_scr[...] = jnp.full(m_scr.shape, NEG, F32)
    acc_scr[...] = jnp.zeros(acc_scr.shape, F32)

    @pl.when(qi == 0)
    def _():
        kj = lax.broadcasted_iota(jnp.int32, (t, t), 0)
        qj = lax.broadcasted_iota(jnp.int32, (t, t), 1)
        valid = (kj // CHUNK) <= (qj // CHUNK)
        if mode == "diff":
            main_tile_scr[...] = slope * (kj - qj).astype(F32)
            diag_tile_scr[...] = jnp.where(valid, -slope * jnp.abs(qj - kj).astype(F32), NEG)
        else:
            diag_tile_scr[...] = jnp.where(valid, 0.0, NEG).astype(F32)


    def score(kb, diag, slot):
        off = pl.multiple_of(kb * t, t)
        kblk = k_ref[pl.ds(off, t), :]
        if mode == "diff" and not diag:
            shift = -slope * ((qi - kb) * t).astype(F32)
        else:
            shift = 0.0
        for i in range(2):
            ki = kblk if mode == "diff" else kblk[:, i * LANES:(i + 1) * LANES]
            st = lax.dot_general(ki, qs[i], NT, preferred_element_type=F32)
            if diag:
                st = st + diag_tile_scr[...]
            elif mode == "diff":
                st = st + main_tile_scr[...]
            st_scrs[slot][i] = st
            m_old = m_scr[i]
            m_new = jnp.maximum(m_old, jnp.max(st, axis=0, keepdims=True) + shift)
            m_scr[i] = m_new
            msub_scrs[slot][i] = m_new - shift
            alpha_scrs[slot][i] = jnp.exp2(m_old - m_new)

    def accumulate(kb, slot):
        vt = vt_ref[kb]
        for i in range(2):
            alpha = alpha_scrs[slot][i]
            vi = vt if mode == "diff" else vt[i * v_rows:(i + 1) * v_rows, :]
            vi = jnp.concatenate([vi, ones], axis=0)
            p = jnp.exp2(st_scrs[slot][i] - msub_scrs[slot][i])
            acc_scr[i] = alpha * acc_scr[i] + jnp.dot(vi, p.astype(BF16), preferred_element_type=F32)

    lead = jnp.logical_and(qi >= 2, qi % 2 == 0).astype(jnp.int32)

    @pl.when(lead == 1)
    def _():
        score(0, False, 0)
        accumulate(0, 0)

    @pl.when(qi == 0)
    def _():
        score(0, True, 0)

    @pl.when(qi >= 1)
    def _():
        score(lead, False, 0)

    def pair(kb):
        score(kb + 1, False, 1)
        accumulate(kb, 0)
        score(kb + 2, False, 0)
        accumulate(kb + 1, 1)

    def quad_body(j, carry):
        pair(lead + 4 * j)
        pair(lead + 4 * j + 2)
        return carry

    def pair_body(j, carry):
        pair(lead + 2 * j)
        return carry

    n_pairs = jnp.maximum((qi - lead - 1) // 2, 0)
    lax.fori_loop(0, n_pairs // 2, quad_body, 0)
    lax.fori_loop(n_pairs - n_pairs % 2, n_pairs, pair_body, 0)

    @pl.when(qi >= 1)
    def _():
        score(qi, True, 1)
        accumulate(qi - 1, 0)
        accumulate(qi, 1)

    @pl.when(qi == 0)
    def _():
        accumulate(0, 0)

    o0 = acc_scr[0, :v_rows] / acc_scr[0, v_rows:v_rows + 1]
    o1 = acc_scr[1, :v_rows] / acc_scr[1, v_rows:v_rows + 1]
    if mode == "diff":
        lp = lam_ref[...]
        lam = (jnp.exp(jnp.sum(lp[0:1] * lp[1:2], axis=-1, keepdims=True))
               - jnp.exp(jnp.sum(lp[2:3] * lp[3:4], axis=-1, keepdims=True)) + lam_init)
        o = (o0 - lam * o1).T
        o_ref[...] = (_rms(o, subg_ref[...]) * (1.0 - lam_init)).astype(o_ref.dtype)
    else:
        o_ref[...] = jnp.concatenate([o0, o1], axis=0).T.astype(o_ref.dtype)


def flash_prompt(q, k, vt, *, mode, t, lam_p=None, sub_g=None, lam_init=0.0):
    bt, t_all = q.shape[0], q.shape[1]
    n_blk = t_all // t
    qw = LANES if mode == "diff" else 2 * LANES
    groups = q.shape[2] // qw
    assert vt.shape == (bt * n_blk, groups * LANES, t), vt.shape
    q_idx = lambda b, h, i, *_: (b, i, h)
    in_specs = [pl.BlockSpec((None, t, qw), q_idx),
                pl.BlockSpec((None, t_all, qw), lambda b, h, i, *_: (b, 0, h)),
                pl.BlockSpec((n_blk, LANES, t), lambda b, h, i, *_: (b, h, 0))]
    args = [q, k, vt]
    if mode == "diff":
        slopes = LOG2E * jnp.exp2(-8.0 * jnp.arange(1, groups + 1, dtype=F32) / groups)
        in_specs += [pl.BlockSpec(lam_p.shape, lambda b, h, i, *_: (0, 0)),
                     pl.BlockSpec((1, LANES), lambda b, h, i, *_: (0, 0))]
        args += [lam_p.astype(F32), sub_g.reshape(1, LANES).astype(F32)]
        v_rows = LANES
    else:
        slopes = jnp.zeros((groups,), F32)
        v_rows = LANES // 2
    stat = pltpu.VMEM((2, 1, t), F32)
    tile = pltpu.VMEM((t, t), F32)
    grid_spec = pltpu.PrefetchScalarGridSpec(
        num_scalar_prefetch=1,
        grid=(bt, groups, n_blk),
        in_specs=in_specs,
        out_specs=pl.BlockSpec((None, t, LANES), q_idx),
        scratch_shapes=[stat, pltpu.VMEM((2, v_rows + ONES_ROWS, t), F32), tile, tile,
                        pltpu.VMEM((2, t, t), F32), pltpu.VMEM((2, t, t), F32), stat, stat, stat, stat])
    return pl.pallas_call(
        functools.partial(_flash_prompt_kernel, mode=mode, t=t, lam_init=lam_init),
        grid_spec=grid_spec,
        out_shape=jax.ShapeDtypeStruct((bt, t_all, groups * LANES), BF16),
        compiler_params=_params(3),
    )(slopes, *args)


def _c_pre_kernel(cq_ref, ckv_ref, kra_ref, krb_ref, gq_ref, gkv_ref, wq_ref, wqr_ref,
                  cq_tab, sq_tab, ck_tab, sk_tab, q_out, lat_out, kr_out):
    qn = _rms(cq_ref[...], gq_ref[...]).astype(BF16)
    qc = jnp.dot(qn, wq_ref[...], preferred_element_type=F32)
    qr = jnp.dot(qn, wqr_ref[...], preferred_element_type=F32)
    cos, sin = cq_tab[...], sq_tab[...]
    for h in range(C_HEADS):
        sl = slice(h * LANES, (h + 1) * LANES)
        q_out[:, sl] = (qc[:, sl] * cos + qr[:, sl] * sin).astype(q_out.dtype)
    lat_out[...] = _rms(ckv_ref[...], gkv_ref[...])
    kr = kra_ref[...] * ck_tab[...] + krb_ref[...] * sk_tab[...]
    kr_out[...] = kr


def c_pre(cq, ckv, kra, krb, gq, gkv, wq, wqr, tabs, t_len, tm=512):
    m = cq.shape[0]
    tm = _row_tile(t_len, tm)
    n_t = t_len // tm
    row = lambda i: (i, 0)
    fixed = lambda i: (0, 0)
    trow = lambda i: (i % n_t, 0)
    qw = wq.shape[1]
    return pl.pallas_call(
        _c_pre_kernel,
        grid=(m // tm,),
        in_specs=[pl.BlockSpec((tm, C_Q_LORA), row), pl.BlockSpec((tm, C_KV_LORA), row),
                  pl.BlockSpec((tm, LANES), row), pl.BlockSpec((tm, LANES), row),
                  pl.BlockSpec((1, C_Q_LORA), fixed), pl.BlockSpec((1, C_KV_LORA), fixed),
                  pl.BlockSpec(wq.shape, fixed), pl.BlockSpec(wqr.shape, fixed),
                  pl.BlockSpec((tm, LANES), trow), pl.BlockSpec((tm, LANES), trow),
                  pl.BlockSpec((tm, LANES), trow), pl.BlockSpec((tm, LANES), trow)],
        out_specs=[pl.BlockSpec((tm, qw), row), pl.BlockSpec((tm, C_KV_LORA), row),
                   pl.BlockSpec((tm, LANES), row)],
        out_shape=[jax.ShapeDtypeStruct((m, qw), BF16), jax.ShapeDtypeStruct((m, C_KV_LORA), F32),
                   jax.ShapeDtypeStruct((m, LANES), F32)],
        compiler_params=_params(1),
    )(cq, ckv, kra, krb, gq.reshape(1, -1).astype(F32), gkv.reshape(1, -1).astype(F32), wq, wqr, *tabs)


def _c_kv_kernel(lat_ref, kr_ref, wk_ref, wv_ref, place_ref, k_out, v_out, *, v_transposed):
    lat = lat_ref[...].astype(BF16)
    k = jnp.dot(lat, wk_ref[...], preferred_element_type=F32)
    k += jnp.dot(kr_ref[...].astype(BF16), place_ref[...], preferred_element_type=F32)
    k_out[...] = k.astype(k_out.dtype)
    if v_transposed:
        v = lax.dot_general(wv_ref[...], lat, NT, preferred_element_type=F32)
    else:
        v = jnp.dot(lat, wv_ref[...], preferred_element_type=F32)
    v_out[...] = v.astype(v_out.dtype)


def c_kv(lat, kr, wk, wv, place, v_transposed, tm=512):
    m = lat.shape[0]
    tm = _row_tile(m, tm)
    row = lambda i: (i, 0)
    fixed = lambda i: (0, 0)
    if v_transposed:
        n_v = wv.shape[0]
        v_spec = pl.BlockSpec((None, n_v, tm), lambda i: (i, 0, 0))
        v_shape = jax.ShapeDtypeStruct((m // tm, n_v, tm), BF16)
    else:
        n_v = wv.shape[1]
        v_spec = pl.BlockSpec((tm, n_v), row)
        v_shape = jax.ShapeDtypeStruct((m, n_v), BF16)
    return pl.pallas_call(
        functools.partial(_c_kv_kernel, v_transposed=v_transposed),
        grid=(m // tm,),
        in_specs=[pl.BlockSpec((tm, C_KV_LORA), row), pl.BlockSpec((tm, LANES), row),
                  pl.BlockSpec(wk.shape, fixed), pl.BlockSpec(wv.shape, fixed), pl.BlockSpec(place.shape, fixed)],
        out_specs=[pl.BlockSpec((tm, wk.shape[1]), row), v_spec],
        out_shape=[jax.ShapeDtypeStruct((m, wk.shape[1]), BF16), v_shape],
        compiler_params=_params(1),
    )(lat, kr, wk, wv, place)


def _prep_in_ab(w):
    n_mix = w.shape[1] - MEM_W - w.shape[0]
    n_q = n_mix // 3
    scale = jnp.concatenate([jnp.full((n_q,), Q_SCALE, F32), jnp.ones((n_mix - n_q,), F32),
                             jnp.full((MEM_W,), Q_SCALE, F32), jnp.ones((w.shape[0],), F32)])
    return (w * scale[None, :]).astype(BF16)


def _prep_in_c(w):
    d = w.shape[0]
    o = C_Q_LORA + C_KV_LORA
    half = C_ROPE // 2
    kr = w[:, o:o + C_ROPE]
    rot = jnp.concatenate([-kr[:, half:], kr[:, :half]], axis=1)
    pad = jnp.zeros((d, LANES - C_ROPE), F32)
    mq = w[:, o + C_ROPE:o + C_ROPE + MEM_W] * (Q_SCALE)
    gate = w[:, o + C_ROPE + MEM_W:]
    return jnp.concatenate([w[:, :o], kr, pad, rot, pad, mq, gate], axis=1).astype(BF16)


def _prep_uq(w_uq):
    r = w_uq.shape[0]
    w = w_uq.reshape(r, C_HEADS, C_NOPE + C_ROPE)
    nope, rope = w[..., :C_NOPE], w[..., C_NOPE:]
    half = C_ROPE // 2
    rot = jnp.concatenate([-rope[..., half:], rope[..., :half]], axis=-1)
    pad = jnp.zeros((r, C_HEADS, LANES - C_NOPE - C_ROPE), F32)
    plain = jnp.concatenate([nope, rope, pad], axis=-1).reshape(r, C_HEADS * LANES)
    rotated = jnp.concatenate([jnp.zeros_like(nope), rot, pad], axis=-1).reshape(r, C_HEADS * LANES)
    return plain.astype(BF16), rotated.astype(BF16)


def _prep_ukv(w_ukv):
    r = w_ukv.shape[0]
    w = w_ukv.reshape(r, C_HEADS, C_NOPE + C_V)
    wk = jnp.concatenate([w[..., :C_NOPE], jnp.zeros((r, C_HEADS, LANES - C_NOPE), F32)], axis=-1)
    wv = w[..., C_NOPE:]
    eye = jnp.eye(C_ROPE, dtype=F32)
    place = jnp.concatenate([jnp.zeros((C_ROPE, C_NOPE), F32), eye,
                             jnp.zeros((C_ROPE, LANES - C_NOPE - C_ROPE), F32)], axis=1)
    place = jnp.tile(place, (1, C_HEADS))
    place = jnp.concatenate([place, jnp.zeros((LANES - C_ROPE, place.shape[1]), F32)], axis=0)
    return (wk.reshape(r, C_HEADS * LANES).astype(BF16), wv.reshape(r, C_HEADS * C_V).astype(BF16),
            place.astype(BF16))


def _rope_tables(pos):
    half = C_ROPE // 2
    inv = jnp.exp(-math.log(ROPE_BASE) * jnp.arange(half, dtype=F32) * 2.0 / C_ROPE)
    ang = pos.astype(F32)[:, None] * inv[None, :]
    cos, sin = jnp.cos(ang), jnp.sin(ang)
    t = pos.shape[0]
    cos2 = jnp.concatenate([cos, cos], axis=1)
    sin2 = jnp.concatenate([sin, sin], axis=1)
    scale = LOG2E * (C_NOPE + C_ROPE) ** -0.5
    z = lambda n: jnp.zeros((t, n), F32)
    cq = jnp.concatenate([jnp.ones((t, C_NOPE), F32), cos2, z(LANES - C_NOPE - C_ROPE)], axis=1) * scale
    sq = jnp.concatenate([z(C_NOPE), sin2, z(LANES - C_NOPE - C_ROPE)], axis=1) * scale
    ck = jnp.concatenate([cos2, z(LANES - C_ROPE)], axis=1)
    sk = jnp.concatenate([sin2, z(LANES - C_ROPE)], axis=1)
    return cq, sq, ck, sk


def _band_bias_kernel(pos_ref, row_ref, o_ref, *, tq, win, n_valid, transposed):
    c = pl.program_id(0)
    q0, k0 = pos_ref[0, c], pos_ref[1, c]
    wp = row_ref.shape[-1]
    n_rows, n_cols = (win, tq) if transposed else (tq, win)
    rows = pltpu.roll(jnp.broadcast_to(row_ref[...], (n_rows, wp)), 0, 1, stride=1, stride_axis=0)
    ri = lax.broadcasted_iota(jnp.int32, (n_rows, n_cols), 0)
    ci = lax.broadcasted_iota(jnp.int32, (n_rows, n_cols), 1)
    qi, kj = (ci, ri) if transposed else (ri, ci)
    qp, kp = q0 + qi, k0 + kj
    qc, kc = qp // CHUNK, kp // CHUNK
    valid = (kp >= 0) & (kc <= qc) & (kc >= qc - A_BAND_CHUNKS) & (kj < n_valid)
    o_ref[...] = jnp.where(valid, rows[:, :n_cols], NEG)


def band_bias(rel_bias, q0, k0, *, tq, win, n_valid, transposed=False):
    n_cls = q0.shape[0]
    heads = rel_bias.shape[1]
    wp = -(-(tq + win) // LANES) * LANES
    mm = jnp.arange(wp, dtype=jnp.int32)
    mm = jnp.where(mm < (tq if transposed else win), mm, mm - wp)
    mm = mm if transposed else -mm
    rel = jnp.clip((q0 - k0)[:, None] + mm[None, :], -A_REL_CLIP, A_REL_CLIP) + A_REL_CLIP
    rows = jnp.moveaxis((LOG2E * rel_bias.astype(F32))[rel], -1, 1).reshape(n_cls, heads, 1, wp)
    pos = jnp.stack([q0, k0]).astype(jnp.int32)
    if transposed:
        out_spec = pl.BlockSpec((None, None, win, tq), lambda c, h, *_: (c, h // 2, 0, h % 2))
        out_shape = jax.ShapeDtypeStruct((n_cls, heads // 2, win, 2 * tq), F32)
    else:
        out_spec = pl.BlockSpec((None, None, tq, win), lambda c, h, *_: (c, h, 0, 0))
        out_shape = jax.ShapeDtypeStruct((n_cls, heads, tq, win), F32)
    grid_spec = pltpu.PrefetchScalarGridSpec(
        num_scalar_prefetch=1,
        grid=(n_cls, heads),
        in_specs=[pl.BlockSpec((None, None, 1, wp), lambda c, h, *_: (c, h, 0, 0))],
        out_specs=out_spec)
    return pl.pallas_call(
        functools.partial(_band_bias_kernel, tq=tq, win=win, n_valid=n_valid, transposed=transposed),
        grid_spec=grid_spec,
        out_shape=out_shape,
        compiler_params=_params(2),
    )(pos, rows)


A_TQ = 256
A_SUB = 4
FLASH_T = 512
ONES_ROWS = 16


def _trunk(x, pos0, mem_k, mem_vt, past, wts):
    bt, t, d = x.shape
    m = bt * t
    depth = len(wts["w_out"])
    x2 = x.reshape(m, d)
    new_a, new_b, new_c = [], [], []
    pos = pos0 + jnp.arange(t, dtype=jnp.int32)
    one = lambda width, dt: (width, (dt,))
    for i in range(depth):
        kind, j = i % N_MIXERS, i // N_MIXERS
        if kind == 0:
            hd = A_HEADS * HEAD_DIM
            w_in = wts["w_in_a"][j]
            if past is None:
                tq = min(A_TQ, t)
                q, k, k16, v, mq, gate, vt = norm_matmul(
                    x2, wts["norm_g"][i], w_in,
                    (one(hd, BF16), (hd, (F32, BF16)), one(hd, F32), one(MEM_W, BF16), one(d, BF16)),
                    t_seg=2, t_blk=tq)
                k3, v3 = k.reshape(bt, t, hd), v.reshape(bt, t, hd)
                back_blocks = A_PAST_ROWS // tq
                n_win = min(back_blocks + 1, t // tq)
                win = n_win * tq
                n_cls = n_win
                q0 = jnp.arange(n_cls, dtype=jnp.int32) * tq
                k0 = jnp.clip(q0 - A_PAST_ROWS, 0, t - win)
                bias = band_bias(wts["rel_bias_a"][j], q0, k0, tq=tq, win=win, n_valid=win, transposed=True)
                br = attn_window_t(q.reshape(bt, t, hd), k16.reshape(bt, t, hd), vt, bias,
                                   tq=tq, n_sub=min(A_SUB, t // tq), n_win=n_win, back_blocks=back_blocks)
                keep = min(A_PAST_ROWS, t)
                new_a.append((k3[:, t - keep:].reshape(bt, keep, A_HEADS, HEAD_DIM),
                              v3[:, t - keep:].reshape(bt, keep, A_HEADS, HEAD_DIM)))
            else:
                q, k, v, mq, gate = norm_matmul(
                    x2, wts["norm_g"][i], w_in,
                    (one(hd, BF16), one(hd, F32), one(hd, F32), one(MEM_W, BF16), one(d, BF16)))
                q3, k3, v3 = (a.reshape(bt, t, hd) for a in (q, k, v))
                ck, cv = past[0][j], past[1][j]
                rows = ck.shape[1]
                n_keys = rows + t
                pad = (-n_keys) % LANES
                zeros = jnp.zeros((bt, pad, hd), F32)
                kk = jnp.concatenate([ck.reshape(bt, rows, hd), k3, zeros], axis=1)
                vv = jnp.concatenate([cv.reshape(bt, rows, hd), v3, zeros], axis=1)
                start = jnp.full((1,), pos0, jnp.int32)
                bias = band_bias(wts["rel_bias_a"][j], start, start - rows, tq=t, win=n_keys + pad,
                                 n_valid=n_keys)
                br = attn_window(q3, kk, vv, bias, tq=t, win=n_keys + pad, back=0)
                new_a.append((k3.reshape(bt, t, A_HEADS, HEAD_DIM), v3.reshape(bt, t, A_HEADS, HEAD_DIM)))
        elif kind == 1:
            hd = B_HEADS * 2 * HEAD_DIM
            lam_init = 0.8 - 0.6 * math.exp(-0.3 * i)
            common = dict(mode="diff", lam_p=wts["lambda_b"][j], sub_g=wts["subln_g_b"][j], lam_init=lam_init)
            w_in = wts["w_in_b"][j]
            if past is None:
                q, k, k16, v, mq, gate, vt = norm_matmul(
                    x2, wts["norm_g"][i], w_in,
                    (one(hd, BF16), (hd, (F32, BF16)), one(hd, F32), one(MEM_W, BF16), one(d, BF16)),
                    wt=w_in[:, 2 * hd:3 * hd].T, t_blk=FLASH_T, tm=FLASH_T,
                    split={1: (B_HEADS, 2, HEAD_DIM), 3: (B_HEADS, 2 * HEAD_DIM)})
                br = flash_prompt(q.reshape(bt, t, hd), k16.reshape(bt, t, hd), vt, t=FLASH_T, **common)
            else:
                q, k, v, mq, gate = norm_matmul(
                    x2, wts["norm_g"][i], w_in,
                    (one(hd, BF16), one(hd, F32), one(hd, F32), one(MEM_W, BF16), one(d, BF16)))
                ck, cv = past[2][j], past[3][j]
                rows = ck.shape[1]
                br = flash_attention(q.reshape(bt, t, hd), ck.reshape(bt, rows, hd), cv.reshape(bt, rows, hd),
                                     k.reshape(bt, t, hd), v.reshape(bt, t, hd),
                                     tq=t, tk=rows, q_pos0=rows, **common)
            new_b.append((k.reshape(bt, t, B_HEADS, 2, HEAD_DIM), v.reshape(bt, t, B_HEADS, 2 * HEAD_DIM)))
        else:
            cq, ckv, kra, krb, mq, gate = norm_matmul(
                x2, wts["norm_g"][i], wts["w_in_c"][j],
                (one(C_Q_LORA, F32), one(C_KV_LORA, F32), one(LANES, F32), one(LANES, F32),
                 one(MEM_W, BF16), one(d, BF16)))
            wq, wqr = wts["w_uq_c"][j]
            wk, wv, place = wts["w_ukv_c"][j]
            qcat, lat, kr = c_pre(cq, ckv, kra, krb, wts["q_norm_g_c"][j], wts["kv_norm_g_c"][j],
                                  wq, wqr, _rope_tables(pos), t)
            q3 = qcat.reshape(bt, t, -1)
            if past is None:
                kcat, vt = c_kv(lat, kr, wk, wv.T, place, v_transposed=True, tm=FLASH_T)
                br = flash_prompt(q3, kcat.reshape(bt, t, -1), vt, mode="mla", t=FLASH_T)
            else:
                kcat, vcat = c_kv(lat, kr, wk, wv, place, v_transposed=False)
                cl, cr = past[4][j], past[5][j]
                rows = cl.shape[1]
                cr = jnp.pad(cr.reshape(bt * rows, -1), ((0, 0), (0, LANES - C_ROPE)))
                kc_, vc_ = c_kv(cl.reshape(bt * rows, -1), cr, wk, wv, place, v_transposed=False)
                br = flash_attention(q3, kc_.reshape(bt, rows, -1), vc_.reshape(bt, rows, -1),
                                     kcat.reshape(bt, t, -1), vcat.reshape(bt, t, -1),
                                     mode="mla", tq=t, tk=rows, q_pos0=rows)
            new_c.append((lat.reshape(bt, t, -1), kr[:, :C_ROPE].reshape(bt, t, -1)))
        last = i == depth - 1
        x2 = gated_out(br.reshape(m, -1), mq, gate, x2, wts["w_out"][i], wts["final_g"],
                       mem_k[i], mem_vt[i], final=last)
    return x2.reshape(bt, t, d), new_a, new_b, new_c


def kernel(x_prompt, x_sample, cache_a_k, cache_a_v, cache_b_k, cache_b_v, cache_c_lat, cache_c_rope,
           cache_mem_k, cache_mem_v, mem_prompt, norm_g, final_g, mem_norm_g, w_mem_kv, w_out, w_in_a,
           rel_bias_a, w_in_b, lambda_b, subln_g_b, w_in_c, q_norm_g_c, kv_norm_g_c, w_uq_c, w_ukv_c):
    depth = w_out.shape[0]
    bp, n_mem, d = mem_prompt.shape
    mem_heads = MEM_W // HEAD_DIM
    wts = dict(
        norm_g=norm_g, final_g=final_g, rel_bias_a=rel_bias_a, lambda_b=lambda_b, subln_g_b=subln_g_b,
        q_norm_g_c=q_norm_g_c, kv_norm_g_c=kv_norm_g_c,
        w_out=[w_out[i].astype(BF16) for i in range(depth)],
        w_in_a=[_prep_in_ab(w_in_a[j]) for j in range(w_in_a.shape[0])],
        w_in_b=[_prep_in_ab(w_in_b[j]) for j in range(w_in_b.shape[0])],
        w_in_c=[_prep_in_c(w_in_c[j]) for j in range(w_in_c.shape[0])],
        w_uq_c=[_prep_uq(w_uq_c[j]) for j in range(w_uq_c.shape[0])],
        w_ukv_c=[_prep_ukv(w_ukv_c[j]) for j in range(w_ukv_c.shape[0])],
    )
    mem2 = mem_prompt.reshape(bp * n_mem, d)
    mem_k_p, mem_v_p, mem_vt_p = [], [], []
    for i in range(depth):
        w_kv = w_mem_kv[i].astype(BF16)
        mk, mv, mvt = norm_matmul(mem2, mem_norm_g[i], w_kv, ((MEM_W, (F32,)), (MEM_W, (F32,))),
                                  wt=w_kv[:, MEM_W:].T, t_blk=n_mem, tm=n_mem)
        mem_k_p.append(mk.reshape(bp, n_mem, MEM_W))
        mem_v_p.append(mv.reshape(bp, n_mem, MEM_W))
        mem_vt_p.append(mvt)

    y_p, na_p, nb_p, nc_p = _trunk(x_prompt, 0, mem_k_p, mem_vt_p, None, wts)

    bs = x_sample.shape[0]
    mem_k_s = [cache_mem_k[i].reshape(bs, n_mem, MEM_W) for i in range(depth)]
    mem_vt_s = [jnp.swapaxes(cache_mem_v[i].reshape(bs, n_mem, MEM_W), 1, 2).astype(BF16) for i in range(depth)]
    past = (cache_a_k, cache_a_v, cache_b_k, cache_b_v, cache_c_lat, cache_c_rope)
    y_s, na_s, nb_s, nc_s = _trunk(x_sample, cache_b_k.shape[2], mem_k_s, mem_vt_s, past, wts)

    stk = lambda lst, n: jnp.stack([s[n] for s in lst])
    heads4 = lambda lst: jnp.stack([a.reshape(bp, n_mem, mem_heads, HEAD_DIM) for a in lst])
    return (y_p, y_s,
            stk(na_p, 0), stk(na_p, 1), stk(na_s, 0), stk(na_s, 1),
            stk(nb_p, 0), stk(nb_p, 1), stk(nb_s, 0), stk(nb_s, 1),
            stk(nc_p, 0), stk(nc_p, 1), stk(nc_s, 0), stk(nc_s, 1),
            heads4(mem_k_p), heads4(mem_v_p))
```

```python
import functools
import itertools
import math

import jax
import jax.numpy as jnp
from jax import lax
from jax.experimental import pallas as pl
from jax.experimental.pallas import tpu as pltpu

F32 = jnp.float32
BF16 = jnp.bfloat16

LANES = 128
VMEM_LIMIT = 56 * 1024 * 1024

EPS = 1e-6
NEG = -1e30
LOG2E = math.log2(math.e)
CHUNK = 64
HEAD_DIM = 64
Q_SCALE = LOG2E * HEAD_DIM ** -0.5
MEM_W = 256
A_HEADS = 12
A_PAST_ROWS = 512
A_BAND_CHUNKS = 8
A_REL_CLIP = 128
B_HEADS = 6
C_HEADS = 12
C_NOPE = 64
C_ROPE = 32
C_V = 64
C_Q_LORA = 384
C_KV_LORA = 256
ROPE_BASE = 10000.0
N_MIXERS = 3


def _params(n_grid):
    return pltpu.CompilerParams(dimension_semantics=("arbitrary",) * n_grid,
                                vmem_limit_bytes=VMEM_LIMIT)


def _row_tile(m, want):
    t = min(m, want)
    assert m % t == 0, (m, t)
    return t


def _rms(x, g):
    return x * lax.rsqrt(jnp.mean(x * x, axis=-1, keepdims=True) + EPS) * g


NT = (((1,), (1,)), ((), ()))


def _store_blocks(o_ref, rt):
    blk = o_ref.shape[-1]
    for jb in range(o_ref.shape[0]):
        o_ref[jb] = rt[:, jb * blk:(jb + 1) * blk]


def _norm_matmul_kernel(x_ref, g_ref, w_ref, *rest, segs, has_t, t_seg):
    h = _rms(x_ref[...], g_ref[...]).astype(BF16)
    out_refs = rest[1:] if has_t else rest
    off = n_out = 0
    for si, (width, dtypes) in enumerate(segs):
        r = jnp.dot(h, w_ref[:, off:off + width], preferred_element_type=F32)
        if si == t_seg:
            _store_blocks(out_refs[-1], r.T.astype(BF16))
        for dt in dtypes:
            o_ref = out_refs[n_out]
            if len(o_ref.shape) == 2:
                o_ref[...] = r.astype(dt)
            else:
                w = o_ref.shape[-1]
                for c, ix in enumerate(itertools.product(*[range(n) for n in o_ref.shape[1:-1]])):
                    o_ref[(slice(None),) + ix + (slice(None),)] = r[:, c * w:(c + 1) * w].astype(dt)
            n_out += 1
        off += width
    if has_t:
        _store_blocks(out_refs[n_out], lax.dot_general(rest[0][...], h, NT, preferred_element_type=F32).astype(BF16))


def norm_matmul(x, g, w, segs, wt=None, t_seg=None, t_blk=None, tm=512, split=None):
    m, d = x.shape
    tm = _row_tile(m, tm)
    n = w.shape[1]
    assert n == sum(wd for wd, _ in segs)
    row = lambda i: (i, 0)
    fixed = lambda i: (0, 0)
    in_specs = [pl.BlockSpec((tm, d), row), pl.BlockSpec((1, d), fixed), pl.BlockSpec((d, n), fixed)]
    args = [x, g.reshape(1, d).astype(F32), w]
    out_specs = [pl.BlockSpec((tm, wd), row) for wd, dts in segs for _ in dts]
    out_shape = [jax.ShapeDtypeStruct((m, wd), dt) for wd, dts in segs for dt in dts]
    for pos, tail in (split or {}).items():
        assert math.prod(tail) == out_shape[pos].shape[1]
        out_specs[pos] = pl.BlockSpec((tm,) + tail, lambda i, n=len(tail): (i,) + (0,) * n)
        out_shape[pos] = jax.ShapeDtypeStruct((m,) + tail, out_shape[pos].dtype)
    assert wt is None or t_seg is None
    if wt is not None:
        in_specs.append(pl.BlockSpec(wt.shape, fixed))
        args.append(wt)
    if wt is not None or t_seg is not None:
        n_t = wt.shape[0] if wt is not None else segs[t_seg][0]
        assert tm % t_blk == 0
        out_specs.append(pl.BlockSpec((tm // t_blk, n_t, t_blk), lambda i: (i, 0, 0)))
        out_shape.append(jax.ShapeDtypeStruct((m // t_blk, n_t, t_blk), BF16))
    return pl.pallas_call(
        functools.partial(_norm_matmul_kernel, segs=tuple(segs), has_t=wt is not None, t_seg=t_seg),
        grid=(m // tm,),
        in_specs=in_specs,
        out_specs=out_specs,
        out_shape=out_shape,
        compiler_params=_params(1),
    )(*args)


def _gated_out_kernel(br_ref, mq_ref, gate_ref, x_ref, w_ref, fg_ref, mk_ref, mvt_ref, o_ref, *, final):
    nb = br_ref.shape[-1]
    tm = x_ref.shape[0]
    halves = _half_masks()
    mq = mq_ref[...]
    mem = []
    for pair in range(mq.shape[1] // LANES):
        cols = slice(pair * LANES, (pair + 1) * LANES)
        q = mq[:, cols]
        q2 = jnp.concatenate([jnp.where(halves[h], q, jnp.zeros_like(q)) for h in range(2)], axis=0)
        st = lax.dot_general(mk_ref[:, cols].astype(BF16), q2, NT, preferred_element_type=F32)
        p = jnp.exp2(st - jnp.max(st, axis=0, keepdims=True))
        l = jnp.sum(p, axis=0, keepdims=True)
        o = jnp.dot(mvt_ref[cols, :], p.astype(BF16), preferred_element_type=F32) / l
        mem.append(jnp.concatenate([o[:HEAD_DIM, :tm], o[HEAD_DIM:, tm:]], axis=0).T)
    mo = jnp.concatenate(mem, axis=1)
    gate = gate_ref[...].astype(F32)
    sg = gate * jax.nn.sigmoid(gate)
    y1 = (br_ref[...].astype(F32) * sg[:, :nb]).astype(BF16)
    y2 = (mo * sg[:, nb:]).astype(BF16)
    acc = jnp.dot(y1, w_ref[:nb, :], preferred_element_type=F32)
    acc += jnp.dot(y2, w_ref[nb:, :], preferred_element_type=F32)
    xn = x_ref[...] + acc
    o_ref[...] = _rms(xn, fg_ref[...]) if final else xn


def gated_out(br, mq, gate, x, w, final_g, mem_k, mem_vt, final, tm=1024):
    m, d = x.shape
    t_len = m // mem_k.shape[0]
    tm = _row_tile(t_len, tm)
    n_t = t_len // tm
    nb, nm, ng = br.shape[1], mq.shape[1], gate.shape[1]
    row = lambda i: (i, 0)
    fixed = lambda i: (0, 0)
    per_batch = lambda i: (i // n_t, 0, 0)
    return pl.pallas_call(
        functools.partial(_gated_out_kernel, final=final),
        grid=(m // tm,),
        in_specs=[pl.BlockSpec((tm, nb), row), pl.BlockSpec((tm, nm), row), pl.BlockSpec((tm, ng), row),
                  pl.BlockSpec((tm, d), row), pl.BlockSpec((ng, d), fixed), pl.BlockSpec((1, d), fixed),
                  pl.BlockSpec((None,) + mem_k.shape[1:], per_batch),
                  pl.BlockSpec((None,) + mem_vt.shape[1:], per_batch)],
        out_specs=pl.BlockSpec((tm, d), row),
        out_shape=jax.ShapeDtypeStruct((m, d), F32),
        compiler_params=_params(1),
    )(br, mq, gate, x, w, final_g.reshape(1, d).astype(F32), mem_k, mem_vt)


def _half_masks():
    lane = lax.broadcasted_iota(jnp.int32, (1, LANES), 1)
    lo = lane < HEAD_DIM
    return lo, jnp.logical_not(lo)


def _attn_window_kernel(q_ref, k_ref, v_ref, *rest, tq, win, back):
    if len(rest) == 2:
        bias_ref, o_ref = rest
    else:
        bias_ref, (o_ref,) = None, rest
    tk_all = k_ref.shape[0]
    if win == tk_all:
        k = k_ref[...]
        v = v_ref[...]
    else:
        start = jnp.clip(pl.program_id(2) * tq - back, 0, tk_all - win)
        start = pl.multiple_of(start, CHUNK)
        k = k_ref[pl.ds(start, win), :]
        v = v_ref[pl.ds(start, win), :]
    k = k.astype(BF16)
    v = v.astype(BF16)
    q = q_ref[...]
    halves = _half_masks()
    outs = []
    for i in range(2):
        qm = jnp.where(halves[i], q, jnp.zeros_like(q))
        s = lax.dot_general(qm, k, (((1,), (1,)), ((), ())), preferred_element_type=F32)
        if bias_ref is not None:
            s = s + bias_ref[i]
        m = jnp.max(s, axis=-1, keepdims=True)
        p = jnp.exp2(s - m)
        l = jnp.sum(p, axis=-1, keepdims=True)
        o = jnp.dot(p.astype(BF16), v, preferred_element_type=F32)
        outs.append(o / l)
    o_ref[...] = jnp.where(halves[0], outs[0], outs[1]).astype(o_ref.dtype)


def attn_window(q, k, v, bias, *, tq, win, back):
    bt, t_q, gw = q.shape
    g = gw // LANES
    t_k = k.shape[1]
    tq = _row_tile(t_q, tq)
    in_specs = [pl.BlockSpec((None, tq, LANES), lambda b, h, i: (b, i, h)),
                pl.BlockSpec((None, t_k, LANES), lambda b, h, i: (b, 0, h)),
                pl.BlockSpec((None, t_k, LANES), lambda b, h, i: (b, 0, h))]
    args = [q, k, v]
    if bias is not None:
        last = bias.shape[0] - 1
        in_specs.append(pl.BlockSpec((None, 2, tq, win), lambda b, h, i: (jnp.minimum(i, last), h, 0, 0)))
        args.append(bias)
    return pl.pallas_call(
        functools.partial(_attn_window_kernel, tq=tq, win=win, back=back),
        grid=(bt, g, t_q // tq),
        in_specs=in_specs,
        out_specs=pl.BlockSpec((None, tq, LANES), lambda b, h, i: (b, i, h)),
        out_shape=jax.ShapeDtypeStruct((bt, t_q, gw), BF16),
        compiler_params=_params(3),
    )(*args)


def _attn_window_t_kernel(q_ref, k_ref, vt_ref, *rest, tq, n_sub, n_win, back_blocks):
    if len(rest) == 2:
        bias_ref, o_ref = rest
    else:
        bias_ref, (o_ref,) = None, rest
    n_blk, _, blk = vt_ref.shape
    halves = _half_masks()

    def score(sub):
        i = pl.program_id(2) * n_sub + sub
        if n_win == n_blk:
            b0 = 0
            k = k_ref[...]
        else:
            b0 = jnp.clip(i - back_blocks, 0, n_blk - n_win)
            k = k_ref[pl.ds(pl.multiple_of(b0 * blk, blk), n_win * blk), :]
        q = q_ref[sub * tq:(sub + 1) * tq, :]
        q2 = jnp.concatenate([jnp.where(halves[h], q, jnp.zeros_like(q)) for h in range(2)], axis=0)
        st = lax.dot_general(k.astype(BF16), q2, NT, preferred_element_type=F32)
        if bias_ref is not None:
            st = st + bias_ref[jnp.minimum(i, bias_ref.shape[0] - 1)]
        return b0, st, jnp.max(st, axis=0, keepdims=True)

    def accumulate(sub, b0, st, m):
        p = jnp.exp2(st - m)
        l = jnp.sum(p, axis=0, keepdims=True)
        pb = p.astype(BF16)
        o = None
        for j in range(n_win):
            oj = jnp.dot(vt_ref[b0 + j], pb[j * blk:(j + 1) * blk, :], preferred_element_type=F32)
            o = oj if o is None else o + oj
        o = o / l
        o = jnp.concatenate([o[:HEAD_DIM, :tq], o[HEAD_DIM:, tq:]], axis=0).T
        o_ref[sub * tq:(sub + 1) * tq, :] = o.astype(o_ref.dtype)

    staged = [score(sub) for sub in range(n_sub)]
    for sub in range(n_sub):
        accumulate(sub, *staged[sub])


def attn_window_t(q, k, vt, bias, *, tq, n_sub, n_win, back_blocks):
    bt, t_q, gw = q.shape
    g = gw // LANES
    t_k = k.shape[1]
    blk = vt.shape[2]
    n_blk = t_k // blk
    assert vt.shape == (bt * n_blk, gw, blk), vt.shape
    rows = n_sub * tq
    assert t_q % rows == 0
    in_specs = [pl.BlockSpec((None, rows, LANES), lambda b, h, i: (b, i, h)),
                pl.BlockSpec((None, t_k, LANES), lambda b, h, i: (b, 0, h)),
                pl.BlockSpec((n_blk, LANES, blk), lambda b, h, i: (b, h, 0))]
    args = [q, k, vt]
    if bias is not None:
        in_specs.append(pl.BlockSpec((bias.shape[0], None, n_win * blk, 2 * tq), lambda b, h, i: (0, h, 0, 0)))
        args.append(bias)
    return pl.pallas_call(
        functools.partial(_attn_window_t_kernel, tq=tq, n_sub=n_sub, n_win=n_win, back_blocks=back_blocks),
        grid=(bt, g, t_q // rows),
        in_specs=in_specs,
        out_specs=pl.BlockSpec((None, rows, LANES), lambda b, h, i: (b, i, h)),
        out_shape=jax.ShapeDtypeStruct((bt, t_q, gw), BF16),
        compiler_params=_params(3),
    )(*args)


def _flash_kernel(slope_ref, q_ref, km_ref, vm_ref, kt_ref, vt_ref, *rest,
                  mode, tq, tk, tt, n_main, q_pos0, lam_init):
    if mode == "diff":
        lam_ref, subg_ref, o_ref, m_scr, l_scr, acc_scr = rest
    else:
        o_ref, m_scr, l_scr, acc_scr = rest
    head = pl.program_id(1)
    qi = pl.program_id(2)
    q = q_ref[...]
    halves = _half_masks()
    if mode == "diff":
        qs = [jnp.where(halves[i], q, jnp.zeros_like(q)) for i in range(2)]
        slope = slope_ref[head]
    else:
        qs = [q[:, :LANES], q[:, LANES:]]
        slope = None

    m_scr[...] = jnp.full(m_scr.shape, NEG, F32)
    l_scr[...] = jnp.zeros(l_scr.shape, F32)
    acc_scr[...] = jnp.zeros(acc_scr.shape, F32)

    def update(i, s, shift, v):
        m_old = m_scr[i]
        m_new = jnp.maximum(m_old, jnp.max(s, axis=-1, keepdims=True) + shift)
        alpha = jnp.exp2(m_old - m_new)
        p = jnp.exp2(s - (m_new - shift))
        l_scr[i] = alpha * l_scr[i] + jnp.sum(p, axis=-1, keepdims=True)
        acc_scr[i] = alpha * acc_scr[i] + jnp.dot(p.astype(BF16), v, preferred_element_type=F32)
        m_scr[i] = m_new

    def k_of(kblk, i):
        return kblk if mode == "diff" else kblk[:, i * LANES:(i + 1) * LANES]

    nt = (((1,), (1,)), ((), ()))
    q_base = q_pos0 + qi * tq

    if mode == "diff":
        ri = lax.broadcasted_iota(jnp.int32, (tq, tk), 0)
        ci = lax.broadcasted_iota(jnp.int32, (tq, tk), 1)
        main_tile = slope * (ci - ri).astype(F32)

    def main_body(kb, carry):
        off = pl.multiple_of(kb * tk, tk)
        kblk = km_ref[pl.ds(off, tk), :].astype(BF16)
        vblk = vm_ref[pl.ds(off, tk), :].astype(BF16)
        for i in range(2):
            s = lax.dot_general(qs[i], k_of(kblk, i), nt, preferred_element_type=F32)
            if mode == "diff":
                shift = -slope * (q_base - kb * tk).astype(F32)
                update(i, s + main_tile, shift, vblk)
            else:
                update(i, s, 0.0, vblk)
        return carry

    lax.fori_loop(0, n_main, main_body, 0)

    ri = lax.broadcasted_iota(jnp.int32, (tq, tt), 0)
    ci = lax.broadcasted_iota(jnp.int32, (tq, tt), 1)
    valid = (ci // CHUNK) <= (ri // CHUNK)
    if mode == "diff":
        tail_tile = jnp.where(valid, -slope * jnp.abs(ri - ci).astype(F32), NEG)
    else:
        tail_tile = jnp.where(valid, 0.0, NEG).astype(F32)
    kblk = kt_ref[...].astype(BF16)
    vblk = vt_ref[...].astype(BF16)
    for i in range(2):
        s = lax.dot_general(qs[i], k_of(kblk, i), nt, preferred_element_type=F32)
        update(i, s + tail_tile, 0.0, vblk)

    o0 = acc_scr[0] / l_scr[0]
    o1 = acc_scr[1] / l_scr[1]
    if mode == "diff":
        lp = lam_ref[...]
        lam = (jnp.exp(jnp.sum(lp[0:1] * lp[1:2], axis=-1, keepdims=True))
               - jnp.exp(jnp.sum(lp[2:3] * lp[3:4], axis=-1, keepdims=True)) + lam_init)
        o = o0 - lam * o1
        o_ref[...] = (_rms(o, subg_ref[...]) * (1.0 - lam_init)).astype(o_ref.dtype)
    else:
        o_ref[...] = jnp.where(halves[0], o0, o1).astype(o_ref.dtype)


def flash_attention(q, k_main, v_main, k_tail, v_tail, *, mode, tq, tk, q_pos0,
                    lam_p=None, sub_g=None, lam_init=0.0):
    bt, t_q = q.shape[0], q.shape[1]
    tq = _row_tile(t_q, tq)
    qw = LANES if mode == "diff" else 2 * LANES
    groups = q.shape[2] // qw
    t_main = k_main.shape[1]
    tk = _row_tile(t_main, tk)
    tt, n_main = k_tail.shape[1], t_main // tk
    tail_idx = lambda b, h, i, *_: (b, 0, h)
    q_idx = lambda b, h, i, *_: (b, i, h)
    main_idx = lambda b, h, i, *_: (b, 0, h)
    in_specs = [pl.BlockSpec((None, tq, qw), q_idx),
                pl.BlockSpec((None, t_main, qw), main_idx),
                pl.BlockSpec((None, t_main, LANES), main_idx),
                pl.BlockSpec((None, tt, qw), tail_idx),
                pl.BlockSpec((None, tt, LANES), tail_idx)]
    args = [q, k_main, v_main, k_tail, v_tail]
    if mode == "diff":
        slopes = LOG2E * jnp.exp2(-8.0 * jnp.arange(1, groups + 1, dtype=F32) / groups)
        in_specs += [pl.BlockSpec(lam_p.shape, lambda b, h, i, *_: (0, 0)),
                     pl.BlockSpec((1, LANES), lambda b, h, i, *_: (0, 0))]
        args += [lam_p.astype(F32), sub_g.reshape(1, LANES).astype(F32)]
    else:
        slopes = jnp.zeros((groups,), F32)
    grid_spec = pltpu.PrefetchScalarGridSpec(
        num_scalar_prefetch=1,
        grid=(bt, groups, t_q // tq),
        in_specs=in_specs,
        out_specs=pl.BlockSpec((None, tq, LANES), q_idx),
        scratch_shapes=[pltpu.VMEM((2, tq, 1), F32), pltpu.VMEM((2, tq, 1), F32),
                        pltpu.VMEM((2, tq, LANES), F32)])
    return pl.pallas_call(
        functools.partial(_flash_kernel, mode=mode, tq=tq, tk=tk, tt=tt, n_main=n_main,
                          q_pos0=q_pos0, lam_init=lam_init),
        grid_spec=grid_spec,
        out_shape=jax.ShapeDtypeStruct((bt, t_q, groups * LANES), BF16),
        compiler_params=_params(3),
    )(slopes, *args)


def _flash_prompt_kernel(slope_ref, q_ref, k_ref, vt_ref, *rest, mode, t, lam_init):
    if mode == "diff":
        lam_ref, subg_ref, o_ref = rest[:3]
        rest = rest[3:]
    else:
        o_ref = rest[0]
        rest = rest[1:]
    m_scr, acc_scr, main_tile_scr, diag_tile_scr = rest[:4]
    st_scrs, msub_scrs, alpha_scrs = rest[4:6], rest[6:8], rest[8:10]
    head = pl.program_id(1)
    qi = pl.program_id(2)
    q = q_ref[...]
    halves = _half_masks()
    if mode == "diff":
        qs = [jnp.where(halves[i], q, jnp.zeros_like(q)) for i in range(2)]
        slope = slope_ref[head]
    else:
        qs = [q[:, :LANES], q[:, LANES:]]
    v_rows = acc_scr.shape[1] - ONES_ROWS
    ones = jnp.ones((ONES_ROWS, t), BF16)

    m_scr[...] = jnp.full(m_scr.shape, NEG, F32)
    acc_scr[...] = jnp.zeros(acc_scr.shape, F32)

    @pl.when(qi == 0)
    def _():
        kj = lax.broadcasted_iota(jnp.int32, (t, t), 0)
        qj = lax.broadcasted_iota(jnp.int32, (t, t), 1)
        valid = (kj // CHUNK) <= (qj // CHUNK)
        if mode == "diff":
            main_tile_scr[...] = slope * (kj - qj).astype(F32)
            diag_tile_scr[...] = jnp.where(valid, -slope * jnp.abs(qj - kj).astype(F32), NEG)
        else:
            diag_tile_scr[...] = jnp.where(valid, 0.0, NEG).astype(F32)


    def score(kb, diag, slot):
        off = pl.multiple_of(kb * t, t)
        kblk = k_ref[pl.ds(off, t), :]
        if mode == "diff" and not diag:
            shift = -slope * ((qi - kb) * t).astype(F32)
        else:
            shift = 0.0
        for i in range(2):
            ki = kblk if mode == "diff" else kblk[:, i * LANES:(i + 1) * LANES]
            st = lax.dot_general(ki, qs[i], NT, preferred_element_type=F32)
            if diag:
                st = st + diag_tile_scr[...]
            elif mode == "diff":
                st = st + main_tile_scr[...]
            st_scrs[slot][i] = st
            m_old = m_scr[i]
            m_new = jnp.maximum(m_old, jnp.max(st, axis=0, keepdims=True) + shift)
            m_scr[i] = m_new
            msub_scrs[slot][i] = m_new - shift
            alpha_scrs[slot][i] = jnp.exp2(m_old - m_new)

    def accumulate(kb, slot):
        vt = vt_ref[kb]
        for i in range(2):
            alpha = alpha_scrs[slot][i]
            vi = vt if mode == "diff" else vt[i * v_rows:(i + 1) * v_rows, :]
            vi = jnp.concatenate([vi, ones], axis=0)
            p = jnp.exp2(st_scrs[slot][i] - msub_scrs[slot][i])
            acc_scr[i] = alpha * acc_scr[i] + jnp.dot(vi, p.astype(BF16), preferred_element_type=F32)

    lead = jnp.logical_and(qi >= 2, qi % 2 == 0).astype(jnp.int32)

    @pl.when(lead == 1)
    def _():
        score(0, False, 0)
        accumulate(0, 0)

    @pl.when(qi == 0)
    def _():
        score(0, True, 0)

    @pl.when(qi >= 1)
    def _():
        score(lead, False, 0)

    def pair(kb):
        score(kb + 1, False, 1)
        accumulate(kb, 0)
        score(kb + 2, False, 0)
        accumulate(kb + 1, 1)

    def quad_body(j, carry):
        pair(lead + 4 * j)
        pair(lead + 4 * j + 2)
        return carry

    def pair_body(j, carry):
        pair(lead + 2 * j)
        return carry

    n_pairs = jnp.maximum((qi - lead - 1) // 2, 0)
    lax.fori_loop(0, n_pairs // 2, quad_body, 0)
    lax.fori_loop(n_pairs - n_pairs % 2, n_pairs, pair_body, 0)

    @pl.when(qi >= 1)
    def _():
        score(qi, True, 1)
        accumulate(qi - 1, 0)
        accumulate(qi, 1)

    @pl.when(qi == 0)
    def _():
        accumulate(0, 0)

    o0 = acc_scr[0, :v_rows] / acc_scr[0, v_rows:v_rows + 1]
    o1 = acc_scr[1, :v_rows] / acc_scr[1, v_rows:v_rows + 1]
    if mode == "diff":
        lp = lam_ref[...]
        lam = (jnp.exp(jnp.sum(lp[0:1] * lp[1:2], axis=-1, keepdims=True))
               - jnp.exp(jnp.sum(lp[2:3] * lp[3:4], axis=-1, keepdims=True)) + lam_init)
        o = (o0 - lam * o1).T
        o_ref[...] = (_rms(o, subg_ref[...]) * (1.0 - lam_init)).astype(o_ref.dtype)
    else:
        o_ref[...] = jnp.concatenate([o0, o1], axis=0).T.astype(o_ref.dtype)


def flash_prompt(q, k, vt, *, mode, t, lam_p=None, sub_g=None, lam_init=0.0):
    bt, t_all = q.shape[0], q.shape[1]
    n_blk = t_all // t
    qw = LANES if mode == "diff" else 2 * LANES
    groups = q.shape[2] // qw
    assert vt.shape == (bt * n_blk, groups * LANES, t), vt.shape
    q_idx = lambda b, h, i, *_: (b, i, h)
    in_specs = [pl.BlockSpec((None, t, qw), q_idx),
                pl.BlockSpec((None, t_all, qw), lambda b, h, i, *_: (b, 0, h)),
                pl.BlockSpec((n_blk, LANES, t), lambda b, h, i, *_: (b, h, 0))]
    args = [q, k, vt]
    if mode == "diff":
        slopes = LOG2E * jnp.exp2(-8.0 * jnp.arange(1, groups + 1, dtype=F32) / groups)
        in_specs += [pl.BlockSpec(lam_p.shape, lambda b, h, i, *_: (0, 0)),
                     pl.BlockSpec((1, LANES), lambda b, h, i, *_: (0, 0))]
        args += [lam_p.astype(F32), sub_g.reshape(1, LANES).astype(F32)]
        v_rows = LANES
    else:
        slopes = jnp.zeros((groups,), F32)
        v_rows = LANES // 2
    stat = pltpu.VMEM((2, 1, t), F32)
    tile = pltpu.VMEM((t, t), F32)
    grid_spec = pltpu.PrefetchScalarGridSpec(
        num_scalar_prefetch=1,
        grid=(bt, groups, n_blk),
        in_specs=in_specs,
        out_specs=pl.BlockSpec((None, t, LANES), q_idx),
        scratch_shapes=[stat, pltpu.VMEM((2, v_rows + ONES_ROWS, t), F32), tile, tile,
                        pltpu.VMEM((2, t, t), F32), pltpu.VMEM((2, t, t), F32), stat, stat, stat, stat])
    return pl.pallas_call(
        functools.partial(_flash_prompt_kernel, mode=mode, t=t, lam_init=lam_init),
        grid_spec=grid_spec,
        out_shape=jax.ShapeDtypeStruct((bt, t_all, groups * LANES), BF16),
        compiler_params=_params(3),
    )(slopes, *args)


def _c_pre_kernel(cq_ref, ckv_ref, kra_ref, krb_ref, gq_ref, gkv_ref, wq_ref, wqr_ref,
                  cq_tab, sq_tab, ck_tab, sk_tab, q_out, lat_out, kr_out):
    qn = _rms(cq_ref[...], gq_ref[...]).astype(BF16)
    qc = jnp.dot(qn, wq_ref[...], preferred_element_type=F32)
    qr = jnp.dot(qn, wqr_ref[...], preferred_element_type=F32)
    cos, sin = cq_tab[...], sq_tab[...]
    for h in range(C_HEADS):
        sl = slice(h * LANES, (h + 1) * LANES)
        q_out[:, sl] = (qc[:, sl] * cos + qr[:, sl] * sin).astype(q_out.dtype)
    lat_out[...] = _rms(ckv_ref[...], gkv_ref[...])
    kr = kra_ref[...] * ck_tab[...] + krb_ref[...] * sk_tab[...]
    kr_out[...] = kr


def c_pre(cq, ckv, kra, krb, gq, gkv, wq, wqr, tabs, t_len, tm=512):
    m = cq.shape[0]
    tm = _row_tile(t_len, tm)
    n_t = t_len // tm
    row = lambda i: (i, 0)
    fixed = lambda i: (0, 0)
    trow = lambda i: (i % n_t, 0)
    qw = wq.shape[1]
    return pl.pallas_call(
        _c_pre_kernel,
        grid=(m // tm,),
        in_specs=[pl.BlockSpec((tm, C_Q_LORA), row), pl.BlockSpec((tm, C_KV_LORA), row),
                  pl.BlockSpec((tm, LANES), row), pl.BlockSpec((tm, LANES), row),
                  pl.BlockSpec((1, C_Q_LORA), fixed), pl.BlockSpec((1, C_KV_LORA), fixed),
                  pl.BlockSpec(wq.shape, fixed), pl.BlockSpec(wqr.shape, fixed),
                  pl.BlockSpec((tm, LANES), trow), pl.BlockSpec((tm, LANES), trow),
                  pl.BlockSpec((tm, LANES), trow), pl.BlockSpec((tm, LANES), trow)],
        out_specs=[pl.BlockSpec((tm, qw), row), pl.BlockSpec((tm, C_KV_LORA), row),
                   pl.BlockSpec((tm, LANES), row)],
        out_shape=[jax.ShapeDtypeStruct((m, qw), BF16), jax.ShapeDtypeStruct((m, C_KV_LORA), F32),
                   jax.ShapeDtypeStruct((m, LANES), F32)],
        compiler_params=_params(1),
    )(cq, ckv, kra, krb, gq.reshape(1, -1).astype(F32), gkv.reshape(1, -1).astype(F32), wq, wqr, *tabs)


def _c_kv_kernel(lat_ref, kr_ref, wk_ref, wv_ref, place_ref, k_out, v_out, *, v_transposed):
    lat = lat_ref[...].astype(BF16)
    k = jnp.dot(lat, wk_ref[...], preferred_element_type=F32)
    k += jnp.dot(kr_ref[...].astype(BF16), place_ref[...], preferred_element_type=F32)
    k_out[...] = k.astype(k_out.dtype)
    if v_transposed:
        v = lax.dot_general(wv_ref[...], lat, NT, preferred_element_type=F32)
    else:
        v = jnp.dot(lat, wv_ref[...], preferred_element_type=F32)
    v_out[...] = v.astype(v_out.dtype)


def c_kv(lat, kr, wk, wv, place, v_transposed, tm=512):
    m = lat.shape[0]
    tm = _row_tile(m, tm)
    row = lambda i: (i, 0)
    fixed = lambda i: (0, 0)
    if v_transposed:
        n_v = wv.shape[0]
        v_spec = pl.BlockSpec((None, n_v, tm), lambda i: (i, 0, 0))
        v_shape = jax.ShapeDtypeStruct((m // tm, n_v, tm), BF16)
    else:
        n_v = wv.shape[1]
        v_spec = pl.BlockSpec((tm, n_v), row)
        v_shape = jax.ShapeDtypeStruct((m, n_v), BF16)
    return pl.pallas_call(
        functools.partial(_c_kv_kernel, v_transposed=v_transposed),
        grid=(m // tm,),
        in_specs=[pl.BlockSpec((tm, C_KV_LORA), row), pl.BlockSpec((tm, LANES), row),
                  pl.BlockSpec(wk.shape, fixed), pl.BlockSpec(wv.shape, fixed), pl.BlockSpec(place.shape, fixed)],
        out_specs=[pl.BlockSpec((tm, wk.shape[1]), row), v_spec],
        out_shape=[jax.ShapeDtypeStruct((m, wk.shape[1]), BF16), v_shape],
        compiler_params=_params(1),
    )(lat, kr, wk, wv, place)


def _prep_in_ab(w):
    n_mix = w.shape[1] - MEM_W - w.shape[0]
    n_q = n_mix // 3
    scale = jnp.concatenate([jnp.full((n_q,), Q_SCALE, F32), jnp.ones((n_mix - n_q,), F32),
                             jnp.full((MEM_W,), Q_SCALE, F32), jnp.ones((w.shape[0],), F32)])
    return (w * scale[None, :]).astype(BF16)


def _prep_in_c(w):
    d = w.shape[0]
    o = C_Q_LORA + C_KV_LORA
    half = C_ROPE // 2
    kr = w[:, o:o + C_ROPE]
    rot = jnp.concatenate([-kr[:, half:], kr[:, :half]], axis=1)
    pad = jnp.zeros((d, LANES - C_ROPE), F32)
    mq = w[:, o + C_ROPE:o + C_ROPE + MEM_W] * (Q_SCALE)
    gate = w[:, o + C_ROPE + MEM_W:]
    return jnp.concatenate([w[:, :o], kr, pad, rot, pad, mq, gate], axis=1).astype(BF16)


def _prep_uq(w_uq):
    r = w_uq.shape[0]
    w = w_uq.reshape(r, C_HEADS, C_NOPE + C_ROPE)
    nope, rope = w[..., :C_NOPE], w[..., C_NOPE:]
    half = C_ROPE // 2
    rot = jnp.concatenate([-rope[..., half:], rope[..., :half]], axis=-1)
    pad = jnp.zeros((r, C_HEADS, LANES - C_NOPE - C_ROPE), F32)
    plain = jnp.concatenate([nope, rope, pad], axis=-1).reshape(r, C_HEADS * LANES)
    rotated = jnp.concatenate([jnp.zeros_like(nope), rot, pad], axis=-1).reshape(r, C_HEADS * LANES)
    return plain.astype(BF16), rotated.astype(BF16)


def _prep_ukv(w_ukv):
    r = w_ukv.shape[0]
    w = w_ukv.reshape(r, C_HEADS, C_NOPE + C_V)
    wk = jnp.concatenate([w[..., :C_NOPE], jnp.zeros((r, C_HEADS, LANES - C_NOPE), F32)], axis=-1)
    wv = w[..., C_NOPE:]
    eye = jnp.eye(C_ROPE, dtype=F32)
    place = jnp.concatenate([jnp.zeros((C_ROPE, C_NOPE), F32), eye,
                             jnp.zeros((C_ROPE, LANES - C_NOPE - C_ROPE), F32)], axis=1)
    place = jnp.tile(place, (1, C_HEADS))
    place = jnp.concatenate([place, jnp.zeros((LANES - C_ROPE, place.shape[1]), F32)], axis=0)
    return (wk.reshape(r, C_HEADS * LANES).astype(BF16), wv.reshape(r, C_HEADS * C_V).astype(BF16),
            place.astype(BF16))


def _rope_tables(pos):
    half = C_ROPE // 2
    inv = jnp.exp(-math.log(ROPE_BASE) * jnp.arange(half, dtype=F32) * 2.0 / C_ROPE)
    ang = pos.astype(F32)[:, None] * inv[None, :]
    cos, sin = jnp.cos(ang), jnp.sin(ang)
    t = pos.shape[0]
    cos2 = jnp.concatenate([cos, cos], axis=1)
    sin2 = jnp.concatenate([sin, sin], axis=1)
    scale = LOG2E * (C_NOPE + C_ROPE) ** -0.5
    z = lambda n: jnp.zeros((t, n), F32)
    cq = jnp.concatenate([jnp.ones((t, C_NOPE), F32), cos2, z(LANES - C_NOPE - C_ROPE)], axis=1) * scale
    sq = jnp.concatenate([z(C_NOPE), sin2, z(LANES - C_NOPE - C_ROPE)], axis=1) * scale
    ck = jnp.concatenate([cos2, z(LANES - C_ROPE)], axis=1)
    sk = jnp.concatenate([sin2, z(LANES - C_ROPE)], axis=1)
    return cq, sq, ck, sk


def _band_bias_kernel(pos_ref, row_ref, o_ref, *, tq, win, n_valid, transposed):
    c = pl.program_id(0)
    q0, k0 = pos_ref[0, c], pos_ref[1, c]
    wp = row_ref.shape[-1]
    n_rows, n_cols = (win, tq) if transposed else (tq, win)
    rows = pltpu.roll(jnp.broadcast_to(row_ref[...], (n_rows, wp)), 0, 1, stride=1, stride_axis=0)
    ri = lax.broadcasted_iota(jnp.int32, (n_rows, n_cols), 0)
    ci = lax.broadcasted_iota(jnp.int32, (n_rows, n_cols), 1)
    qi, kj = (ci, ri) if transposed else (ri, ci)
    qp, kp = q0 + qi, k0 + kj
    qc, kc = qp // CHUNK, kp // CHUNK
    valid = (kp >= 0) & (kc <= qc) & (kc >= qc - A_BAND_CHUNKS) & (kj < n_valid)
    o_ref[...] = jnp.where(valid, rows[:, :n_cols], NEG)


def band_bias(rel_bias, q0, k0, *, tq, win, n_valid, transposed=False):
    n_cls = q0.shape[0]
    heads = rel_bias.shape[1]
    wp = -(-(tq + win) // LANES) * LANES
    mm = jnp.arange(wp, dtype=jnp.int32)
    mm = jnp.where(mm < (tq if transposed else win), mm, mm - wp)
    mm = mm if transposed else -mm
    rel = jnp.clip((q0 - k0)[:, None] + mm[None, :], -A_REL_CLIP, A_REL_CLIP) + A_REL_CLIP
    rows = jnp.moveaxis((LOG2E * rel_bias.astype(F32))[rel], -1, 1).reshape(n_cls, heads, 1, wp)
    pos = jnp.stack([q0, k0]).astype(jnp.int32)
    if transposed:
        out_spec = pl.BlockSpec((None, None, win, tq), lambda c, h, *_: (c, h // 2, 0, h % 2))
        out_shape = jax.ShapeDtypeStruct((n_cls, heads // 2, win, 2 * tq), F32)
    else:
        out_spec = pl.BlockSpec((None, None, tq, win), lambda c, h, *_: (c, h, 0, 0))
        out_shape = jax.ShapeDtypeStruct((n_cls, heads, tq, win), F32)
    grid_spec = pltpu.PrefetchScalarGridSpec(
        num_scalar_prefetch=1,
        grid=(n_cls, heads),
        in_specs=[pl.BlockSpec((None, None, 1, wp), lambda c, h, *_: (c, h, 0, 0))],
        out_specs=out_spec)
    return pl.pallas_call(
        functools.partial(_band_bias_kernel, tq=tq, win=win, n_valid=n_valid, transposed=transposed),
        grid_spec=grid_spec,
        out_shape=out_shape,
        compiler_params=_params(2),
    )(pos, rows)


A_TQ = 256
A_SUB = 8
FLASH_T = 512
ONES_ROWS = 16


def _trunk(x, pos0, mem_k, mem_vt, past, wts):
    bt, t, d = x.shape
    m = bt * t
    depth = len(wts["w_out"])
    x2 = x.reshape(m, d)
    new_a, new_b, new_c = [], [], []
    pos = pos0 + jnp.arange(t, dtype=jnp.int32)
    one = lambda width, dt: (width, (dt,))
    for i in range(depth):
        kind, j = i % N_MIXERS, i // N_MIXERS
        if kind == 0:
            hd = A_HEADS * HEAD_DIM
            w_in = wts["w_in_a"][j]
            if past is None:
                tq = min(A_TQ, t)
                q, k, k16, v, mq, gate, vt = norm_matmul(
                    x2, wts["norm_g"][i], w_in,
                    (one(hd, BF16), (hd, (F32, BF16)), one(hd, F32), one(MEM_W, BF16), one(d, BF16)),
                    t_seg=2, t_blk=tq)
                k3, v3 = k.reshape(bt, t, hd), v.reshape(bt, t, hd)
                back_blocks = A_PAST_ROWS // tq
                n_win = min(back_blocks + 1, t // tq)
                win = n_win * tq
                n_cls = n_win
                q0 = jnp.arange(n_cls, dtype=jnp.int32) * tq
                k0 = jnp.clip(q0 - A_PAST_ROWS, 0, t - win)
                bias = band_bias(wts["rel_bias_a"][j], q0, k0, tq=tq, win=win, n_valid=win, transposed=True)
                br = attn_window_t(q.reshape(bt, t, hd), k16.reshape(bt, t, hd), vt, bias,
                                   tq=tq, n_sub=min(A_SUB, t // tq), n_win=n_win, back_blocks=back_blocks)
                keep = min(A_PAST_ROWS, t)
                new_a.append((k3[:, t - keep:].reshape(bt, keep, A_HEADS, HEAD_DIM),
                              v3[:, t - keep:].reshape(bt, keep, A_HEADS, HEAD_DIM)))
            else:
                q, k, v, mq, gate = norm_matmul(
                    x2, wts["norm_g"][i], w_in,
                    (one(hd, BF16), one(hd, F32), one(hd, F32), one(MEM_W, BF16), one(d, BF16)))
                q3, k3, v3 = (a.reshape(bt, t, hd) for a in (q, k, v))
                ck, cv = past[0][j], past[1][j]
                rows = ck.shape[1]
                n_keys = rows + t
                pad = (-n_keys) % LANES
                zeros = jnp.zeros((bt, pad, hd), F32)
                kk = jnp.concatenate([ck.reshape(bt, rows, hd), k3, zeros], axis=1)
                vv = jnp.concatenate([cv.reshape(bt, rows, hd), v3, zeros], axis=1)
                start = jnp.full((1,), pos0, jnp.int32)
                bias = band_bias(wts["rel_bias_a"][j], start, start - rows, tq=t, win=n_keys + pad,
                                 n_valid=n_keys)
                br = attn_window(q3, kk, vv, bias, tq=t, win=n_keys + pad, back=0)
                new_a.append((k3.reshape(bt, t, A_HEADS, HEAD_DIM), v3.reshape(bt, t, A_HEADS, HEAD_DIM)))
        elif kind == 1:
            hd = B_HEADS * 2 * HEAD_DIM
            lam_init = 0.8 - 0.6 * math.exp(-0.3 * i)
            common = dict(mode="diff", lam_p=wts["lambda_b"][j], sub_g=wts["subln_g_b"][j], lam_init=lam_init)
            w_in = wts["w_in_b"][j]
            if past is None:
                q, k, k16, v, mq, gate, vt = norm_matmul(
                    x2, wts["norm_g"][i], w_in,
                    (one(hd, BF16), (hd, (F32, BF16)), one(hd, F32), one(MEM_W, BF16), one(d, BF16)),
                    wt=w_in[:, 2 * hd:3 * hd].T, t_blk=FLASH_T, tm=FLASH_T,
                    split={1: (B_HEADS, 2, HEAD_DIM), 3: (B_HEADS, 2 * HEAD_DIM)})
                br = flash_prompt(q.reshape(bt, t, hd), k16.reshape(bt, t, hd), vt, t=FLASH_T, **common)
            else:
                q, k, v, mq, gate = norm_matmul(
                    x2, wts["norm_g"][i], w_in,
                    (one(hd, BF16), one(hd, F32), one(hd, F32), one(MEM_W, BF16), one(d, BF16)))
                ck, cv = past[2][j], past[3][j]
                rows = ck.shape[1]
                br = flash_attention(q.reshape(bt, t, hd), ck.reshape(bt, rows, hd), cv.reshape(bt, rows, hd),
                                     k.reshape(bt, t, hd), v.reshape(bt, t, hd),
                                     tq=t, tk=rows, q_pos0=rows, **common)
            new_b.append((k.reshape(bt, t, B_HEADS, 2, HEAD_DIM), v.reshape(bt, t, B_HEADS, 2 * HEAD_DIM)))
        else:
            cq, ckv, kra, krb, mq, gate = norm_matmul(
                x2, wts["norm_g"][i], wts["w_in_c"][j],
                (one(C_Q_LORA, F32), one(C_KV_LORA, F32), one(LANES, F32), one(LANES, F32),
                 one(MEM_W, BF16), one(d, BF16)))
            wq, wqr = wts["w_uq_c"][j]
            wk, wv, place = wts["w_ukv_c"][j]
            qcat, lat, kr = c_pre(cq, ckv, kra, krb, wts["q_norm_g_c"][j], wts["kv_norm_g_c"][j],
                                  wq, wqr, _rope_tables(pos), t)
            q3 = qcat.reshape(bt, t, -1)
            if past is None:
                kcat, vt = c_kv(lat, kr, wk, wv.T, place, v_transposed=True, tm=FLASH_T)
                br = flash_prompt(q3, kcat.reshape(bt, t, -1), vt, mode="mla", t=FLASH_T)
            else:
                kcat, vcat = c_kv(lat, kr, wk, wv, place, v_transposed=False)
                cl, cr = past[4][j], past[5][j]
                rows = cl.shape[1]
                cr = jnp.pad(cr.reshape(bt * rows, -1), ((0, 0), (0, LANES - C_ROPE)))
                kc_, vc_ = c_kv(cl.reshape(bt * rows, -1), cr, wk, wv, place, v_transposed=False)
                br = flash_attention(q3, kc_.reshape(bt, rows, -1), vc_.reshape(bt, rows, -1),
                                     kcat.reshape(bt, t, -1), vcat.reshape(bt, t, -1),
                                     mode="mla", tq=t, tk=rows, q_pos0=rows)
            new_c.append((lat.reshape(bt, t, -1), kr[:, :C_ROPE].reshape(bt, t, -1)))
        last = i == depth - 1
        x2 = gated_out(br.reshape(m, -1), mq, gate, x2, wts["w_out"][i], wts["final_g"],
                       mem_k[i], mem_vt[i], final=last)
    return x2.reshape(bt, t, d), new_a, new_b, new_c


def kernel(x_prompt, x_sample, cache_a_k, cache_a_v, cache_b_k, cache_b_v, cache_c_lat, cache_c_rope,
           cache_mem_k, cache_mem_v, mem_prompt, norm_g, final_g, mem_norm_g, w_mem_kv, w_out, w_in_a,
           rel_bias_a, w_in_b, lambda_b, subln_g_b, w_in_c, q_norm_g_c, kv_norm_g_c, w_uq_c, w_ukv_c):
    depth = w_out.shape[0]
    bp, n_mem, d = mem_prompt.shape
    mem_heads = MEM_W // HEAD_DIM
    wts = dict(
        norm_g=norm_g, final_g=final_g, rel_bias_a=rel_bias_a, lambda_b=lambda_b, subln_g_b=subln_g_b,
        q_norm_g_c=q_norm_g_c, kv_norm_g_c=kv_norm_g_c,
        w_out=[w_out[i].astype(BF16) for i in range(depth)],
        w_in_a=[_prep_in_ab(w_in_a[j]) for j in range(w_in_a.shape[0])],
        w_in_b=[_prep_in_ab(w_in_b[j]) for j in range(w_in_b.shape[0])],
        w_in_c=[_prep_in_c(w_in_c[j]) for j in range(w_in_c.shape[0])],
        w_uq_c=[_prep_uq(w_uq_c[j]) for j in range(w_uq_c.shape[0])],
        w_ukv_c=[_prep_ukv(w_ukv_c[j]) for j in range(w_ukv_c.shape[0])],
    )
    mem2 = mem_prompt.reshape(bp * n_mem, d)
    mem_k_p, mem_v_p, mem_vt_p = [], [], []
    for i in range(depth):
        w_kv = w_mem_kv[i].astype(BF16)
        mk, mv, mvt = norm_matmul(mem2, mem_norm_g[i], w_kv, ((MEM_W, (F32,)), (MEM_W, (F32,))),
                                  wt=w_kv[:, MEM_W:].T, t_blk=n_mem, tm=n_mem)
        mem_k_p.append(mk.reshape(bp, n_mem, MEM_W))
        mem_v_p.append(mv.reshape(bp, n_mem, MEM_W))
        mem_vt_p.append(mvt)

    y_p, na_p, nb_p, nc_p = _trunk(x_prompt, 0, mem_k_p, mem_vt_p, None, wts)

    bs = x_sample.shape[0]
    mem_k_s = [cache_mem_k[i].reshape(bs, n_mem, MEM_W) for i in range(depth)]
    mem_vt_s = [jnp.swapaxes(cache_mem_v[i].reshape(bs, n_mem, MEM_W), 1, 2).astype(BF16) for i in range(depth)]
    past = (cache_a_k, cache_a_v, cache_b_k, cache_b_v, cache_c_lat, cache_c_rope)
    y_s, na_s, nb_s, nc_s = _trunk(x_sample, cache_b_k.shape[2], mem_k_s, mem_vt_s, past, wts)

    stk = lambda lst, n: jnp.stack([s[n] for s in lst])
    heads4 = lambda lst: jnp.stack([a.reshape(bp, n_mem, mem_heads, HEAD_DIM) for a in lst])
    return (y_p, y_s,
            stk(na_p, 0), stk(na_p, 1), stk(na_s, 0), stk(na_s, 1),
            stk(nb_p, 0), stk(nb_p, 1), stk(nb_s, 0), stk(nb_s, 1),
            stk(nc_p, 0), stk(nc_p, 1), stk(nc_s, 0), stk(nc_s, 1),
            heads4(mem_k_p), heads4(mem_v_p))
```

```python
import functools
import itertools
import math

import jax
import jax.numpy as jnp
from jax import lax
from jax.experimental import pallas as pl
from jax.experimental.pallas import tpu as pltpu

F32 = jnp.float32
BF16 = jnp.bfloat16

LANES = 128
VMEM_LIMIT = 56 * 1024 * 1024

EPS = 1e-6
NEG = -1e30
LOG2E = math.log2(math.e)
CHUNK = 64
HEAD_DIM = 64
Q_SCALE = LOG2E * HEAD_DIM ** -0.5
MEM_W = 256
A_HEADS = 12
A_PAST_ROWS = 512
A_BAND_CHUNKS = 8
A_REL_CLIP = 128
B_HEADS = 6
C_HEADS = 12
C_NOPE = 64
C_ROPE = 32
C_V = 64
C_Q_LORA = 384
C_KV_LORA = 256
ROPE_BASE = 10000.0
N_MIXERS = 3


def _params(n_grid):
    return pltpu.CompilerParams(dimension_semantics=("arbitrary",) * n_grid,
                                vmem_limit_bytes=VMEM_LIMIT)


def _row_tile(m, want):
    t = min(m, want)
    assert m % t == 0, (m, t)
    return t


def _rms(x, g):
    return x * lax.rsqrt(jnp.mean(x * x, axis=-1, keepdims=True) + EPS) * g


NT = (((1,), (1,)), ((), ()))


def _store_blocks(o_ref, rt):
    blk = o_ref.shape[-1]
    for jb in range(o_ref.shape[0]):
        o_ref[jb] = rt[:, jb * blk:(jb + 1) * blk]


def _norm_matmul_kernel(x_ref, g_ref, w_ref, *rest, segs, has_t, t_seg):
    h = _rms(x_ref[...], g_ref[...]).astype(BF16)
    out_refs = rest[1:] if has_t else rest
    off = n_out = 0
    for si, (width, dtypes) in enumerate(segs):
        r = jnp.dot(h, w_ref[:, off:off + width], preferred_element_type=F32)
        if si == t_seg:
            _store_blocks(out_refs[-1], r.T.astype(BF16))
        for dt in dtypes:
            o_ref = out_refs[n_out]
            if len(o_ref.shape) == 2:
                o_ref[...] = r.astype(dt)
            else:
                w = o_ref.shape[-1]
                for c, ix in enumerate(itertools.product(*[range(n) for n in o_ref.shape[1:-1]])):
                    o_ref[(slice(None),) + ix + (slice(None),)] = r[:, c * w:(c + 1) * w].astype(dt)
            n_out += 1
        off += width
    if has_t:
        _store_blocks(out_refs[n_out], lax.dot_general(rest[0][...], h, NT, preferred_element_type=F32).astype(BF16))


def norm_matmul(x, g, w, segs, wt=None, t_seg=None, t_blk=None, tm=512, split=None):
    m, d = x.shape
    tm = _row_tile(m, tm)
    n = w.shape[1]
    assert n == sum(wd for wd, _ in segs)
    row = lambda i: (i, 0)
    fixed = lambda i: (0, 0)
    in_specs = [pl.BlockSpec((tm, d), row), pl.BlockSpec((1, d), fixed), pl.BlockSpec((d, n), fixed)]
    args = [x, g.reshape(1, d).astype(F32), w]
    out_specs = [pl.BlockSpec((tm, wd), row) for wd, dts in segs for _ in dts]
    out_shape = [jax.ShapeDtypeStruct((m, wd), dt) for wd, dts in segs for dt in dts]
    for pos, tail in (split or {}).items():
        assert math.prod(tail) == out_shape[pos].shape[1]
        out_specs[pos] = pl.BlockSpec((tm,) + tail, lambda i, n=len(tail): (i,) + (0,) * n)
        out_shape[pos] = jax.ShapeDtypeStruct((m,) + tail, out_shape[pos].dtype)
    assert wt is None or t_seg is None
    if wt is not None:
        in_specs.append(pl.BlockSpec(wt.shape, fixed))
        args.append(wt)
    if wt is not None or t_seg is not None:
        n_t = wt.shape[0] if wt is not None else segs[t_seg][0]
        assert tm % t_blk == 0
        out_specs.append(pl.BlockSpec((tm // t_blk, n_t, t_blk), lambda i: (i, 0, 0)))
        out_shape.append(jax.ShapeDtypeStruct((m // t_blk, n_t, t_blk), BF16))
    return pl.pallas_call(
        functools.partial(_norm_matmul_kernel, segs=tuple(segs), has_t=wt is not None, t_seg=t_seg),
        grid=(m // tm,),
        in_specs=in_specs,
        out_specs=out_specs,
        out_shape=out_shape,
        compiler_params=_params(1),
    )(*args)


def _gated_out_kernel(br_ref, mq_ref, gate_ref, x_ref, w_ref, fg_ref, mk_ref, mvt_ref, o_ref, *, final):
    nb = br_ref.shape[-1]
    tm = x_ref.shape[0]
    halves = _half_masks()
    mq = mq_ref[...]
    mem = []
    for pair in range(mq.shape[1] // LANES):
        cols = slice(pair * LANES, (pair + 1) * LANES)
        q = mq[:, cols]
        q2 = jnp.concatenate([jnp.where(halves[h], q, jnp.zeros_like(q)) for h in range(2)], axis=0)
        st = lax.dot_general(mk_ref[:, cols].astype(BF16), q2, NT, preferred_element_type=F32)
        p = jnp.exp2(st - jnp.max(st, axis=0, keepdims=True))
        l = jnp.sum(p, axis=0, keepdims=True)
        o = jnp.dot(mvt_ref[cols, :], p.astype(BF16), preferred_element_type=F32) / l
        mem.append(jnp.concatenate([o[:HEAD_DIM, :tm], o[HEAD_DIM:, tm:]], axis=0).T)
    mo = jnp.concatenate(mem, axis=1)
    gate = gate_ref[...].astype(F32)
    sg = gate * jax.nn.sigmoid(gate)
    y1 = (br_ref[...].astype(F32) * sg[:, :nb]).astype(BF16)
    y2 = (mo * sg[:, nb:]).astype(BF16)
    acc = jnp.dot(y1, w_ref[:nb, :], preferred_element_type=F32)
    acc += jnp.dot(y2, w_ref[nb:, :], preferred_element_type=F32)
    xn = x_ref[...] + acc
    o_ref[...] = _rms(xn, fg_ref[...]) if final else xn


def gated_out(br, mq, gate, x, w, final_g, mem_k, mem_vt, final, tm=1024):
    m, d = x.shape
    t_len = m // mem_k.shape[0]
    tm = _row_tile(t_len, tm)
    n_t = t_len // tm
    nb, nm, ng = br.shape[1], mq.shape[1], gate.shape[1]
    row = lambda i: (i, 0)
    fixed = lambda i: (0, 0)
    per_batch = lambda i: (i // n_t, 0, 0)
    return pl.pallas_call(
        functools.partial(_gated_out_kernel, final=final),
        grid=(m // tm,),
        in_specs=[pl.BlockSpec((tm, nb), row), pl.BlockSpec((tm, nm), row), pl.BlockSpec((tm, ng), row),
                  pl.BlockSpec((tm, d), row), pl.BlockSpec((ng, d), fixed), pl.BlockSpec((1, d), fixed),
                  pl.BlockSpec((None,) + mem_k.shape[1:], per_batch),
                  pl.BlockSpec((None,) + mem_vt.shape[1:], per_batch)],
        out_specs=pl.BlockSpec((tm, d), row),
        out_shape=jax.ShapeDtypeStruct((m, d), F32),
        compiler_params=_params(1),
    )(br, mq, gate, x, w, final_g.reshape(1, d).astype(F32), mem_k, mem_vt)


def _half_masks():
    lane = lax.broadcasted_iota(jnp.int32, (1, LANES), 1)
    lo = lane < HEAD_DIM
    return lo, jnp.logical_not(lo)


def _attn_window_kernel(q_ref, k_ref, v_ref, *rest, tq, win, back):
    if len(rest) == 2:
        bias_ref, o_ref = rest
    else:
        bias_ref, (o_ref,) = None, rest
    tk_all = k_ref.shape[0]
    if win == tk_all:
        k = k_ref[...]
        v = v_ref[...]
    else:
        start = jnp.clip(pl.program_id(2) * tq - back, 0, tk_all - win)
        start = pl.multiple_of(start, CHUNK)
        k = k_ref[pl.ds(start, win), :]
        v = v_ref[pl.ds(start, win), :]
    k = k.astype(BF16)
    v = v.astype(BF16)
    q = q_ref[...]
    halves = _half_masks()
    outs = []
    for i in range(2):
        qm = jnp.where(halves[i], q, jnp.zeros_like(q))
        s = lax.dot_general(qm, k, (((1,), (1,)), ((), ())), preferred_element_type=F32)
        if bias_ref is not None:
            s = s + bias_ref[i]
        m = jnp.max(s, axis=-1, keepdims=True)
        p = jnp.exp2(s - m)
        l = jnp.sum(p, axis=-1, keepdims=True)
        o = jnp.dot(p.astype(BF16), v, preferred_element_type=F32)
        outs.append(o / l)
    o_ref[...] = jnp.where(halves[0], outs[0], outs[1]).astype(o_ref.dtype)


def attn_window(q, k, v, bias, *, tq, win, back):
    bt, t_q, gw = q.shape
    g = gw // LANES
    t_k = k.shape[1]
    tq = _row_tile(t_q, tq)
    in_specs = [pl.BlockSpec((None, tq, LANES), lambda b, h, i: (b, i, h)),
                pl.BlockSpec((None, t_k, LANES), lambda b, h, i: (b, 0, h)),
                pl.BlockSpec((None, t_k, LANES), lambda b, h, i: (b, 0, h))]
    args = [q, k, v]
    if bias is not None:
        last = bias.shape[0] - 1
        in_specs.append(pl.BlockSpec((None, 2, tq, win), lambda b, h, i: (jnp.minimum(i, last), h, 0, 0)))
        args.append(bias)
    return pl.pallas_call(
        functools.partial(_attn_window_kernel, tq=tq, win=win, back=back),
        grid=(bt, g, t_q // tq),
        in_specs=in_specs,
        out_specs=pl.BlockSpec((None, tq, LANES), lambda b, h, i: (b, i, h)),
        out_shape=jax.ShapeDtypeStruct((bt, t_q, gw), BF16),
        compiler_params=_params(3),
    )(*args)


def _attn_window_t_kernel(q_ref, k_ref, vt_ref, *rest, tq, n_sub, n_win, back_blocks):
    if len(rest) == 2:
        bias_ref, o_ref = rest
    else:
        bias_ref, (o_ref,) = None, rest
    n_blk, _, blk = vt_ref.shape
    halves = _half_masks()

    def score(sub):
        i = pl.program_id(2) * n_sub + sub
        if n_win == n_blk:
            b0 = 0
            k = k_ref[...]
        else:
            b0 = jnp.clip(i - back_blocks, 0, n_blk - n_win)
            k = k_ref[pl.ds(pl.multiple_of(b0 * blk, blk), n_win * blk), :]
        q = q_ref[sub * tq:(sub + 1) * tq, :]
        q2 = jnp.concatenate([jnp.where(halves[h], q, jnp.zeros_like(q)) for h in range(2)], axis=0)
        st = lax.dot_general(k.astype(BF16), q2, NT, preferred_element_type=F32)
        if bias_ref is not None:
            st = st + bias_ref[jnp.minimum(i, bias_ref.shape[0] - 1)]
        return b0, st, jnp.max(st, axis=0, keepdims=True)

    def accumulate(sub, b0, st, m):
        p = jnp.exp2(st - m)
        l = jnp.sum(p, axis=0, keepdims=True)
        pb = p.astype(BF16)
        o = None
        for j in range(n_win):
            oj = jnp.dot(vt_ref[b0 + j], pb[j * blk:(j + 1) * blk, :], preferred_element_type=F32)
            o = oj if o is None else o + oj
        o = o / l
        o = jnp.concatenate([o[:HEAD_DIM, :tq], o[HEAD_DIM:, tq:]], axis=0).T
        o_ref[sub * tq:(sub + 1) * tq, :] = o.astype(o_ref.dtype)

    staged = [score(sub) for sub in range(n_sub)]
    for sub in range(n_sub):
        accumulate(sub, *staged[sub])


def attn_window_t(q, k, vt, bias, *, tq, n_sub, n_win, back_blocks):
    bt, t_q, gw = q.shape
    g = gw // LANES
    t_k = k.shape[1]
    blk = vt.shape[2]
    n_blk = t_k // blk
    assert vt.shape == (bt * n_blk, gw, blk), vt.shape
    rows = n_sub * tq
    assert t_q % rows == 0
    in_specs = [pl.BlockSpec((None, rows, LANES), lambda b, h, i: (b, i, h)),
                pl.BlockSpec((None, t_k, LANES), lambda b, h, i: (b, 0, h)),
                pl.BlockSpec((n_blk, LANES, blk), lambda b, h, i: (b, h, 0))]
    args = [q, k, vt]
    if bias is not None:
        in_specs.append(pl.BlockSpec((bias.shape[0], None, n_win * blk, 2 * tq), lambda b, h, i: (0, h, 0, 0)))
        args.append(bias)
    return pl.pallas_call(
        functools.partial(_attn_window_t_kernel, tq=tq, n_sub=n_sub, n_win=n_win, back_blocks=back_blocks),
        grid=(bt, g, t_q // rows),
        in_specs=in_specs,
        out_specs=pl.BlockSpec((None, rows, LANES), lambda b, h, i: (b, i, h)),
        out_shape=jax.ShapeDtypeStruct((bt, t_q, gw), BF16),
        compiler_params=_params(3),
    )(*args)


def _flash_kernel(slope_ref, q_ref, km_ref, vm_ref, kt_ref, vt_ref, *rest,
                  mode, tq, tk, tt, n_main, q_pos0, lam_init):
    if mode == "diff":
        lam_ref, subg_ref, o_ref, m_scr, l_scr, acc_scr = rest
    else:
        o_ref, m_scr, l_scr, acc_scr = rest
    head = pl.program_id(1)
    qi = pl.program_id(2)
    q = q_ref[...]
    halves = _half_masks()
    if mode == "diff":
        qs = [jnp.where(halves[i], q, jnp.zeros_like(q)) for i in range(2)]
        slope = slope_ref[head]
    else:
        qs = [q[:, :LANES], q[:, LANES:]]
        slope = None

    m_scr[...] = jnp.full(m_scr.shape, NEG, F32)
    l_scr[...] = jnp.zeros(l_scr.shape, F32)
    acc_scr[...] = jnp.zeros(acc_scr.shape, F32)

    def update(i, s, shift, v):
        m_old = m_scr[i]
        m_new = jnp.maximum(m_old, jnp.max(s, axis=-1, keepdims=True) + shift)
        alpha = jnp.exp2(m_old - m_new)
        p = jnp.exp2(s - (m_new - shift))
        l_scr[i] = alpha * l_scr[i] + jnp.sum(p, axis=-1, keepdims=True)
        acc_scr[i] = alpha * acc_scr[i] + jnp.dot(p.astype(BF16), v, preferred_element_type=F32)
        m_scr[i] = m_new

    def k_of(kblk, i):
        return kblk if mode == "diff" else kblk[:, i * LANES:(i + 1) * LANES]

    nt = (((1,), (1,)), ((), ()))
    q_base = q_pos0 + qi * tq

    if mode == "diff":
        ri = lax.broadcasted_iota(jnp.int32, (tq, tk), 0)
        ci = lax.broadcasted_iota(jnp.int32, (tq, tk), 1)
        main_tile = slope * (ci - ri).astype(F32)

    def main_body(kb, carry):
        off = pl.multiple_of(kb * tk, tk)
        kblk = km_ref[pl.ds(off, tk), :].astype(BF16)
        vblk = vm_ref[pl.ds(off, tk), :].astype(BF16)
        for i in range(2):
            s = lax.dot_general(qs[i], k_of(kblk, i), nt, preferred_element_type=F32)
            if mode == "diff":
                shift = -slope * (q_base - kb * tk).astype(F32)
                update(i, s + main_tile, shift, vblk)
            else:
                update(i, s, 0.0, vblk)
        return carry

    lax.fori_loop(0, n_main, main_body, 0)

    ri = lax.broadcasted_iota(jnp.int32, (tq, tt), 0)
    ci = lax.broadcasted_iota(jnp.int32, (tq, tt), 1)
    valid = (ci // CHUNK) <= (ri // CHUNK)
    if mode == "diff":
        tail_tile = jnp.where(valid, -slope * jnp.abs(ri - ci).astype(F32), NEG)
    else:
        tail_tile = jnp.where(valid, 0.0, NEG).astype(F32)
    kblk = kt_ref[...].astype(BF16)
    vblk = vt_ref[...].astype(BF16)
    for i in range(2):
        s = lax.dot_general(qs[i], k_of(kblk, i), nt, preferred_element_type=F32)
        update(i, s + tail_tile, 0.0, vblk)

    o0 = acc_scr[0] / l_scr[0]
    o1 = acc_scr[1] / l_scr[1]
    if mode == "diff":
        lp = lam_ref[...]
        lam = (jnp.exp(jnp.sum(lp[0:1] * lp[1:2], axis=-1, keepdims=True))
               - jnp.exp(jnp.sum(lp[2:3] * lp[3:4], axis=-1, keepdims=True)) + lam_init)
        o = o0 - lam * o1
        o_ref[...] = (_rms(o, subg_ref[...]) * (1.0 - lam_init)).astype(o_ref.dtype)
    else:
        o_ref[...] = jnp.where(halves[0], o0, o1).astype(o_ref.dtype)


def flash_attention(q, k_main, v_main, k_tail, v_tail, *, mode, tq, tk, q_pos0,
                    lam_p=None, sub_g=None, lam_init=0.0):
    bt, t_q = q.shape[0], q.shape[1]
    tq = _row_tile(t_q, tq)
    qw = LANES if mode == "diff" else 2 * LANES
    groups = q.shape[2] // qw
    t_main = k_main.shape[1]
    tk = _row_tile(t_main, tk)
    tt, n_main = k_tail.shape[1], t_main // tk
    tail_idx = lambda b, h, i, *_: (b, 0, h)
    q_idx = lambda b, h, i, *_: (b, i, h)
    main_idx = lambda b, h, i, *_: (b, 0, h)
    in_specs = [pl.BlockSpec((None, tq, qw), q_idx),
                pl.BlockSpec((None, t_main, qw), main_idx),
                pl.BlockSpec((None, t_main, LANES), main_idx),
                pl.BlockSpec((None, tt, qw), tail_idx),
                pl.BlockSpec((None, tt, LANES), tail_idx)]
    args = [q, k_main, v_main, k_tail, v_tail]
    if mode == "diff":
        slopes = LOG2E * jnp.exp2(-8.0 * jnp.arange(1, groups + 1, dtype=F32) / groups)
        in_specs += [pl.BlockSpec(lam_p.shape, lambda b, h, i, *_: (0, 0)),
                     pl.BlockSpec((1, LANES), lambda b, h, i, *_: (0, 0))]
        args += [lam_p.astype(F32), sub_g.reshape(1, LANES).astype(F32)]
    else:
        slopes = jnp.zeros((groups,), F32)
    grid_spec = pltpu.PrefetchScalarGridSpec(
        num_scalar_prefetch=1,
        grid=(bt, groups, t_q // tq),
        in_specs=in_specs,
        out_specs=pl.BlockSpec((None, tq, LANES), q_idx),
        scratch_shapes=[pltpu.VMEM((2, tq, 1), F32), pltpu.VMEM((2, tq, 1), F32),
                        pltpu.VMEM((2, tq, LANES), F32)])
    return pl.pallas_call(
        functools.partial(_flash_kernel, mode=mode, tq=tq, tk=tk, tt=tt, n_main=n_main,
                          q_pos0=q_pos0, lam_init=lam_init),
        grid_spec=grid_spec,
        out_shape=jax.ShapeDtypeStruct((bt, t_q, groups * LANES), BF16),
        compiler_params=_params(3),
    )(slopes, *args)


def _flash_prompt_kernel(slope_ref, q_ref, k_ref, vt_ref, *rest, mode, t, lam_init):
    if mode == "diff":
        lam_ref, subg_ref, o_ref = rest[:3]
        rest = rest[3:]
    else:
        o_ref = rest[0]
        rest = rest[1:]
    m_scr, acc_scr, main_tile_scr, diag_tile_scr = rest[:4]
    st_scrs, msub_scrs, alpha_scrs = rest[4:6], rest[6:8], rest[8:10]
    head = pl.program_id(1)
    qi = pl.program_id(2)
    q = q_ref[...]
    halves = _half_masks()
    if mode == "diff":
        qs = [jnp.where(halves[i], q, jnp.zeros_like(q)) for i in range(2)]
        slope = slope_ref[head]
    else:
        qs = [q[:, :LANES], q[:, LANES:]]
    v_rows = acc_scr.shape[1] - ONES_ROWS
    ones = jnp.ones((ONES_ROWS, t), BF16)

    m_scr[...] = jnp.full(m_scr.shape, NEG, F32)
    acc_scr[...] = jnp.zeros(acc_scr.shape, F32)

    @pl.when(qi == 0)
    def _():
        kj = lax.broadcasted_iota(jnp.int32, (t, t), 0)
        qj = lax.broadcasted_iota(jnp.int32, (t, t), 1)
        valid = (kj // CHUNK) <= (qj // CHUNK)
        if mode == "diff":
            main_tile_scr[...] = slope * (kj - qj).astype(F32)
            diag_tile_scr[...] = jnp.where(valid, -slope * jnp.abs(qj - kj).astype(F32), NEG)
        else:
            diag_tile_scr[...] = jnp.where(valid, 0.0, NEG).astype(F32)


    def score(kb, diag, slot):
        off = pl.multiple_of(kb * t, t)
        kblk = k_ref[pl.ds(off, t), :]
        if mode == "diff" and not diag:
            shift = -slope * ((qi - kb) * t).astype(F32)
        else:
            shift = 0.0
        for i in range(2):
            ki = kblk if mode == "diff" else kblk[:, i * LANES:(i + 1) * LANES]
            st = lax.dot_general(ki, qs[i], NT, preferred_element_type=F32)
            if diag:
                st = st + diag_tile_scr[...]
            elif mode == "diff":
                st = st + main_tile_scr[...]
            st_scrs[slot][i] = st
            m_old = m_scr[i]
            m_new = jnp.maximum(m_old, jnp.max(st, axis=0, keepdims=True) + shift)
            m_scr[i] = m_new
            msub_scrs[slot][i] = m_new - shift
            alpha_scrs[slot][i] = jnp.exp2(m_old - m_new)

    def accumulate(kb, slot):
        vt = vt_ref[kb]
        for i in range(2):
            alpha = alpha_scrs[slot][i]
            vi = vt if mode == "diff" else vt[i * v_rows:(i + 1) * v_rows, :]
            vi = jnp.concatenate([vi, ones], axis=0)
            p = jnp.exp2(st_scrs[slot][i] - msub_scrs[slot][i])
            acc_scr[i] = alpha * acc_scr[i] + jnp.dot(vi, p.astype(BF16), preferred_element_type=F32)

    lead = jnp.logical_and(qi >= 2, qi % 2 == 0).astype(jnp.int32)

    @pl.when(lead == 1)
    def _():
        score(0, False, 0)
        accumulate(0, 0)

    @pl.when(qi == 0)
    def _():
        score(0, True, 0)

    @pl.when(qi >= 1)
    def _():
        score(lead, False, 0)

    def pair(kb):
        score(kb + 1, False, 1)
        accumulate(kb, 0)
        score(kb + 2, False, 0)
        accumulate(kb + 1, 1)

    def quad_body(j, carry):
        pair(lead + 4 * j)
        pair(lead + 4 * j + 2)
        return carry

    def pair_body(j, carry):
        pair(lead + 2 * j)
        return carry

    n_pairs = jnp.maximum((qi - lead - 1) // 2, 0)
    lax.fori_loop(0, n_pairs // 2, quad_body, 0)
    lax.fori_loop(n_pairs - n_pairs % 2, n_pairs, pair_body, 0)

    @pl.when(qi >= 1)
    def _():
        score(qi, True, 1)
        accumulate(qi - 1, 0)
        accumulate(qi, 1)

    @pl.when(qi == 0)
    def _():
        accumulate(0, 0)

    o0 = acc_scr[0, :v_rows] / acc_scr[0, v_rows:v_rows + 1]
    o1 = acc_scr[1, :v_rows] / acc_scr[1, v_rows:v_rows + 1]
    if mode == "diff":
        lp = lam_ref[...]
        lam = (jnp.exp(jnp.sum(lp[0:1] * lp[1:2], axis=-1, keepdims=True))
               - jnp.exp(jnp.sum(lp[2:3] * lp[3:4], axis=-1, keepdims=True)) + lam_init)
        o = (o0 - lam * o1).T
        o_ref[...] = (_rms(o, subg_ref[...]) * (1.0 - lam_init)).astype(o_ref.dtype)
    else:
        o_ref[...] = jnp.concatenate([o0, o1], axis=0).T.astype(o_ref.dtype)


def flash_prompt(q, k, vt, *, mode, t, lam_p=None, sub_g=None, lam_init=0.0):
    bt, t_all = q.shape[0], q.shape[1]
    n_blk = t_all // t
    qw = LANES if mode == "diff" else 2 * LANES
    groups = q.shape[2] // qw
    assert vt.shape == (bt * n_blk, groups * LANES, t), vt.shape
    q_idx = lambda b, h, i, *_: (b, i, h)
    in_specs = [pl.BlockSpec((None, t, qw), q_idx),
                pl.BlockSpec((None, t_all, qw), lambda b, h, i, *_: (b, 0, h)),
                pl.BlockSpec((n_blk, LANES, t), lambda b, h, i, *_: (b, h, 0))]
    args = [q, k, vt]
    if mode == "diff":
        slopes = LOG2E * jnp.exp2(-8.0 * jnp.arange(1, groups + 1, dtype=F32) / groups)
        in_specs += [pl.BlockSpec(lam_p.shape, lambda b, h, i, *_: (0, 0)),
                     pl.BlockSpec((1, LANES), lambda b, h, i, *_: (0, 0))]
        args += [lam_p.astype(F32), sub_g.reshape(1, LANES).astype(F32)]
        v_rows = LANES
    else:
        slopes = jnp.zeros((groups,), F32)
        v_rows = LANES // 2
    stat = pltpu.VMEM((2, 1, t), F32)
    tile = pltpu.VMEM((t, t), F32)
    grid_spec = pltpu.PrefetchScalarGridSpec(
        num_scalar_prefetch=1,
        grid=(bt, groups, n_blk),
        in_specs=in_specs,
        out_specs=pl.BlockSpec((None, t, LANES), q_idx),
        scratch_shapes=[stat, pltpu.VMEM((2, v_rows + ONES_ROWS, t), F32), tile, tile,
                        pltpu.VMEM((2, t, t), F32), pltpu.VMEM((2, t, t), F32), stat, stat, stat, stat])
    return pl.pallas_call(
        functools.partial(_flash_prompt_kernel, mode=mode, t=t, lam_init=lam_init),
        grid_spec=grid_spec,
        out_shape=jax.ShapeDtypeStruct((bt, t_all, groups * LANES), BF16),
        compiler_params=_params(3),
    )(slopes, *args)


def _c_pre_kernel(cq_ref, ckv_ref, kra_ref, krb_ref, gq_ref, gkv_ref, wq_ref, wqr_ref,
                  cq_tab, sq_tab, ck_tab, sk_tab, q_out, lat_out, kr_out):
    qn = _rms(cq_ref[...], gq_ref[...]).astype(BF16)
    qc = jnp.dot(qn, wq_ref[...], preferred_element_type=F32)
    qr = jnp.dot(qn, wqr_ref[...], preferred_element_type=F32)
    cos, sin = cq_tab[...], sq_tab[...]
    for h in range(C_HEADS):
        sl = slice(h * LANES, (h + 1) * LANES)
        q_out[:, sl] = (qc[:, sl] * cos + qr[:, sl] * sin).astype(q_out.dtype)
    lat_out[...] = _rms(ckv_ref[...], gkv_ref[...])
    kr = kra_ref[...] * ck_tab[...] + krb_ref[...] * sk_tab[...]
    kr_out[...] = kr


def c_pre(cq, ckv, kra, krb, gq, gkv, wq, wqr, tabs, t_len, tm=512):
    m = cq.shape[0]
    tm = _row_tile(t_len, tm)
    n_t = t_len // tm
    row = lambda i: (i, 0)
    fixed = lambda i: (0, 0)
    trow = lambda i: (i % n_t, 0)
    qw = wq.shape[1]
    return pl.pallas_call(
        _c_pre_kernel,
        grid=(m // tm,),
        in_specs=[pl.BlockSpec((tm, C_Q_LORA), row), pl.BlockSpec((tm, C_KV_LORA), row),
                  pl.BlockSpec((tm, LANES), row), pl.BlockSpec((tm, LANES), row),
                  pl.BlockSpec((1, C_Q_LORA), fixed), pl.BlockSpec((1, C_KV_LORA), fixed),
                  pl.BlockSpec(wq.shape, fixed), pl.BlockSpec(wqr.shape, fixed),
                  pl.BlockSpec((tm, LANES), trow), pl.BlockSpec((tm, LANES), trow),
                  pl.BlockSpec((tm, LANES), trow), pl.BlockSpec((tm, LANES), trow)],
        out_specs=[pl.BlockSpec((tm, qw), row), pl.BlockSpec((tm, C_KV_LORA), row),
                   pl.BlockSpec((tm, LANES), row)],
        out_shape=[jax.ShapeDtypeStruct((m, qw), BF16), jax.ShapeDtypeStruct((m, C_KV_LORA), F32),
                   jax.ShapeDtypeStruct((m, LANES), F32)],
        compiler_params=_params(1),
    )(cq, ckv, kra, krb, gq.reshape(1, -1).astype(F32), gkv.reshape(1, -1).astype(F32), wq, wqr, *tabs)


def _c_kv_kernel(lat_ref, kr_ref, wk_ref, wv_ref, place_ref, k_out, v_out, *, v_transposed):
    lat = lat_ref[...].astype(BF16)
    k = jnp.dot(lat, wk_ref[...], preferred_element_type=F32)
    k += jnp.dot(kr_ref[...].astype(BF16), place_ref[...], preferred_element_type=F32)
    k_out[...] = k.astype(k_out.dtype)
    if v_transposed:
        v = lax.dot_general(wv_ref[...], lat, NT, preferred_element_type=F32)
    else:
        v = jnp.dot(lat, wv_ref[...], preferred_element_type=F32)
    v_out[...] = v.astype(v_out.dtype)


def c_kv(lat, kr, wk, wv, place, v_transposed, tm=512):
    m = lat.shape[0]
    tm = _row_tile(m, tm)
    row = lambda i: (i, 0)
    fixed = lambda i: (0, 0)
    if v_transposed:
        n_v = wv.shape[0]
        v_spec = pl.BlockSpec((None, n_v, tm), lambda i: (i, 0, 0))
        v_shape = jax.ShapeDtypeStruct((m // tm, n_v, tm), BF16)
    else:
        n_v = wv.shape[1]
        v_spec = pl.BlockSpec((tm, n_v), row)
        v_shape = jax.ShapeDtypeStruct((m, n_v), BF16)
    return pl.pallas_call(
        functools.partial(_c_kv_kernel, v_transposed=v_transposed),
        grid=(m // tm,),
        in_specs=[pl.BlockSpec((tm, C_KV_LORA), row), pl.BlockSpec((tm, LANES), row),
                  pl.BlockSpec(wk.shape, fixed), pl.BlockSpec(wv.shape, fixed), pl.BlockSpec(place.shape, fixed)],
        out_specs=[pl.BlockSpec((tm, wk.shape[1]), row), v_spec],
        out_shape=[jax.ShapeDtypeStruct((m, wk.shape[1]), BF16), v_shape],
        compiler_params=_params(1),
    )(lat, kr, wk, wv, place)


def _prep_in_ab(w):
    n_mix = w.shape[1] - MEM_W - w.shape[0]
    n_q = n_mix // 3
    scale = jnp.concatenate([jnp.full((n_q,), Q_SCALE, F32), jnp.ones((n_mix - n_q,), F32),
                             jnp.full((MEM_W,), Q_SCALE, F32), jnp.ones((w.shape[0],), F32)])
    return (w * scale[None, :]).astype(BF16)


def _prep_in_c(w):
    d = w.shape[0]
    o = C_Q_LORA + C_KV_LORA
    half = C_ROPE // 2
    kr = w[:, o:o + C_ROPE]
    rot = jnp.concatenate([-kr[:, half:], kr[:, :half]], axis=1)
    pad = jnp.zeros((d, LANES - C_ROPE), F32)
    mq = w[:, o + C_ROPE:o + C_ROPE + MEM_W] * (Q_SCALE)
    gate = w[:, o + C_ROPE + MEM_W:]
    return jnp.concatenate([w[:, :o], kr, pad, rot, pad, mq, gate], axis=1).astype(BF16)


def _prep_uq(w_uq):
    r = w_uq.shape[0]
    w = w_uq.reshape(r, C_HEADS, C_NOPE + C_ROPE)
    nope, rope = w[..., :C_NOPE], w[..., C_NOPE:]
    half = C_ROPE // 2
    rot = jnp.concatenate([-rope[..., half:], rope[..., :half]], axis=-1)
    pad = jnp.zeros((r, C_HEADS, LANES - C_NOPE - C_ROPE), F32)
    plain = jnp.concatenate([nope, rope, pad], axis=-1).reshape(r, C_HEADS * LANES)
    rotated = jnp.concatenate([jnp.zeros_like(nope), rot, pad], axis=-1).reshape(r, C_HEADS * LANES)
    return plain.astype(BF16), rotated.astype(BF16)


def _prep_ukv(w_ukv):
    r = w_ukv.shape[0]
    w = w_ukv.reshape(r, C_HEADS, C_NOPE + C_V)
    wk = jnp.concatenate([w[..., :C_NOPE], jnp.zeros((r, C_HEADS, LANES - C_NOPE), F32)], axis=-1)
    wv = w[..., C_NOPE:]
    eye = jnp.eye(C_ROPE, dtype=F32)
    place = jnp.concatenate([jnp.zeros((C_ROPE, C_NOPE), F32), eye,
                             jnp.zeros((C_ROPE, LANES - C_NOPE - C_ROPE), F32)], axis=1)
    place = jnp.tile(place, (1, C_HEADS))
    place = jnp.concatenate([place, jnp.zeros((LANES - C_ROPE, place.shape[1]), F32)], axis=0)
    return (wk.reshape(r, C_HEADS * LANES).astype(BF16), wv.reshape(r, C_HEADS * C_V).astype(BF16),
            place.astype(BF16))


def _rope_tables(pos):
    half = C_ROPE // 2
    inv = jnp.exp(-math.log(ROPE_BASE) * jnp.arange(half, dtype=F32) * 2.0 / C_ROPE)
    ang = pos.astype(F32)[:, None] * inv[None, :]
    cos, sin = jnp.cos(ang), jnp.sin(ang)
    t = pos.shape[0]
    cos2 = jnp.concatenate([cos, cos], axis=1)
    sin2 = jnp.concatenate([sin, sin], axis=1)
    scale = LOG2E * (C_NOPE + C_ROPE) ** -0.5
    z = lambda n: jnp.zeros((t, n), F32)
    cq = jnp.concatenate([jnp.ones((t, C_NOPE), F32), cos2, z(LANES - C_NOPE - C_ROPE)], axis=1) * scale
    sq = jnp.concatenate([z(C_NOPE), sin2, z(LANES - C_NOPE - C_ROPE)], axis=1) * scale
    ck = jnp.concatenate([cos2, z(LANES - C_ROPE)], axis=1)
    sk = jnp.concatenate([sin2, z(LANES - C_ROPE)], axis=1)
    return cq, sq, ck, sk


def _band_bias_kernel(pos_ref, row_ref, o_ref, *, tq, win, n_valid, transposed):
    c = pl.program_id(0)
    q0, k0 = pos_ref[0, c], pos_ref[1, c]
    wp = row_ref.shape[-1]
    n_rows, n_cols = (win, tq) if transposed else (tq, win)
    rows = pltpu.roll(jnp.broadcast_to(row_ref[...], (n_rows, wp)), 0, 1, stride=1, stride_axis=0)
    ri = lax.broadcasted_iota(jnp.int32, (n_rows, n_cols), 0)
    ci = lax.broadcasted_iota(jnp.int32, (n_rows, n_cols), 1)
    qi, kj = (ci, ri) if transposed else (ri, ci)
    qp, kp = q0 + qi, k0 + kj
    qc, kc = qp // CHUNK, kp // CHUNK
    valid = (kp >= 0) & (kc <= qc) & (kc >= qc - A_BAND_CHUNKS) & (kj < n_valid)
    o_ref[...] = jnp.where(valid, rows[:, :n_cols], NEG)


def band_bias(rel_bias, q0, k0, *, tq, win, n_valid, transposed=False):
    n_cls = q0.shape[0]
    heads = rel_bias.shape[1]
    wp = -(-(tq + win) // LANES) * LANES
    mm = jnp.arange(wp, dtype=jnp.int32)
    mm = jnp.where(mm < (tq if transposed else win), mm, mm - wp)
    mm = mm if transposed else -mm
    rel = jnp.clip((q0 - k0)[:, None] + mm[None, :], -A_REL_CLIP, A_REL_CLIP) + A_REL_CLIP
    rows = jnp.moveaxis((LOG2E * rel_bias.astype(F32))[rel], -1, 1).reshape(n_cls, heads, 1, wp)
    pos = jnp.stack([q0, k0]).astype(jnp.int32)
    if transposed:
        out_spec = pl.BlockSpec((None, None, win, tq), lambda c, h, *_: (c, h // 2, 0, h % 2))
        out_shape = jax.ShapeDtypeStruct((n_cls, heads // 2, win, 2 * tq), F32)
    else:
        out_spec = pl.BlockSpec((None, None, tq, win), lambda c, h, *_: (c, h, 0, 0))
        out_shape = jax.ShapeDtypeStruct((n_cls, heads, tq, win), F32)
    grid_spec = pltpu.PrefetchScalarGridSpec(
        num_scalar_prefetch=1,
        grid=(n_cls, heads),
        in_specs=[pl.BlockSpec((None, None, 1, wp), lambda c, h, *_: (c, h, 0, 0))],
        out_specs=out_spec)
    return pl.pallas_call(
        functools.partial(_band_bias_kernel, tq=tq, win=win, n_valid=n_valid, transposed=transposed),
        grid_spec=grid_spec,
        out_shape=out_shape,
        compiler_params=_params(2),
    )(pos, rows)


A_TQ = 256
A_SUB = 16
FLASH_T = 512
ONES_ROWS = 16


def _trunk(x, pos0, mem_k, mem_vt, past, wts):
    bt, t, d = x.shape
    m = bt * t
    depth = len(wts["w_out"])
    x2 = x.reshape(m, d)
    new_a, new_b, new_c = [], [], []
    pos = pos0 + jnp.arange(t, dtype=jnp.int32)
    one = lambda width, dt: (width, (dt,))
    for i in range(depth):
        kind, j = i % N_MIXERS, i // N_MIXERS
        if kind == 0:
            hd = A_HEADS * HEAD_DIM
            w_in = wts["w_in_a"][j]
            if past is None:
                tq = min(A_TQ, t)
                q, k, k16, v, mq, gate, vt = norm_matmul(
                    x2, wts["norm_g"][i], w_in,
                    (one(hd, BF16), (hd, (F32, BF16)), one(hd, F32), one(MEM_W, BF16), one(d, BF16)),
                    t_seg=2, t_blk=tq)
                k3, v3 = k.reshape(bt, t, hd), v.reshape(bt, t, hd)
                back_blocks = A_PAST_ROWS // tq
                n_win = min(back_blocks + 1, t // tq)
                win = n_win * tq
                n_cls = n_win
                q0 = jnp.arange(n_cls, dtype=jnp.int32) * tq
                k0 = jnp.clip(q0 - A_PAST_ROWS, 0, t - win)
                bias = band_bias(wts["rel_bias_a"][j], q0, k0, tq=tq, win=win, n_valid=win, transposed=True)
                br = attn_window_t(q.reshape(bt, t, hd), k16.reshape(bt, t, hd), vt, bias,
                                   tq=tq, n_sub=min(A_SUB, t // tq), n_win=n_win, back_blocks=back_blocks)
                keep = min(A_PAST_ROWS, t)
                new_a.append((k3[:, t - keep:].reshape(bt, keep, A_HEADS, HEAD_DIM),
                              v3[:, t - keep:].reshape(bt, keep, A_HEADS, HEAD_DIM)))
            else:
                q, k, v, mq, gate = norm_matmul(
                    x2, wts["norm_g"][i], w_in,
                    (one(hd, BF16), one(hd, F32), one(hd, F32), one(MEM_W, BF16), one(d, BF16)))
                q3, k3, v3 = (a.reshape(bt, t, hd) for a in (q, k, v))
                ck, cv = past[0][j], past[1][j]
                rows = ck.shape[1]
                n_keys = rows + t
                pad = (-n_keys) % LANES
                zeros = jnp.zeros((bt, pad, hd), F32)
                kk = jnp.concatenate([ck.reshape(bt, rows, hd), k3, zeros], axis=1)
                vv = jnp.concatenate([cv.reshape(bt, rows, hd), v3, zeros], axis=1)
                start = jnp.full((1,), pos0, jnp.int32)
                bias = band_bias(wts["rel_bias_a"][j], start, start - rows, tq=t, win=n_keys + pad,
                                 n_valid=n_keys)
                br = attn_window(q3, kk, vv, bias, tq=t, win=n_keys + pad, back=0)
                new_a.append((k3.reshape(bt, t, A_HEADS, HEAD_DIM), v3.reshape(bt, t, A_HEADS, HEAD_DIM)))
        elif kind == 1:
            hd = B_HEADS * 2 * HEAD_DIM
            lam_init = 0.8 - 0.6 * math.exp(-0.3 * i)
            common = dict(mode="diff", lam_p=wts["lambda_b"][j], sub_g=wts["subln_g_b"][j], lam_init=lam_init)
            w_in = wts["w_in_b"][j]
            if past is None:
                q, k, k16, v, mq, gate, vt = norm_matmul(
                    x2, wts["norm_g"][i], w_in,
                    (one(hd, BF16), (hd, (F32, BF16)), one(hd, F32), one(MEM_W, BF16), one(d, BF16)),
                    wt=w_in[:, 2 * hd:3 * hd].T, t_blk=FLASH_T, tm=FLASH_T,
                    split={1: (B_HEADS, 2, HEAD_DIM), 3: (B_HEADS, 2 * HEAD_DIM)})
                br = flash_prompt(q.reshape(bt, t, hd), k16.reshape(bt, t, hd), vt, t=FLASH_T, **common)
            else:
                q, k, v, mq, gate = norm_matmul(
                    x2, wts["norm_g"][i], w_in,
                    (one(hd, BF16), one(hd, F32), one(hd, F32), one(MEM_W, BF16), one(d, BF16)))
                ck, cv = past[2][j], past[3][j]
                rows = ck.shape[1]
                br = flash_attention(q.reshape(bt, t, hd), ck.reshape(bt, rows, hd), cv.reshape(bt, rows, hd),
                                     k.reshape(bt, t, hd), v.reshape(bt, t, hd),
                                     tq=t, tk=rows, q_pos0=rows, **common)
            new_b.append((k.reshape(bt, t, B_HEADS, 2, HEAD_DIM), v.reshape(bt, t, B_HEADS, 2 * HEAD_DIM)))
        else:
            cq, ckv, kra, krb, mq, gate = norm_matmul(
                x2, wts["norm_g"][i], wts["w_in_c"][j],
                (one(C_Q_LORA, F32), one(C_KV_LORA, F32), one(LANES, F32), one(LANES, F32),
                 one(MEM_W, BF16), one(d, BF16)))
            wq, wqr = wts["w_uq_c"][j]
            wk, wv, place = wts["w_ukv_c"][j]
            qcat, lat, kr = c_pre(cq, ckv, kra, krb, wts["q_norm_g_c"][j], wts["kv_norm_g_c"][j],
                                  wq, wqr, _rope_tables(pos), t)
            q3 = qcat.reshape(bt, t, -1)
            if past is None:
                kcat, vt = c_kv(lat, kr, wk, wv.T, place, v_transposed=True, tm=FLASH_T)
                br = flash_prompt(q3, kcat.reshape(bt, t, -1), vt, mode="mla", t=FLASH_T)
            else:
                kcat, vcat = c_kv(lat, kr, wk, wv, place, v_transposed=False)
                cl, cr = past[4][j], past[5][j]
                rows = cl.shape[1]
                cr = jnp.pad(cr.reshape(bt * rows, -1), ((0, 0), (0, LANES - C_ROPE)))
                kc_, vc_ = c_kv(cl.reshape(bt * rows, -1), cr, wk, wv, place, v_transposed=False)
                br = flash_attention(q3, kc_.reshape(bt, rows, -1), vc_.reshape(bt, rows, -1),
                                     kcat.reshape(bt, t, -1), vcat.reshape(bt, t, -1),
                                     mode="mla", tq=t, tk=rows, q_pos0=rows)
            new_c.append((lat.reshape(bt, t, -1), kr[:, :C_ROPE].reshape(bt, t, -1)))
        last = i == depth - 1
        x2 = gated_out(br.reshape(m, -1), mq, gate, x2, wts["w_out"][i], wts["final_g"],
                       mem_k[i], mem_vt[i], final=last)
    return x2.reshape(bt, t, d), new_a, new_b, new_c


def kernel(x_prompt, x_sample, cache_a_k, cache_a_v, cache_b_k, cache_b_v, cache_c_lat, cache_c_rope,
           cache_mem_k, cache_mem_v, mem_prompt, norm_g, final_g, mem_norm_g, w_mem_kv, w_out, w_in_a,
           rel_bias_a, w_in_b, lambda_b, subln_g_b, w_in_c, q_norm_g_c, kv_norm_g_c, w_uq_c, w_ukv_c):
    depth = w_out.shape[0]
    bp, n_mem, d = mem_prompt.shape
    mem_heads = MEM_W // HEAD_DIM
    wts = dict(
        norm_g=norm_g, final_g=final_g, rel_bias_a=rel_bias_a, lambda_b=lambda_b, subln_g_b=subln_g_b,
        q_norm_g_c=q_norm_g_c, kv_norm_g_c=kv_norm_g_c,
        w_out=[w_out[i].astype(BF16) for i in range(depth)],
        w_in_a=[_prep_in_ab(w_in_a[j]) for j in range(w_in_a.shape[0])],
        w_in_b=[_prep_in_ab(w_in_b[j]) for j in range(w_in_b.shape[0])],
        w_in_c=[_prep_in_c(w_in_c[j]) for j in range(w_in_c.shape[0])],
        w_uq_c=[_prep_uq(w_uq_c[j]) for j in range(w_uq_c.shape[0])],
        w_ukv_c=[_prep_ukv(w_ukv_c[j]) for j in range(w_ukv_c.shape[0])],
    )
    mem2 = mem_prompt.reshape(bp * n_mem, d)
    mem_k_p, mem_v_p, mem_vt_p = [], [], []
    for i in range(depth):
        w_kv = w_mem_kv[i].astype(BF16)
        mk, mv, mvt = norm_matmul(mem2, mem_norm_g[i], w_kv, ((MEM_W, (F32,)), (MEM_W, (F32,))),
                                  wt=w_kv[:, MEM_W:].T, t_blk=n_mem, tm=n_mem)
        mem_k_p.append(mk.reshape(bp, n_mem, MEM_W))
        mem_v_p.append(mv.reshape(bp, n_mem, MEM_W))
        mem_vt_p.append(mvt)

    y_p, na_p, nb_p, nc_p = _trunk(x_prompt, 0, mem_k_p, mem_vt_p, None, wts)

    bs = x_sample.shape[0]
    mem_k_s = [cache_mem_k[i].reshape(bs, n_mem, MEM_W) for i in range(depth)]
    mem_vt_s = [jnp.swapaxes(cache_mem_v[i].reshape(bs, n_mem, MEM_W), 1, 2).astype(BF16) for i in range(depth)]
    past = (cache_a_k, cache_a_v, cache_b_k, cache_b_v, cache_c_lat, cache_c_rope)
    y_s, na_s, nb_s, nc_s = _trunk(x_sample, cache_b_k.shape[2], mem_k_s, mem_vt_s, past, wts)

    stk = lambda lst, n: jnp.stack([s[n] for s in lst])
    heads4 = lambda lst: jnp.stack([a.reshape(bp, n_mem, mem_heads, HEAD_DIM) for a in lst])
    return (y_p, y_s,
            stk(na_p, 0), stk(na_p, 1), stk(na_s, 0), stk(na_s, 1),
            stk(nb_p, 0), stk(nb_p, 1), stk(nb_s, 0), stk(nb_s, 1),
            stk(nc_p, 0), stk(nc_p, 1), stk(nc_s, 0), stk(nc_s, 1),
            heads4(mem_k_p), heads4(mem_v_p))
```

```python
import functools
import itertools
import math

import jax
import jax.numpy as jnp
from jax import lax
from jax.experimental import pallas as pl
from jax.experimental.pallas import tpu as pltpu

F32 = jnp.float32
BF16 = jnp.bfloat16

LANES = 128
VMEM_LIMIT = 56 * 1024 * 1024

EPS = 1e-6
NEG = -1e30
LOG2E = math.log2(math.e)
CHUNK = 64
HEAD_DIM = 64
Q_SCALE = LOG2E * HEAD_DIM ** -0.5
MEM_W = 256
A_HEADS = 12
A_PAST_ROWS = 512
A_BAND_CHUNKS = 8
A_REL_CLIP = 128
B_HEADS = 6
C_HEADS = 12
C_NOPE = 64
C_ROPE = 32
C_V = 64
C_Q_LORA = 384
C_KV_LORA = 256
ROPE_BASE = 10000.0
N_MIXERS = 3


def _params(n_grid):
    return pltpu.CompilerParams(dimension_semantics=("arbitrary",) * n_grid,
                                vmem_limit_bytes=VMEM_LIMIT)


def _row_tile(m, want):
    t = min(m, want)
    assert m % t == 0, (m, t)
    return t


def _rms(x, g):
    return x * lax.rsqrt(jnp.mean(x * x, axis=-1, keepdims=True) + EPS) * g


NT = (((1,), (1,)), ((), ()))


def _store_blocks(o_ref, rt):
    blk = o_ref.shape[-1]
    for jb in range(o_ref.shape[0]):
        o_ref[jb] = rt[:, jb * blk:(jb + 1) * blk]


def _norm_matmul_kernel(x_ref, g_ref, w_ref, *rest, segs, has_t, t_seg):
    h = _rms(x_ref[...], g_ref[...]).astype(BF16)
    out_refs = rest[1:] if has_t else rest
    off = n_out = 0
    for si, (width, dtypes) in enumerate(segs):
        r = jnp.dot(h, w_ref[:, off:off + width], preferred_element_type=F32)
        if si == t_seg:
            _store_blocks(out_refs[-1], r.T.astype(BF16))
        for dt in dtypes:
            o_ref = out_refs[n_out]
            if len(o_ref.shape) == 2:
                o_ref[...] = r.astype(dt)
            else:
                w = o_ref.shape[-1]
                for c, ix in enumerate(itertools.product(*[range(n) for n in o_ref.shape[1:-1]])):
                    o_ref[(slice(None),) + ix + (slice(None),)] = r[:, c * w:(c + 1) * w].astype(dt)
            n_out += 1
        off += width
    if has_t:
        _store_blocks(out_refs[n_out], lax.dot_general(rest[0][...], h, NT, preferred_element_type=F32).astype(BF16))


def norm_matmul(x, g, w, segs, wt=None, t_seg=None, t_blk=None, tm=512, split=None):
    m, d = x.shape
    tm = _row_tile(m, tm)
    n = w.shape[1]
    assert n == sum(wd for wd, _ in segs)
    row = lambda i: (i, 0)
    fixed = lambda i: (0, 0)
    in_specs = [pl.BlockSpec((tm, d), row), pl.BlockSpec((1, d), fixed), pl.BlockSpec((d, n), fixed)]
    args = [x, g.reshape(1, d).astype(F32), w]
    out_specs = [pl.BlockSpec((tm, wd), row) for wd, dts in segs for _ in dts]
    out_shape = [jax.ShapeDtypeStruct((m, wd), dt) for wd, dts in segs for dt in dts]
    for pos, tail in (split or {}).items():
        assert math.prod(tail) == out_shape[pos].shape[1]
        out_specs[pos] = pl.BlockSpec((tm,) + tail, lambda i, n=len(tail): (i,) + (0,) * n)
        out_shape[pos] = jax.ShapeDtypeStruct((m,) + tail, out_shape[pos].dtype)
    assert wt is None or t_seg is None
    if wt is not None:
        in_specs.append(pl.BlockSpec(wt.shape, fixed))
        args.append(wt)
    if wt is not None or t_seg is not None:
        n_t = wt.shape[0] if wt is not None else segs[t_seg][0]
        assert tm % t_blk == 0
        out_specs.append(pl.BlockSpec((tm // t_blk, n_t, t_blk), lambda i: (i, 0, 0)))
        out_shape.append(jax.ShapeDtypeStruct((m // t_blk, n_t, t_blk), BF16))
    return pl.pallas_call(
        functools.partial(_norm_matmul_kernel, segs=tuple(segs), has_t=wt is not None, t_seg=t_seg),
        grid=(m // tm,),
        in_specs=in_specs,
        out_specs=out_specs,
        out_shape=out_shape,
        compiler_params=_params(1),
    )(*args)


def _gated_out_kernel(br_ref, mq_ref, gate_ref, x_ref, w_ref, fg_ref, mk_ref, mvt_ref, o_ref, *, final):
    nb = br_ref.shape[-1]
    tm = x_ref.shape[0]
    halves = _half_masks()
    mq = mq_ref[...]
    mem = []
    for pair in range(mq.shape[1] // LANES):
        cols = slice(pair * LANES, (pair + 1) * LANES)
        q = mq[:, cols]
        q2 = jnp.concatenate([jnp.where(halves[h], q, jnp.zeros_like(q)) for h in range(2)], axis=0)
        st = lax.dot_general(mk_ref[:, cols].astype(BF16), q2, NT, preferred_element_type=F32)
        p = jnp.exp2(st - jnp.max(st, axis=0, keepdims=True))
        l = jnp.sum(p, axis=0, keepdims=True)
        o = jnp.dot(mvt_ref[cols, :], p.astype(BF16), preferred_element_type=F32) / l
        mem.append(jnp.concatenate([o[:HEAD_DIM, :tm], o[HEAD_DIM:, tm:]], axis=0).T)
    mo = jnp.concatenate(mem, axis=1)
    gate = gate_ref[...].astype(F32)
    sg = gate * jax.nn.sigmoid(gate)
    y1 = (br_ref[...].astype(F32) * sg[:, :nb]).astype(BF16)
    y2 = (mo * sg[:, nb:]).astype(BF16)
    acc = jnp.dot(y1, w_ref[:nb, :], preferred_element_type=F32)
    acc += jnp.dot(y2, w_ref[nb:, :], preferred_element_type=F32)
    xn = x_ref[...] + acc
    o_ref[...] = _rms(xn, fg_ref[...]) if final else xn


def gated_out(br, mq, gate, x, w, final_g, mem_k, mem_vt, final, tm=1024):
    m, d = x.shape
    t_len = m // mem_k.shape[0]
    tm = _row_tile(t_len, tm)
    n_t = t_len // tm
    nb, nm, ng = br.shape[1], mq.shape[1], gate.shape[1]
    row = lambda i: (i, 0)
    fixed = lambda i: (0, 0)
    per_batch = lambda i: (i // n_t, 0, 0)
    return pl.pallas_call(
        functools.partial(_gated_out_kernel, final=final),
        grid=(m // tm,),
        in_specs=[pl.BlockSpec((tm, nb), row), pl.BlockSpec((tm, nm), row), pl.BlockSpec((tm, ng), row),
                  pl.BlockSpec((tm, d), row), pl.BlockSpec((ng, d), fixed), pl.BlockSpec((1, d), fixed),
                  pl.BlockSpec((None,) + mem_k.shape[1:], per_batch),
                  pl.BlockSpec((None,) + mem_vt.shape[1:], per_batch)],
        out_specs=pl.BlockSpec((tm, d), row),
        out_shape=jax.ShapeDtypeStruct((m, d), F32),
        compiler_params=_params(1),
    )(br, mq, gate, x, w, final_g.reshape(1, d).astype(F32), mem_k, mem_vt)


def _half_masks():
    lane = lax.broadcasted_iota(jnp.int32, (1, LANES), 1)
    lo = lane < HEAD_DIM
    return lo, jnp.logical_not(lo)


def _attn_window_kernel(q_ref, k_ref, v_ref, *rest, tq, win, back):
    if len(rest) == 2:
        bias_ref, o_ref = rest
    else:
        bias_ref, (o_ref,) = None, rest
    tk_all = k_ref.shape[0]
    if win == tk_all:
        k = k_ref[...]
        v = v_ref[...]
    else:
        start = jnp.clip(pl.program_id(2) * tq - back, 0, tk_all - win)
        start = pl.multiple_of(start, CHUNK)
        k = k_ref[pl.ds(start, win), :]
        v = v_ref[pl.ds(start, win), :]
    k = k.astype(BF16)
    v = v.astype(BF16)
    q = q_ref[...]
    halves = _half_masks()
    outs = []
    for i in range(2):
        qm = jnp.where(halves[i], q, jnp.zeros_like(q))
        s = lax.dot_general(qm, k, (((1,), (1,)), ((), ())), preferred_element_type=F32)
        if bias_ref is not None:
            s = s + bias_ref[i]
        m = jnp.max(s, axis=-1, keepdims=True)
        p = jnp.exp2(s - m)
        l = jnp.sum(p, axis=-1, keepdims=True)
        o = jnp.dot(p.astype(BF16), v, preferred_element_type=F32)
        outs.append(o / l)
    o_ref[...] = jnp.where(halves[0], outs[0], outs[1]).astype(o_ref.dtype)


def attn_window(q, k, v, bias, *, tq, win, back):
    bt, t_q, gw = q.shape
    g = gw // LANES
    t_k = k.shape[1]
    tq = _row_tile(t_q, tq)
    in_specs = [pl.BlockSpec((None, tq, LANES), lambda b, h, i: (b, i, h)),
                pl.BlockSpec((None, t_k, LANES), lambda b, h, i: (b, 0, h)),
                pl.BlockSpec((None, t_k, LANES), lambda b, h, i: (b, 0, h))]
    args = [q, k, v]
    if bias is not None:
        last = bias.shape[0] - 1
        in_specs.append(pl.BlockSpec((None, 2, tq, win), lambda b, h, i: (jnp.minimum(i, last), h, 0, 0)))
        args.append(bias)
    return pl.pallas_call(
        functools.partial(_attn_window_kernel, tq=tq, win=win, back=back),
        grid=(bt, g, t_q // tq),
        in_specs=in_specs,
        out_specs=pl.BlockSpec((None, tq, LANES), lambda b, h, i: (b, i, h)),
        out_shape=jax.ShapeDtypeStruct((bt, t_q, gw), BF16),
        compiler_params=_params(3),
    )(*args)


def _attn_window_t_kernel(q_ref, k_ref, vt_ref, *rest, tq, n_sub, n_win, back_blocks):
    if len(rest) == 2:
        bias_ref, o_ref = rest
    else:
        bias_ref, (o_ref,) = None, rest
    n_blk, _, blk = vt_ref.shape
    halves = _half_masks()

    def score(sub):
        i = pl.program_id(2) * n_sub + sub
        if n_win == n_blk:
            b0 = 0
            k = k_ref[...]
        else:
            b0 = jnp.clip(i - back_blocks, 0, n_blk - n_win)
            k = k_ref[pl.ds(pl.multiple_of(b0 * blk, blk), n_win * blk), :]
        q = q_ref[sub * tq:(sub + 1) * tq, :]
        q2 = jnp.concatenate([jnp.where(halves[h], q, jnp.zeros_like(q)) for h in range(2)], axis=0)
        st = lax.dot_general(k.astype(BF16), q2, NT, preferred_element_type=F32)
        if bias_ref is not None:
            st = st + bias_ref[jnp.minimum(i, bias_ref.shape[0] - 1)]
        return b0, st, jnp.max(st, axis=0, keepdims=True)

    def accumulate(sub, b0, st, m):
        p = jnp.exp2(st - m)
        l = jnp.sum(p, axis=0, keepdims=True)
        pb = p.astype(BF16)
        o = None
        for j in range(n_win):
            oj = jnp.dot(vt_ref[b0 + j], pb[j * blk:(j + 1) * blk, :], preferred_element_type=F32)
            o = oj if o is None else o + oj
        o = o / l
        o = jnp.concatenate([o[:HEAD_DIM, :tq], o[HEAD_DIM:, tq:]], axis=0).T
        o_ref[sub * tq:(sub + 1) * tq, :] = o.astype(o_ref.dtype)

    staged = [score(sub) for sub in range(n_sub)]
    for sub in range(n_sub):
        accumulate(sub, *staged[sub])


def attn_window_t(q, k, vt, bias, *, tq, n_sub, n_win, back_blocks):
    bt, t_q, gw = q.shape
    g = gw // LANES
    t_k = k.shape[1]
    blk = vt.shape[2]
    n_blk = t_k // blk
    assert vt.shape == (bt * n_blk, gw, blk), vt.shape
    rows = n_sub * tq
    assert t_q % rows == 0
    in_specs = [pl.BlockSpec((None, rows, LANES), lambda b, h, i: (b, i, h)),
                pl.BlockSpec((None, t_k, LANES), lambda b, h, i: (b, 0, h)),
                pl.BlockSpec((n_blk, LANES, blk), lambda b, h, i: (b, h, 0))]
    args = [q, k, vt]
    if bias is not None:
        in_specs.append(pl.BlockSpec((bias.shape[0], None, n_win * blk, 2 * tq), lambda b, h, i: (0, h, 0, 0)))
        args.append(bias)
    return pl.pallas_call(
        functools.partial(_attn_window_t_kernel, tq=tq, n_sub=n_sub, n_win=n_win, back_blocks=back_blocks),
        grid=(bt, g, t_q // rows),
        in_specs=in_specs,
        out_specs=pl.BlockSpec((None, rows, LANES), lambda b, h, i: (b, i, h)),
        out_shape=jax.ShapeDtypeStruct((bt, t_q, gw), BF16),
        compiler_params=_params(3),
    )(*args)


def _flash_kernel(slope_ref, q_ref, km_ref, vm_ref, kt_ref, vt_ref, *rest,
                  mode, tq, tk, tt, n_main, q_pos0, lam_init):
    if mode == "diff":
        lam_ref, subg_ref, o_ref, m_scr, l_scr, acc_scr = rest
    else:
        o_ref, m_scr, l_scr, acc_scr = rest
    head = pl.program_id(1)
    qi = pl.program_id(2)
    q = q_ref[...]
    halves = _half_masks()
    if mode == "diff":
        qs = [jnp.where(halves[i], q, jnp.zeros_like(q)) for i in range(2)]
        slope = slope_ref[head]
    else:
        qs = [q[:, :LANES], q[:, LANES:]]
        slope = None

    m_scr[...] = jnp.full(m_scr.shape, NEG, F32)
    l_scr[...] = jnp.zeros(l_scr.shape, F32)
    acc_scr[...] = jnp.zeros(acc_scr.shape, F32)

    def update(i, s, shift, v):
        m_old = m_scr[i]
        m_new = jnp.maximum(m_old, jnp.max(s, axis=-1, keepdims=True) + shift)
        alpha = jnp.exp2(m_old - m_new)
        p = jnp.exp2(s - (m_new - shift))
        l_scr[i] = alpha * l_scr[i] + jnp.sum(p, axis=-1, keepdims=True)
        acc_scr[i] = alpha * acc_scr[i] + jnp.dot(p.astype(BF16), v, preferred_element_type=F32)
        m_scr[i] = m_new

    def k_of(kblk, i):
        return kblk if mode == "diff" else kblk[:, i * LANES:(i + 1) * LANES]

    nt = (((1,), (1,)), ((), ()))
    q_base = q_pos0 + qi * tq

    if mode == "diff":
        ri = lax.broadcasted_iota(jnp.int32, (tq, tk), 0)
        ci = lax.broadcasted_iota(jnp.int32, (tq, tk), 1)
        main_tile = slope * (ci - ri).astype(F32)

    def main_body(kb, carry):
        off = pl.multiple_of(kb * tk, tk)
        kblk = km_ref[pl.ds(off, tk), :].astype(BF16)
        vblk = vm_ref[pl.ds(off, tk), :].astype(BF16)
        for i in range(2):
            s = lax.dot_general(qs[i], k_of(kblk, i), nt, preferred_element_type=F32)
            if mode == "diff":
                shift = -slope * (q_base - kb * tk).astype(F32)
                update(i, s + main_tile, shift, vblk)
            else:
                update(i, s, 0.0, vblk)
        return carry

    lax.fori_loop(0, n_main, main_body, 0)

    ri = lax.broadcasted_iota(jnp.int32, (tq, tt), 0)
    ci = lax.broadcasted_iota(jnp.int32, (tq, tt), 1)
    valid = (ci // CHUNK) <= (ri // CHUNK)
    if mode == "diff":
        tail_tile = jnp.where(valid, -slope * jnp.abs(ri - ci).astype(F32), NEG)
    else:
        tail_tile = jnp.where(valid, 0.0, NEG).astype(F32)
    kblk = kt_ref[...].astype(BF16)
    vblk = vt_ref[...].astype(BF16)
    for i in range(2):
        s = lax.dot_general(qs[i], k_of(kblk, i), nt, preferred_element_type=F32)
        update(i, s + tail_tile, 0.0, vblk)

    o0 = acc_scr[0] / l_scr[0]
    o1 = acc_scr[1] / l_scr[1]
    if mode == "diff":
        lp = lam_ref[...]
        lam = (jnp.exp(jnp.sum(lp[0:1] * lp[1:2], axis=-1, keepdims=True))
               - jnp.exp(jnp.sum(lp[2:3] * lp[3:4], axis=-1, keepdims=True)) + lam_init)
        o = o0 - lam * o1
        o_ref[...] = (_rms(o, subg_ref[...]) * (1.0 - lam_init)).astype(o_ref.dtype)
    else:
        o_ref[...] = jnp.where(halves[0], o0, o1).astype(o_ref.dtype)


def flash_attention(q, k_main, v_main, k_tail, v_tail, *, mode, tq, tk, q_pos0,
                    lam_p=None, sub_g=None, lam_init=0.0):
    bt, t_q = q.shape[0], q.shape[1]
    tq = _row_tile(t_q, tq)
    qw = LANES if mode == "diff" else 2 * LANES
    groups = q.shape[2] // qw
    t_main = k_main.shape[1]
    tk = _row_tile(t_main, tk)
    tt, n_main = k_tail.shape[1], t_main // tk
    tail_idx = lambda b, h, i, *_: (b, 0, h)
    q_idx = lambda b, h, i, *_: (b, i, h)
    main_idx = lambda b, h, i, *_: (b, 0, h)
    in_specs = [pl.BlockSpec((None, tq, qw), q_idx),
                pl.BlockSpec((None, t_main, qw), main_idx),
                pl.BlockSpec((None, t_main, LANES), main_idx),
                pl.BlockSpec((None, tt, qw), tail_idx),
                pl.BlockSpec((None, tt, LANES), tail_idx)]
    args = [q, k_main, v_main, k_tail, v_tail]
    if mode == "diff":
        slopes = LOG2E * jnp.exp2(-8.0 * jnp.arange(1, groups + 1, dtype=F32) / groups)
        in_specs += [pl.BlockSpec(lam_p.shape, lambda b, h, i, *_: (0, 0)),
                     pl.BlockSpec((1, LANES), lambda b, h, i, *_: (0, 0))]
        args += [lam_p.astype(F32), sub_g.reshape(1, LANES).astype(F32)]
    else:
        slopes = jnp.zeros((groups,), F32)
    grid_spec = pltpu.PrefetchScalarGridSpec(
        num_scalar_prefetch=1,
        grid=(bt, groups, t_q // tq),
        in_specs=in_specs,
        out_specs=pl.BlockSpec((None, tq, LANES), q_idx),
        scratch_shapes=[pltpu.VMEM((2, tq, 1), F32), pltpu.VMEM((2, tq, 1), F32),
                        pltpu.VMEM((2, tq, LANES), F32)])
    return pl.pallas_call(
        functools.partial(_flash_kernel, mode=mode, tq=tq, tk=tk, tt=tt, n_main=n_main,
                          q_pos0=q_pos0, lam_init=lam_init),
        grid_spec=grid_spec,
        out_shape=jax.ShapeDtypeStruct((bt, t_q, groups * LANES), BF16),
        compiler_params=_params(3),
    )(slopes, *args)


def _flash_prompt_kernel(slope_ref, q_ref, k_ref, vt_ref, *rest, mode, t, lam_init):
    if mode == "diff":
        lam_ref, subg_ref, o_ref = rest[:3]
        rest = rest[3:]
    else:
        o_ref = rest[0]
        rest = rest[1:]
    m_scr, acc_scr, main_tile_scr, diag_tile_scr = rest[:4]
    st_scrs, msub_scrs, alpha_scrs = rest[4:6], rest[6:8], rest[8:10]
    head = pl.program_id(1)
    qi = pl.program_id(2)
    q = q_ref[...]
    halves = _half_masks()
    if mode == "diff":
        qs = [jnp.where(halves[i], q, jnp.zeros_like(q)) for i in range(2)]
        slope = slope_ref[head]
    else:
        qs = [q[:, :LANES], q[:, LANES:]]
    v_rows = acc_scr.shape[1] - ONES_ROWS
    ones = jnp.ones((ONES_ROWS, t), BF16)

    m_scr[...] = jnp.full(m_scr.shape, NEG, F32)
    acc_scr[...] = jnp.zeros(acc_scr.shape, F32)

    @pl.when(qi == 0)
    def _():
        kj = lax.broadcasted_iota(jnp.int32, (t, t), 0)
        qj = lax.broadcasted_iota(jnp.int32, (t, t), 1)
        valid = (kj // CHUNK) <= (qj // CHUNK)
        if mode == "diff":
            main_tile_scr[...] = slope * kj[:, :LANES].astype(F32)
            diag_tile_scr[...] = jnp.where(valid, slope * (qj - jnp.abs(qj - kj)).astype(F32), NEG)
        else:
            diag_tile_scr[...] = jnp.where(valid, 0.0, NEG).astype(F32)


    def score(kb, diag, slot):
        off = pl.multiple_of(kb * t, t)
        kblk = k_ref[pl.ds(off, t), :]
        if mode == "diff" and not diag:
            shift = -slope * ((qi - kb) * t).astype(F32)
        else:
            shift = 0.0
        for i in range(2):
            ki = kblk if mode == "diff" else kblk[:, i * LANES:(i + 1) * LANES]
            st = lax.dot_general(ki, qs[i], NT, preferred_element_type=F32)
            if diag:
                st = st + diag_tile_scr[...]
            elif mode == "diff":
                st = st + jnp.concatenate([main_tile_scr[...]] * (t // LANES), axis=1)
            st_scrs[slot][i] = st
            m_old = m_scr[i]
            m_new = jnp.maximum(m_old, jnp.max(st, axis=0, keepdims=True) + shift)
            m_scr[i] = m_new
            msub_scrs[slot][i] = m_new - shift
            alpha_scrs[slot][i] = jnp.exp2(m_old - m_new)

    def accumulate(kb, slot):
        vt = vt_ref[kb]
        for i in range(2):
            alpha = alpha_scrs[slot][i]
            vi = vt if mode == "diff" else vt[i * v_rows:(i + 1) * v_rows, :]
            vi = jnp.concatenate([vi, ones], axis=0)
            p = jnp.exp2(st_scrs[slot][i] - msub_scrs[slot][i])
            acc_scr[i] = alpha * acc_scr[i] + jnp.dot(vi, p.astype(BF16), preferred_element_type=F32)

    lead = jnp.logical_and(qi >= 2, qi % 2 == 0).astype(jnp.int32)

    @pl.when(lead == 1)
    def _():
        score(0, False, 0)
        accumulate(0, 0)

    @pl.when(qi == 0)
    def _():
        score(0, True, 0)

    @pl.when(qi >= 1)
    def _():
        score(lead, False, 0)

    def pair(kb):
        score(kb + 1, False, 1)
        accumulate(kb, 0)
        score(kb + 2, False, 0)
        accumulate(kb + 1, 1)

    def quad_body(j, carry):
        pair(lead + 4 * j)
        pair(lead + 4 * j + 2)
        return carry

    def pair_body(j, carry):
        pair(lead + 2 * j)
        return carry

    n_pairs = jnp.maximum((qi - lead - 1) // 2, 0)
    lax.fori_loop(0, n_pairs // 2, quad_body, 0)
    lax.fori_loop(n_pairs - n_pairs % 2, n_pairs, pair_body, 0)

    @pl.when(qi >= 1)
    def _():
        score(qi, True, 1)
        accumulate(qi - 1, 0)
        accumulate(qi, 1)

    @pl.when(qi == 0)
    def _():
        accumulate(0, 0)

    o0 = acc_scr[0, :v_rows] / acc_scr[0, v_rows:v_rows + 1]
    o1 = acc_scr[1, :v_rows] / acc_scr[1, v_rows:v_rows + 1]
    if mode == "diff":
        lp = lam_ref[...]
        lam = (jnp.exp(jnp.sum(lp[0:1] * lp[1:2], axis=-1, keepdims=True))
               - jnp.exp(jnp.sum(lp[2:3] * lp[3:4], axis=-1, keepdims=True)) + lam_init)
        o = (o0 - lam * o1).T
        o_ref[...] = (_rms(o, subg_ref[...]) * (1.0 - lam_init)).astype(o_ref.dtype)
    else:
        o_ref[...] = jnp.concatenate([o0, o1], axis=0).T.astype(o_ref.dtype)


def flash_prompt(q, k, vt, *, mode, t, lam_p=None, sub_g=None, lam_init=0.0):
    bt, t_all = q.shape[0], q.shape[1]
    n_blk = t_all // t
    qw = LANES if mode == "diff" else 2 * LANES
    groups = q.shape[2] // qw
    assert vt.shape == (bt * n_blk, groups * LANES, t), vt.shape
    q_idx = lambda b, h, i, *_: (b, i, h)
    in_specs = [pl.BlockSpec((None, t, qw), q_idx),
                pl.BlockSpec((None, t_all, qw), lambda b, h, i, *_: (b, 0, h)),
                pl.BlockSpec((n_blk, LANES, t), lambda b, h, i, *_: (b, h, 0))]
    args = [q, k, vt]
    if mode == "diff":
        slopes = LOG2E * jnp.exp2(-8.0 * jnp.arange(1, groups + 1, dtype=F32) / groups)
        in_specs += [pl.BlockSpec(lam_p.shape, lambda b, h, i, *_: (0, 0)),
                     pl.BlockSpec((1, LANES), lambda b, h, i, *_: (0, 0))]
        args += [lam_p.astype(F32), sub_g.reshape(1, LANES).astype(F32)]
        v_rows = LANES
    else:
        slopes = jnp.zeros((groups,), F32)
        v_rows = LANES // 2
    stat = pltpu.VMEM((2, 1, t), F32)
    tile = pltpu.VMEM((t, t), F32)
    col_tile = pltpu.VMEM((t, LANES), F32)
    grid_spec = pltpu.PrefetchScalarGridSpec(
        num_scalar_prefetch=1,
        grid=(bt, groups, n_blk),
        in_specs=in_specs,
        out_specs=pl.BlockSpec((None, t, LANES), q_idx),
        scratch_shapes=[stat, pltpu.VMEM((2, v_rows + ONES_ROWS, t), F32), col_tile, tile,
                        pltpu.VMEM((2, t, t), F32), pltpu.VMEM((2, t, t), F32), stat, stat, stat, stat])
    return pl.pallas_call(
        functools.partial(_flash_prompt_kernel, mode=mode, t=t, lam_init=lam_init),
        grid_spec=grid_spec,
        out_shape=jax.ShapeDtypeStruct((bt, t_all, groups * LANES), BF16),
        compiler_params=_params(3),
    )(slopes, *args)


def _c_pre_kernel(cq_ref, ckv_ref, kra_ref, krb_ref, gq_ref, gkv_ref, wq_ref, wqr_ref,
                  cq_tab, sq_tab, ck_tab, sk_tab, q_out, lat_out, kr_out):
    qn = _rms(cq_ref[...], gq_ref[...]).astype(BF16)
    qc = jnp.dot(qn, wq_ref[...], preferred_element_type=F32)
    qr = jnp.dot(qn, wqr_ref[...], preferred_element_type=F32)
    cos, sin = cq_tab[...], sq_tab[...]
    for h in range(C_HEADS):
        sl = slice(h * LANES, (h + 1) * LANES)
        q_out[:, sl] = (qc[:, sl] * cos + qr[:, sl] * sin).astype(q_out.dtype)
    lat_out[...] = _rms(ckv_ref[...], gkv_ref[...])
    kr = kra_ref[...] * ck_tab[...] + krb_ref[...] * sk_tab[...]
    kr_out[...] = kr


def c_pre(cq, ckv, kra, krb, gq, gkv, wq, wqr, tabs, t_len, tm=512):
    m = cq.shape[0]
    tm = _row_tile(t_len, tm)
    n_t = t_len // tm
    row = lambda i: (i, 0)
    fixed = lambda i: (0, 0)
    trow = lambda i: (i % n_t, 0)
    qw = wq.shape[1]
    return pl.pallas_call(
        _c_pre_kernel,
        grid=(m // tm,),
        in_specs=[pl.BlockSpec((tm, C_Q_LORA), row), pl.BlockSpec((tm, C_KV_LORA), row),
                  pl.BlockSpec((tm, LANES), row), pl.BlockSpec((tm, LANES), row),
                  pl.BlockSpec((1, C_Q_LORA), fixed), pl.BlockSpec((1, C_KV_LORA), fixed),
                  pl.BlockSpec(wq.shape, fixed), pl.BlockSpec(wqr.shape, fixed),
                  pl.BlockSpec((tm, LANES), trow), pl.BlockSpec((tm, LANES), trow),
                  pl.BlockSpec((tm, LANES), trow), pl.BlockSpec((tm, LANES), trow)],
        out_specs=[pl.BlockSpec((tm, qw), row), pl.BlockSpec((tm, C_KV_LORA), row),
                   pl.BlockSpec((tm, LANES), row)],
        out_shape=[jax.ShapeDtypeStruct((m, qw), BF16), jax.ShapeDtypeStruct((m, C_KV_LORA), F32),
                   jax.ShapeDtypeStruct((m, LANES), F32)],
        compiler_params=_params(1),
    )(cq, ckv, kra, krb, gq.reshape(1, -1).astype(F32), gkv.reshape(1, -1).astype(F32), wq, wqr, *tabs)


def _c_kv_kernel(lat_ref, kr_ref, wk_ref, wv_ref, place_ref, k_out, v_out, *, v_transposed):
    lat = lat_ref[...].astype(BF16)
    k = jnp.dot(lat, wk_ref[...], preferred_element_type=F32)
    k += jnp.dot(kr_ref[...].astype(BF16), place_ref[...], preferred_element_type=F32)
    k_out[...] = k.astype(k_out.dtype)
    if v_transposed:
        v = lax.dot_general(wv_ref[...], lat, NT, preferred_element_type=F32)
    else:
        v = jnp.dot(lat, wv_ref[...], preferred_element_type=F32)
    v_out[...] = v.astype(v_out.dtype)


def c_kv(lat, kr, wk, wv, place, v_transposed, tm=512):
    m = lat.shape[0]
    tm = _row_tile(m, tm)
    row = lambda i: (i, 0)
    fixed = lambda i: (0, 0)
    if v_transposed:
        n_v = wv.shape[0]
        v_spec = pl.BlockSpec((None, n_v, tm), lambda i: (i, 0, 0))
        v_shape = jax.ShapeDtypeStruct((m // tm, n_v, tm), BF16)
    else:
        n_v = wv.shape[1]
        v_spec = pl.BlockSpec((tm, n_v), row)
        v_shape = jax.ShapeDtypeStruct((m, n_v), BF16)
    return pl.pallas_call(
        functools.partial(_c_kv_kernel, v_transposed=v_transposed),
        grid=(m // tm,),
        in_specs=[pl.BlockSpec((tm, C_KV_LORA), row), pl.BlockSpec((tm, LANES), row),
                  pl.BlockSpec(wk.shape, fixed), pl.BlockSpec(wv.shape, fixed), pl.BlockSpec(place.shape, fixed)],
        out_specs=[pl.BlockSpec((tm, wk.shape[1]), row), v_spec],
        out_shape=[jax.ShapeDtypeStruct((m, wk.shape[1]), BF16), v_shape],
        compiler_params=_params(1),
    )(lat, kr, wk, wv, place)


def _prep_in_ab(w):
    n_mix = w.shape[1] - MEM_W - w.shape[0]
    n_q = n_mix // 3
    scale = jnp.concatenate([jnp.full((n_q,), Q_SCALE, F32), jnp.ones((n_mix - n_q,), F32),
                             jnp.full((MEM_W,), Q_SCALE, F32), jnp.ones((w.shape[0],), F32)])
    return (w * scale[None, :]).astype(BF16)


def _prep_in_c(w):
    d = w.shape[0]
    o = C_Q_LORA + C_KV_LORA
    half = C_ROPE // 2
    kr = w[:, o:o + C_ROPE]
    rot = jnp.concatenate([-kr[:, half:], kr[:, :half]], axis=1)
    pad = jnp.zeros((d, LANES - C_ROPE), F32)
    mq = w[:, o + C_ROPE:o + C_ROPE + MEM_W] * (Q_SCALE)
    gate = w[:, o + C_ROPE + MEM_W:]
    return jnp.concatenate([w[:, :o], kr, pad, rot, pad, mq, gate], axis=1).astype(BF16)


def _prep_uq(w_uq):
    r = w_uq.shape[0]
    w = w_uq.reshape(r, C_HEADS, C_NOPE + C_ROPE)
    nope, rope = w[..., :C_NOPE], w[..., C_NOPE:]
    half = C_ROPE // 2
    rot = jnp.concatenate([-rope[..., half:], rope[..., :half]], axis=-1)
    pad = jnp.zeros((r, C_HEADS, LANES - C_NOPE - C_ROPE), F32)
    plain = jnp.concatenate([nope, rope, pad], axis=-1).reshape(r, C_HEADS * LANES)
    rotated = jnp.concatenate([jnp.zeros_like(nope), rot, pad], axis=-1).reshape(r, C_HEADS * LANES)
    return plain.astype(BF16), rotated.astype(BF16)


def _prep_ukv(w_ukv):
    r = w_ukv.shape[0]
    w = w_ukv.reshape(r, C_HEADS, C_NOPE + C_V)
    wk = jnp.concatenate([w[..., :C_NOPE], jnp.zeros((r, C_HEADS, LANES - C_NOPE), F32)], axis=-1)
    wv = w[..., C_NOPE:]
    eye = jnp.eye(C_ROPE, dtype=F32)
    place = jnp.concatenate([jnp.zeros((C_ROPE, C_NOPE), F32), eye,
                             jnp.zeros((C_ROPE, LANES - C_NOPE - C_ROPE), F32)], axis=1)
    place = jnp.tile(place, (1, C_HEADS))
    place = jnp.concatenate([place, jnp.zeros((LANES - C_ROPE, place.shape[1]), F32)], axis=0)
    return (wk.reshape(r, C_HEADS * LANES).astype(BF16), wv.reshape(r, C_HEADS * C_V).astype(BF16),
            place.astype(BF16))


def _rope_tables(pos):
    half = C_ROPE // 2
    inv = jnp.exp(-math.log(ROPE_BASE) * jnp.arange(half, dtype=F32) * 2.0 / C_ROPE)
    ang = pos.astype(F32)[:, None] * inv[None, :]
    cos, sin = jnp.cos(ang), jnp.sin(ang)
    t = pos.shape[0]
    cos2 = jnp.concatenate([cos, cos], axis=1)
    sin2 = jnp.concatenate([sin, sin], axis=1)
    scale = LOG2E * (C_NOPE + C_ROPE) ** -0.5
    z = lambda n: jnp.zeros((t, n), F32)
    cq = jnp.concatenate([jnp.ones((t, C_NOPE), F32), cos2, z(LANES - C_NOPE - C_ROPE)], axis=1) * scale
    sq = jnp.concatenate([z(C_NOPE), sin2, z(LANES - C_NOPE - C_ROPE)], axis=1) * scale
    ck = jnp.concatenate([cos2, z(LANES - C_ROPE)], axis=1)
    sk = jnp.concatenate([sin2, z(LANES - C_ROPE)], axis=1)
    return cq, sq, ck, sk


def _band_bias_kernel(pos_ref, row_ref, o_ref, *, tq, win, n_valid, transposed):
    c = pl.program_id(0)
    q0, k0 = pos_ref[0, c], pos_ref[1, c]
    wp = row_ref.shape[-1]
    n_rows, n_cols = (win, tq) if transposed else (tq, win)
    rows = pltpu.roll(jnp.broadcast_to(row_ref[...], (n_rows, wp)), 0, 1, stride=1, stride_axis=0)
    ri = lax.broadcasted_iota(jnp.int32, (n_rows, n_cols), 0)
    ci = lax.broadcasted_iota(jnp.int32, (n_rows, n_cols), 1)
    qi, kj = (ci, ri) if transposed else (ri, ci)
    qp, kp = q0 + qi, k0 + kj
    qc, kc = qp // CHUNK, kp // CHUNK
    valid = (kp >= 0) & (kc <= qc) & (kc >= qc - A_BAND_CHUNKS) & (kj < n_valid)
    o_ref[...] = jnp.where(valid, rows[:, :n_cols], NEG)


def band_bias(rel_bias, q0, k0, *, tq, win, n_valid, transposed=False):
    n_cls = q0.shape[0]
    heads = rel_bias.shape[1]
    wp = -(-(tq + win) // LANES) * LANES
    mm = jnp.arange(wp, dtype=jnp.int32)
    mm = jnp.where(mm < (tq if transposed else win), mm, mm - wp)
    mm = mm if transposed else -mm
    rel = jnp.clip((q0 - k0)[:, None] + mm[None, :], -A_REL_CLIP, A_REL_CLIP) + A_REL_CLIP
    rows = jnp.moveaxis((LOG2E * rel_bias.astype(F32))[rel], -1, 1).reshape(n_cls, heads, 1, wp)
    pos = jnp.stack([q0, k0]).astype(jnp.int32)
    if transposed:
        out_spec = pl.BlockSpec((None, None, win, tq), lambda c, h, *_: (c, h // 2, 0, h % 2))
        out_shape = jax.ShapeDtypeStruct((n_cls, heads // 2, win, 2 * tq), F32)
    else:
        out_spec = pl.BlockSpec((None, None, tq, win), lambda c, h, *_: (c, h, 0, 0))
        out_shape = jax.ShapeDtypeStruct((n_cls, heads, tq, win), F32)
    grid_spec = pltpu.PrefetchScalarGridSpec(
        num_scalar_prefetch=1,
        grid=(n_cls, heads),
        in_specs=[pl.BlockSpec((None, None, 1, wp), lambda c, h, *_: (c, h, 0, 0))],
        out_specs=out_spec)
    return pl.pallas_call(
        functools.partial(_band_bias_kernel, tq=tq, win=win, n_valid=n_valid, transposed=transposed),
        grid_spec=grid_spec,
        out_shape=out_shape,
        compiler_params=_params(2),
    )(pos, rows)


A_TQ = 256
A_SUB = 16
FLASH_T = 512
ONES_ROWS = 16


def _trunk(x, pos0, mem_k, mem_vt, past, wts):
    bt, t, d = x.shape
    m = bt * t
    depth = len(wts["w_out"])
    x2 = x.reshape(m, d)
    new_a, new_b, new_c = [], [], []
    pos = pos0 + jnp.arange(t, dtype=jnp.int32)
    one = lambda width, dt: (width, (dt,))
    for i in range(depth):
        kind, j = i % N_MIXERS, i // N_MIXERS
        if kind == 0:
            hd = A_HEADS * HEAD_DIM
            w_in = wts["w_in_a"][j]
            if past is None:
                tq = min(A_TQ, t)
                q, k, k16, v, mq, gate, vt = norm_matmul(
                    x2, wts["norm_g"][i], w_in,
                    (one(hd, BF16), (hd, (F32, BF16)), one(hd, F32), one(MEM_W, BF16), one(d, BF16)),
                    t_seg=2, t_blk=tq)
                k3, v3 = k.reshape(bt, t, hd), v.reshape(bt, t, hd)
                back_blocks = A_PAST_ROWS // tq
                n_win = min(back_blocks + 1, t // tq)
                win = n_win * tq
                n_cls = n_win
                q0 = jnp.arange(n_cls, dtype=jnp.int32) * tq
                k0 = jnp.clip(q0 - A_PAST_ROWS, 0, t - win)
                bias = band_bias(wts["rel_bias_a"][j], q0, k0, tq=tq, win=win, n_valid=win, transposed=True)
                br = attn_window_t(q.reshape(bt, t, hd), k16.reshape(bt, t, hd), vt, bias,
                                   tq=tq, n_sub=min(A_SUB, t // tq), n_win=n_win, back_blocks=back_blocks)
                keep = min(A_PAST_ROWS, t)
                new_a.append((k3[:, t - keep:].reshape(bt, keep, A_HEADS, HEAD_DIM),
                              v3[:, t - keep:].reshape(bt, keep, A_HEADS, HEAD_DIM)))
            else:
                q, k, v, mq, gate = norm_matmul(
                    x2, wts["norm_g"][i], w_in,
                    (one(hd, BF16), one(hd, F32), one(hd, F32), one(MEM_W, BF16), one(d, BF16)))
                q3, k3, v3 = (a.reshape(bt, t, hd) for a in (q, k, v))
                ck, cv = past[0][j], past[1][j]
                rows = ck.shape[1]
                n_keys = rows + t
                pad = (-n_keys) % LANES
                zeros = jnp.zeros((bt, pad, hd), F32)
                kk = jnp.concatenate([ck.reshape(bt, rows, hd), k3, zeros], axis=1)
                vv = jnp.concatenate([cv.reshape(bt, rows, hd), v3, zeros], axis=1)
                start = jnp.full((1,), pos0, jnp.int32)
                bias = band_bias(wts["rel_bias_a"][j], start, start - rows, tq=t, win=n_keys + pad,
                                 n_valid=n_keys)
                br = attn_window(q3, kk, vv, bias, tq=t, win=n_keys + pad, back=0)
                new_a.append((k3.reshape(bt, t, A_HEADS, HEAD_DIM), v3.reshape(bt, t, A_HEADS, HEAD_DIM)))
        elif kind == 1:
            hd = B_HEADS * 2 * HEAD_DIM
            lam_init = 0.8 - 0.6 * math.exp(-0.3 * i)
            common = dict(mode="diff", lam_p=wts["lambda_b"][j], sub_g=wts["subln_g_b"][j], lam_init=lam_init)
            w_in = wts["w_in_b"][j]
            if past is None:
                q, k, k16, v, mq, gate, vt = norm_matmul(
                    x2, wts["norm_g"][i], w_in,
                    (one(hd, BF16), (hd, (F32, BF16)), one(hd, F32), one(MEM_W, BF16), one(d, BF16)),
                    wt=w_in[:, 2 * hd:3 * hd].T, t_blk=FLASH_T, tm=FLASH_T,
                    split={1: (B_HEADS, 2, HEAD_DIM), 3: (B_HEADS, 2 * HEAD_DIM)})
                br = flash_prompt(q.reshape(bt, t, hd), k16.reshape(bt, t, hd), vt, t=FLASH_T, **common)
            else:
                q, k, v, mq, gate = norm_matmul(
                    x2, wts["norm_g"][i], w_in,
                    (one(hd, BF16), one(hd, F32), one(hd, F32), one(MEM_W, BF16), one(d, BF16)))
                ck, cv = past[2][j], past[3][j]
                rows = ck.shape[1]
                br = flash_attention(q.reshape(bt, t, hd), ck.reshape(bt, rows, hd), cv.reshape(bt, rows, hd),
                                     k.reshape(bt, t, hd), v.reshape(bt, t, hd),
                                     tq=t, tk=rows, q_pos0=rows, **common)
            new_b.append((k.reshape(bt, t, B_HEADS, 2, HEAD_DIM), v.reshape(bt, t, B_HEADS, 2 * HEAD_DIM)))
        else:
            cq, ckv, kra, krb, mq, gate = norm_matmul(
                x2, wts["norm_g"][i], wts["w_in_c"][j],
                (one(C_Q_LORA, F32), one(C_KV_LORA, F32), one(LANES, F32), one(LANES, F32),
                 one(MEM_W, BF16), one(d, BF16)))
            wq, wqr = wts["w_uq_c"][j]
            wk, wv, place = wts["w_ukv_c"][j]
            qcat, lat, kr = c_pre(cq, ckv, kra, krb, wts["q_norm_g_c"][j], wts["kv_norm_g_c"][j],
                                  wq, wqr, _rope_tables(pos), t)
            q3 = qcat.reshape(bt, t, -1)
            if past is None:
                kcat, vt = c_kv(lat, kr, wk, wv.T, place, v_transposed=True, tm=FLASH_T)
                br = flash_prompt(q3, kcat.reshape(bt, t, -1), vt, mode="mla", t=FLASH_T)
            else:
                kcat, vcat = c_kv(lat, kr, wk, wv, place, v_transposed=False)
                cl, cr = past[4][j], past[5][j]
                rows = cl.shape[1]
                cr = jnp.pad(cr.reshape(bt * rows, -1), ((0, 0), (0, LANES - C_ROPE)))
                kc_, vc_ = c_kv(cl.reshape(bt * rows, -1), cr, wk, wv, place, v_transposed=False)
                br = flash_attention(q3, kc_.reshape(bt, rows, -1), vc_.reshape(bt, rows, -1),
                                     kcat.reshape(bt, t, -1), vcat.reshape(bt, t, -1),
                                     mode="mla", tq=t, tk=rows, q_pos0=rows)
            new_c.append((lat.reshape(bt, t, -1), kr[:, :C_ROPE].reshape(bt, t, -1)))
        last = i == depth - 1
        x2 = gated_out(br.reshape(m, -1), mq, gate, x2, wts["w_out"][i], wts["final_g"],
                       mem_k[i], mem_vt[i], final=last)
    return x2.reshape(bt, t, d), new_a, new_b, new_c


def kernel(x_prompt, x_sample, cache_a_k, cache_a_v, cache_b_k, cache_b_v, cache_c_lat, cache_c_rope,
           cache_mem_k, cache_mem_v, mem_prompt, norm_g, final_g, mem_norm_g, w_mem_kv, w_out, w_in_a,
           rel_bias_a, w_in_b, lambda_b, subln_g_b, w_in_c, q_norm_g_c, kv_norm_g_c, w_uq_c, w_ukv_c):
    depth = w_out.shape[0]
    bp, n_mem, d = mem_prompt.shape
    mem_heads = MEM_W // HEAD_DIM
    wts = dict(
        norm_g=norm_g, final_g=final_g, rel_bias_a=rel_bias_a, lambda_b=lambda_b, subln_g_b=subln_g_b,
        q_norm_g_c=q_norm_g_c, kv_norm_g_c=kv_norm_g_c,
        w_out=[w_out[i].astype(BF16) for i in range(depth)],
        w_in_a=[_prep_in_ab(w_in_a[j]) for j in range(w_in_a.shape[0])],
        w_in_b=[_prep_in_ab(w_in_b[j]) for j in range(w_in_b.shape[0])],
        w_in_c=[_prep_in_c(w_in_c[j]) for j in range(w_in_c.shape[0])],
        w_uq_c=[_prep_uq(w_uq_c[j]) for j in range(w_uq_c.shape[0])],
        w_ukv_c=[_prep_ukv(w_ukv_c[j]) for j in range(w_ukv_c.shape[0])],
    )
    mem2 = mem_prompt.reshape(bp * n_mem, d)
    mem_k_p, mem_v_p, mem_vt_p = [], [], []
    for i in range(depth):
        w_kv = w_mem_kv[i].astype(BF16)
        mk, mv, mvt = norm_matmul(mem2, mem_norm_g[i], w_kv, ((MEM_W, (F32,)), (MEM_W, (F32,))),
                                  wt=w_kv[:, MEM_W:].T, t_blk=n_mem, tm=n_mem)
        mem_k_p.append(mk.reshape(bp, n_mem, MEM_W))
        mem_v_p.append(mv.reshape(bp, n_mem, MEM_W))
        mem_vt_p.append(mvt)

    y_p, na_p, nb_p, nc_p = _trunk(x_prompt, 0, mem_k_p, mem_vt_p, None, wts)

    bs = x_sample.shape[0]
    mem_k_s = [cache_mem_k[i].reshape(bs, n_mem, MEM_W) for i in range(depth)]
    mem_vt_s = [jnp.swapaxes(cache_mem_v[i].reshape(bs, n_mem, MEM_W), 1, 2).astype(BF16) for i in range(depth)]
    past = (cache_a_k, cache_a_v, cache_b_k, cache_b_v, cache_c_lat, cache_c_rope)
    y_s, na_s, nb_s, nc_s = _trunk(x_sample, cache_b_k.shape[2], mem_k_s, mem_vt_s, past, wts)

    stk = lambda lst, n: jnp.stack([s[n] for s in lst])
    heads4 = lambda lst: jnp.stack([a.reshape(bp, n_mem, mem_heads, HEAD_DIM) for a in lst])
    return (y_p, y_s,
            stk(na_p, 0), stk(na_p, 1), stk(na_s, 0), stk(na_s, 1),
            stk(nb_p, 0), stk(nb_p, 1), stk(nb_s, 0), stk(nb_s, 1),
            stk(nc_p, 0), stk(nc_p, 1), stk(nc_s, 0), stk(nc_s, 1),
            heads4(mem_k_p), heads4(mem_v_p))
```
